```python
import jax, jax.numpy as jnp
from jax import lax
import numpy as np

D_MODEL = 2048
BATCH = 1
SEQ = 8192
DEPTH = 1

D_CONV = D_MODEL // 2
CONV_WIDTH = 31
HEAD_DIM = 128
HEADS_PER_GROUP = 4
ATTN_PATTERNS = ((128, 1), (512, 4), (2048, 16))
N_GROUPS = len(ATTN_PATTERNS)
N_HEADS = N_GROUPS * HEADS_PER_GROUP
ATTN_WIDTH = N_HEADS * HEAD_DIM
ATTN_OUT_WIDTH = HEADS_PER_GROUP * HEAD_DIM
ATTN_BLOCK = 128
N_BRANCHES = 2
D_IN = 2 * D_CONV + 3 * ATTN_WIDTH + N_BRANCHES * D_MODEL
N_EXPERTS = 32
TOP_K = 4
D_EXPERT = D_MODEL
SWIGLU_LIMIT = 7.0
SWIGLU_ALPHA = 1.702
MOE_BLOCK = 256
NORM_EPS = 1e-5

kernel_name = "hybrid_conv_dilated_attn_moe"


def rmsnorm(x, g):
    xf = x.astype(jnp.float32)
    y = xf * lax.rsqrt(jnp.mean(xf * xf, axis=-1, keepdims=True) + NORM_EPS)
    return (y * g.astype(jnp.float32)).astype(x.dtype)


def layernorm(x, g, b):
    xf = x.astype(jnp.float32)
    mu = jnp.mean(xf, axis=-1, keepdims=True)
    var = jnp.mean(jnp.square(xf - mu), axis=-1, keepdims=True)
    y = (xf - mu) * lax.rsqrt(var + NORM_EPS)
    return (y * g.astype(jnp.float32) + b.astype(jnp.float32)).astype(x.dtype)


def causal_depthwise_conv(u, w, b):
    k_w, c = w.shape
    y = lax.conv_general_dilated(
        u, w[:, None, :], window_strides=(1,), padding=[(k_w - 1, 0)],
        dimension_numbers=("NWC", "WIO", "NWC"), feature_group_count=c)
    return y + b


def dilated_window_attention(q, k, v, window, dilation):
    b_sz, s_len, n_h, d_h = q.shape
    m_len = s_len // dilation
    n_blk = -(-m_len // ATTN_BLOCK)
    m_pad = n_blk * ATTN_BLOCK

    def to_blocks(t):
        t = t.reshape(b_sz, m_len, dilation, n_h, d_h)
        t = jnp.pad(t, ((0, 0), (0, m_pad - m_len), (0, 0), (0, 0), (0, 0)))
        return t.reshape(b_sz, n_blk, ATTN_BLOCK, dilation, n_h, d_h)

    def with_prev(t):
        prev = jnp.pad(t[:, :-1], ((0, 0), (1, 0), (0, 0), (0, 0), (0, 0), (0, 0)))
        return jnp.concatenate([prev, t], axis=2)

    qb = to_blocks(q)
    kk = with_prev(to_blocks(k))
    vv = with_prev(to_blocks(v))
    scale = HEAD_DIM ** -0.5
    s = jnp.einsum("bnqrhc,bnkrhc->bnrhqk", qb, kk).astype(jnp.float32) * scale
    qi = jnp.arange(ATTN_BLOCK)
    ki = jnp.arange(2 * ATTN_BLOCK)
    steps = qi[:, None] + ATTN_BLOCK - ki[None, :]
    band = (steps >= 0) & (steps <= window // dilation)
    key_m = (jnp.arange(n_blk)[:, None] - 1) * ATTN_BLOCK + ki[None, :]
    mask = band[None] & (key_m >= 0)[:, None, :]
    s = jnp.where(mask[None, :, None, None], s, -jnp.inf)
    lse = jax.nn.logsumexp(s, axis=-1)
    p = jnp.exp(s - lse[..., None])
    o = jnp.einsum("bnrhqk,bnkrhc->bnqrhc", p.astype(vv.dtype), vv)
    o = o.reshape(b_sz, m_pad, dilation, n_h, d_h)[:, :m_len].reshape(b_sz, s_len, n_h, d_h)
    lse = jnp.transpose(lse, (0, 1, 4, 2, 3)).reshape(b_sz, m_pad, dilation, n_h)
    lse = lse[:, :m_len].reshape(b_sz, s_len, n_h)
    return o, lse


def hybrid_mixer(h, w_in, b_gate, conv_w, conv_b, conv_ln_g, conv_ln_b,
                 w_conv_out, b_conv_out, w_attn_out, w_out):
    b_sz, s_len, _ = h.shape
    proj = h @ w_in
    o1 = 2 * D_CONV
    o2 = o1 + ATTN_WIDTH
    o3 = o2 + ATTN_WIDTH
    o4 = o3 + ATTN_WIDTH
    conv_in, q, k, v, gate_logits = (proj[..., :o1], proj[..., o1:o2], proj[..., o2:o3],
                                     proj[..., o3:o4], proj[..., o4:])

    a, g = conv_in[..., :D_CONV], conv_in[..., D_CONV:]
    u = a * jax.nn.sigmoid(g)
    u = causal_depthwise_conv(u, conv_w, conv_b)
    u = jax.nn.silu(layernorm(u, conv_ln_g, conv_ln_b))
    y_conv = u @ w_conv_out + b_conv_out

    q = q.reshape(b_sz, s_len, N_HEADS, HEAD_DIM)
    k = k.reshape(b_sz, s_len, N_HEADS, HEAD_DIM)
    v = v.reshape(b_sz, s_len, N_HEADS, HEAD_DIM)
    outs, lses = [], []
    for gi, (window, dilation) in enumerate(ATTN_PATTERNS):
        hs = slice(gi * HEADS_PER_GROUP, (gi + 1) * HEADS_PER_GROUP)
        o_g, lse_g = dilated_window_attention(q[:, :, hs], k[:, :, hs], v[:, :, hs], window, dilation)
        outs.append(o_g)
        lses.append(lse_g)
    mix_w = jax.nn.softmax(jnp.stack(lses, axis=0), axis=0)
    o = jnp.sum(mix_w[..., None] * jnp.stack(outs, axis=0).astype(jnp.float32), axis=0)
    o = o.astype(h.dtype).reshape(b_sz, s_len, ATTN_OUT_WIDTH)
    y_attn = o @ w_attn_out

    gates = jax.nn.sigmoid(gate_logits + b_gate)
    g_conv, g_attn = gates[..., :D_MODEL], gates[..., D_MODEL:]
    return (g_conv * y_conv + g_attn * y_attn) @ w_out


def moe_ffn(h, w_router, b_router, w_exp_in, b_exp_in, w_exp_out, b_exp_out):
    b_sz, s_len, d = h.shape
    n_tok = b_sz * s_len
    n_assign = n_tok * TOP_K
    ht = h.reshape(n_tok, d)
    logits = (ht @ w_router + b_router).astype(jnp.float32)
    top_val, top_idx = lax.top_k(logits, TOP_K)
    gate = jax.nn.softmax(top_val, axis=-1)

    flat_e = top_idx.reshape(-1)
    flat_tok = jnp.arange(n_assign, dtype=jnp.int32) // TOP_K
    flat_gate = gate.reshape(-1)
    order = jnp.argsort(flat_e)
    sorted_e = flat_e[order]
    counts = jnp.zeros((N_EXPERTS,), jnp.int32).at[flat_e].add(1)
    padded = (counts + MOE_BLOCK - 1) // MOE_BLOCK * MOE_BLOCK
    start = jnp.cumsum(counts) - counts
    pad_end = jnp.cumsum(padded)
    pad_start = pad_end - padded
    dest = pad_start[sorted_e] + (jnp.arange(n_assign, dtype=jnp.int32) - start[sorted_e])
    n_blocks = -(-(n_assign + N_EXPERTS * (MOE_BLOCK - 1)) // MOE_BLOCK)
    n_rows = n_blocks * MOE_BLOCK
    row_tok = jnp.full((n_rows,), n_tok, jnp.int32).at[dest].set(flat_tok[order])
    row_gate = jnp.zeros((n_rows,), jnp.float32).at[dest].set(flat_gate[order])
    block_e = jnp.searchsorted(pad_end, jnp.arange(n_blocks, dtype=jnp.int32) * MOE_BLOCK, side="right")
    block_e = jnp.minimum(block_e, N_EXPERTS - 1).astype(jnp.int32)

    h_pad = jnp.concatenate([ht, jnp.zeros((1, d), ht.dtype)], axis=0)
    xs = h_pad[row_tok].reshape(n_blocks, MOE_BLOCK, d)

    def expert_block(args):
        xb, e = args
        gu = xb @ w_exp_in[e] + b_exp_in[e]
        g, u = gu[:, :D_EXPERT], gu[:, D_EXPERT:]
        g = jnp.minimum(g, SWIGLU_LIMIT)
        u = jnp.clip(u, -SWIGLU_LIMIT, SWIGLU_LIMIT)
        act = (u + 1.0) * (g * jax.nn.sigmoid(SWIGLU_ALPHA * g))
        return act @ w_exp_out[e] + b_exp_out[e]

    ys = lax.map(expert_block, (xs, block_e)).reshape(n_rows, d)
    out = jnp.zeros((n_tok + 1, d), ys.dtype).at[row_tok].add(ys * row_gate[:, None].astype(ys.dtype))
    return out[:n_tok].reshape(b_sz, s_len, d)


def setup_inputs(seed: int = 0) -> dict:
    key = jax.random.key(seed)
    ks = jax.random.split(key, 20)
    L = DEPTH

    def nrm(k, shape, scale):
        return jax.random.normal(k, shape, jnp.float32) * scale

    return {
        "x": nrm(ks[0], (BATCH, SEQ, D_MODEL), 1.0),
        "norm1_g": 1.0 + nrm(ks[1], (L, D_MODEL), 0.02),
        "w_in": nrm(ks[2], (L, D_MODEL, D_IN), D_MODEL ** -0.5),
        "b_gate": nrm(ks[3], (L, N_BRANCHES * D_MODEL), 0.02),
        "conv_w": nrm(ks[4], (L, CONV_WIDTH, D_CONV), CONV_WIDTH ** -0.5),
        "conv_b": nrm(ks[5], (L, D_CONV), 0.02),
        "conv_ln_g": 1.0 + nrm(ks[6], (L, D_CONV), 0.02),
        "conv_ln_b": nrm(ks[7], (L, D_CONV), 0.02),
        "w_conv_out": nrm(ks[8], (L, D_CONV, D_MODEL), D_CONV ** -0.5),
        "b_conv_out": nrm(ks[9], (L, D_MODEL), 0.02),
        "w_attn_out": nrm(ks[10], (L, ATTN_OUT_WIDTH, D_MODEL), ATTN_OUT_WIDTH ** -0.5),
        "w_out": nrm(ks[11], (L, D_MODEL, D_MODEL), D_MODEL ** -0.5),
        "norm2_g": 1.0 + nrm(ks[12], (L, D_MODEL), 0.02),
        "w_router": nrm(ks[13], (L, D_MODEL, N_EXPERTS), D_MODEL ** -0.5),
        "b_router": nrm(ks[14], (L, N_EXPERTS), 0.01),
        "w_exp_in": nrm(ks[15], (L, N_EXPERTS, D_MODEL, 2 * D_EXPERT), D_MODEL ** -0.5),
        "b_exp_in": nrm(ks[16], (L, N_EXPERTS, 2 * D_EXPERT), 0.02),
        "w_exp_out": nrm(ks[17], (L, N_EXPERTS, D_EXPERT, D_MODEL), D_EXPERT ** -0.5),
        "b_exp_out": nrm(ks[18], (L, N_EXPERTS, D_MODEL), 0.02),
        "norm_f_g": 1.0 + nrm(ks[19], (D_MODEL,), 0.02),
    }


def reference(x, norm1_g, w_in, b_gate, conv_w, conv_b, conv_ln_g, conv_ln_b,
              w_conv_out, b_conv_out, w_attn_out, w_out, norm2_g, w_router, b_router,
              w_exp_in, b_exp_in, w_exp_out, b_exp_out, norm_f_g):
    for l in range(DEPTH):
        h = rmsnorm(x, norm1_g[l])
        x = x + hybrid_mixer(h, w_in[l], b_gate[l], conv_w[l], conv_b[l], conv_ln_g[l],
                             conv_ln_b[l], w_conv_out[l], b_conv_out[l], w_attn_out[l], w_out[l])
        h = rmsnorm(x, norm2_g[l])
        x = x + moe_ffn(h, w_router[l], b_router[l], w_exp_in[l], b_exp_in[l],
                        w_exp_out[l], b_exp_out[l])
    return rmsnorm(x, norm_f_g)
```

```python
import functools

import jax
import jax.numpy as jnp
from jax import lax
from jax.experimental import pallas as pl
from jax.experimental.pallas import tpu as pltpu

D_MODEL = 2048
SEQ = 8192
D_CONV = D_MODEL // 2
CONV_WIDTH = 31
HEAD_DIM = 128
HEADS_PER_GROUP = 4
ATTN_PATTERNS = ((128, 1), (512, 4), (2048, 16))
N_GROUPS = len(ATTN_PATTERNS)
ATTN_WIDTH = N_GROUPS * HEADS_PER_GROUP * HEAD_DIM
ATTN_OUT_WIDTH = HEADS_PER_GROUP * HEAD_DIM
ATTN_BLOCK = 128
D_IN = 2 * D_CONV + 3 * ATTN_WIDTH + 2 * D_MODEL
N_EXPERTS = 32
TOP_K = 4
D_EXPERT = D_MODEL
SWIGLU_LIMIT = 7.0
SWIGLU_ALPHA = 1.702
NORM_EPS = 1e-5

CB = 512
N_CB = D_IN // CB
CB_CONV_A, CB_CONV_G = 0, 2
CB_Q, CB_K, CB_V = 4, 7, 10
CB_GATE_C, CB_GATE_A = 13, 17
GATE_COL0 = CB_GATE_C * CB

VMEM_LIMIT = 56 * 1024 * 1024

F32 = jnp.float32
BF16 = jnp.bfloat16


def _sigmoid(z):
    return 1.0 / (1.0 + jnp.exp(-z))


def _params(n_axes, vmem=VMEM_LIMIT):
    return pltpu.CompilerParams(
        dimension_semantics=("arbitrary",) * n_axes, vmem_limit_bytes=vmem)


IN_TM = 512
IN_TN = 1536


def _inproj_kernel(x_ref, g_ref, w_ref, bg_ref, o_ref, wbf_ref):
    j = pl.program_id(0)
    i = pl.program_id(1)

    @pl.when(i == 0)
    def _():
        wbf_ref[...] = w_ref[...].astype(BF16)

    x = x_ref[...]
    ms = jnp.mean(x * x, axis=-1, keepdims=True)
    h = ((x * lax.rsqrt(ms + NORM_EPS)) * g_ref[...]).astype(BF16)
    acc = jnp.dot(h, wbf_ref[...], preferred_element_type=F32)

    first_gate_tile = GATE_COL0 // IN_TN
    n_sub = IN_TN // CB

    def store(val):
        for c in range(n_sub):
            o_ref[c] = val[:, c * CB:(c + 1) * CB].astype(BF16)

    @pl.when(j < first_gate_tile)
    def _():
        store(acc)

    @pl.when(j == first_gate_tile)
    def _():
        col = j * IN_TN + lax.broadcasted_iota(jnp.int32, (1, IN_TN), 1)
        store(jnp.where(col >= GATE_COL0, _sigmoid(acc + bg_ref[...]), acc))

    @pl.when(j > first_gate_tile)
    def _():
        store(_sigmoid(acc + bg_ref[...]))


def _in_projection(x2, norm_g, w_in, b_gate_ext):
    s = x2.shape[0]
    grid = (D_IN // IN_TN, s // IN_TM)
    return pl.pallas_call(
        _inproj_kernel,
        grid=grid,
        in_specs=[
            pl.BlockSpec((IN_TM, D_MODEL), lambda j, i: (i, 0)),
            pl.BlockSpec((1, D_MODEL), lambda j, i: (0, 0)),
            pl.BlockSpec((D_MODEL, IN_TN), lambda j, i: (0, j)),
            pl.BlockSpec((1, IN_TN), lambda j, i: (0, j)),
        ],
        out_specs=pl.BlockSpec((IN_TN // CB, IN_TM, CB), lambda j, i: (j, i, 0)),
        out_shape=jax.ShapeDtypeStruct((N_CB, s, CB), BF16),
        scratch_shapes=[pltpu.VMEM((D_MODEL, IN_TN), BF16)],
        compiler_params=_params(2),
        name="in_projection",
    )(x2, norm_g, w_in, b_gate_ext)


CONV_TS = 256
CONV_HALO = 32
CONV_RC = 32
CONV_N = CONV_TS + CONV_HALO


def _conv_kernel(ac_ref, gc_ref, ah_ref, gh_ref, w_ref, cb_ref, lg_ref, lb_ref, o_ref, r_ref):
    i = pl.program_id(0)
    has_prev = i > 0
    half = D_CONV // 2
    for c in range(2):
        cs = slice(c * half, (c + 1) * half)
        uh = ah_ref[c].astype(F32) * _sigmoid(gh_ref[c].astype(F32))
        r_ref[0, 0:CONV_HALO, cs] = jnp.where(has_prev, uh, 0.0)
        r_ref[0, CONV_HALO:CONV_N, cs] = ac_ref[c].astype(F32) * _sigmoid(gc_ref[c].astype(F32))
    u_ext = r_ref[0]
    for b in range(1, 8):
        r_ref[b] = pltpu.roll(u_ext, CONV_N - b, axis=0)

    def chunk(ci, carry):
        r0 = pl.multiple_of(ci * CONV_RC, CONV_RC)
        acc = jnp.broadcast_to(cb_ref[...], (CONV_RC, D_CONV))
        for k in range(CONV_WIDTH):
            kp = k + (CONV_HALO - (CONV_WIDTH - 1))
            a, b = kp // 8, kp % 8
            acc = acc + w_ref[k:k + 1, :] * r_ref[b, pl.ds(r0 + 8 * a, CONV_RC), :]
        mu = jnp.mean(acc, axis=-1, keepdims=True)
        d = acc - mu
        var = jnp.mean(d * d, axis=-1, keepdims=True)
        y = (d * lax.rsqrt(var + NORM_EPS)) * lg_ref[...] + lb_ref[...]
        o_ref[pl.ds(r0, CONV_RC), :] = (y * _sigmoid(y)).astype(BF16)
        return carry

    lax.fori_loop(0, CONV_TS // CONV_RC, chunk, 0)


def _conv_branch(proj3, conv_w, conv_b, ln_g, ln_b):
    s = proj3.shape[1]
    hb = CONV_TS // CONV_HALO
    cur = lambda blk: pl.BlockSpec((2, CONV_TS, CB), lambda i: (blk // 2, i, 0))
    halo = lambda blk: pl.BlockSpec(
        (2, CONV_HALO, CB), lambda i: (blk // 2, jnp.maximum(i * hb - 1, 0), 0))
    vec = pl.BlockSpec((1, D_CONV), lambda i: (0, 0))
    return pl.pallas_call(
        _conv_kernel,
        grid=(s // CONV_TS,),
        in_specs=[cur(CB_CONV_A), cur(CB_CONV_G), halo(CB_CONV_A), halo(CB_CONV_G),
                  pl.BlockSpec((CONV_WIDTH, D_CONV), lambda i: (0, 0)), vec, vec, vec],
        out_specs=pl.BlockSpec((CONV_TS, D_CONV), lambda i: (i, 0)),
        out_shape=jax.ShapeDtypeStruct((s, D_CONV), BF16),
        scratch_shapes=[pltpu.VMEM((8, CONV_N, D_CONV), F32)],
        compiler_params=_params(1),
        name="conv_branch",
    )(proj3, proj3, proj3, proj3, conv_w, conv_b, ln_g, ln_b)


def _attn_kernel(q_ref, kc_ref, kp_ref, vc_ref, vp_ref, o_ref, lse_ref, *, tq):
    n = pl.program_id(0)
    blk = ATTN_BLOCK
    scale = HEAD_DIM ** -0.5
    qi = lax.broadcasted_iota(jnp.int32, (blk, 2 * blk), 0)
    ki = lax.broadcasted_iota(jnp.int32, (blk, 2 * blk), 1)
    band = (ki >= qi) & (ki <= qi + blk)
    lane = lax.broadcasted_iota(jnp.int32, (blk, HEADS_PER_GROUP), 1)
    for j in range(tq // blk):
        mask = band
        if j == 0:
            mask = band & ((ki >= blk) | (n > 0))
        rows = slice(j * blk, (j + 1) * blk)
        lse_tile = jnp.zeros((blk, HEADS_PER_GROUP), F32)
        for hh in range(HEADS_PER_GROUP):
            cs = slice(hh * HEAD_DIM, (hh + 1) * HEAD_DIM)
            q = q_ref[rows, cs]
            if j == 0:
                k = jnp.concatenate([kp_ref[:, cs], kc_ref[0:blk, cs]], axis=0)
                v = jnp.concatenate([vp_ref[:, cs], vc_ref[0:blk, cs]], axis=0)
            else:
                k = kc_ref[(j - 1) * blk:(j + 1) * blk, cs]
                v = vc_ref[(j - 1) * blk:(j + 1) * blk, cs]
            s = lax.dot_general(q, k, (((1,), (1,)), ((), ())), preferred_element_type=F32) * scale
            s = jnp.where(mask, s, -jnp.inf)
            m = jnp.max(s, axis=-1, keepdims=True)
            p = jnp.exp(s - m)
            l = jnp.sum(p, axis=-1, keepdims=True)
            o = jnp.dot(p.astype(BF16), v, preferred_element_type=F32) / l
            o_ref[rows, cs] = o.astype(BF16)
            lse_tile = jnp.where(lane == hh, m + jnp.log(l), lse_tile)
        lse_ref[0, rows, :] = lse_tile


def _dilated_attention(proj3, group):
    _, dil = ATTN_PATTERNS[group]
    s = proj3.shape[1]
    m_len = s // dil
    tq = min(512, m_len)
    view = proj3.reshape(N_CB, m_len, dil * CB)
    tb = tq // ATTN_BLOCK
    cur = lambda cb: pl.BlockSpec((None, tq, CB), lambda n, r: (cb + group, n, r))
    prev = lambda cb: pl.BlockSpec(
        (None, ATTN_BLOCK, CB), lambda n, r: (cb + group, jnp.maximum(n * tb - 1, 0), r))
    o, lse = pl.pallas_call(
        functools.partial(_attn_kernel, tq=tq),
        grid=(m_len // tq, dil),
        in_specs=[cur(CB_Q), cur(CB_K), prev(CB_K), cur(CB_V), prev(CB_V)],
        out_specs=[pl.BlockSpec((tq, CB), lambda n, r: (n, r)),
                   pl.BlockSpec((1, tq, HEADS_PER_GROUP), lambda n, r: (r, n, 0))],
        out_shape=[jax.ShapeDtypeStruct((m_len, dil * CB), BF16),
                   jax.ShapeDtypeStruct((dil, m_len, HEADS_PER_GROUP), F32)],
        compiler_params=_params(2),
        name=f"dilated_attention_g{group}",
    )(view, view, view, view, view)
    o = o.reshape(s, CB)
    lse = jnp.transpose(lse, (1, 0, 2)).reshape(s, HEADS_PER_GROUP)
    return o, lse


MIX_TM = 256


def _mixer_out_kernel(x_ref, o0_ref, o1_ref, o2_ref, l0_ref, l1_ref, l2_ref, u_ref,
                      gc0, gc1, gc2, gc3, ga0, ga1, ga2, ga3,
                      wa_ref, wc_ref, bc_ref, wo_ref, g2_ref, wrh_ref, wrl_ref, br_ref,
                      x1_ref, h2_ref, logit_ref):
    l0, l1, l2 = l0_ref[...], l1_ref[...], l2_ref[...]
    m = jnp.maximum(jnp.maximum(l0, l1), l2)
    e0, e1, e2 = jnp.exp(l0 - m), jnp.exp(l1 - m), jnp.exp(l2 - m)
    den = e0 + e1 + e2
    mix = (e0 / den, e1 / den, e2 / den)
    o_refs = (o0_ref, o1_ref, o2_ref)
    parts = []
    for hh in range(HEADS_PER_GROUP):
        cs = slice(hh * HEAD_DIM, (hh + 1) * HEAD_DIM)
        acc = mix[0][:, hh:hh + 1] * o_refs[0][:, cs].astype(F32)
        for g in range(1, N_GROUPS):
            acc = acc + mix[g][:, hh:hh + 1] * o_refs[g][:, cs].astype(F32)
        parts.append(acc)
    o = jnp.concatenate(parts, axis=1).astype(BF16)
    ya = jnp.dot(o, wa_ref[...], preferred_element_type=F32)
    yc = jnp.dot(u_ref[...], wc_ref[...], preferred_element_type=F32) + bc_ref[...]
    gcs = (gc0, gc1, gc2, gc3)
    gas = (ga0, ga1, ga2, ga3)
    merged = []
    for c in range(D_MODEL // CB):
        cs = slice(c * CB, (c + 1) * CB)
        merged.append((gcs[c][...].astype(F32) * yc[:, cs]
                       + gas[c][...].astype(F32) * ya[:, cs]).astype(BF16))
    merged = jnp.concatenate(merged, axis=1)
    x1 = x_ref[...] + jnp.dot(merged, wo_ref[...], preferred_element_type=F32)
    x1_ref[...] = x1
    ms = jnp.mean(x1 * x1, axis=-1, keepdims=True)
    h2 = (x1 * lax.rsqrt(ms + NORM_EPS)) * g2_ref[...]
    h2_hi = h2.astype(BF16)
    h2_ref[...] = h2_hi
    h2_lo = (h2 - h2_hi.astype(F32)).astype(BF16)
    logits = (jnp.dot(h2_hi, wrh_ref[...], preferred_element_type=F32)
              + jnp.dot(h2_hi, wrl_ref[...], preferred_element_type=F32)
              + jnp.dot(h2_lo, wrh_ref[...], preferred_element_type=F32))
    logit_ref[...] = logits + br_ref[...]


def _mixer_output(x2, outs, lses, u_ln, proj3, wa, wc, bc, wo, g2, wr_hi, wr_lo, br):
    s = x2.shape[0]
    row = lambda w: pl.BlockSpec((MIX_TM, w), lambda i: (i, 0))
    gate = lambda cb: pl.BlockSpec((None, MIX_TM, CB), lambda i: (cb, i, 0))
    full = lambda a: pl.BlockSpec(a.shape, lambda i: (0,) * a.ndim)
    in_specs = ([row(D_MODEL)] + [row(CB)] * 3 + [row(HEADS_PER_GROUP)] * 3 + [row(D_CONV)]
                + [gate(CB_GATE_C + c) for c in range(4)]
                + [gate(CB_GATE_A + c) for c in range(4)]
                + [full(a) for a in (wa, wc, bc, wo, g2, wr_hi, wr_lo, br)])
    return pl.pallas_call(
        _mixer_out_kernel,
        grid=(s // MIX_TM,),
        in_specs=in_specs,
        out_specs=[row(D_MODEL), row(D_MODEL), row(N_EXPERTS)],
        out_shape=[jax.ShapeDtypeStruct((s, D_MODEL), F32),
                   jax.ShapeDtypeStruct((s, D_MODEL), BF16),
                   jax.ShapeDtypeStruct((s, N_EXPERTS), F32)],
        compiler_params=_params(1),
        name="mixer_output",
    )(x2, *outs, *lses, u_ln, *([proj3] * 8), wa, wc, bc, wo, g2, wr_hi, wr_lo, br)


ROW_BLK = 256
UNIT_BLOCKS = 6
UNIT_ROWS = ROW_BLK * UNIT_BLOCKS
EXP_TF = 256
EXP_NF = D_EXPERT // EXP_TF


def _expert_kernel(ue_ref, ub0_ref, unb_ref, nu_ref,
                   xs_ref, wg_ref, wu_ref, bg_ref, bu_ref, wo_ref, bo_ref,
                   ys_ref,
                   xbuf, acc, wgu_bf, wo_bf, sem_x, sem_y):
    u = pl.program_id(0)
    f = pl.program_id(1)
    n_units = nu_ref[0]

    def x_copy(unit, slot, b):
        src = xs_ref.at[pl.ds((ub0_ref[unit] + b) * ROW_BLK, ROW_BLK)]
        dst = xbuf.at[slot, pl.ds(b * ROW_BLK, ROW_BLK)]
        return pltpu.make_async_copy(src, dst, sem_x.at[slot])

    def y_copy(unit, b):
        src = acc.at[pl.ds(b * ROW_BLK, ROW_BLK)]
        dst = ys_ref.at[pl.ds((ub0_ref[unit] + b) * ROW_BLK, ROW_BLK)]
        return pltpu.make_async_copy(src, dst, sem_y.at[0])

    def start_x(unit, slot):
        lax.fori_loop(0, unb_ref[unit], lambda b, c: (x_copy(unit, slot, b).start(), c)[1], 0)

    def wait_x(unit, slot):
        lax.fori_loop(0, unb_ref[unit], lambda b, c: (x_copy(unit, slot, b).wait(), c)[1], 0)

    @pl.when(u < n_units)
    def _():
        slot = lax.rem(u, 2)
        nblk = unb_ref[u]

        @pl.when(f == 0)
        def _():
            @pl.when(u == 0)
            def _():
                start_x(u, slot)

            wait_x(u, slot)

            @pl.when(u + 1 < n_units)
            def _():
                start_x(u + 1, 1 - slot)

            def init(b, c):
                r0 = pl.multiple_of(b * ROW_BLK, ROW_BLK)
                acc[pl.ds(r0, ROW_BLK), :] = jnp.broadcast_to(bo_ref[...], (ROW_BLK, D_MODEL))
                return c

            lax.fori_loop(0, nblk, init, 0)

        wgu_bf[:, 0:EXP_TF] = wg_ref[...].astype(BF16)
        wgu_bf[:, EXP_TF:2 * EXP_TF] = wu_ref[...].astype(BF16)
        wo_bf[...] = wo_ref[...].astype(BF16)

        def sub(b, c):
            r0 = pl.multiple_of(b * ROW_BLK, ROW_BLK)
            x = xbuf[slot, pl.ds(r0, ROW_BLK), :]
            gu = jnp.dot(x, wgu_bf[...], preferred_element_type=F32)
            g = jnp.minimum(gu[:, 0:EXP_TF] + bg_ref[...], SWIGLU_LIMIT)
            up = jnp.clip(gu[:, EXP_TF:2 * EXP_TF] + bu_ref[...], -SWIGLU_LIMIT, SWIGLU_LIMIT)
            act = (up + 1.0) * (g * _sigmoid(SWIGLU_ALPHA * g))
            acc[pl.ds(r0, ROW_BLK), :] += jnp.dot(
                act.astype(BF16), wo_bf[...], preferred_element_type=F32)

            @pl.when(f == EXP_NF - 1)
            def _():
                y_copy(u, b).start()

            return c

        lax.fori_loop(0, nblk, sub, 0)

        @pl.when(f == EXP_NF - 1)
        def _():
            lax.fori_loop(0, nblk, lambda b, c: (y_copy(u, b).wait(), c)[1], 0)

    @pl.when((u == pl.num_programs(0) - 1) & (f == EXP_NF - 1))
    def _():
        n_blocks = ys_ref.shape[0] // ROW_BLK
        acc[0:ROW_BLK, :] = jnp.zeros((ROW_BLK, D_MODEL), F32)

        def tail_copy(b):
            return pltpu.make_async_copy(acc.at[pl.ds(0, ROW_BLK)],
                                         ys_ref.at[pl.ds(b * ROW_BLK, ROW_BLK)], sem_y.at[0])

        lax.fori_loop(nu_ref[1], n_blocks, lambda b, c: (tail_copy(b).start(), c)[1], 0)
        lax.fori_loop(nu_ref[1], n_blocks, lambda b, c: (tail_copy(b).wait(), c)[1], 0)


def _expert_ffn(xs, unit_e, unit_blk0, unit_nblk, n_units, w_in, b_in, w_out, b_out):
    n_rows = xs.shape[0]
    n_units_max = unit_e.shape[0]

    def f_eff(u, f, nu):
        return jnp.where(u < nu[0], f, EXP_NF - 1)

    def e_eff(u, ue, nu):
        return ue[jnp.minimum(u, jnp.maximum(nu[0] - 1, 0))]

    in_specs = [
        pl.BlockSpec(memory_space=pl.ANY),
        pl.BlockSpec((None, D_MODEL, EXP_TF),
                     lambda u, f, ue, ub, un, nu: (e_eff(u, ue, nu), 0, f_eff(u, f, nu))),
        pl.BlockSpec((None, D_MODEL, EXP_TF),
                     lambda u, f, ue, ub, un, nu: (e_eff(u, ue, nu), 0, EXP_NF + f_eff(u, f, nu))),
        pl.BlockSpec((None, 1, EXP_TF),
                     lambda u, f, ue, ub, un, nu: (e_eff(u, ue, nu), 0, f_eff(u, f, nu))),
        pl.BlockSpec((None, 1, EXP_TF),
                     lambda u, f, ue, ub, un, nu: (e_eff(u, ue, nu), 0, EXP_NF + f_eff(u, f, nu))),
        pl.BlockSpec((None, EXP_TF, D_MODEL),
                     lambda u, f, ue, ub, un, nu: (e_eff(u, ue, nu), f_eff(u, f, nu), 0)),
        pl.BlockSpec((None, 1, D_MODEL),
                     lambda u, f, ue, ub, un, nu: (e_eff(u, ue, nu), 0, 0)),
    ]
    grid_spec = pltpu.PrefetchScalarGridSpec(
        num_scalar_prefetch=4,
        grid=(n_units_max, EXP_NF),
        in_specs=in_specs,
        out_specs=pl.BlockSpec(memory_space=pl.ANY),
        scratch_shapes=[
            pltpu.VMEM((2, UNIT_ROWS, D_MODEL), BF16),
            pltpu.VMEM((UNIT_ROWS, D_MODEL), F32),
            pltpu.VMEM((D_MODEL, 2 * EXP_TF), BF16),
            pltpu.VMEM((EXP_TF, D_MODEL), BF16),
            pltpu.SemaphoreType.DMA((2,)),
            pltpu.SemaphoreType.DMA((1,)),
        ],
    )
    return pl.pallas_call(
        _expert_kernel,
        grid_spec=grid_spec,
        out_shape=jax.ShapeDtypeStruct((n_rows, D_MODEL), F32),
        compiler_params=_params(2),
        name="expert_ffn",
    )(unit_e, unit_blk0, unit_nblk, n_units, xs,
      w_in, w_in, b_in[:, None, :], b_in[:, None, :], w_out, b_out[:, None, :])


FIN_TM = 512


def _final_kernel(x_ref, y_ref, g_ref, o_ref):
    x = x_ref[...] + y_ref[...]
    ms = jnp.mean(x * x, axis=-1, keepdims=True)
    o_ref[...] = (x * lax.rsqrt(ms + NORM_EPS)) * g_ref[...]


def _final_norm(x1, moe, g):
    s = x1.shape[0]
    row = pl.BlockSpec((FIN_TM, D_MODEL), lambda i: (i, 0))
    return pl.pallas_call(
        _final_kernel,
        grid=(s // FIN_TM,),
        in_specs=[row, row, pl.BlockSpec((1, D_MODEL), lambda i: (0, 0))],
        out_specs=row,
        out_shape=jax.ShapeDtypeStruct((s, D_MODEL), F32),
        compiler_params=_params(1),
        name="final_norm",
    )(x1, moe, g)


def _route(logits):
    n_tok = logits.shape[0]
    n_assign = n_tok * TOP_K
    top_val, top_idx = lax.top_k(logits, TOP_K)
    gate = jax.nn.softmax(top_val, axis=-1)
    flat_e = top_idx.reshape(-1)
    flat_tok = jnp.arange(n_assign, dtype=jnp.int32) // TOP_K
    order = jnp.argsort(flat_e)
    sorted_e = flat_e[order]
    counts = jnp.zeros((N_EXPERTS,), jnp.int32).at[flat_e].add(1)
    nblk_e = (counts + ROW_BLK - 1) // ROW_BLK
    padded = nblk_e * ROW_BLK
    start = jnp.cumsum(counts) - counts
    pad_start = jnp.cumsum(padded) - padded
    dest = pad_start[sorted_e] + (jnp.arange(n_assign, dtype=jnp.int32) - start[sorted_e])
    n_blocks = -(-(n_assign + N_EXPERTS * (ROW_BLK - 1)) // ROW_BLK)
    n_rows = n_blocks * ROW_BLK
    row_tok = jnp.full((n_rows,), n_tok, jnp.int32).at[dest].set(flat_tok[order])
    row_gate = jnp.zeros((n_rows,), F32).at[dest].set(gate.reshape(-1)[order])

    n_units_max = (n_blocks + (UNIT_BLOCKS - 1) * N_EXPERTS) // UNIT_BLOCKS
    chunks_e = (nblk_e + UNIT_BLOCKS - 1) // UNIT_BLOCKS
    chunk_end = jnp.cumsum(chunks_e)
    chunk_start = chunk_end - chunks_e
    n_units = chunk_end[-1]
    uid = jnp.arange(n_units_max, dtype=jnp.int32)
    unit_e = jnp.minimum(jnp.searchsorted(chunk_end, uid, side="right"), N_EXPERTS - 1).astype(jnp.int32)
    c_in_e = uid - chunk_start[unit_e]
    unit_blk0 = (pad_start[unit_e] // ROW_BLK + c_in_e * UNIT_BLOCKS).astype(jnp.int32)
    unit_nblk = jnp.clip(nblk_e[unit_e] - c_in_e * UNIT_BLOCKS, 0, UNIT_BLOCKS).astype(jnp.int32)
    active = uid < n_units
    unit_blk0 = jnp.where(active, unit_blk0, 0)
    unit_nblk = jnp.where(active, unit_nblk, 0)
    unit_counts = jnp.stack([n_units, jnp.sum(nblk_e)]).astype(jnp.int32)
    return row_tok, row_gate, unit_e, unit_blk0, unit_nblk, unit_counts


def kernel(x, norm1_g, w_in, b_gate, conv_w, conv_b, conv_ln_g, conv_ln_b, w_conv_out, b_conv_out,
           w_attn_out, w_out, norm2_g, w_router, b_router, w_exp_in, b_exp_in, w_exp_out, b_exp_out,
           norm_f_g):
    b_sz, s_len, d = x.shape
    x2 = x.reshape(b_sz * s_len, d)
    l = 0
    b_gate_ext = jnp.concatenate([jnp.zeros((GATE_COL0,), F32), b_gate[l]])[None, :]
    proj3 = _in_projection(x2, norm1_g[l][None, :], w_in[l], b_gate_ext)
    u_ln = _conv_branch(proj3, conv_w[l], conv_b[l][None, :], conv_ln_g[l][None, :],
                        conv_ln_b[l][None, :])
    outs, lses = [], []
    for g in range(N_GROUPS):
        o_g, lse_g = _dilated_attention(proj3, g)
        outs.append(o_g)
        lses.append(lse_g)
    wr = w_router[l]
    wr_hi = wr.astype(BF16)
    wr_lo = (wr - wr_hi.astype(F32)).astype(BF16)
    x1, h2, logits = _mixer_output(
        x2, outs, lses, u_ln, proj3,
        w_attn_out[l].astype(BF16), w_conv_out[l].astype(BF16), b_conv_out[l][None, :],
        w_out[l].astype(BF16), norm2_g[l][None, :], wr_hi, wr_lo, b_router[l][None, :])

    row_tok, row_gate, unit_e, unit_blk0, unit_nblk, n_units = _route(logits)
    h_pad = jnp.concatenate([h2, jnp.zeros((1, d), h2.dtype)], axis=0)
    xs = h_pad[row_tok]
    ys = _expert_ffn(xs, unit_e, unit_blk0, unit_nblk, n_units,
                     w_exp_in[l], b_exp_in[l], w_exp_out[l], b_exp_out[l])
    n_tok = x2.shape[0]
    moe = jnp.zeros((n_tok + 1, d), F32).at[row_tok].add(ys * row_gate[:, None])[:n_tok]
    out = _final_norm(x1, moe, norm_f_g[None, :])
    return out.reshape(b_sz, s_len, d)
```

```python
import functools

import jax
import jax.numpy as jnp
from jax import lax
from jax.experimental import pallas as pl
from jax.experimental.pallas import tpu as pltpu

D_MODEL = 2048
SEQ = 8192
D_CONV = D_MODEL // 2
CONV_WIDTH = 31
HEAD_DIM = 128
HEADS_PER_GROUP = 4
ATTN_PATTERNS = ((128, 1), (512, 4), (2048, 16))
N_GROUPS = len(ATTN_PATTERNS)
ATTN_WIDTH = N_GROUPS * HEADS_PER_GROUP * HEAD_DIM
ATTN_OUT_WIDTH = HEADS_PER_GROUP * HEAD_DIM
ATTN_BLOCK = 128
D_IN = 2 * D_CONV + 3 * ATTN_WIDTH + 2 * D_MODEL
N_EXPERTS = 32
TOP_K = 4
D_EXPERT = D_MODEL
SWIGLU_LIMIT = 7.0
SWIGLU_ALPHA = 1.702
NORM_EPS = 1e-5

CB = 512
N_CB = D_IN // CB
CB_CONV_A, CB_CONV_G = 0, 2
CB_Q, CB_K, CB_V = 4, 7, 10
CB_GATE_C, CB_GATE_A = 13, 17
GATE_COL0 = CB_GATE_C * CB

VMEM_LIMIT = 56 * 1024 * 1024

F32 = jnp.float32
BF16 = jnp.bfloat16
U32 = jnp.uint32
I32 = jnp.int32

PACK_W = D_MODEL // 2
HI_MASK = 0xFFFF0000


def _sigmoid(z):
    return 1.0 / (1.0 + jnp.exp(-z))


def _pack_bf16_pair(lo, hi):
    lo_bits = lax.bitcast_convert_type(lo.astype(BF16).astype(F32), U32) >> 16
    hi_bits = lax.bitcast_convert_type(hi.astype(BF16).astype(F32), U32) & jnp.uint32(HI_MASK)
    return hi_bits | lo_bits


def _unpack_bf16_pair(w):
    lo = lax.bitcast_convert_type(w << 16, F32).astype(BF16)
    hi = lax.bitcast_convert_type(w & jnp.uint32(HI_MASK), F32).astype(BF16)
    return lo, hi


def _params(n_axes, vmem=VMEM_LIMIT):
    return pltpu.CompilerParams(
        dimension_semantics=("arbitrary",) * n_axes, vmem_limit_bytes=vmem)


IN_TM = 512
IN_TN = 1536


def _inproj_kernel(x_ref, g_ref, w_ref, bg_ref, o_ref, wbf_ref):
    j = pl.program_id(0)
    i = pl.program_id(1)

    @pl.when(i == 0)
    def _():
        wbf_ref[...] = w_ref[...].astype(BF16)

    x = x_ref[...]
    ms = jnp.mean(x * x, axis=-1, keepdims=True)
    h = ((x * lax.rsqrt(ms + NORM_EPS)) * g_ref[...]).astype(BF16)
    acc = jnp.dot(h, wbf_ref[...], preferred_element_type=F32)

    first_gate_tile = GATE_COL0 // IN_TN
    n_sub = IN_TN // CB

    def store(val):
        for c in range(n_sub):
            o_ref[c] = val[:, c * CB:(c + 1) * CB].astype(BF16)

    @pl.when(j < first_gate_tile)
    def _():
        store(acc)

    @pl.when(j == first_gate_tile)
    def _():
        col = j * IN_TN + lax.broadcasted_iota(I32, (1, IN_TN), 1)
        store(jnp.where(col >= GATE_COL0, _sigmoid(acc + bg_ref[...]), acc))

    @pl.when(j > first_gate_tile)
    def _():
        store(_sigmoid(acc + bg_ref[...]))


def _in_projection(x2, norm_g, w_in, b_gate_ext):
    s = x2.shape[0]
    grid = (D_IN // IN_TN, s // IN_TM)
    return pl.pallas_call(
        _inproj_kernel,
        grid=grid,
        in_specs=[
            pl.BlockSpec((IN_TM, D_MODEL), lambda j, i: (i, 0)),
            pl.BlockSpec((1, D_MODEL), lambda j, i: (0, 0)),
            pl.BlockSpec((D_MODEL, IN_TN), lambda j, i: (0, j)),
            pl.BlockSpec((1, IN_TN), lambda j, i: (0, j)),
        ],
        out_specs=pl.BlockSpec((IN_TN // CB, IN_TM, CB), lambda j, i: (j, i, 0)),
        out_shape=jax.ShapeDtypeStruct((N_CB, s, CB), BF16),
        scratch_shapes=[pltpu.VMEM((D_MODEL, IN_TN), BF16)],
        compiler_params=_params(2),
        name="in_projection",
    )(x2, norm_g, w_in, b_gate_ext)


CONV_TS = 256
CONV_HALO = 32
CONV_RC = 32
CONV_N = CONV_TS + CONV_HALO


def _conv_kernel(ac_ref, gc_ref, ah_ref, gh_ref, w_ref, cb_ref, lg_ref, lb_ref, o_ref, r_ref):
    i = pl.program_id(0)
    has_prev = i > 0
    half = D_CONV // 2
    for c in range(2):
        cs = slice(c * half, (c + 1) * half)
        uh = ah_ref[c].astype(F32) * _sigmoid(gh_ref[c].astype(F32))
        r_ref[0, 0:CONV_HALO, cs] = jnp.where(has_prev, uh, 0.0)
        r_ref[0, CONV_HALO:CONV_N, cs] = ac_ref[c].astype(F32) * _sigmoid(gc_ref[c].astype(F32))
    u_ext = r_ref[0]
    for b in range(1, 8):
        r_ref[b] = pltpu.roll(u_ext, CONV_N - b, axis=0)

    def chunk(ci, carry):
        r0 = pl.multiple_of(ci * CONV_RC, CONV_RC)
        acc = jnp.broadcast_to(cb_ref[...], (CONV_RC, D_CONV))
        for k in range(CONV_WIDTH):
            kp = k + (CONV_HALO - (CONV_WIDTH - 1))
            a, b = kp // 8, kp % 8
            acc = acc + w_ref[k:k + 1, :] * r_ref[b, pl.ds(r0 + 8 * a, CONV_RC), :]
        mu = jnp.mean(acc, axis=-1, keepdims=True)
        d = acc - mu
        var = jnp.mean(d * d, axis=-1, keepdims=True)
        y = (d * lax.rsqrt(var + NORM_EPS)) * lg_ref[...] + lb_ref[...]
        o_ref[pl.ds(r0, CONV_RC), :] = (y * _sigmoid(y)).astype(BF16)
        return carry

    lax.fori_loop(0, CONV_TS // CONV_RC, chunk, 0)


def _conv_branch(proj3, conv_w, conv_b, ln_g, ln_b):
    s = proj3.shape[1]
    hb = CONV_TS // CONV_HALO
    cur = lambda blk: pl.BlockSpec((2, CONV_TS, CB), lambda i: (blk // 2, i, 0))
    halo = lambda blk: pl.BlockSpec(
        (2, CONV_HALO, CB), lambda i: (blk // 2, jnp.maximum(i * hb - 1, 0), 0))
    vec = pl.BlockSpec((1, D_CONV), lambda i: (0, 0))
    return pl.pallas_call(
        _conv_kernel,
        grid=(s // CONV_TS,),
        in_specs=[cur(CB_CONV_A), cur(CB_CONV_G), halo(CB_CONV_A), halo(CB_CONV_G),
                  pl.BlockSpec((CONV_WIDTH, D_CONV), lambda i: (0, 0)), vec, vec, vec],
        out_specs=pl.BlockSpec((CONV_TS, D_CONV), lambda i: (i, 0)),
        out_shape=jax.ShapeDtypeStruct((s, D_CONV), BF16),
        scratch_shapes=[pltpu.VMEM((8, CONV_N, D_CONV), F32)],
        compiler_params=_params(1),
        name="conv_branch",
    )(proj3, proj3, proj3, proj3, conv_w, conv_b, ln_g, ln_b)


def _attn_kernel(q_ref, kc_ref, kp_ref, vc_ref, vp_ref, o_ref, lse_ref, *, tq):
    n = pl.program_id(0)
    blk = ATTN_BLOCK
    scale = HEAD_DIM ** -0.5
    qi = lax.broadcasted_iota(I32, (blk, 2 * blk), 0)
    ki = lax.broadcasted_iota(I32, (blk, 2 * blk), 1)
    band = (ki >= qi) & (ki <= qi + blk)
    lane = lax.broadcasted_iota(I32, (blk, HEADS_PER_GROUP), 1)
    for j in range(tq // blk):
        mask = band
        if j == 0:
            mask = band & ((ki >= blk) | (n > 0))
        rows = slice(j * blk, (j + 1) * blk)
        lse_tile = jnp.zeros((blk, HEADS_PER_GROUP), F32)
        for hh in range(HEADS_PER_GROUP):
            cs = slice(hh * HEAD_DIM, (hh + 1) * HEAD_DIM)
            q = q_ref[rows, cs]
            if j == 0:
                k = jnp.concatenate([kp_ref[:, cs], kc_ref[0:blk, cs]], axis=0)
                v = jnp.concatenate([vp_ref[:, cs], vc_ref[0:blk, cs]], axis=0)
            else:
                k = kc_ref[(j - 1) * blk:(j + 1) * blk, cs]
                v = vc_ref[(j - 1) * blk:(j + 1) * blk, cs]
            s = lax.dot_general(q, k, (((1,), (1,)), ((), ())), preferred_element_type=F32) * scale
            s = jnp.where(mask, s, -jnp.inf)
            m = jnp.max(s, axis=-1, keepdims=True)
            p = jnp.exp(s - m)
            l = jnp.sum(p, axis=-1, keepdims=True)
            o = jnp.dot(p.astype(BF16), v, preferred_element_type=F32) / l
            o_ref[rows, cs] = o.astype(BF16)
            lse_tile = jnp.where(lane == hh, m + jnp.log(l), lse_tile)
        lse_ref[0, rows, :] = lse_tile


def _dilated_attention(proj3, group):
    _, dil = ATTN_PATTERNS[group]
    s = proj3.shape[1]
    m_len = s // dil
    tq = min(512, m_len)
    if dil == 1:
        view, cbs = proj3, (CB_Q + group, CB_K + group, CB_V + group)
    else:
        qkv = jnp.stack([proj3[CB_Q + group], proj3[CB_K + group], proj3[CB_V + group]])
        view, cbs = qkv.reshape(3, m_len, dil * CB), (0, 1, 2)
    tb = tq // ATTN_BLOCK
    cur = lambda cb: pl.BlockSpec((None, tq, CB), lambda n, r: (cb, n, r))
    prev = lambda cb: pl.BlockSpec(
        (None, ATTN_BLOCK, CB), lambda n, r: (cb, jnp.maximum(n * tb - 1, 0), r))
    o, lse = pl.pallas_call(
        functools.partial(_attn_kernel, tq=tq),
        grid=(m_len // tq, dil),
        in_specs=[cur(cbs[0]), cur(cbs[1]), prev(cbs[1]), cur(cbs[2]), prev(cbs[2])],
        out_specs=[pl.BlockSpec((tq, CB), lambda n, r: (n, r)),
                   pl.BlockSpec((1, tq, HEADS_PER_GROUP), lambda n, r: (r, n, 0))],
        out_shape=[jax.ShapeDtypeStruct((m_len, dil * CB), BF16),
                   jax.ShapeDtypeStruct((dil, m_len, HEADS_PER_GROUP), F32)],
        compiler_params=_params(2),
        name=f"dilated_attention_g{group}",
    )(view, view, view, view, view)
    o = o.reshape(s, CB)
    lse = jnp.transpose(lse, (1, 0, 2)).reshape(s, HEADS_PER_GROUP)
    return o, lse


MIX_TM = 256


def _mixer_out_kernel(x_ref, o0_ref, o1_ref, o2_ref, l0_ref, l1_ref, l2_ref, u_ref,
                      gc0, gc1, gc2, gc3, ga0, ga1, ga2, ga3,
                      wa_ref, wc_ref, bc_ref, wo_ref, g2_ref, wrh_ref, wrl_ref, br_ref,
                      x1_ref, hp_ref, idx_ref, gate_ref, pos_ref, cnt_ref, carry_ref):
    i = pl.program_id(0)
    tm = MIX_TM

    @pl.when(i == 0)
    def _():
        carry_ref[...] = jnp.zeros_like(carry_ref)

    l0, l1, l2 = l0_ref[...], l1_ref[...], l2_ref[...]
    m = jnp.maximum(jnp.maximum(l0, l1), l2)
    e0, e1, e2 = jnp.exp(l0 - m), jnp.exp(l1 - m), jnp.exp(l2 - m)
    den = e0 + e1 + e2
    mix = (e0 / den, e1 / den, e2 / den)
    o_refs = (o0_ref, o1_ref, o2_ref)
    parts = []
    for hh in range(HEADS_PER_GROUP):
        cs = slice(hh * HEAD_DIM, (hh + 1) * HEAD_DIM)
        acc = mix[0][:, hh:hh + 1] * o_refs[0][:, cs].astype(F32)
        for g in range(1, N_GROUPS):
            acc = acc + mix[g][:, hh:hh + 1] * o_refs[g][:, cs].astype(F32)
        parts.append(acc)
    o = jnp.concatenate(parts, axis=1).astype(BF16)
    ya = jnp.dot(o, wa_ref[...], preferred_element_type=F32)
    yc = jnp.dot(u_ref[...], wc_ref[...], preferred_element_type=F32) + bc_ref[...]
    gcs = (gc0, gc1, gc2, gc3)
    gas = (ga0, ga1, ga2, ga3)
    merged = []
    for c in range(D_MODEL // CB):
        cs = slice(c * CB, (c + 1) * CB)
        merged.append((gcs[c][...].astype(F32) * yc[:, cs]
                       + gas[c][...].astype(F32) * ya[:, cs]).astype(BF16))
    merged = jnp.concatenate(merged, axis=1)
    x1 = x_ref[...] + jnp.dot(merged, wo_ref[...], preferred_element_type=F32)
    x1_ref[...] = x1
    ms = jnp.mean(x1 * x1, axis=-1, keepdims=True)
    h2 = (x1 * lax.rsqrt(ms + NORM_EPS)) * g2_ref[...]
    h2_hi = h2.astype(BF16)
    hp_ref[...] = _pack_bf16_pair(h2[:, 0:PACK_W], h2[:, PACK_W:D_MODEL])
    h2_lo = (h2 - h2_hi.astype(F32)).astype(BF16)
    logits = (jnp.dot(h2_hi, wrh_ref[...], preferred_element_type=F32)
              + jnp.dot(h2_hi, wrl_ref[...], preferred_element_type=F32)
              + jnp.dot(h2_lo, wrh_ref[...], preferred_element_type=F32)) + br_ref[...]

    e_iota = lax.broadcasted_iota(I32, (tm, N_EXPERTS), 1).astype(F32)
    k_lane = lax.broadcasted_iota(I32, (tm, TOP_K), 1)
    vals = logits
    sels = []
    idx_t = jnp.zeros((tm, TOP_K), F32)
    val_t = jnp.zeros((tm, TOP_K), F32)
    for k in range(TOP_K):
        mk = jnp.max(vals, axis=-1, keepdims=True)
        ik = jnp.min(jnp.where(vals == mk, e_iota, float(N_EXPERTS)), axis=-1, keepdims=True)
        sel = e_iota == ik
        sels.append(sel)
        vals = jnp.where(sel, -jnp.inf, vals)
        idx_t = jnp.where(k_lane == k, ik, idx_t)
        val_t = jnp.where(k_lane == k, mk, val_t)
    ex = jnp.exp(val_t - val_t[:, 0:1])
    gate_ref[...] = ex / jnp.sum(ex, axis=-1, keepdims=True)
    idx_ref[...] = idx_t.astype(I32)

    cnt = jnp.zeros((tm, N_EXPERTS), F32)
    for sel in sels:
        cnt = cnt + jnp.where(sel, 1.0, 0.0)
    ri = lax.broadcasted_iota(I32, (tm, tm), 0)
    ci = lax.broadcasted_iota(I32, (tm, tm), 1)
    lower = jnp.where(ci < ri, 1.0, 0.0).astype(BF16)
    prefix = jnp.dot(lower, cnt.astype(BF16), preferred_element_type=F32) + carry_ref[...]
    pos_t = jnp.zeros((tm, TOP_K), F32)
    for k, sel in enumerate(sels):
        pk = jnp.sum(jnp.where(sel, prefix, 0.0), axis=-1, keepdims=True)
        pos_t = jnp.where(k_lane == k, pk, pos_t)
    pos_ref[...] = pos_t.astype(I32)
    carry_ref[...] = carry_ref[...] + jnp.sum(cnt, axis=0, keepdims=True)
    cnt_ref[...] = carry_ref[...]


def _mixer_output(x2, outs, lses, u_ln, proj3, wa, wc, bc, wo, g2, wr_hi, wr_lo, br):
    s = x2.shape[0]
    row = lambda w: pl.BlockSpec((MIX_TM, w), lambda i: (i, 0))
    gate = lambda cb: pl.BlockSpec((None, MIX_TM, CB), lambda i: (cb, i, 0))
    full = lambda a: pl.BlockSpec(a.shape, lambda i: (0,) * a.ndim)
    in_specs = ([row(D_MODEL)] + [row(CB)] * 3 + [row(HEADS_PER_GROUP)] * 3 + [row(D_CONV)]
                + [gate(CB_GATE_C + c) for c in range(4)]
                + [gate(CB_GATE_A + c) for c in range(4)]
                + [full(a) for a in (wa, wc, bc, wo, g2, wr_hi, wr_lo, br)])
    return pl.pallas_call(
        _mixer_out_kernel,
        grid=(s // MIX_TM,),
        in_specs=in_specs,
        out_specs=[row(D_MODEL), row(PACK_W), row(TOP_K), row(TOP_K), row(TOP_K),
                   pl.BlockSpec((1, N_EXPERTS), lambda i: (0, 0))],
        out_shape=[jax.ShapeDtypeStruct((s, D_MODEL), F32),
                   jax.ShapeDtypeStruct((s, PACK_W), U32),
                   jax.ShapeDtypeStruct((s, TOP_K), I32),
                   jax.ShapeDtypeStruct((s, TOP_K), F32),
                   jax.ShapeDtypeStruct((s, TOP_K), I32),
                   jax.ShapeDtypeStruct((1, N_EXPERTS), F32)],
        scratch_shapes=[pltpu.VMEM((1, N_EXPERTS), F32)],
        compiler_params=_params(1),
        name="mixer_output",
    )(x2, *outs, *lses, u_ln, *([proj3] * 8), wa, wc, bc, wo, g2, wr_hi, wr_lo, br)


ROW_BLK = 256
DISP_TM = 256


def _dispatch_kernel(fill_ref, dest_ref, h_ref, xs_ref, zbuf, sem):
    i = pl.program_id(0)
    n_fill = fill_ref.shape[0]

    @pl.when(i == 0)
    def _():
        zbuf[...] = jnp.zeros_like(zbuf)

        def fill_copy(j):
            return pltpu.make_async_copy(
                zbuf, xs_ref.at[pl.ds(fill_ref[j] * ROW_BLK, ROW_BLK)], sem.at[0])

        def start(j, c):
            @pl.when(fill_ref[j] >= 0)
            def _():
                fill_copy(j).start()
            return c

        def wait(j, c):
            @pl.when(fill_ref[j] >= 0)
            def _():
                fill_copy(j).wait()
            return c

        lax.fori_loop(0, n_fill, start, 0)
        lax.fori_loop(0, n_fill, wait, 0)

    def row(t, c):
        for k in range(TOP_K):
            d = dest_ref[0, t * TOP_K + k]
            pltpu.make_async_copy(h_ref.at[pl.ds(t, 1)], xs_ref.at[pl.ds(d, 1)], sem.at[0]).start()
        return c

    lax.fori_loop(0, DISP_TM, row, 0)
    for _ in range(TOP_K):
        pltpu.make_async_copy(h_ref, xs_ref.at[pl.ds(0, DISP_TM)], sem.at[0]).wait()


def _dispatch(hp, dest, fill_blocks, n_rows):
    s = hp.shape[0]
    dest3 = dest.reshape(s // DISP_TM, 1, DISP_TM * TOP_K)
    grid_spec = pltpu.PrefetchScalarGridSpec(
        num_scalar_prefetch=1,
        grid=(s // DISP_TM,),
        in_specs=[
            pl.BlockSpec((None, 1, DISP_TM * TOP_K), lambda i, fr: (i, 0, 0),
                         memory_space=pltpu.SMEM),
            pl.BlockSpec((DISP_TM, PACK_W), lambda i, fr: (i, 0)),
        ],
        out_specs=pl.BlockSpec(memory_space=pl.ANY),
        scratch_shapes=[pltpu.VMEM((ROW_BLK, PACK_W), U32), pltpu.SemaphoreType.DMA((1,))],
    )
    return pl.pallas_call(
        _dispatch_kernel,
        grid_spec=grid_spec,
        out_shape=jax.ShapeDtypeStruct((n_rows, PACK_W), U32),
        compiler_params=_params(1),
        name="moe_dispatch",
    )(fill_blocks, dest3, hp)


UNIT_BLOCKS = 6
UNIT_ROWS = ROW_BLK * UNIT_BLOCKS
EXP_TF = 256
EXP_NF = D_EXPERT // EXP_TF


def _expert_kernel(ue_ref, ub0_ref, unb_ref, nu_ref,
                   xs_ref, wg_ref, wu_ref, bg_ref, bu_ref, wo_ref, bo_ref,
                   ys_ref,
                   xbuf, xb16, acc, wgu_bf, wo_bf, sem_x, sem_y):
    u = pl.program_id(0)
    f = pl.program_id(1)
    n_units = nu_ref[0]

    def x_copy(unit, slot, b):
        src = xs_ref.at[pl.ds((ub0_ref[unit] + b) * ROW_BLK, ROW_BLK)]
        dst = xbuf.at[slot, pl.ds(b * ROW_BLK, ROW_BLK)]
        return pltpu.make_async_copy(src, dst, sem_x.at[slot])

    def y_copy(unit, b):
        src = acc.at[pl.ds(b * ROW_BLK, ROW_BLK)]
        dst = ys_ref.at[pl.ds((ub0_ref[unit] + b) * ROW_BLK, ROW_BLK)]
        return pltpu.make_async_copy(src, dst, sem_y.at[0])

    def start_x(unit, slot):
        lax.fori_loop(0, unb_ref[unit], lambda b, c: (x_copy(unit, slot, b).start(), c)[1], 0)

    def wait_x(unit, slot):
        lax.fori_loop(0, unb_ref[unit], lambda b, c: (x_copy(unit, slot, b).wait(), c)[1], 0)

    @pl.when(u < n_units)
    def _():
        slot = lax.rem(u, 2)
        nblk = unb_ref[u]

        @pl.when(f == 0)
        def _():
            @pl.when(u == 0)
            def _():
                start_x(u, slot)

            wait_x(u, slot)

            @pl.when(u + 1 < n_units)
            def _():
                start_x(u + 1, 1 - slot)

            def init(b, c):
                r0 = pl.multiple_of(b * ROW_BLK, ROW_BLK)
                lo, hi = _unpack_bf16_pair(xbuf[slot, pl.ds(r0, ROW_BLK), :])
                xb16[pl.ds(r0, ROW_BLK), 0:PACK_W] = lo
                xb16[pl.ds(r0, ROW_BLK), PACK_W:D_MODEL] = hi
                acc[pl.ds(r0, ROW_BLK), :] = jnp.broadcast_to(bo_ref[...], (ROW_BLK, D_MODEL))
                return c

            lax.fori_loop(0, nblk, init, 0)

        wgu_bf[:, 0:EXP_TF] = wg_ref[...].astype(BF16)
        wgu_bf[:, EXP_TF:2 * EXP_TF] = wu_ref[...].astype(BF16)
        wo_bf[...] = wo_ref[...].astype(BF16)

        def sub(b, c):
            r0 = pl.multiple_of(b * ROW_BLK, ROW_BLK)
            x = xb16[pl.ds(r0, ROW_BLK), :]
            gu = jnp.dot(x, wgu_bf[...], preferred_element_type=F32)
            g = jnp.minimum(gu[:, 0:EXP_TF] + bg_ref[...], SWIGLU_LIMIT)
            up = jnp.clip(gu[:, EXP_TF:2 * EXP_TF] + bu_ref[...], -SWIGLU_LIMIT, SWIGLU_LIMIT)
            act = (up + 1.0) * (g * _sigmoid(SWIGLU_ALPHA * g))
            acc[pl.ds(r0, ROW_BLK), :] += jnp.dot(
                act.astype(BF16), wo_bf[...], preferred_element_type=F32)

            @pl.when(f == EXP_NF - 1)
            def _():
                y_copy(u, b).start()

            return c

        lax.fori_loop(0, nblk, sub, 0)

        @pl.when(f == EXP_NF - 1)
        def _():
            lax.fori_loop(0, nblk, lambda b, c: (y_copy(u, b).wait(), c)[1], 0)

    @pl.when((u == pl.num_programs(0) - 1) & (f == EXP_NF - 1))
    def _():
        n_blocks = ys_ref.shape[0] // ROW_BLK
        acc[0:ROW_BLK, :] = jnp.zeros((ROW_BLK, D_MODEL), F32)

        def tail_copy(b):
            return pltpu.make_async_copy(acc.at[pl.ds(0, ROW_BLK)],
                                         ys_ref.at[pl.ds(b * ROW_BLK, ROW_BLK)], sem_y.at[0])

        lax.fori_loop(nu_ref[1], n_blocks, lambda b, c: (tail_copy(b).start(), c)[1], 0)
        lax.fori_loop(nu_ref[1], n_blocks, lambda b, c: (tail_copy(b).wait(), c)[1], 0)


def _expert_ffn(xs, unit_e, unit_blk0, unit_nblk, unit_counts, w_in, b_in, w_out, b_out):
    n_rows = xs.shape[0]
    n_units_max = unit_e.shape[0]

    def f_eff(u, f, nu):
        return jnp.where(u < nu[0], f, EXP_NF - 1)

    def e_eff(u, ue, nu):
        return ue[jnp.minimum(u, jnp.maximum(nu[0] - 1, 0))]

    in_specs = [
        pl.BlockSpec(memory_space=pl.ANY),
        pl.BlockSpec((None, D_MODEL, EXP_TF),
                     lambda u, f, ue, ub, un, nu: (e_eff(u, ue, nu), 0, f_eff(u, f, nu))),
        pl.BlockSpec((None, D_MODEL, EXP_TF),
                     lambda u, f, ue, ub, un, nu: (e_eff(u, ue, nu), 0, EXP_NF + f_eff(u, f, nu))),
        pl.BlockSpec((None, 1, EXP_TF),
                     lambda u, f, ue, ub, un, nu: (e_eff(u, ue, nu), 0, f_eff(u, f, nu))),
        pl.BlockSpec((None, 1, EXP_TF),
                     lambda u, f, ue, ub, un, nu: (e_eff(u, ue, nu), 0, EXP_NF + f_eff(u, f, nu))),
        pl.BlockSpec((None, EXP_TF, D_MODEL),
                     lambda u, f, ue, ub, un, nu: (e_eff(u, ue, nu), f_eff(u, f, nu), 0)),
        pl.BlockSpec((None, 1, D_MODEL),
                     lambda u, f, ue, ub, un, nu: (e_eff(u, ue, nu), 0, 0)),
    ]
    grid_spec = pltpu.PrefetchScalarGridSpec(
        num_scalar_prefetch=4,
        grid=(n_units_max, EXP_NF),
        in_specs=in_specs,
        out_specs=pl.BlockSpec(memory_space=pl.ANY),
        scratch_shapes=[
            pltpu.VMEM((2, UNIT_ROWS, PACK_W), U32),
            pltpu.VMEM((UNIT_ROWS, D_MODEL), BF16),
            pltpu.VMEM((UNIT_ROWS, D_MODEL), F32),
            pltpu.VMEM((D_MODEL, 2 * EXP_TF), BF16),
            pltpu.VMEM((EXP_TF, D_MODEL), BF16),
            pltpu.SemaphoreType.DMA((2,)),
            pltpu.SemaphoreType.DMA((1,)),
        ],
    )
    return pl.pallas_call(
        _expert_kernel,
        grid_spec=grid_spec,
        out_shape=jax.ShapeDtypeStruct((n_rows, D_MODEL), F32),
        compiler_params=_params(2),
        name="expert_ffn",
    )(unit_e, unit_blk0, unit_nblk, unit_counts, xs,
      w_in, w_in, b_in[:, None, :], b_in[:, None, :], w_out, b_out[:, None, :])


COMB_TM = 256


def _combine_kernel(dcur_ref, dnxt_ref, x1_ref, gate_ref, g_ref, ys_ref, o_ref, ybuf, sem):
    i = pl.program_id(0)
    n = pl.num_programs(0)
    slot = lax.rem(i, 2)

    def issue(dref, s):
        def row(t, c):
            for k in range(TOP_K):
                d = dref[0, t * TOP_K + k]
                pltpu.make_async_copy(ys_ref.at[pl.ds(d, 1)], ybuf.at[s, k, pl.ds(t, 1)],
                                      sem.at[s]).start()
            return c

        lax.fori_loop(0, COMB_TM, row, 0)

    @pl.when(i == 0)
    def _():
        issue(dcur_ref, slot)

    @pl.when(i + 1 < n)
    def _():
        issue(dnxt_ref, 1 - slot)

    for k in range(TOP_K):
        pltpu.make_async_copy(ys_ref.at[pl.ds(0, COMB_TM)], ybuf.at[slot, k], sem.at[slot]).wait()

    gate = gate_ref[...]
    y = x1_ref[...]
    for k in range(TOP_K):
        y = y + gate[:, k:k + 1] * ybuf[slot, k]
    ms = jnp.mean(y * y, axis=-1, keepdims=True)
    o_ref[...] = (y * lax.rsqrt(ms + NORM_EPS)) * g_ref[...]


def _combine(dest, x1, gate, norm_g, ys):
    s = x1.shape[0]
    nt = s // COMB_TM
    dest3 = dest.reshape(nt, 1, COMB_TM * TOP_K)
    smem = lambda imap: pl.BlockSpec((None, 1, COMB_TM * TOP_K), imap, memory_space=pltpu.SMEM)
    row = lambda w: pl.BlockSpec((COMB_TM, w), lambda i: (i, 0))
    return pl.pallas_call(
        _combine_kernel,
        grid=(nt,),
        in_specs=[smem(lambda i: (i, 0, 0)),
                  smem(lambda i: (jnp.minimum(i + 1, nt - 1), 0, 0)),
                  row(D_MODEL), row(TOP_K), pl.BlockSpec((1, D_MODEL), lambda i: (0, 0)),
                  pl.BlockSpec(memory_space=pl.ANY)],
        out_specs=row(D_MODEL),
        out_shape=jax.ShapeDtypeStruct((s, D_MODEL), F32),
        scratch_shapes=[pltpu.VMEM((2, TOP_K, COMB_TM, D_MODEL), F32),
                        pltpu.SemaphoreType.DMA((2,))],
        compiler_params=_params(1),
        name="moe_combine",
    )(dest3, dest3, x1, gate, norm_g, ys)


def _route_tables(idx, pos, counts, n_tok):
    n_assign = n_tok * TOP_K
    counts = counts.reshape(N_EXPERTS).astype(I32)
    nblk_e = (counts + ROW_BLK - 1) // ROW_BLK
    pad_start = (jnp.cumsum(nblk_e) - nblk_e) * ROW_BLK
    onehot = idx[..., None] == jnp.arange(N_EXPERTS, dtype=I32)
    dest = pos + jnp.sum(jnp.where(onehot, pad_start, 0), axis=-1)

    n_blocks = -(-(n_assign + N_EXPERTS * (ROW_BLK - 1)) // ROW_BLK)
    used = jnp.sum(nblk_e)
    blk_ids = jnp.arange(n_blocks, dtype=I32)
    last_blk = pad_start // ROW_BLK + nblk_e - 1
    fill_blocks = jnp.concatenate([jnp.where(nblk_e > 0, last_blk, -1),
                                   jnp.where(blk_ids >= used, blk_ids, -1)]).astype(I32)

    n_units_max = (n_blocks + (UNIT_BLOCKS - 1) * N_EXPERTS) // UNIT_BLOCKS
    chunks_e = (nblk_e + UNIT_BLOCKS - 1) // UNIT_BLOCKS
    chunk_end = jnp.cumsum(chunks_e)
    chunk_start = chunk_end - chunks_e
    n_units = chunk_end[-1]
    uid = jnp.arange(n_units_max, dtype=I32)
    unit_e = jnp.minimum(jnp.searchsorted(chunk_end, uid, side="right"), N_EXPERTS - 1).astype(I32)
    c_in_e = uid - chunk_start[unit_e]
    unit_blk0 = (pad_start[unit_e] // ROW_BLK + c_in_e * UNIT_BLOCKS).astype(I32)
    unit_nblk = jnp.clip(nblk_e[unit_e] - c_in_e * UNIT_BLOCKS, 0, UNIT_BLOCKS).astype(I32)
    active = uid < n_units
    unit_blk0 = jnp.where(active, unit_blk0, 0)
    unit_nblk = jnp.where(active, unit_nblk, 0)
    unit_counts = jnp.stack([n_units, used]).astype(I32)
    return dest.astype(I32), fill_blocks, n_blocks * ROW_BLK, unit_e, unit_blk0, unit_nblk, unit_counts


def kernel(x, norm1_g, w_in, b_gate, conv_w, conv_b, conv_ln_g, conv_ln_b, w_conv_out, b_conv_out,
           w_attn_out, w_out, norm2_g, w_router, b_router, w_exp_in, b_exp_in, w_exp_out, b_exp_out,
           norm_f_g):
    b_sz, s_len, d = x.shape
    n_tok = b_sz * s_len
    x2 = x.reshape(n_tok, d)
    l = 0
    b_gate_ext = jnp.concatenate([jnp.zeros((GATE_COL0,), F32), b_gate[l]])[None, :]
    proj3 = _in_projection(x2, norm1_g[l][None, :], w_in[l], b_gate_ext)
    u_ln = _conv_branch(proj3, conv_w[l], conv_b[l][None, :], conv_ln_g[l][None, :],
                        conv_ln_b[l][None, :])
    outs, lses = [], []
    for g in range(N_GROUPS):
        o_g, lse_g = _dilated_attention(proj3, g)
        outs.append(o_g)
        lses.append(lse_g)
    wr = w_router[l]
    wr_hi = wr.astype(BF16)
    wr_lo = (wr - wr_hi.astype(F32)).astype(BF16)
    x1, hp, idx, gate, pos, counts = _mixer_output(
        x2, outs, lses, u_ln, proj3,
        w_attn_out[l].astype(BF16), w_conv_out[l].astype(BF16), b_conv_out[l][None, :],
        w_out[l].astype(BF16), norm2_g[l][None, :], wr_hi, wr_lo, b_router[l][None, :])

    dest, fill_blocks, n_rows, unit_e, unit_blk0, unit_nblk, unit_counts = _route_tables(
        idx, pos, counts, n_tok)
    xs = _dispatch(hp, dest, fill_blocks, n_rows)
    ys = _expert_ffn(xs, unit_e, unit_blk0, unit_nblk, unit_counts,
                     w_exp_in[l], b_exp_in[l], w_exp_out[l], b_exp_out[l])
    out = _combine(dest, x1, gate, norm_f_g[None, :], ys)
    return out.reshape(b_sz, s_len, d)
```

```python
import functools

import jax
import jax.numpy as jnp
from jax import lax
from jax.experimental import pallas as pl
from jax.experimental.pallas import tpu as pltpu

D_MODEL = 2048
SEQ = 8192
D_CONV = D_MODEL // 2
CONV_WIDTH = 31
HEAD_DIM = 128
HEADS_PER_GROUP = 4
ATTN_PATTERNS = ((128, 1), (512, 4), (2048, 16))
N_GROUPS = len(ATTN_PATTERNS)
ATTN_WIDTH = N_GROUPS * HEADS_PER_GROUP * HEAD_DIM
ATTN_OUT_WIDTH = HEADS_PER_GROUP * HEAD_DIM
ATTN_BLOCK = 128
D_IN = 2 * D_CONV + 3 * ATTN_WIDTH + 2 * D_MODEL
N_EXPERTS = 32
TOP_K = 4
D_EXPERT = D_MODEL
SWIGLU_LIMIT = 7.0
SWIGLU_ALPHA = 1.702
NORM_EPS = 1e-5

CB = 512
N_CB = D_IN // CB
CB_CONV_A, CB_CONV_G = 0, 2
CB_Q, CB_K, CB_V = 4, 7, 10
CB_GATE_C, CB_GATE_A = 13, 17
GATE_COL0 = CB_GATE_C * CB

VMEM_LIMIT = 56 * 1024 * 1024

F32 = jnp.float32
BF16 = jnp.bfloat16
U32 = jnp.uint32
I32 = jnp.int32

PACK_W = D_MODEL // 2
HI_MASK = 0xFFFF0000


def _sigmoid(z):
    return 1.0 / (1.0 + jnp.exp(-z))


def _pack_bf16_pair(lo, hi):
    lo_bits = lax.bitcast_convert_type(lo.astype(BF16).astype(F32), U32) >> 16
    hi_bits = lax.bitcast_convert_type(hi.astype(BF16).astype(F32), U32) & jnp.uint32(HI_MASK)
    return hi_bits | lo_bits


def _unpack_bf16_pair(w):
    lo = lax.bitcast_convert_type(w << 16, F32).astype(BF16)
    hi = lax.bitcast_convert_type(w & jnp.uint32(HI_MASK), F32).astype(BF16)
    return lo, hi


def _params(n_axes, vmem=VMEM_LIMIT):
    return pltpu.CompilerParams(
        dimension_semantics=("arbitrary",) * n_axes, vmem_limit_bytes=vmem)


IN_TM = 512
IN_TN = 1536


def _inproj_kernel(x_ref, g_ref, w_ref, bg_ref, o_ref, wbf_ref):
    j = pl.program_id(0)
    i = pl.program_id(1)

    @pl.when(i == 0)
    def _():
        wbf_ref[...] = w_ref[...].astype(BF16)

    x = x_ref[...]
    ms = jnp.mean(x * x, axis=-1, keepdims=True)
    h = ((x * lax.rsqrt(ms + NORM_EPS)) * g_ref[...]).astype(BF16)
    acc = jnp.dot(h, wbf_ref[...], preferred_element_type=F32)

    first_gate_tile = GATE_COL0 // IN_TN
    n_sub = IN_TN // CB

    def store(val):
        for c in range(n_sub):
            o_ref[c] = val[:, c * CB:(c + 1) * CB].astype(BF16)

    @pl.when(j < first_gate_tile)
    def _():
        store(acc)

    @pl.when(j == first_gate_tile)
    def _():
        col = j * IN_TN + lax.broadcasted_iota(I32, (1, IN_TN), 1)
        store(jnp.where(col >= GATE_COL0, _sigmoid(acc + bg_ref[...]), acc))

    @pl.when(j > first_gate_tile)
    def _():
        store(_sigmoid(acc + bg_ref[...]))


def _in_projection(x2, norm_g, w_in, b_gate_ext):
    s = x2.shape[0]
    grid = (D_IN // IN_TN, s // IN_TM)
    return pl.pallas_call(
        _inproj_kernel,
        grid=grid,
        in_specs=[
            pl.BlockSpec((IN_TM, D_MODEL), lambda j, i: (i, 0)),
            pl.BlockSpec((1, D_MODEL), lambda j, i: (0, 0)),
            pl.BlockSpec((D_MODEL, IN_TN), lambda j, i: (0, j)),
            pl.BlockSpec((1, IN_TN), lambda j, i: (0, j)),
        ],
        out_specs=pl.BlockSpec((IN_TN // CB, IN_TM, CB), lambda j, i: (j, i, 0)),
        out_shape=jax.ShapeDtypeStruct((N_CB, s, CB), BF16),
        scratch_shapes=[pltpu.VMEM((D_MODEL, IN_TN), BF16)],
        compiler_params=_params(2),
        name="in_projection",
    )(x2, norm_g, w_in, b_gate_ext)


CONV_TS = 256
CONV_HALO = 32
CONV_RC = 32
CONV_N = CONV_TS + CONV_HALO


def _conv_kernel(ac_ref, gc_ref, ah_ref, gh_ref, w_ref, cb_ref, lg_ref, lb_ref, o_ref, r_ref):
    i = pl.program_id(0)
    has_prev = i > 0
    half = D_CONV // 2
    for c in range(2):
        cs = slice(c * half, (c + 1) * half)
        uh = ah_ref[c].astype(F32) * _sigmoid(gh_ref[c].astype(F32))
        r_ref[0, 0:CONV_HALO, cs] = jnp.where(has_prev, uh, 0.0)
        r_ref[0, CONV_HALO:CONV_N, cs] = ac_ref[c].astype(F32) * _sigmoid(gc_ref[c].astype(F32))
    u_ext = r_ref[0]
    for b in range(1, 8):
        r_ref[b] = pltpu.roll(u_ext, CONV_N - b, axis=0)

    def chunk(ci, carry):
        r0 = pl.multiple_of(ci * CONV_RC, CONV_RC)
        acc = jnp.broadcast_to(cb_ref[...], (CONV_RC, D_CONV))
        for k in range(CONV_WIDTH):
            kp = k + (CONV_HALO - (CONV_WIDTH - 1))
            a, b = kp // 8, kp % 8
            acc = acc + w_ref[k:k + 1, :] * r_ref[b, pl.ds(r0 + 8 * a, CONV_RC), :]
        mu = jnp.mean(acc, axis=-1, keepdims=True)
        d = acc - mu
        var = jnp.mean(d * d, axis=-1, keepdims=True)
        y = (d * lax.rsqrt(var + NORM_EPS)) * lg_ref[...] + lb_ref[...]
        o_ref[pl.ds(r0, CONV_RC), :] = (y * _sigmoid(y)).astype(BF16)
        return carry

    lax.fori_loop(0, CONV_TS // CONV_RC, chunk, 0)


def _conv_branch(proj3, conv_w, conv_b, ln_g, ln_b):
    s = proj3.shape[1]
    hb = CONV_TS // CONV_HALO
    cur = lambda blk: pl.BlockSpec((2, CONV_TS, CB), lambda i: (blk // 2, i, 0))
    halo = lambda blk: pl.BlockSpec(
        (2, CONV_HALO, CB), lambda i: (blk // 2, jnp.maximum(i * hb - 1, 0), 0))
    vec = pl.BlockSpec((1, D_CONV), lambda i: (0, 0))
    return pl.pallas_call(
        _conv_kernel,
        grid=(s // CONV_TS,),
        in_specs=[cur(CB_CONV_A), cur(CB_CONV_G), halo(CB_CONV_A), halo(CB_CONV_G),
                  pl.BlockSpec((CONV_WIDTH, D_CONV), lambda i: (0, 0)), vec, vec, vec],
        out_specs=pl.BlockSpec((CONV_TS, D_CONV), lambda i: (i, 0)),
        out_shape=jax.ShapeDtypeStruct((s, D_CONV), BF16),
        scratch_shapes=[pltpu.VMEM((8, CONV_N, D_CONV), F32)],
        compiler_params=_params(1),
        name="conv_branch",
    )(proj3, proj3, proj3, proj3, conv_w, conv_b, ln_g, ln_b)


def _attn_kernel(q_ref, kc_ref, kp_ref, vc_ref, vp_ref, o_ref, lse_ref, *, tq):
    n = pl.program_id(0)
    blk = ATTN_BLOCK
    scale = HEAD_DIM ** -0.5
    qi = lax.broadcasted_iota(I32, (blk, 2 * blk), 0)
    ki = lax.broadcasted_iota(I32, (blk, 2 * blk), 1)
    band = (ki >= qi) & (ki <= qi + blk)
    lane = lax.broadcasted_iota(I32, (blk, HEADS_PER_GROUP), 1)
    for j in range(tq // blk):
        mask = band
        if j == 0:
            mask = band & ((ki >= blk) | (n > 0))
        rows = slice(j * blk, (j + 1) * blk)
        lse_tile = jnp.zeros((blk, HEADS_PER_GROUP), F32)
        for hh in range(HEADS_PER_GROUP):
            cs = slice(hh * HEAD_DIM, (hh + 1) * HEAD_DIM)
            q = q_ref[rows, cs]
            if j == 0:
                k = jnp.concatenate([kp_ref[:, cs], kc_ref[0:blk, cs]], axis=0)
                v = jnp.concatenate([vp_ref[:, cs], vc_ref[0:blk, cs]], axis=0)
            else:
                k = kc_ref[(j - 1) * blk:(j + 1) * blk, cs]
                v = vc_ref[(j - 1) * blk:(j + 1) * blk, cs]
            s = lax.dot_general(q, k, (((1,), (1,)), ((), ())), preferred_element_type=F32) * scale
            s = jnp.where(mask, s, -jnp.inf)
            m = jnp.max(s, axis=-1, keepdims=True)
            p = jnp.exp(s - m)
            l = jnp.sum(p, axis=-1, keepdims=True)
            o = jnp.dot(p.astype(BF16), v, preferred_element_type=F32) / l
            o_ref[rows, cs] = o.astype(BF16)
            lse_tile = jnp.where(lane == hh, m + jnp.log(l), lse_tile)
        lse_ref[0, rows, :] = lse_tile


def _dilated_attention(proj3, group):
    _, dil = ATTN_PATTERNS[group]
    s = proj3.shape[1]
    m_len = s // dil
    tq = min(512, m_len)
    if dil == 1:
        view, cbs = proj3, (CB_Q + group, CB_K + group, CB_V + group)
    else:
        qkv = jnp.stack([proj3[CB_Q + group], proj3[CB_K + group], proj3[CB_V + group]])
        view, cbs = qkv.reshape(3, m_len, dil * CB), (0, 1, 2)
    tb = tq // ATTN_BLOCK
    cur = lambda cb: pl.BlockSpec((None, tq, CB), lambda n, r: (cb, n, r))
    prev = lambda cb: pl.BlockSpec(
        (None, ATTN_BLOCK, CB), lambda n, r: (cb, jnp.maximum(n * tb - 1, 0), r))
    o, lse = pl.pallas_call(
        functools.partial(_attn_kernel, tq=tq),
        grid=(m_len // tq, dil),
        in_specs=[cur(cbs[0]), cur(cbs[1]), prev(cbs[1]), cur(cbs[2]), prev(cbs[2])],
        out_specs=[pl.BlockSpec((tq, CB), lambda n, r: (n, r)),
                   pl.BlockSpec((1, tq, HEADS_PER_GROUP), lambda n, r: (r, n, 0))],
        out_shape=[jax.ShapeDtypeStruct((m_len, dil * CB), BF16),
                   jax.ShapeDtypeStruct((dil, m_len, HEADS_PER_GROUP), F32)],
        compiler_params=_params(2),
        name=f"dilated_attention_g{group}",
    )(view, view, view, view, view)
    o = o.reshape(s, CB)
    lse = jnp.transpose(lse, (1, 0, 2)).reshape(s, HEADS_PER_GROUP)
    return o, lse


MIX_TM = 256


def _mixer_out_kernel(x_ref, o0_ref, o1_ref, o2_ref, l0_ref, l1_ref, l2_ref, u_ref,
                      gc0, gc1, gc2, gc3, ga0, ga1, ga2, ga3,
                      wa_ref, wc_ref, bc_ref, wo_ref, g2_ref, wrh_ref, wrl_ref, br_ref,
                      x1_ref, hp_ref, idx_ref, gate_ref, pos_ref, cnt_ref, carry_ref):
    i = pl.program_id(0)
    tm = MIX_TM

    @pl.when(i == 0)
    def _():
        carry_ref[...] = jnp.zeros_like(carry_ref)

    l0, l1, l2 = l0_ref[...], l1_ref[...], l2_ref[...]
    m = jnp.maximum(jnp.maximum(l0, l1), l2)
    e0, e1, e2 = jnp.exp(l0 - m), jnp.exp(l1 - m), jnp.exp(l2 - m)
    den = e0 + e1 + e2
    mix = (e0 / den, e1 / den, e2 / den)
    o_refs = (o0_ref, o1_ref, o2_ref)
    parts = []
    for hh in range(HEADS_PER_GROUP):
        cs = slice(hh * HEAD_DIM, (hh + 1) * HEAD_DIM)
        acc = mix[0][:, hh:hh + 1] * o_refs[0][:, cs].astype(F32)
        for g in range(1, N_GROUPS):
            acc = acc + mix[g][:, hh:hh + 1] * o_refs[g][:, cs].astype(F32)
        parts.append(acc)
    o = jnp.concatenate(parts, axis=1).astype(BF16)
    ya = jnp.dot(o, wa_ref[...], preferred_element_type=F32)
    yc = jnp.dot(u_ref[...], wc_ref[...], preferred_element_type=F32) + bc_ref[...]
    gcs = (gc0, gc1, gc2, gc3)
    gas = (ga0, ga1, ga2, ga3)
    merged = []
    for c in range(D_MODEL // CB):
        cs = slice(c * CB, (c + 1) * CB)
        merged.append((gcs[c][...].astype(F32) * yc[:, cs]
                       + gas[c][...].astype(F32) * ya[:, cs]).astype(BF16))
    merged = jnp.concatenate(merged, axis=1)
    x1 = x_ref[...] + jnp.dot(merged, wo_ref[...], preferred_element_type=F32)
    x1_ref[...] = x1
    ms = jnp.mean(x1 * x1, axis=-1, keepdims=True)
    h2 = (x1 * lax.rsqrt(ms + NORM_EPS)) * g2_ref[...]
    h2_hi = h2.astype(BF16)
    hp_ref[...] = _pack_bf16_pair(h2[:, 0:PACK_W], h2[:, PACK_W:D_MODEL])
    h2_lo = (h2 - h2_hi.astype(F32)).astype(BF16)
    logits = (jnp.dot(h2_hi, wrh_ref[...], preferred_element_type=F32)
              + jnp.dot(h2_hi, wrl_ref[...], preferred_element_type=F32)
              + jnp.dot(h2_lo, wrh_ref[...], preferred_element_type=F32)) + br_ref[...]

    e_iota = lax.broadcasted_iota(I32, (tm, N_EXPERTS), 1).astype(F32)
    k_lane = lax.broadcasted_iota(I32, (tm, TOP_K), 1)
    vals = logits
    sels = []
    idx_t = jnp.zeros((tm, TOP_K), F32)
    val_t = jnp.zeros((tm, TOP_K), F32)
    for k in range(TOP_K):
        mk = jnp.max(vals, axis=-1, keepdims=True)
        ik = jnp.min(jnp.where(vals == mk, e_iota, float(N_EXPERTS)), axis=-1, keepdims=True)
        sel = e_iota == ik
        sels.append(sel)
        vals = jnp.where(sel, -jnp.inf, vals)
        idx_t = jnp.where(k_lane == k, ik, idx_t)
        val_t = jnp.where(k_lane == k, mk, val_t)
    ex = jnp.exp(val_t - val_t[:, 0:1])
    gate_ref[...] = ex / jnp.sum(ex, axis=-1, keepdims=True)
    idx_ref[...] = idx_t.astype(I32)

    cnt = jnp.zeros((tm, N_EXPERTS), F32)
    for sel in sels:
        cnt = cnt + jnp.where(sel, 1.0, 0.0)
    ri = lax.broadcasted_iota(I32, (tm, tm), 0)
    ci = lax.broadcasted_iota(I32, (tm, tm), 1)
    lower = jnp.where(ci < ri, 1.0, 0.0).astype(BF16)
    prefix = jnp.dot(lower, cnt.astype(BF16), preferred_element_type=F32) + carry_ref[...]
    pos_t = jnp.zeros((tm, TOP_K), F32)
    for k, sel in enumerate(sels):
        pk = jnp.sum(jnp.where(sel, prefix, 0.0), axis=-1, keepdims=True)
        pos_t = jnp.where(k_lane == k, pk, pos_t)
    pos_ref[...] = pos_t.astype(I32)
    carry_ref[...] = carry_ref[...] + jnp.sum(cnt, axis=0, keepdims=True)
    cnt_ref[...] = carry_ref[...]


def _mixer_output(x2, outs, lses, u_ln, proj3, wa, wc, bc, wo, g2, wr_hi, wr_lo, br):
    s = x2.shape[0]
    row = lambda w: pl.BlockSpec((MIX_TM, w), lambda i: (i, 0))
    gate = lambda cb: pl.BlockSpec((None, MIX_TM, CB), lambda i: (cb, i, 0))
    full = lambda a: pl.BlockSpec(a.shape, lambda i: (0,) * a.ndim)
    in_specs = ([row(D_MODEL)] + [row(CB)] * 3 + [row(HEADS_PER_GROUP)] * 3 + [row(D_CONV)]
                + [gate(CB_GATE_C + c) for c in range(4)]
                + [gate(CB_GATE_A + c) for c in range(4)]
                + [full(a) for a in (wa, wc, bc, wo, g2, wr_hi, wr_lo, br)])
    return pl.pallas_call(
        _mixer_out_kernel,
        grid=(s // MIX_TM,),
        in_specs=in_specs,
        out_specs=[row(D_MODEL), row(PACK_W), row(TOP_K), row(TOP_K), row(TOP_K),
                   pl.BlockSpec((1, N_EXPERTS), lambda i: (0, 0))],
        out_shape=[jax.ShapeDtypeStruct((s, D_MODEL), F32),
                   jax.ShapeDtypeStruct((s, PACK_W), U32),
                   jax.ShapeDtypeStruct((s, TOP_K), I32),
                   jax.ShapeDtypeStruct((s, TOP_K), F32),
                   jax.ShapeDtypeStruct((s, TOP_K), I32),
                   jax.ShapeDtypeStruct((1, N_EXPERTS), F32)],
        scratch_shapes=[pltpu.VMEM((1, N_EXPERTS), F32)],
        compiler_params=_params(1),
        name="mixer_output",
    )(x2, *outs, *lses, u_ln, *([proj3] * 8), wa, wc, bc, wo, g2, wr_hi, wr_lo, br)


ROW_BLK = 256
DISP_TM = 256


def _dispatch_kernel(fill_ref, dest_ref, h_ref, xs_ref, zbuf, sem):
    i = pl.program_id(0)
    n_fill = fill_ref.shape[0]

    @pl.when(i == 0)
    def _():
        zbuf[...] = jnp.zeros_like(zbuf)

        def fill_copy(j):
            return pltpu.make_async_copy(
                zbuf, xs_ref.at[pl.ds(fill_ref[j] * ROW_BLK, ROW_BLK)], sem.at[0])

        def start(j, c):
            @pl.when(fill_ref[j] >= 0)
            def _():
                fill_copy(j).start()
            return c

        def wait(j, c):
            @pl.when(fill_ref[j] >= 0)
            def _():
                fill_copy(j).wait()
            return c

        lax.fori_loop(0, n_fill, start, 0)
        lax.fori_loop(0, n_fill, wait, 0)

    def row(t, c):
        for k in range(TOP_K):
            d = dest_ref[0, t * TOP_K + k]
            pltpu.make_async_copy(h_ref.at[pl.ds(t, 1)], xs_ref.at[pl.ds(d, 1)], sem.at[0]).start()
        return c

    lax.fori_loop(0, DISP_TM, row, 0)
    for _ in range(TOP_K):
        pltpu.make_async_copy(h_ref, xs_ref.at[pl.ds(0, DISP_TM)], sem.at[0]).wait()


def _dispatch(hp, dest, fill_blocks, n_rows):
    s = hp.shape[0]
    dest3 = dest.reshape(s // DISP_TM, 1, DISP_TM * TOP_K)
    grid_spec = pltpu.PrefetchScalarGridSpec(
        num_scalar_prefetch=1,
        grid=(s // DISP_TM,),
        in_specs=[
            pl.BlockSpec((None, 1, DISP_TM * TOP_K), lambda i, fr: (i, 0, 0),
                         memory_space=pltpu.SMEM),
            pl.BlockSpec((DISP_TM, PACK_W), lambda i, fr: (i, 0)),
        ],
        out_specs=pl.BlockSpec(memory_space=pl.ANY),
        scratch_shapes=[pltpu.VMEM((ROW_BLK, PACK_W), U32), pltpu.SemaphoreType.DMA((1,))],
    )
    return pl.pallas_call(
        _dispatch_kernel,
        grid_spec=grid_spec,
        out_shape=jax.ShapeDtypeStruct((n_rows, PACK_W), U32),
        compiler_params=_params(1),
        name="moe_dispatch",
    )(fill_blocks, dest3, hp)


UNIT_BLOCKS = 6
UNIT_ROWS = ROW_BLK * UNIT_BLOCKS
EXP_TF = 256
EXP_NF = D_EXPERT // EXP_TF


def _expert_kernel(ue_ref, ub0_ref, unb_ref, nu_ref,
                   xs_ref, wg_ref, wu_ref, bg_ref, bu_ref, wo_ref, bo_ref,
                   ys_ref,
                   xbuf, xb16, acc, sem_x, sem_y):
    u = pl.program_id(0)
    f = pl.program_id(1)
    n_units = nu_ref[0]

    def x_copy(unit, slot, b):
        src = xs_ref.at[pl.ds((ub0_ref[unit] + b) * ROW_BLK, ROW_BLK)]
        dst = xbuf.at[slot, pl.ds(b * ROW_BLK, ROW_BLK)]
        return pltpu.make_async_copy(src, dst, sem_x.at[slot])

    def y_copy(unit, b):
        src = acc.at[pl.ds(b * ROW_BLK, ROW_BLK)]
        dst = ys_ref.at[pl.ds((ub0_ref[unit] + b) * ROW_BLK, ROW_BLK)]
        return pltpu.make_async_copy(src, dst, sem_y.at[0])

    def start_x(unit, slot):
        lax.fori_loop(0, unb_ref[unit], lambda b, c: (x_copy(unit, slot, b).start(), c)[1], 0)

    def wait_x(unit, slot):
        lax.fori_loop(0, unb_ref[unit], lambda b, c: (x_copy(unit, slot, b).wait(), c)[1], 0)

    @pl.when(u < n_units)
    def _():
        slot = lax.rem(u, 2)
        nblk = unb_ref[u]

        @pl.when(f == 0)
        def _():
            @pl.when(u == 0)
            def _():
                start_x(u, slot)

            wait_x(u, slot)

            @pl.when(u + 1 < n_units)
            def _():
                start_x(u + 1, 1 - slot)

            def init(b, c):
                r0 = pl.multiple_of(b * ROW_BLK, ROW_BLK)
                lo, hi = _unpack_bf16_pair(xbuf[slot, pl.ds(r0, ROW_BLK), :])
                xb16[pl.ds(r0, ROW_BLK), 0:PACK_W] = lo
                xb16[pl.ds(r0, ROW_BLK), PACK_W:D_MODEL] = hi
                acc[pl.ds(r0, ROW_BLK), :] = jnp.broadcast_to(bo_ref[...], (ROW_BLK, D_MODEL))
                return c

            lax.fori_loop(0, nblk, init, 0)

        def block(b):
            r0 = pl.multiple_of(b * ROW_BLK, ROW_BLK)
            x = xb16[pl.ds(r0, ROW_BLK), :]
            wgu = jnp.concatenate([wg_ref[...].astype(BF16), wu_ref[...].astype(BF16)], axis=1)
            gu = jnp.dot(x, wgu, preferred_element_type=F32)
            g = jnp.minimum(gu[:, 0:EXP_TF] + bg_ref[...], SWIGLU_LIMIT)
            up = jnp.clip(gu[:, EXP_TF:2 * EXP_TF] + bu_ref[...], -SWIGLU_LIMIT, SWIGLU_LIMIT)
            act = (up + 1.0) * (g * _sigmoid(SWIGLU_ALPHA * g))
            acc[pl.ds(r0, ROW_BLK), :] += jnp.dot(
                act.astype(BF16), wo_ref[...].astype(BF16), preferred_element_type=F32)

            @pl.when(f == EXP_NF - 1)
            def _():
                y_copy(u, b).start()

        def pair(p, c):
            block(2 * p)
            block(2 * p + 1)
            return c

        lax.fori_loop(0, nblk // 2, pair, 0)

        @pl.when(lax.rem(nblk, 2) == 1)
        def _():
            block(nblk - 1)

        @pl.when(f == EXP_NF - 1)
        def _():
            lax.fori_loop(0, nblk, lambda b, c: (y_copy(u, b).wait(), c)[1], 0)

    @pl.when((u == pl.num_programs(0) - 1) & (f == EXP_NF - 1))
    def _():
        n_blocks = ys_ref.shape[0] // ROW_BLK
        acc[0:ROW_BLK, :] = jnp.zeros((ROW_BLK, D_MODEL), F32)

        def tail_copy(b):
            return pltpu.make_async_copy(acc.at[pl.ds(0, ROW_BLK)],
                                         ys_ref.at[pl.ds(b * ROW_BLK, ROW_BLK)], sem_y.at[0])

        lax.fori_loop(nu_ref[1], n_blocks, lambda b, c: (tail_copy(b).start(), c)[1], 0)
        lax.fori_loop(nu_ref[1], n_blocks, lambda b, c: (tail_copy(b).wait(), c)[1], 0)


def _expert_ffn(xs, unit_e, unit_blk0, unit_nblk, unit_counts, w_in, b_in, w_out, b_out):
    n_rows = xs.shape[0]
    n_units_max = unit_e.shape[0]

    def f_eff(u, f, nu):
        return jnp.where(u < nu[0], f, EXP_NF - 1)

    def e_eff(u, ue, nu):
        return ue[jnp.minimum(u, jnp.maximum(nu[0] - 1, 0))]

    in_specs = [
        pl.BlockSpec(memory_space=pl.ANY),
        pl.BlockSpec((None, D_MODEL, EXP_TF),
                     lambda u, f, ue, ub, un, nu: (e_eff(u, ue, nu), 0, f_eff(u, f, nu))),
        pl.BlockSpec((None, D_MODEL, EXP_TF),
                     lambda u, f, ue, ub, un, nu: (e_eff(u, ue, nu), 0, EXP_NF + f_eff(u, f, nu))),
        pl.BlockSpec((None, 1, EXP_TF),
                     lambda u, f, ue, ub, un, nu: (e_eff(u, ue, nu), 0, f_eff(u, f, nu))),
        pl.BlockSpec((None, 1, EXP_TF),
                     lambda u, f, ue, ub, un, nu: (e_eff(u, ue, nu), 0, EXP_NF + f_eff(u, f, nu))),
        pl.BlockSpec((None, EXP_TF, D_MODEL),
                     lambda u, f, ue, ub, un, nu: (e_eff(u, ue, nu), f_eff(u, f, nu), 0)),
        pl.BlockSpec((None, 1, D_MODEL),
                     lambda u, f, ue, ub, un, nu: (e_eff(u, ue, nu), 0, 0)),
    ]
    grid_spec = pltpu.PrefetchScalarGridSpec(
        num_scalar_prefetch=4,
        grid=(n_units_max, EXP_NF),
        in_specs=in_specs,
        out_specs=pl.BlockSpec(memory_space=pl.ANY),
        scratch_shapes=[
            pltpu.VMEM((2, UNIT_ROWS, PACK_W), U32),
            pltpu.VMEM((UNIT_ROWS, D_MODEL), BF16),
            pltpu.VMEM((UNIT_ROWS, D_MODEL), F32),
            pltpu.SemaphoreType.DMA((2,)),
            pltpu.SemaphoreType.DMA((1,)),
        ],
    )
    return pl.pallas_call(
        _expert_kernel,
        grid_spec=grid_spec,
        out_shape=jax.ShapeDtypeStruct((n_rows, D_MODEL), F32),
        compiler_params=_params(2),
        name="expert_ffn",
    )(unit_e, unit_blk0, unit_nblk, unit_counts, xs,
      w_in, w_in, b_in[:, None, :], b_in[:, None, :], w_out, b_out[:, None, :])


COMB_TM = 256


def _combine_kernel(dcur_ref, dnxt_ref, x1_ref, gate_ref, g_ref, ys_ref, o_ref, ybuf, sem):
    i = pl.program_id(0)
    n = pl.num_programs(0)
    slot = lax.rem(i, 2)

    def issue(dref, s):
        def row(t, c):
            for k in range(TOP_K):
                d = dref[0, t * TOP_K + k]
                pltpu.make_async_copy(ys_ref.at[pl.ds(d, 1)], ybuf.at[s, k, pl.ds(t, 1)],
                                      sem.at[s]).start()
            return c

        lax.fori_loop(0, COMB_TM, row, 0)

    @pl.when(i == 0)
    def _():
        issue(dcur_ref, slot)

    @pl.when(i + 1 < n)
    def _():
        issue(dnxt_ref, 1 - slot)

    for k in range(TOP_K):
        pltpu.make_async_copy(ys_ref.at[pl.ds(0, COMB_TM)], ybuf.at[slot, k], sem.at[slot]).wait()

    gate = gate_ref[...]
    y = x1_ref[...]
    for k in range(TOP_K):
        y = y + gate[:, k:k + 1] * ybuf[slot, k]
    ms = jnp.mean(y * y, axis=-1, keepdims=True)
    o_ref[...] = (y * lax.rsqrt(ms + NORM_EPS)) * g_ref[...]


def _combine(dest, x1, gate, norm_g, ys):
    s = x1.shape[0]
    nt = s // COMB_TM
    dest3 = dest.reshape(nt, 1, COMB_TM * TOP_K)
    smem = lambda imap: pl.BlockSpec((None, 1, COMB_TM * TOP_K), imap, memory_space=pltpu.SMEM)
    row = lambda w: pl.BlockSpec((COMB_TM, w), lambda i: (i, 0))
    return pl.pallas_call(
        _combine_kernel,
        grid=(nt,),
        in_specs=[smem(lambda i: (i, 0, 0)),
                  smem(lambda i: (jnp.minimum(i + 1, nt - 1), 0, 0)),
                  row(D_MODEL), row(TOP_K), pl.BlockSpec((1, D_MODEL), lambda i: (0, 0)),
                  pl.BlockSpec(memory_space=pl.ANY)],
        out_specs=row(D_MODEL),
        out_shape=jax.ShapeDtypeStruct((s, D_MODEL), F32),
        scratch_shapes=[pltpu.VMEM((2, TOP_K, COMB_TM, D_MODEL), F32),
                        pltpu.SemaphoreType.DMA((2,))],
        compiler_params=_params(1),
        name="moe_combine",
    )(dest3, dest3, x1, gate, norm_g, ys)


def _route_tables(idx, pos, counts, n_tok):
    n_assign = n_tok * TOP_K
    counts = counts.reshape(N_EXPERTS).astype(I32)
    nblk_e = (counts + ROW_BLK - 1) // ROW_BLK
    pad_start = (jnp.cumsum(nblk_e) - nblk_e) * ROW_BLK
    onehot = idx[..., None] == jnp.arange(N_EXPERTS, dtype=I32)
    dest = pos + jnp.sum(jnp.where(onehot, pad_start, 0), axis=-1)

    n_blocks = -(-(n_assign + N_EXPERTS * (ROW_BLK - 1)) // ROW_BLK)
    used = jnp.sum(nblk_e)
    blk_ids = jnp.arange(n_blocks, dtype=I32)
    last_blk = pad_start // ROW_BLK + nblk_e - 1
    fill_blocks = jnp.concatenate([jnp.where(nblk_e > 0, last_blk, -1),
                                   jnp.where(blk_ids >= used, blk_ids, -1)]).astype(I32)

    n_units_max = (n_blocks + (UNIT_BLOCKS - 1) * N_EXPERTS) // UNIT_BLOCKS
    chunks_e = (nblk_e + UNIT_BLOCKS - 1) // UNIT_BLOCKS
    chunk_end = jnp.cumsum(chunks_e)
    chunk_start = chunk_end - chunks_e
    n_units = chunk_end[-1]
    uid = jnp.arange(n_units_max, dtype=I32)
    unit_e = jnp.minimum(jnp.searchsorted(chunk_end, uid, side="right"), N_EXPERTS - 1).astype(I32)
    c_in_e = uid - chunk_start[unit_e]
    unit_blk0 = (pad_start[unit_e] // ROW_BLK + c_in_e * UNIT_BLOCKS).astype(I32)
    unit_nblk = jnp.clip(nblk_e[unit_e] - c_in_e * UNIT_BLOCKS, 0, UNIT_BLOCKS).astype(I32)
    active = uid < n_units
    unit_blk0 = jnp.where(active, unit_blk0, 0)
    unit_nblk = jnp.where(active, unit_nblk, 0)
    unit_counts = jnp.stack([n_units, used]).astype(I32)
    return dest.astype(I32), fill_blocks, n_blocks * ROW_BLK, unit_e, unit_blk0, unit_nblk, unit_counts


def kernel(x, norm1_g, w_in, b_gate, conv_w, conv_b, conv_ln_g, conv_ln_b, w_conv_out, b_conv_out,
           w_attn_out, w_out, norm2_g, w_router, b_router, w_exp_in, b_exp_in, w_exp_out, b_exp_out,
           norm_f_g):
    b_sz, s_len, d = x.shape
    n_tok = b_sz * s_len
    x2 = x.reshape(n_tok, d)
    l = 0
    b_gate_ext = jnp.concatenate([jnp.zeros((GATE_COL0,), F32), b_gate[l]])[None, :]
    proj3 = _in_projection(x2, norm1_g[l][None, :], w_in[l], b_gate_ext)
    u_ln = _conv_branch(proj3, conv_w[l], conv_b[l][None, :], conv_ln_g[l][None, :],
                        conv_ln_b[l][None, :])
    outs, lses = [], []
    for g in range(N_GROUPS):
        o_g, lse_g = _dilated_attention(proj3, g)
        outs.append(o_g)
        lses.append(lse_g)
    wr = w_router[l]
    wr_hi = wr.astype(BF16)
    wr_lo = (wr - wr_hi.astype(F32)).astype(BF16)
    x1, hp, idx, gate, pos, counts = _mixer_output(
        x2, outs, lses, u_ln, proj3,
        w_attn_out[l].astype(BF16), w_conv_out[l].astype(BF16), b_conv_out[l][None, :],
        w_out[l].astype(BF16), norm2_g[l][None, :], wr_hi, wr_lo, b_router[l][None, :])

    dest, fill_blocks, n_rows, unit_e, unit_blk0, unit_nblk, unit_counts = _route_tables(
        idx, pos, counts, n_tok)
    xs = _dispatch(hp, dest, fill_blocks, n_rows)
    ys = _expert_ffn(xs, unit_e, unit_blk0, unit_nblk, unit_counts,
                     w_exp_in[l], b_exp_in[l], w_exp_out[l], b_exp_out[l])
    out = _combine(dest, x1, gate, norm_f_g[None, :], ys)
    return out.reshape(b_sz, s_len, d)
```

```python
import functools

import jax
import jax.numpy as jnp
from jax import lax
from jax.experimental import pallas as pl
from jax.experimental.pallas import tpu as pltpu

D_MODEL = 2048
SEQ = 8192
D_CONV = D_MODEL // 2
CONV_WIDTH = 31
HEAD_DIM = 128
HEADS_PER_GROUP = 4
ATTN_PATTERNS = ((128, 1), (512, 4), (2048, 16))
N_GROUPS = len(ATTN_PATTERNS)
ATTN_WIDTH = N_GROUPS * HEADS_PER_GROUP * HEAD_DIM
ATTN_OUT_WIDTH = HEADS_PER_GROUP * HEAD_DIM
ATTN_BLOCK = 128
D_IN = 2 * D_CONV + 3 * ATTN_WIDTH + 2 * D_MODEL
N_EXPERTS = 32
TOP_K = 4
D_EXPERT = D_MODEL
SWIGLU_LIMIT = 7.0
SWIGLU_ALPHA = 1.702
NORM_EPS = 1e-5

CB = 512
N_CB = D_IN // CB
CB_CONV_A, CB_CONV_G = 0, 2
CB_Q, CB_K, CB_V = 4, 7, 10
CB_GATE_C, CB_GATE_A = 13, 17
GATE_COL0 = CB_GATE_C * CB

VMEM_LIMIT = 56 * 1024 * 1024

F32 = jnp.float32
BF16 = jnp.bfloat16
U32 = jnp.uint32
I32 = jnp.int32

PACK_W = D_MODEL // 2
HI_MASK = 0xFFFF0000


def _sigmoid(z):
    return 1.0 / (1.0 + jnp.exp(-z))


def _pack_bf16_pair(lo, hi):
    lo_bits = lax.bitcast_convert_type(lo.astype(BF16).astype(F32), U32) >> 16
    hi_bits = lax.bitcast_convert_type(hi.astype(BF16).astype(F32), U32) & jnp.uint32(HI_MASK)
    return hi_bits | lo_bits


def _unpack_bf16_pair(w):
    lo = lax.bitcast_convert_type(w << 16, F32).astype(BF16)
    hi = lax.bitcast_convert_type(w & jnp.uint32(HI_MASK), F32).astype(BF16)
    return lo, hi


def _params(n_axes, vmem=VMEM_LIMIT):
    return pltpu.CompilerParams(
        dimension_semantics=("arbitrary",) * n_axes, vmem_limit_bytes=vmem)


IN_TM = 512
IN_TN = 1536


def _inproj_kernel(x_ref, g_ref, w_ref, bg_ref, o_ref, wbf_ref):
    j = pl.program_id(0)
    i = pl.program_id(1)

    @pl.when(i == 0)
    def _():
        wbf_ref[...] = w_ref[...].astype(BF16)

    x = x_ref[...]
    ms = jnp.mean(x * x, axis=-1, keepdims=True)
    h = ((x * lax.rsqrt(ms + NORM_EPS)) * g_ref[...]).astype(BF16)
    acc = jnp.dot(h, wbf_ref[...], preferred_element_type=F32)

    first_gate_tile = GATE_COL0 // IN_TN
    n_sub = IN_TN // CB

    def store(val):
        for c in range(n_sub):
            o_ref[c] = val[:, c * CB:(c + 1) * CB].astype(BF16)

    @pl.when(j < first_gate_tile)
    def _():
        store(acc)

    @pl.when(j == first_gate_tile)
    def _():
        col = j * IN_TN + lax.broadcasted_iota(I32, (1, IN_TN), 1)
        store(jnp.where(col >= GATE_COL0, _sigmoid(acc + bg_ref[...]), acc))

    @pl.when(j > first_gate_tile)
    def _():
        store(_sigmoid(acc + bg_ref[...]))


def _in_projection(x2, norm_g, w_in, b_gate_ext):
    s = x2.shape[0]
    grid = (D_IN // IN_TN, s // IN_TM)
    return pl.pallas_call(
        _inproj_kernel,
        grid=grid,
        in_specs=[
            pl.BlockSpec((IN_TM, D_MODEL), lambda j, i: (i, 0)),
            pl.BlockSpec((1, D_MODEL), lambda j, i: (0, 0)),
            pl.BlockSpec((D_MODEL, IN_TN), lambda j, i: (0, j)),
            pl.BlockSpec((1, IN_TN), lambda j, i: (0, j)),
        ],
        out_specs=pl.BlockSpec((IN_TN // CB, IN_TM, CB), lambda j, i: (j, i, 0)),
        out_shape=jax.ShapeDtypeStruct((N_CB, s, CB), BF16),
        scratch_shapes=[pltpu.VMEM((D_MODEL, IN_TN), BF16)],
        compiler_params=_params(2),
        name="in_projection",
    )(x2, norm_g, w_in, b_gate_ext)


CONV_TS = 256
CONV_HALO = 32
CONV_RC = 32
CONV_N = CONV_TS + CONV_HALO


def _conv_kernel(ac_ref, gc_ref, ah_ref, gh_ref, w_ref, cb_ref, lg_ref, lb_ref, o_ref, r_ref):
    i = pl.program_id(0)
    has_prev = i > 0
    half = D_CONV // 2
    for c in range(2):
        cs = slice(c * half, (c + 1) * half)
        uh = ah_ref[c].astype(F32) * _sigmoid(gh_ref[c].astype(F32))
        r_ref[0, 0:CONV_HALO, cs] = jnp.where(has_prev, uh, 0.0)
        r_ref[0, CONV_HALO:CONV_N, cs] = ac_ref[c].astype(F32) * _sigmoid(gc_ref[c].astype(F32))
    u_ext = r_ref[0]
    for b in range(1, 8):
        r_ref[b] = pltpu.roll(u_ext, CONV_N - b, axis=0)

    def chunk(ci, carry):
        r0 = pl.multiple_of(ci * CONV_RC, CONV_RC)
        acc = jnp.broadcast_to(cb_ref[...], (CONV_RC, D_CONV))
        for k in range(CONV_WIDTH):
            kp = k + (CONV_HALO - (CONV_WIDTH - 1))
            a, b = kp // 8, kp % 8
            acc = acc + w_ref[k:k + 1, :] * r_ref[b, pl.ds(r0 + 8 * a, CONV_RC), :]
        mu = jnp.mean(acc, axis=-1, keepdims=True)
        d = acc - mu
        var = jnp.mean(d * d, axis=-1, keepdims=True)
        y = (d * lax.rsqrt(var + NORM_EPS)) * lg_ref[...] + lb_ref[...]
        o_ref[pl.ds(r0, CONV_RC), :] = (y * _sigmoid(y)).astype(BF16)
        return carry

    lax.fori_loop(0, CONV_TS // CONV_RC, chunk, 0)


def _conv_branch(proj3, conv_w, conv_b, ln_g, ln_b):
    s = proj3.shape[1]
    hb = CONV_TS // CONV_HALO
    cur = lambda blk: pl.BlockSpec((2, CONV_TS, CB), lambda i: (blk // 2, i, 0))
    halo = lambda blk: pl.BlockSpec(
        (2, CONV_HALO, CB), lambda i: (blk // 2, jnp.maximum(i * hb - 1, 0), 0))
    vec = pl.BlockSpec((1, D_CONV), lambda i: (0, 0))
    return pl.pallas_call(
        _conv_kernel,
        grid=(s // CONV_TS,),
        in_specs=[cur(CB_CONV_A), cur(CB_CONV_G), halo(CB_CONV_A), halo(CB_CONV_G),
                  pl.BlockSpec((CONV_WIDTH, D_CONV), lambda i: (0, 0)), vec, vec, vec],
        out_specs=pl.BlockSpec((CONV_TS, D_CONV), lambda i: (i, 0)),
        out_shape=jax.ShapeDtypeStruct((s, D_CONV), BF16),
        scratch_shapes=[pltpu.VMEM((8, CONV_N, D_CONV), F32)],
        compiler_params=_params(1),
        name="conv_branch",
    )(proj3, proj3, proj3, proj3, conv_w, conv_b, ln_g, ln_b)


def _attn_kernel(q_ref, kc_ref, kp_ref, vc_ref, vp_ref, o_ref, lse_ref, *, tq):
    n = pl.program_id(0)
    blk = ATTN_BLOCK
    scale = HEAD_DIM ** -0.5
    qi = lax.broadcasted_iota(I32, (blk, 2 * blk), 0)
    ki = lax.broadcasted_iota(I32, (blk, 2 * blk), 1)
    band = (ki >= qi) & (ki <= qi + blk)
    lane = lax.broadcasted_iota(I32, (blk, HEADS_PER_GROUP), 1)
    for j in range(tq // blk):
        mask = band
        if j == 0:
            mask = band & ((ki >= blk) | (n > 0))
        rows = slice(j * blk, (j + 1) * blk)
        lse_tile = jnp.zeros((blk, HEADS_PER_GROUP), F32)
        for hh in range(HEADS_PER_GROUP):
            cs = slice(hh * HEAD_DIM, (hh + 1) * HEAD_DIM)
            q = q_ref[rows, cs]
            if j == 0:
                k = jnp.concatenate([kp_ref[:, cs], kc_ref[0:blk, cs]], axis=0)
                v = jnp.concatenate([vp_ref[:, cs], vc_ref[0:blk, cs]], axis=0)
            else:
                k = kc_ref[(j - 1) * blk:(j + 1) * blk, cs]
                v = vc_ref[(j - 1) * blk:(j + 1) * blk, cs]
            s = lax.dot_general(q, k, (((1,), (1,)), ((), ())), preferred_element_type=F32) * scale
            s = jnp.where(mask, s, -jnp.inf)
            m = jnp.max(s, axis=-1, keepdims=True)
            p = jnp.exp(s - m)
            l = jnp.sum(p, axis=-1, keepdims=True)
            o = jnp.dot(p.astype(BF16), v, preferred_element_type=F32) / l
            o_ref[rows, cs] = o.astype(BF16)
            lse_tile = jnp.where(lane == hh, m + jnp.log(l), lse_tile)
        lse_ref[0, rows, :] = lse_tile


def _dilated_attention(proj3, group):
    _, dil = ATTN_PATTERNS[group]
    s = proj3.shape[1]
    m_len = s // dil
    tq = min(512, m_len)
    if dil == 1:
        view, cbs = proj3, (CB_Q + group, CB_K + group, CB_V + group)
    else:
        qkv = jnp.stack([proj3[CB_Q + group], proj3[CB_K + group], proj3[CB_V + group]])
        view, cbs = qkv.reshape(3, m_len, dil * CB), (0, 1, 2)
    tb = tq // ATTN_BLOCK
    cur = lambda cb: pl.BlockSpec((None, tq, CB), lambda n, r: (cb, n, r))
    prev = lambda cb: pl.BlockSpec(
        (None, ATTN_BLOCK, CB), lambda n, r: (cb, jnp.maximum(n * tb - 1, 0), r))
    o, lse = pl.pallas_call(
        functools.partial(_attn_kernel, tq=tq),
        grid=(m_len // tq, dil),
        in_specs=[cur(cbs[0]), cur(cbs[1]), prev(cbs[1]), cur(cbs[2]), prev(cbs[2])],
        out_specs=[pl.BlockSpec((tq, CB), lambda n, r: (n, r)),
                   pl.BlockSpec((1, tq, HEADS_PER_GROUP), lambda n, r: (r, n, 0))],
        out_shape=[jax.ShapeDtypeStruct((m_len, dil * CB), BF16),
                   jax.ShapeDtypeStruct((dil, m_len, HEADS_PER_GROUP), F32)],
        compiler_params=_params(2),
        name=f"dilated_attention_g{group}",
    )(view, view, view, view, view)
    o = o.reshape(s, CB)
    lse = jnp.transpose(lse, (1, 0, 2)).reshape(s, HEADS_PER_GROUP)
    return o, lse


MIX_TM = 256


def _mixer_out_kernel(x_ref, o0_ref, o1_ref, o2_ref, l0_ref, l1_ref, l2_ref, u_ref,
                      gc0, gc1, gc2, gc3, ga0, ga1, ga2, ga3,
                      wa_ref, wc_ref, bc_ref, wo_ref, g2_ref, wrh_ref, wrl_ref, br_ref,
                      x1_ref, hp_ref, idx_ref, gate_ref, pos_ref, cnt_ref, carry_ref):
    i = pl.program_id(0)
    tm = MIX_TM

    @pl.when(i == 0)
    def _():
        carry_ref[...] = jnp.zeros_like(carry_ref)

    l0, l1, l2 = l0_ref[...], l1_ref[...], l2_ref[...]
    m = jnp.maximum(jnp.maximum(l0, l1), l2)
    e0, e1, e2 = jnp.exp(l0 - m), jnp.exp(l1 - m), jnp.exp(l2 - m)
    den = e0 + e1 + e2
    mix = (e0 / den, e1 / den, e2 / den)
    o_refs = (o0_ref, o1_ref, o2_ref)
    parts = []
    for hh in range(HEADS_PER_GROUP):
        cs = slice(hh * HEAD_DIM, (hh + 1) * HEAD_DIM)
        acc = mix[0][:, hh:hh + 1] * o_refs[0][:, cs].astype(F32)
        for g in range(1, N_GROUPS):
            acc = acc + mix[g][:, hh:hh + 1] * o_refs[g][:, cs].astype(F32)
        parts.append(acc)
    o = jnp.concatenate(parts, axis=1).astype(BF16)
    ya = jnp.dot(o, wa_ref[...], preferred_element_type=F32)
    yc = jnp.dot(u_ref[...], wc_ref[...], preferred_element_type=F32) + bc_ref[...]
    gcs = (gc0, gc1, gc2, gc3)
    gas = (ga0, ga1, ga2, ga3)
    merged = []
    for c in range(D_MODEL // CB):
        cs = slice(c * CB, (c + 1) * CB)
        merged.append((gcs[c][...].astype(F32) * yc[:, cs]
                       + gas[c][...].astype(F32) * ya[:, cs]).astype(BF16))
    merged = jnp.concatenate(merged, axis=1)
    x1 = x_ref[...] + jnp.dot(merged, wo_ref[...], preferred_element_type=F32)
    x1_ref[...] = x1
    ms = jnp.mean(x1 * x1, axis=-1, keepdims=True)
    h2 = (x1 * lax.rsqrt(ms + NORM_EPS)) * g2_ref[...]
    h2_hi = h2.astype(BF16)
    hp_ref[...] = _pack_bf16_pair(h2[:, 0:PACK_W], h2[:, PACK_W:D_MODEL])
    h2_lo = (h2 - h2_hi.astype(F32)).astype(BF16)
    logits = (jnp.dot(h2_hi, wrh_ref[...], preferred_element_type=F32)
              + jnp.dot(h2_hi, wrl_ref[...], preferred_element_type=F32)
              + jnp.dot(h2_lo, wrh_ref[...], preferred_element_type=F32)) + br_ref[...]

    e_iota = lax.broadcasted_iota(I32, (tm, N_EXPERTS), 1).astype(F32)
    k_lane = lax.broadcasted_iota(I32, (tm, TOP_K), 1)
    vals = logits
    sels = []
    idx_t = jnp.zeros((tm, TOP_K), F32)
    val_t = jnp.zeros((tm, TOP_K), F32)
    for k in range(TOP_K):
        mk = jnp.max(vals, axis=-1, keepdims=True)
        ik = jnp.min(jnp.where(vals == mk, e_iota, float(N_EXPERTS)), axis=-1, keepdims=True)
        sel = e_iota == ik
        sels.append(sel)
        vals = jnp.where(sel, -jnp.inf, vals)
        idx_t = jnp.where(k_lane == k, ik, idx_t)
        val_t = jnp.where(k_lane == k, mk, val_t)
    ex = jnp.exp(val_t - val_t[:, 0:1])
    gate_ref[...] = ex / jnp.sum(ex, axis=-1, keepdims=True)
    idx_ref[...] = idx_t.astype(I32)

    cnt = jnp.zeros((tm, N_EXPERTS), F32)
    for sel in sels:
        cnt = cnt + jnp.where(sel, 1.0, 0.0)
    ri = lax.broadcasted_iota(I32, (tm, tm), 0)
    ci = lax.broadcasted_iota(I32, (tm, tm), 1)
    lower = jnp.where(ci < ri, 1.0, 0.0).astype(BF16)
    prefix = jnp.dot(lower, cnt.astype(BF16), preferred_element_type=F32) + carry_ref[...]
    pos_t = jnp.zeros((tm, TOP_K), F32)
    for k, sel in enumerate(sels):
        pk = jnp.sum(jnp.where(sel, prefix, 0.0), axis=-1, keepdims=True)
        pos_t = jnp.where(k_lane == k, pk, pos_t)
    pos_ref[...] = pos_t.astype(I32)
    carry_ref[...] = carry_ref[...] + jnp.sum(cnt, axis=0, keepdims=True)
    cnt_ref[...] = carry_ref[...]


def _mixer_output(x2, outs, lses, u_ln, proj3, wa, wc, bc, wo, g2, wr_hi, wr_lo, br):
    s = x2.shape[0]
    row = lambda w: pl.BlockSpec((MIX_TM, w), lambda i: (i, 0))
    gate = lambda cb: pl.BlockSpec((None, MIX_TM, CB), lambda i: (cb, i, 0))
    full = lambda a: pl.BlockSpec(a.shape, lambda i: (0,) * a.ndim)
    in_specs = ([row(D_MODEL)] + [row(CB)] * 3 + [row(HEADS_PER_GROUP)] * 3 + [row(D_CONV)]
                + [gate(CB_GATE_C + c) for c in range(4)]
                + [gate(CB_GATE_A + c) for c in range(4)]
                + [full(a) for a in (wa, wc, bc, wo, g2, wr_hi, wr_lo, br)])
    return pl.pallas_call(
        _mixer_out_kernel,
        grid=(s // MIX_TM,),
        in_specs=in_specs,
        out_specs=[row(D_MODEL), row(PACK_W), row(TOP_K), row(TOP_K), row(TOP_K),
                   pl.BlockSpec((1, N_EXPERTS), lambda i: (0, 0))],
        out_shape=[jax.ShapeDtypeStruct((s, D_MODEL), F32),
                   jax.ShapeDtypeStruct((s, PACK_W), U32),
                   jax.ShapeDtypeStruct((s, TOP_K), I32),
                   jax.ShapeDtypeStruct((s, TOP_K), F32),
                   jax.ShapeDtypeStruct((s, TOP_K), I32),
                   jax.ShapeDtypeStruct((1, N_EXPERTS), F32)],
        scratch_shapes=[pltpu.VMEM((1, N_EXPERTS), F32)],
        compiler_params=_params(1),
        name="mixer_output",
    )(x2, *outs, *lses, u_ln, *([proj3] * 8), wa, wc, bc, wo, g2, wr_hi, wr_lo, br)


ROW_BLK = 256
DISP_TM = 256


def _dispatch_kernel(fill_ref, dest_ref, h_ref, xs_ref, zbuf, sem):
    i = pl.program_id(0)
    n_fill = fill_ref.shape[0]

    @pl.when(i == 0)
    def _():
        zbuf[...] = jnp.zeros_like(zbuf)

        def fill_copy(j):
            return pltpu.make_async_copy(
                zbuf, xs_ref.at[pl.ds(fill_ref[j] * ROW_BLK, ROW_BLK)], sem.at[0])

        def start(j, c):
            @pl.when(fill_ref[j] >= 0)
            def _():
                fill_copy(j).start()
            return c

        def wait(j, c):
            @pl.when(fill_ref[j] >= 0)
            def _():
                fill_copy(j).wait()
            return c

        lax.fori_loop(0, n_fill, start, 0)
        lax.fori_loop(0, n_fill, wait, 0)

    def row(t, c):
        for k in range(TOP_K):
            d = dest_ref[0, t * TOP_K + k]
            pltpu.make_async_copy(h_ref.at[pl.ds(t, 1)], xs_ref.at[pl.ds(d, 1)], sem.at[0]).start()
        return c

    lax.fori_loop(0, DISP_TM, row, 0)
    for _ in range(TOP_K):
        pltpu.make_async_copy(h_ref, xs_ref.at[pl.ds(0, DISP_TM)], sem.at[0]).wait()


def _dispatch(hp, dest, fill_blocks, n_rows):
    s = hp.shape[0]
    dest3 = dest.reshape(s // DISP_TM, 1, DISP_TM * TOP_K)
    grid_spec = pltpu.PrefetchScalarGridSpec(
        num_scalar_prefetch=1,
        grid=(s // DISP_TM,),
        in_specs=[
            pl.BlockSpec((None, 1, DISP_TM * TOP_K), lambda i, fr: (i, 0, 0),
                         memory_space=pltpu.SMEM),
            pl.BlockSpec((DISP_TM, PACK_W), lambda i, fr: (i, 0)),
        ],
        out_specs=pl.BlockSpec(memory_space=pl.ANY),
        scratch_shapes=[pltpu.VMEM((ROW_BLK, PACK_W), U32), pltpu.SemaphoreType.DMA((1,))],
    )
    return pl.pallas_call(
        _dispatch_kernel,
        grid_spec=grid_spec,
        out_shape=jax.ShapeDtypeStruct((n_rows, PACK_W), U32),
        compiler_params=_params(1),
        name="moe_dispatch",
    )(fill_blocks, dest3, hp)


UNIT_BLOCKS = 6
UNIT_ROWS = ROW_BLK * UNIT_BLOCKS
EXP_TF = 256
EXP_NF = D_EXPERT // EXP_TF


def _expert_kernel(ue_ref, ub0_ref, unb_ref, nu_ref,
                   xs_ref, win_ref, wout_ref, bin_ref, bo_ref,
                   ys_ref,
                   xbuf, xb16, acc, wg_buf, wu_buf, wo_buf, sem_x, sem_y, sem_w):
    u = pl.program_id(0)
    n_units = nu_ref[0]

    def x_copy(unit, slot, b):
        src = xs_ref.at[pl.ds((ub0_ref[unit] + b) * ROW_BLK, ROW_BLK)]
        dst = xbuf.at[slot, pl.ds(b * ROW_BLK, ROW_BLK)]
        return pltpu.make_async_copy(src, dst, sem_x.at[slot])

    def y_copy(unit, b):
        src = acc.at[pl.ds(b * ROW_BLK, ROW_BLK)]
        dst = ys_ref.at[pl.ds((ub0_ref[unit] + b) * ROW_BLK, ROW_BLK)]
        return pltpu.make_async_copy(src, dst, sem_y.at[0])

    def w_copies(unit, f, slot):
        e = ue_ref[unit]
        c0 = pl.multiple_of(f * EXP_TF, EXP_TF)
        return (
            pltpu.make_async_copy(win_ref.at[e, :, pl.ds(c0, EXP_TF)], wg_buf.at[slot],
                                  sem_w.at[slot]),
            pltpu.make_async_copy(win_ref.at[e, :, pl.ds(D_EXPERT + c0, EXP_TF)], wu_buf.at[slot],
                                  sem_w.at[slot]),
            pltpu.make_async_copy(wout_ref.at[e, pl.ds(c0, EXP_TF), :], wo_buf.at[slot],
                                  sem_w.at[slot]),
        )

    def start_w(unit, f, slot):
        for cp in w_copies(unit, f, slot):
            cp.start()

    def wait_w(unit, f, slot):
        for cp in w_copies(unit, f, slot):
            cp.wait()

    def start_x(unit, slot):
        lax.fori_loop(0, unb_ref[unit], lambda b, c: (x_copy(unit, slot, b).start(), c)[1], 0)

    def wait_x(unit, slot):
        lax.fori_loop(0, unb_ref[unit], lambda b, c: (x_copy(unit, slot, b).wait(), c)[1], 0)

    def wait_y(unit):
        lax.fori_loop(0, unb_ref[unit], lambda b, c: (y_copy(unit, b).wait(), c)[1], 0)

    @pl.when(u < n_units)
    def _():
        xslot = lax.rem(u, 2)
        nblk = unb_ref[u]

        @pl.when(u == 0)
        def _():
            start_x(u, xslot)
            start_w(u, 0, 0)

        wait_x(u, xslot)

        @pl.when(u + 1 < n_units)
        def _():
            start_x(u + 1, 1 - xslot)

        @pl.when(u > 0)
        def _():
            wait_y(u - 1)

        def init(b, c):
            r0 = pl.multiple_of(b * ROW_BLK, ROW_BLK)
            lo, hi = _unpack_bf16_pair(xbuf[xslot, pl.ds(r0, ROW_BLK), :])
            xb16[pl.ds(r0, ROW_BLK), 0:PACK_W] = lo
            xb16[pl.ds(r0, ROW_BLK), PACK_W:D_MODEL] = hi
            acc[pl.ds(r0, ROW_BLK), :] = jnp.broadcast_to(bo_ref[...], (ROW_BLK, D_MODEL))
            return c

        lax.fori_loop(0, nblk, init, 0)

        def block(b, f, ws):
            r0 = pl.multiple_of(b * ROW_BLK, ROW_BLK)
            c0 = pl.multiple_of(f * EXP_TF, EXP_TF)
            x = xb16[pl.ds(r0, ROW_BLK), :]
            wgu = jnp.concatenate([wg_buf[ws].astype(BF16), wu_buf[ws].astype(BF16)], axis=1)
            gu = jnp.dot(x, wgu, preferred_element_type=F32)
            bg = bin_ref[:, pl.ds(c0, EXP_TF)]
            bu = bin_ref[:, pl.ds(D_EXPERT + c0, EXP_TF)]
            g = jnp.minimum(gu[:, 0:EXP_TF] + bg, SWIGLU_LIMIT)
            up = jnp.clip(gu[:, EXP_TF:2 * EXP_TF] + bu, -SWIGLU_LIMIT, SWIGLU_LIMIT)
            act = (up + 1.0) * (g * _sigmoid(SWIGLU_ALPHA * g))
            acc[pl.ds(r0, ROW_BLK), :] += jnp.dot(
                act.astype(BF16), wo_buf[ws].astype(BF16), preferred_element_type=F32)

            @pl.when(f == EXP_NF - 1)
            def _():
                y_copy(u, b).start()

        def f_tile(f, ws):
            def pair(p, c):
                block(2 * p, f, ws)
                block(2 * p + 1, f, ws)
                return c

            lax.fori_loop(0, nblk // 2, pair, 0)

            @pl.when(lax.rem(nblk, 2) == 1)
            def _():
                block(nblk - 1, f, ws)

        def f_pair(fp, c):
            f0 = 2 * fp
            start_w(u, f0 + 1, 1)
            wait_w(u, f0, 0)
            f_tile(f0, 0)

            @pl.when(f0 + 2 < EXP_NF)
            def _():
                start_w(u, f0 + 2, 0)

            @pl.when((f0 + 2 == EXP_NF) & (u + 1 < n_units))
            def _():
                start_w(u + 1, 0, 0)

            wait_w(u, f0 + 1, 1)
            f_tile(f0 + 1, 1)
            return c

        lax.fori_loop(0, EXP_NF // 2, f_pair, 0)

        @pl.when(u == n_units - 1)
        def _():
            wait_y(u)

    @pl.when(u == pl.num_programs(0) - 1)
    def _():
        n_blocks = ys_ref.shape[0] // ROW_BLK
        acc[0:ROW_BLK, :] = jnp.zeros((ROW_BLK, D_MODEL), F32)

        def tail_copy(b):
            return pltpu.make_async_copy(acc.at[pl.ds(0, ROW_BLK)],
                                         ys_ref.at[pl.ds(b * ROW_BLK, ROW_BLK)], sem_y.at[0])

        lax.fori_loop(nu_ref[1], n_blocks, lambda b, c: (tail_copy(b).start(), c)[1], 0)
        lax.fori_loop(nu_ref[1], n_blocks, lambda b, c: (tail_copy(b).wait(), c)[1], 0)


def _expert_ffn(xs, unit_e, unit_blk0, unit_nblk, unit_counts, w_in, b_in, w_out, b_out):
    n_rows = xs.shape[0]
    n_units_max = unit_e.shape[0]

    def e_eff(u, ue, nu):
        return ue[jnp.minimum(u, jnp.maximum(nu[0] - 1, 0))]

    any_spec = pl.BlockSpec(memory_space=pl.ANY)
    in_specs = [
        any_spec, any_spec, any_spec,
        pl.BlockSpec((None, 1, 2 * D_EXPERT), lambda u, ue, ub, un, nu: (e_eff(u, ue, nu), 0, 0)),
        pl.BlockSpec((None, 1, D_MODEL), lambda u, ue, ub, un, nu: (e_eff(u, ue, nu), 0, 0)),
    ]
    grid_spec = pltpu.PrefetchScalarGridSpec(
        num_scalar_prefetch=4,
        grid=(n_units_max,),
        in_specs=in_specs,
        out_specs=any_spec,
        scratch_shapes=[
            pltpu.VMEM((2, UNIT_ROWS, PACK_W), U32),
            pltpu.VMEM((UNIT_ROWS, D_MODEL), BF16),
            pltpu.VMEM((UNIT_ROWS, D_MODEL), F32),
            pltpu.VMEM((2, D_MODEL, EXP_TF), F32),
            pltpu.VMEM((2, D_MODEL, EXP_TF), F32),
            pltpu.VMEM((2, EXP_TF, D_MODEL), F32),
            pltpu.SemaphoreType.DMA((2,)),
            pltpu.SemaphoreType.DMA((1,)),
            pltpu.SemaphoreType.DMA((2,)),
        ],
    )
    return pl.pallas_call(
        _expert_kernel,
        grid_spec=grid_spec,
        out_shape=jax.ShapeDtypeStruct((n_rows, D_MODEL), F32),
        compiler_params=_params(1),
        name="expert_ffn",
    )(unit_e, unit_blk0, unit_nblk, unit_counts, xs,
      w_in, w_out, b_in[:, None, :], b_out[:, None, :])


COMB_TM = 256


def _combine_kernel(dcur_ref, dnxt_ref, x1_ref, gate_ref, g_ref, ys_ref, o_ref, ybuf, sem):
    i = pl.program_id(0)
    n = pl.num_programs(0)
    slot = lax.rem(i, 2)

    def issue(dref, s):
        def row(t, c):
            for k in range(TOP_K):
                d = dref[0, t * TOP_K + k]
                pltpu.make_async_copy(ys_ref.at[pl.ds(d, 1)], ybuf.at[s, k, pl.ds(t, 1)],
                                      sem.at[s]).start()
            return c

        lax.fori_loop(0, COMB_TM, row, 0)

    @pl.when(i == 0)
    def _():
        issue(dcur_ref, slot)

    @pl.when(i + 1 < n)
    def _():
        issue(dnxt_ref, 1 - slot)

    for k in range(TOP_K):
        pltpu.make_async_copy(ys_ref.at[pl.ds(0, COMB_TM)], ybuf.at[slot, k], sem.at[slot]).wait()

    gate = gate_ref[...]
    y = x1_ref[...]
    for k in range(TOP_K):
        y = y + gate[:, k:k + 1] * ybuf[slot, k]
    ms = jnp.mean(y * y, axis=-1, keepdims=True)
    o_ref[...] = (y * lax.rsqrt(ms + NORM_EPS)) * g_ref[...]


def _combine(dest, x1, gate, norm_g, ys):
    s = x1.shape[0]
    nt = s // COMB_TM
    dest3 = dest.reshape(nt, 1, COMB_TM * TOP_K)
    smem = lambda imap: pl.BlockSpec((None, 1, COMB_TM * TOP_K), imap, memory_space=pltpu.SMEM)
    row = lambda w: pl.BlockSpec((COMB_TM, w), lambda i: (i, 0))
    return pl.pallas_call(
        _combine_kernel,
        grid=(nt,),
        in_specs=[smem(lambda i: (i, 0, 0)),
                  smem(lambda i: (jnp.minimum(i + 1, nt - 1), 0, 0)),
                  row(D_MODEL), row(TOP_K), pl.BlockSpec((1, D_MODEL), lambda i: (0, 0)),
                  pl.BlockSpec(memory_space=pl.ANY)],
        out_specs=row(D_MODEL),
        out_shape=jax.ShapeDtypeStruct((s, D_MODEL), F32),
        scratch_shapes=[pltpu.VMEM((2, TOP_K, COMB_TM, D_MODEL), F32),
                        pltpu.SemaphoreType.DMA((2,))],
        compiler_params=_params(1),
        name="moe_combine",
    )(dest3, dest3, x1, gate, norm_g, ys)


def _route_tables(idx, pos, counts, n_tok):
    n_assign = n_tok * TOP_K
    counts = counts.reshape(N_EXPERTS).astype(I32)
    nblk_e = (counts + ROW_BLK - 1) // ROW_BLK
    pad_start = (jnp.cumsum(nblk_e) - nblk_e) * ROW_BLK
    onehot = idx[..., None] == jnp.arange(N_EXPERTS, dtype=I32)
    dest = pos + jnp.sum(jnp.where(onehot, pad_start, 0), axis=-1)

    n_blocks = -(-(n_assign + N_EXPERTS * (ROW_BLK - 1)) // ROW_BLK)
    used = jnp.sum(nblk_e)
    blk_ids = jnp.arange(n_blocks, dtype=I32)
    last_blk = pad_start // ROW_BLK + nblk_e - 1
    fill_blocks = jnp.concatenate([jnp.where(nblk_e > 0, last_blk, -1),
                                   jnp.where(blk_ids >= used, blk_ids, -1)]).astype(I32)

    n_units_max = (n_blocks + (UNIT_BLOCKS - 1) * N_EXPERTS) // UNIT_BLOCKS
    chunks_e = (nblk_e + UNIT_BLOCKS - 1) // UNIT_BLOCKS
    chunk_end = jnp.cumsum(chunks_e)
    chunk_start = chunk_end - chunks_e
    n_units = chunk_end[-1]
    uid = jnp.arange(n_units_max, dtype=I32)
    unit_e = jnp.minimum(jnp.searchsorted(chunk_end, uid, side="right"), N_EXPERTS - 1).astype(I32)
    c_in_e = uid - chunk_start[unit_e]
    unit_blk0 = (pad_start[unit_e] // ROW_BLK + c_in_e * UNIT_BLOCKS).astype(I32)
    unit_nblk = jnp.clip(nblk_e[unit_e] - c_in_e * UNIT_BLOCKS, 0, UNIT_BLOCKS).astype(I32)
    active = uid < n_units
    unit_blk0 = jnp.where(active, unit_blk0, 0)
    unit_nblk = jnp.where(active, unit_nblk, 0)
    unit_counts = jnp.stack([n_units, used]).astype(I32)
    return dest.astype(I32), fill_blocks, n_blocks * ROW_BLK, unit_e, unit_blk0, unit_nblk, unit_counts


def kernel(x, norm1_g, w_in, b_gate, conv_w, conv_b, conv_ln_g, conv_ln_b, w_conv_out, b_conv_out,
           w_attn_out, w_out, norm2_g, w_router, b_router, w_exp_in, b_exp_in, w_exp_out, b_exp_out,
           norm_f_g):
    b_sz, s_len, d = x.shape
    n_tok = b_sz * s_len
    x2 = x.reshape(n_tok, d)
    l = 0
    b_gate_ext = jnp.concatenate([jnp.zeros((GATE_COL0,), F32), b_gate[l]])[None, :]
    proj3 = _in_projection(x2, norm1_g[l][None, :], w_in[l], b_gate_ext)
    u_ln = _conv_branch(proj3, conv_w[l], conv_b[l][None, :], conv_ln_g[l][None, :],
                        conv_ln_b[l][None, :])
    outs, lses = [], []
    for g in range(N_GROUPS):
        o_g, lse_g = _dilated_attention(proj3, g)
        outs.append(o_g)
        lses.append(lse_g)
    wr = w_router[l]
    wr_hi = wr.astype(BF16)
    wr_lo = (wr - wr_hi.astype(F32)).astype(BF16)
    x1, hp, idx, gate, pos, counts = _mixer_output(
        x2, outs, lses, u_ln, proj3,
        w_attn_out[l].astype(BF16), w_conv_out[l].astype(BF16), b_conv_out[l][None, :],
        w_out[l].astype(BF16), norm2_g[l][None, :], wr_hi, wr_lo, b_router[l][None, :])

    dest, fill_blocks, n_rows, unit_e, unit_blk0, unit_nblk, unit_counts = _route_tables(
        idx, pos, counts, n_tok)
    xs = _dispatch(hp, dest, fill_blocks, n_rows)
    ys = _expert_ffn(xs, unit_e, unit_blk0, unit_nblk, unit_counts,
                     w_exp_in[l], b_exp_in[l], w_exp_out[l], b_exp_out[l])
    out = _combine(dest, x1, gate, norm_f_g[None, :], ys)
    return out.reshape(b_sz, s_len, d)
```

```python
import functools

import jax
import jax.numpy as jnp
from jax import lax
from jax.experimental import pallas as pl
from jax.experimental.pallas import tpu as pltpu

D_MODEL = 2048
SEQ = 8192
D_CONV = D_MODEL // 2
CONV_WIDTH = 31
HEAD_DIM = 128
HEADS_PER_GROUP = 4
ATTN_PATTERNS = ((128, 1), (512, 4), (2048, 16))
N_GROUPS = len(ATTN_PATTERNS)
ATTN_WIDTH = N_GROUPS * HEADS_PER_GROUP * HEAD_DIM
ATTN_OUT_WIDTH = HEADS_PER_GROUP * HEAD_DIM
ATTN_BLOCK = 128
D_IN = 2 * D_CONV + 3 * ATTN_WIDTH + 2 * D_MODEL
N_EXPERTS = 32
TOP_K = 4
D_EXPERT = D_MODEL
SWIGLU_LIMIT = 7.0
SWIGLU_ALPHA = 1.702
NORM_EPS = 1e-5

CB = 512
N_CB = D_IN // CB
CB_CONV_A, CB_CONV_G = 0, 2
CB_Q, CB_K, CB_V = 4, 7, 10
CB_GATE_C, CB_GATE_A = 13, 17
GATE_COL0 = CB_GATE_C * CB

VMEM_LIMIT = 56 * 1024 * 1024

F32 = jnp.float32
BF16 = jnp.bfloat16
U32 = jnp.uint32
I32 = jnp.int32

PACK_W = D_MODEL // 2
HI_MASK = 0xFFFF0000


def _sigmoid(z):
    return 1.0 / (1.0 + jnp.exp(-z))


def _pack_bf16_pair(lo, hi):
    lo_bits = lax.bitcast_convert_type(lo.astype(BF16).astype(F32), U32) >> 16
    hi_bits = lax.bitcast_convert_type(hi.astype(BF16).astype(F32), U32) & jnp.uint32(HI_MASK)
    return hi_bits | lo_bits


def _unpack_bf16_pair(w):
    lo = lax.bitcast_convert_type(w << 16, F32).astype(BF16)
    hi = lax.bitcast_convert_type(w & jnp.uint32(HI_MASK), F32).astype(BF16)
    return lo, hi


def _params(n_axes, vmem=VMEM_LIMIT):
    return pltpu.CompilerParams(
        dimension_semantics=("arbitrary",) * n_axes, vmem_limit_bytes=vmem)


IN_TM = 512
IN_TN = 1536


def _inproj_kernel(x_ref, g_ref, w_ref, bg_ref, o_ref, wbf_ref):
    j = pl.program_id(0)
    i = pl.program_id(1)

    @pl.when(i == 0)
    def _():
        wbf_ref[...] = w_ref[...].astype(BF16)

    x = x_ref[...]
    ms = jnp.mean(x * x, axis=-1, keepdims=True)
    h = ((x * lax.rsqrt(ms + NORM_EPS)) * g_ref[...]).astype(BF16)
    acc = jnp.dot(h, wbf_ref[...], preferred_element_type=F32)

    first_gate_tile = GATE_COL0 // IN_TN
    n_sub = IN_TN // CB

    def store(val):
        for c in range(n_sub):
            o_ref[c] = val[:, c * CB:(c + 1) * CB].astype(BF16)

    @pl.when(j < first_gate_tile)
    def _():
        store(acc)

    @pl.when(j == first_gate_tile)
    def _():
        col = j * IN_TN + lax.broadcasted_iota(I32, (1, IN_TN), 1)
        store(jnp.where(col >= GATE_COL0, _sigmoid(acc + bg_ref[...]), acc))

    @pl.when(j > first_gate_tile)
    def _():
        store(_sigmoid(acc + bg_ref[...]))


def _in_projection(x2, norm_g, w_in, b_gate_ext):
    s = x2.shape[0]
    grid = (D_IN // IN_TN, s // IN_TM)
    return pl.pallas_call(
        _inproj_kernel,
        grid=grid,
        in_specs=[
            pl.BlockSpec((IN_TM, D_MODEL), lambda j, i: (i, 0)),
            pl.BlockSpec((1, D_MODEL), lambda j, i: (0, 0)),
            pl.BlockSpec((D_MODEL, IN_TN), lambda j, i: (0, j)),
            pl.BlockSpec((1, IN_TN), lambda j, i: (0, j)),
        ],
        out_specs=pl.BlockSpec((IN_TN // CB, IN_TM, CB), lambda j, i: (j, i, 0)),
        out_shape=jax.ShapeDtypeStruct((N_CB, s, CB), BF16),
        scratch_shapes=[pltpu.VMEM((D_MODEL, IN_TN), BF16)],
        compiler_params=_params(2),
        name="in_projection",
    )(x2, norm_g, w_in, b_gate_ext)


CONV_TS = 256
CONV_HALO = 32
CONV_RC = 32
CONV_N = CONV_TS + CONV_HALO


def _conv_kernel(ac_ref, gc_ref, ah_ref, gh_ref, w_ref, cb_ref, lg_ref, lb_ref, o_ref, r_ref):
    i = pl.program_id(0)
    has_prev = i > 0
    half = D_CONV // 2
    for c in range(2):
        cs = slice(c * half, (c + 1) * half)
        uh = ah_ref[c].astype(F32) * _sigmoid(gh_ref[c].astype(F32))
        r_ref[0, 0:CONV_HALO, cs] = jnp.where(has_prev, uh, 0.0)
        r_ref[0, CONV_HALO:CONV_N, cs] = ac_ref[c].astype(F32) * _sigmoid(gc_ref[c].astype(F32))
    u_ext = r_ref[0]
    for b in range(1, 8):
        r_ref[b] = pltpu.roll(u_ext, CONV_N - b, axis=0)

    def chunk(ci, carry):
        r0 = pl.multiple_of(ci * CONV_RC, CONV_RC)
        acc = jnp.broadcast_to(cb_ref[...], (CONV_RC, D_CONV))
        for k in range(CONV_WIDTH):
            kp = k + (CONV_HALO - (CONV_WIDTH - 1))
            a, b = kp // 8, kp % 8
            acc = acc + w_ref[k:k + 1, :] * r_ref[b, pl.ds(r0 + 8 * a, CONV_RC), :]
        mu = jnp.mean(acc, axis=-1, keepdims=True)
        d = acc - mu
        var = jnp.mean(d * d, axis=-1, keepdims=True)
        y = (d * lax.rsqrt(var + NORM_EPS)) * lg_ref[...] + lb_ref[...]
        o_ref[pl.ds(r0, CONV_RC), :] = (y * _sigmoid(y)).astype(BF16)
        return carry

    lax.fori_loop(0, CONV_TS // CONV_RC, chunk, 0)


def _conv_branch(proj3, conv_w, conv_b, ln_g, ln_b):
    s = proj3.shape[1]
    hb = CONV_TS // CONV_HALO
    cur = lambda blk: pl.BlockSpec((2, CONV_TS, CB), lambda i: (blk // 2, i, 0))
    halo = lambda blk: pl.BlockSpec(
        (2, CONV_HALO, CB), lambda i: (blk // 2, jnp.maximum(i * hb - 1, 0), 0))
    vec = pl.BlockSpec((1, D_CONV), lambda i: (0, 0))
    return pl.pallas_call(
        _conv_kernel,
        grid=(s // CONV_TS,),
        in_specs=[cur(CB_CONV_A), cur(CB_CONV_G), halo(CB_CONV_A), halo(CB_CONV_G),
                  pl.BlockSpec((CONV_WIDTH, D_CONV), lambda i: (0, 0)), vec, vec, vec],
        out_specs=pl.BlockSpec((CONV_TS, D_CONV), lambda i: (i, 0)),
        out_shape=jax.ShapeDtypeStruct((s, D_CONV), BF16),
        scratch_shapes=[pltpu.VMEM((8, CONV_N, D_CONV), F32)],
        compiler_params=_params(1),
        name="conv_branch",
    )(proj3, proj3, proj3, proj3, conv_w, conv_b, ln_g, ln_b)


def _attn_kernel(q_ref, kc_ref, kp_ref, vc_ref, vp_ref, o_ref, lse_ref, *, tq):
    n = pl.program_id(0)
    blk = ATTN_BLOCK
    scale = HEAD_DIM ** -0.5
    qi = lax.broadcasted_iota(I32, (blk, 2 * blk), 0)
    ki = lax.broadcasted_iota(I32, (blk, 2 * blk), 1)
    band = (ki >= qi) & (ki <= qi + blk)
    lane = lax.broadcasted_iota(I32, (blk, HEADS_PER_GROUP), 1)
    for j in range(tq // blk):
        mask = band
        if j == 0:
            mask = band & ((ki >= blk) | (n > 0))
        rows = slice(j * blk, (j + 1) * blk)
        lse_tile = jnp.zeros((blk, HEADS_PER_GROUP), F32)
        for hh in range(HEADS_PER_GROUP):
            cs = slice(hh * HEAD_DIM, (hh + 1) * HEAD_DIM)
            q = q_ref[rows, cs]
            if j == 0:
                k = jnp.concatenate([kp_ref[:, cs], kc_ref[0:blk, cs]], axis=0)
                v = jnp.concatenate([vp_ref[:, cs], vc_ref[0:blk, cs]], axis=0)
            else:
                k = kc_ref[(j - 1) * blk:(j + 1) * blk, cs]
                v = vc_ref[(j - 1) * blk:(j + 1) * blk, cs]
            s = lax.dot_general(q, k, (((1,), (1,)), ((), ())), preferred_element_type=F32) * scale
            s = jnp.where(mask, s, -jnp.inf)
            m = jnp.max(s, axis=-1, keepdims=True)
            p = jnp.exp(s - m)
            l = jnp.sum(p, axis=-1, keepdims=True)
            o = jnp.dot(p.astype(BF16), v, preferred_element_type=F32) / l
            o_ref[rows, cs] = o.astype(BF16)
            lse_tile = jnp.where(lane == hh, m + jnp.log(l), lse_tile)
        lse_ref[0, rows, :] = lse_tile


def _dilated_attention(proj3, group):
    _, dil = ATTN_PATTERNS[group]
    s = proj3.shape[1]
    m_len = s // dil
    tq = min(512, m_len)
    if dil == 1:
        view, cbs = proj3, (CB_Q + group, CB_K + group, CB_V + group)
    else:
        qkv = jnp.stack([proj3[CB_Q + group], proj3[CB_K + group], proj3[CB_V + group]])
        view, cbs = qkv.reshape(3, m_len, dil * CB), (0, 1, 2)
    tb = tq // ATTN_BLOCK
    cur = lambda cb: pl.BlockSpec((None, tq, CB), lambda n, r: (cb, n, r))
    prev = lambda cb: pl.BlockSpec(
        (None, ATTN_BLOCK, CB), lambda n, r: (cb, jnp.maximum(n * tb - 1, 0), r))
    o, lse = pl.pallas_call(
        functools.partial(_attn_kernel, tq=tq),
        grid=(m_len // tq, dil),
        in_specs=[cur(cbs[0]), cur(cbs[1]), prev(cbs[1]), cur(cbs[2]), prev(cbs[2])],
        out_specs=[pl.BlockSpec((tq, CB), lambda n, r: (n, r)),
                   pl.BlockSpec((1, tq, HEADS_PER_GROUP), lambda n, r: (r, n, 0))],
        out_shape=[jax.ShapeDtypeStruct((m_len, dil * CB), BF16),
                   jax.ShapeDtypeStruct((dil, m_len, HEADS_PER_GROUP), F32)],
        compiler_params=_params(2),
        name=f"dilated_attention_g{group}",
    )(view, view, view, view, view)
    o = o.reshape(s, CB)
    lse = jnp.transpose(lse, (1, 0, 2)).reshape(s, HEADS_PER_GROUP)
    return o, lse


MIX_TM = 256


def _mixer_out_kernel(x_ref, o0_ref, o1_ref, o2_ref, l0_ref, l1_ref, l2_ref, u_ref,
                      gc0, gc1, gc2, gc3, ga0, ga1, ga2, ga3,
                      wa_ref, wc_ref, bc_ref, wo_ref, g2_ref, wrh_ref, wrl_ref, br_ref,
                      x1_ref, hp_ref, idx_ref, gate_ref, pos_ref, cnt_ref, carry_ref):
    i = pl.program_id(0)
    tm = MIX_TM

    @pl.when(i == 0)
    def _():
        carry_ref[...] = jnp.zeros_like(carry_ref)

    l0, l1, l2 = l0_ref[...], l1_ref[...], l2_ref[...]
    m = jnp.maximum(jnp.maximum(l0, l1), l2)
    e0, e1, e2 = jnp.exp(l0 - m), jnp.exp(l1 - m), jnp.exp(l2 - m)
    den = e0 + e1 + e2
    mix = (e0 / den, e1 / den, e2 / den)
    o_refs = (o0_ref, o1_ref, o2_ref)
    parts = []
    for hh in range(HEADS_PER_GROUP):
        cs = slice(hh * HEAD_DIM, (hh + 1) * HEAD_DIM)
        acc = mix[0][:, hh:hh + 1] * o_refs[0][:, cs].astype(F32)
        for g in range(1, N_GROUPS):
            acc = acc + mix[g][:, hh:hh + 1] * o_refs[g][:, cs].astype(F32)
        parts.append(acc)
    o = jnp.concatenate(parts, axis=1).astype(BF16)
    ya = jnp.dot(o, wa_ref[...], preferred_element_type=F32)
    yc = jnp.dot(u_ref[...], wc_ref[...], preferred_element_type=F32) + bc_ref[...]
    gcs = (gc0, gc1, gc2, gc3)
    gas = (ga0, ga1, ga2, ga3)
    merged = []
    for c in range(D_MODEL // CB):
        cs = slice(c * CB, (c + 1) * CB)
        merged.append((gcs[c][...].astype(F32) * yc[:, cs]
                       + gas[c][...].astype(F32) * ya[:, cs]).astype(BF16))
    merged = jnp.concatenate(merged, axis=1)
    x1 = x_ref[...] + jnp.dot(merged, wo_ref[...], preferred_element_type=F32)
    x1_ref[...] = x1
    ms = jnp.mean(x1 * x1, axis=-1, keepdims=True)
    h2 = (x1 * lax.rsqrt(ms + NORM_EPS)) * g2_ref[...]
    h2_hi = h2.astype(BF16)
    hp_ref[...] = _pack_bf16_pair(h2[:, 0:PACK_W], h2[:, PACK_W:D_MODEL])
    h2_lo = (h2 - h2_hi.astype(F32)).astype(BF16)
    logits = (jnp.dot(h2_hi, wrh_ref[...], preferred_element_type=F32)
              + jnp.dot(h2_hi, wrl_ref[...], preferred_element_type=F32)
              + jnp.dot(h2_lo, wrh_ref[...], preferred_element_type=F32)) + br_ref[...]

    e_iota = lax.broadcasted_iota(I32, (tm, N_EXPERTS), 1).astype(F32)
    k_lane = lax.broadcasted_iota(I32, (tm, TOP_K), 1)
    vals = logits
    sels = []
    idx_t = jnp.zeros((tm, TOP_K), F32)
    val_t = jnp.zeros((tm, TOP_K), F32)
    for k in range(TOP_K):
        mk = jnp.max(vals, axis=-1, keepdims=True)
        ik = jnp.min(jnp.where(vals == mk, e_iota, float(N_EXPERTS)), axis=-1, keepdims=True)
        sel = e_iota == ik
        sels.append(sel)
        vals = jnp.where(sel, -jnp.inf, vals)
        idx_t = jnp.where(k_lane == k, ik, idx_t)
        val_t = jnp.where(k_lane == k, mk, val_t)
    ex = jnp.exp(val_t - val_t[:, 0:1])
    gate_ref[...] = ex / jnp.sum(ex, axis=-1, keepdims=True)
    idx_ref[...] = idx_t.astype(I32)

    cnt = jnp.zeros((tm, N_EXPERTS), F32)
    for sel in sels:
        cnt = cnt + jnp.where(sel, 1.0, 0.0)
    ri = lax.broadcasted_iota(I32, (tm, tm), 0)
    ci = lax.broadcasted_iota(I32, (tm, tm), 1)
    lower = jnp.where(ci < ri, 1.0, 0.0).astype(BF16)
    prefix = jnp.dot(lower, cnt.astype(BF16), preferred_element_type=F32) + carry_ref[...]
    pos_t = jnp.zeros((tm, TOP_K), F32)
    for k, sel in enumerate(sels):
        pk = jnp.sum(jnp.where(sel, prefix, 0.0), axis=-1, keepdims=True)
        pos_t = jnp.where(k_lane == k, pk, pos_t)
    pos_ref[...] = pos_t.astype(I32)
    carry_ref[...] = carry_ref[...] + jnp.sum(cnt, axis=0, keepdims=True)
    cnt_ref[...] = carry_ref[...]


def _mixer_output(x2, outs, lses, u_ln, proj3, wa, wc, bc, wo, g2, wr_hi, wr_lo, br):
    s = x2.shape[0]
    row = lambda w: pl.BlockSpec((MIX_TM, w), lambda i: (i, 0))
    gate = lambda cb: pl.BlockSpec((None, MIX_TM, CB), lambda i: (cb, i, 0))
    full = lambda a: pl.BlockSpec(a.shape, lambda i: (0,) * a.ndim)
    in_specs = ([row(D_MODEL)] + [row(CB)] * 3 + [row(HEADS_PER_GROUP)] * 3 + [row(D_CONV)]
                + [gate(CB_GATE_C + c) for c in range(4)]
                + [gate(CB_GATE_A + c) for c in range(4)]
                + [full(a) for a in (wa, wc, bc, wo, g2, wr_hi, wr_lo, br)])
    return pl.pallas_call(
        _mixer_out_kernel,
        grid=(s // MIX_TM,),
        in_specs=in_specs,
        out_specs=[row(D_MODEL), row(PACK_W), row(TOP_K), row(TOP_K), row(TOP_K),
                   pl.BlockSpec((1, N_EXPERTS), lambda i: (0, 0))],
        out_shape=[jax.ShapeDtypeStruct((s, D_MODEL), F32),
                   jax.ShapeDtypeStruct((s, PACK_W), U32),
                   jax.ShapeDtypeStruct((s, TOP_K), I32),
                   jax.ShapeDtypeStruct((s, TOP_K), F32),
                   jax.ShapeDtypeStruct((s, TOP_K), I32),
                   jax.ShapeDtypeStruct((1, N_EXPERTS), F32)],
        scratch_shapes=[pltpu.VMEM((1, N_EXPERTS), F32)],
        compiler_params=_params(1),
        name="mixer_output",
    )(x2, *outs, *lses, u_ln, *([proj3] * 8), wa, wc, bc, wo, g2, wr_hi, wr_lo, br)


ROW_BLK = 256
DISP_TM = 256


def _dispatch_kernel(fill_ref, dest_ref, h_ref, xs_ref, zbuf, sem):
    i = pl.program_id(0)
    n_fill = fill_ref.shape[0]

    @pl.when(i == 0)
    def _():
        zbuf[...] = jnp.zeros_like(zbuf)

        def fill_copy(j):
            return pltpu.make_async_copy(
                zbuf, xs_ref.at[pl.ds(fill_ref[j] * ROW_BLK, ROW_BLK)], sem.at[0])

        def start(j, c):
            @pl.when(fill_ref[j] >= 0)
            def _():
                fill_copy(j).start()
            return c

        def wait(j, c):
            @pl.when(fill_ref[j] >= 0)
            def _():
                fill_copy(j).wait()
            return c

        lax.fori_loop(0, n_fill, start, 0)
        lax.fori_loop(0, n_fill, wait, 0)

    def row(t, c):
        for k in range(TOP_K):
            d = dest_ref[0, t * TOP_K + k]
            pltpu.make_async_copy(h_ref.at[pl.ds(t, 1)], xs_ref.at[pl.ds(d, 1)], sem.at[0]).start()
        return c

    lax.fori_loop(0, DISP_TM, row, 0)
    for _ in range(TOP_K):
        pltpu.make_async_copy(h_ref, xs_ref.at[pl.ds(0, DISP_TM)], sem.at[0]).wait()


def _dispatch(hp, dest, fill_blocks, n_rows):
    s = hp.shape[0]
    dest3 = dest.reshape(s // DISP_TM, 1, DISP_TM * TOP_K)
    grid_spec = pltpu.PrefetchScalarGridSpec(
        num_scalar_prefetch=1,
        grid=(s // DISP_TM,),
        in_specs=[
            pl.BlockSpec((None, 1, DISP_TM * TOP_K), lambda i, fr: (i, 0, 0),
                         memory_space=pltpu.SMEM),
            pl.BlockSpec((DISP_TM, PACK_W), lambda i, fr: (i, 0)),
        ],
        out_specs=pl.BlockSpec(memory_space=pl.ANY),
        scratch_shapes=[pltpu.VMEM((ROW_BLK, PACK_W), U32), pltpu.SemaphoreType.DMA((1,))],
    )
    return pl.pallas_call(
        _dispatch_kernel,
        grid_spec=grid_spec,
        out_shape=jax.ShapeDtypeStruct((n_rows, PACK_W), U32),
        compiler_params=_params(1),
        name="moe_dispatch",
    )(fill_blocks, dest3, hp)


UNIT_BLOCKS = 6
UNIT_ROWS = ROW_BLK * UNIT_BLOCKS
EXP_TF = 256
EXP_NF = D_EXPERT // EXP_TF


def _expert_kernel(ue_ref, ub0_ref, unb_ref, nu_ref,
                   xs_ref, win_ref, wout_ref, bin_ref, bo_ref,
                   ys_ref,
                   xbuf, xb16, acc, wg_buf, wu_buf, wo_buf, sem_x, sem_y, sem_w):
    u = pl.program_id(0)
    n_units = nu_ref[0]

    def x_copy(unit, slot, b):
        src = xs_ref.at[pl.ds((ub0_ref[unit] + b) * ROW_BLK, ROW_BLK)]
        dst = xbuf.at[slot, pl.ds(b * ROW_BLK, ROW_BLK)]
        return pltpu.make_async_copy(src, dst, sem_x.at[slot])

    def y_copy(unit, b):
        src = acc.at[pl.ds(b * ROW_BLK, ROW_BLK)]
        dst = ys_ref.at[pl.ds((ub0_ref[unit] + b) * ROW_BLK, ROW_BLK)]
        return pltpu.make_async_copy(src, dst, sem_y.at[0])

    def w_copies(unit, f, slot):
        e = ue_ref[unit]
        c0 = pl.multiple_of(f * EXP_TF, EXP_TF)
        return (
            pltpu.make_async_copy(win_ref.at[e, :, pl.ds(c0, EXP_TF)], wg_buf.at[slot],
                                  sem_w.at[slot]),
            pltpu.make_async_copy(win_ref.at[e, :, pl.ds(D_EXPERT + c0, EXP_TF)], wu_buf.at[slot],
                                  sem_w.at[slot]),
            pltpu.make_async_copy(wout_ref.at[e, pl.ds(c0, EXP_TF), :], wo_buf.at[slot],
                                  sem_w.at[slot]),
        )

    def start_w(unit, f, slot):
        for cp in w_copies(unit, f, slot):
            cp.start()

    def wait_w(unit, f, slot):
        for cp in w_copies(unit, f, slot):
            cp.wait()

    def start_x(unit, slot):
        lax.fori_loop(0, unb_ref[unit], lambda b, c: (x_copy(unit, slot, b).start(), c)[1], 0)

    def wait_x(unit, slot):
        lax.fori_loop(0, unb_ref[unit], lambda b, c: (x_copy(unit, slot, b).wait(), c)[1], 0)

    def wait_y(unit):
        lax.fori_loop(0, unb_ref[unit], lambda b, c: (y_copy(unit, b).wait(), c)[1], 0)

    @pl.when(u < n_units)
    def _():
        xslot = lax.rem(u, 2)
        nblk = unb_ref[u]

        @pl.when(u == 0)
        def _():
            start_x(u, xslot)
            start_w(u, 0, 0)

        wait_x(u, xslot)

        @pl.when(u + 1 < n_units)
        def _():
            start_x(u + 1, 1 - xslot)

        @pl.when(u > 0)
        def _():
            wait_y(u - 1)

        def init(b, c):
            r0 = pl.multiple_of(b * ROW_BLK, ROW_BLK)
            lo, hi = _unpack_bf16_pair(xbuf[xslot, pl.ds(r0, ROW_BLK), :])
            xb16[pl.ds(r0, ROW_BLK), 0:PACK_W] = lo
            xb16[pl.ds(r0, ROW_BLK), PACK_W:D_MODEL] = hi
            acc[pl.ds(r0, ROW_BLK), :] = jnp.broadcast_to(bo_ref[...], (ROW_BLK, D_MODEL))
            return c

        lax.fori_loop(0, nblk, init, 0)

        def group(b0, n_blk, f, ws):
            rows = n_blk * ROW_BLK
            r0 = pl.multiple_of(b0 * ROW_BLK, ROW_BLK)
            c0 = pl.multiple_of(f * EXP_TF, EXP_TF)
            x = xb16[pl.ds(r0, rows), :]
            wgu = jnp.concatenate([wg_buf[ws].astype(BF16), wu_buf[ws].astype(BF16)], axis=1)
            gu = jnp.dot(x, wgu, preferred_element_type=F32)
            bg = bin_ref[:, pl.ds(c0, EXP_TF)]
            bu = bin_ref[:, pl.ds(D_EXPERT + c0, EXP_TF)]
            g = jnp.minimum(gu[:, 0:EXP_TF] + bg, SWIGLU_LIMIT)
            up = jnp.clip(gu[:, EXP_TF:2 * EXP_TF] + bu, -SWIGLU_LIMIT, SWIGLU_LIMIT)
            act = (up + 1.0) * (g * _sigmoid(SWIGLU_ALPHA * g))
            acc[pl.ds(r0, rows), :] += jnp.dot(
                act.astype(BF16), wo_buf[ws].astype(BF16), preferred_element_type=F32)

            @pl.when(f == EXP_NF - 1)
            def _():
                for j in range(n_blk):
                    y_copy(u, b0 + j).start()

        def f_tile(f, ws):
            n_quad = nblk // 4
            rem = nblk - 4 * n_quad
            lax.fori_loop(0, n_quad, lambda q, c: (group(4 * q, 4, f, ws), c)[1], 0)

            @pl.when(rem >= 2)
            def _():
                group(4 * n_quad, 2, f, ws)

            @pl.when(lax.rem(rem, 2) == 1)
            def _():
                group(nblk - 1, 1, f, ws)

        def f_step(f, c):
            ws = lax.rem(f, 2)

            @pl.when(f + 1 < EXP_NF)
            def _():
                start_w(u, f + 1, 1 - ws)

            @pl.when((f + 1 == EXP_NF) & (u + 1 < n_units))
            def _():
                start_w(u + 1, 0, 1 - ws)

            wait_w(u, f, ws)
            f_tile(f, ws)
            return c

        lax.fori_loop(0, EXP_NF, f_step, 0)

        @pl.when(u == n_units - 1)
        def _():
            wait_y(u)

    @pl.when(u == pl.num_programs(0) - 1)
    def _():
        n_blocks = ys_ref.shape[0] // ROW_BLK
        acc[0:ROW_BLK, :] = jnp.zeros((ROW_BLK, D_MODEL), F32)

        def tail_copy(b):
            return pltpu.make_async_copy(acc.at[pl.ds(0, ROW_BLK)],
                                         ys_ref.at[pl.ds(b * ROW_BLK, ROW_BLK)], sem_y.at[0])

        lax.fori_loop(nu_ref[1], n_blocks, lambda b, c: (tail_copy(b).start(), c)[1], 0)
        lax.fori_loop(nu_ref[1], n_blocks, lambda b, c: (tail_copy(b).wait(), c)[1], 0)


def _expert_ffn(xs, unit_e, unit_blk0, unit_nblk, unit_counts, w_in, b_in, w_out, b_out):
    n_rows = xs.shape[0]
    n_units_max = unit_e.shape[0]

    def e_eff(u, ue, nu):
        return ue[jnp.minimum(u, jnp.maximum(nu[0] - 1, 0))]

    any_spec = pl.BlockSpec(memory_space=pl.ANY)
    in_specs = [
        any_spec, any_spec, any_spec,
        pl.BlockSpec((None, 1, 2 * D_EXPERT), lambda u, ue, ub, un, nu: (e_eff(u, ue, nu), 0, 0)),
        pl.BlockSpec((None, 1, D_MODEL), lambda u, ue, ub, un, nu: (e_eff(u, ue, nu), 0, 0)),
    ]
    grid_spec = pltpu.PrefetchScalarGridSpec(
        num_scalar_prefetch=4,
        grid=(n_units_max,),
        in_specs=in_specs,
        out_specs=any_spec,
        scratch_shapes=[
            pltpu.VMEM((2, UNIT_ROWS, PACK_W), U32),
            pltpu.VMEM((UNIT_ROWS, D_MODEL), BF16),
            pltpu.VMEM((UNIT_ROWS, D_MODEL), F32),
            pltpu.VMEM((2, D_MODEL, EXP_TF), F32),
            pltpu.VMEM((2, D_MODEL, EXP_TF), F32),
            pltpu.VMEM((2, EXP_TF, D_MODEL), F32),
            pltpu.SemaphoreType.DMA((2,)),
            pltpu.SemaphoreType.DMA((1,)),
            pltpu.SemaphoreType.DMA((2,)),
        ],
    )
    return pl.pallas_call(
        _expert_kernel,
        grid_spec=grid_spec,
        out_shape=jax.ShapeDtypeStruct((n_rows, D_MODEL), F32),
        compiler_params=_params(1),
        name="expert_ffn",
    )(unit_e, unit_blk0, unit_nblk, unit_counts, xs,
      w_in, w_out, b_in[:, None, :], b_out[:, None, :])


COMB_TM = 256


def _combine_kernel(dcur_ref, dnxt_ref, x1_ref, gate_ref, g_ref, ys_ref, o_ref, ybuf, sem):
    i = pl.program_id(0)
    n = pl.num_programs(0)
    slot = lax.rem(i, 2)

    def issue(dref, s):
        def row(t, c):
            for k in range(TOP_K):
                d = dref[0, t * TOP_K + k]
                pltpu.make_async_copy(ys_ref.at[pl.ds(d, 1)], ybuf.at[s, k, pl.ds(t, 1)],
                                      sem.at[s]).start()
            return c

        lax.fori_loop(0, COMB_TM, row, 0)

    @pl.when(i == 0)
    def _():
        issue(dcur_ref, slot)

    @pl.when(i + 1 < n)
    def _():
        issue(dnxt_ref, 1 - slot)

    for k in range(TOP_K):
        pltpu.make_async_copy(ys_ref.at[pl.ds(0, COMB_TM)], ybuf.at[slot, k], sem.at[slot]).wait()

    gate = gate_ref[...]
    y = x1_ref[...]
    for k in range(TOP_K):
        y = y + gate[:, k:k + 1] * ybuf[slot, k]
    ms = jnp.mean(y * y, axis=-1, keepdims=True)
    o_ref[...] = (y * lax.rsqrt(ms + NORM_EPS)) * g_ref[...]


def _combine(dest, x1, gate, norm_g, ys):
    s = x1.shape[0]
    nt = s // COMB_TM
    dest3 = dest.reshape(nt, 1, COMB_TM * TOP_K)
    smem = lambda imap: pl.BlockSpec((None, 1, COMB_TM * TOP_K), imap, memory_space=pltpu.SMEM)
    row = lambda w: pl.BlockSpec((COMB_TM, w), lambda i: (i, 0))
    return pl.pallas_call(
        _combine_kernel,
        grid=(nt,),
        in_specs=[smem(lambda i: (i, 0, 0)),
                  smem(lambda i: (jnp.minimum(i + 1, nt - 1), 0, 0)),
                  row(D_MODEL), row(TOP_K), pl.BlockSpec((1, D_MODEL), lambda i: (0, 0)),
                  pl.BlockSpec(memory_space=pl.ANY)],
        out_specs=row(D_MODEL),
        out_shape=jax.ShapeDtypeStruct((s, D_MODEL), F32),
        scratch_shapes=[pltpu.VMEM((2, TOP_K, COMB_TM, D_MODEL), F32),
                        pltpu.SemaphoreType.DMA((2,))],
        compiler_params=_params(1),
        name="moe_combine",
    )(dest3, dest3, x1, gate, norm_g, ys)


def _route_tables(idx, pos, counts, n_tok):
    n_assign = n_tok * TOP_K
    counts = counts.reshape(N_EXPERTS).astype(I32)
    nblk_e = (counts + ROW_BLK - 1) // ROW_BLK
    pad_start = (jnp.cumsum(nblk_e) - nblk_e) * ROW_BLK
    onehot = idx[..., None] == jnp.arange(N_EXPERTS, dtype=I32)
    dest = pos + jnp.sum(jnp.where(onehot, pad_start, 0), axis=-1)

    n_blocks = -(-(n_assign + N_EXPERTS * (ROW_BLK - 1)) // ROW_BLK)
    used = jnp.sum(nblk_e)
    blk_ids = jnp.arange(n_blocks, dtype=I32)
    last_blk = pad_start // ROW_BLK + nblk_e - 1
    fill_blocks = jnp.concatenate([jnp.where(nblk_e > 0, last_blk, -1),
                                   jnp.where(blk_ids >= used, blk_ids, -1)]).astype(I32)

    n_units_max = (n_blocks + (UNIT_BLOCKS - 1) * N_EXPERTS) // UNIT_BLOCKS
    chunks_e = (nblk_e + UNIT_BLOCKS - 1) // UNIT_BLOCKS
    chunk_end = jnp.cumsum(chunks_e)
    chunk_start = chunk_end - chunks_e
    n_units = chunk_end[-1]
    uid = jnp.arange(n_units_max, dtype=I32)
    unit_e = jnp.minimum(jnp.searchsorted(chunk_end, uid, side="right"), N_EXPERTS - 1).astype(I32)
    c_in_e = uid - chunk_start[unit_e]
    unit_blk0 = (pad_start[unit_e] // ROW_BLK + c_in_e * UNIT_BLOCKS).astype(I32)
    unit_nblk = jnp.clip(nblk_e[unit_e] - c_in_e * UNIT_BLOCKS, 0, UNIT_BLOCKS).astype(I32)
    active = uid < n_units
    unit_blk0 = jnp.where(active, unit_blk0, 0)
    unit_nblk = jnp.where(active, unit_nblk, 0)
    unit_counts = jnp.stack([n_units, used]).astype(I32)
    return dest.astype(I32), fill_blocks, n_blocks * ROW_BLK, unit_e, unit_blk0, unit_nblk, unit_counts


def kernel(x, norm1_g, w_in, b_gate, conv_w, conv_b, conv_ln_g, conv_ln_b, w_conv_out, b_conv_out,
           w_attn_out, w_out, norm2_g, w_router, b_router, w_exp_in, b_exp_in, w_exp_out, b_exp_out,
           norm_f_g):
    b_sz, s_len, d = x.shape
    n_tok = b_sz * s_len
    x2 = x.reshape(n_tok, d)
    l = 0
    b_gate_ext = jnp.concatenate([jnp.zeros((GATE_COL0,), F32), b_gate[l]])[None, :]
    proj3 = _in_projection(x2, norm1_g[l][None, :], w_in[l], b_gate_ext)
    u_ln = _conv_branch(proj3, conv_w[l], conv_b[l][None, :], conv_ln_g[l][None, :],
                        conv_ln_b[l][None, :])
    outs, lses = [], []
    for g in range(N_GROUPS):
        o_g, lse_g = _dilated_attention(proj3, g)
        outs.append(o_g)
        lses.append(lse_g)
    wr = w_router[l]
    wr_hi = wr.astype(BF16)
    wr_lo = (wr - wr_hi.astype(F32)).astype(BF16)
    x1, hp, idx, gate, pos, counts = _mixer_output(
        x2, outs, lses, u_ln, proj3,
        w_attn_out[l].astype(BF16), w_conv_out[l].astype(BF16), b_conv_out[l][None, :],
        w_out[l].astype(BF16), norm2_g[l][None, :], wr_hi, wr_lo, b_router[l][None, :])

    dest, fill_blocks, n_rows, unit_e, unit_blk0, unit_nblk, unit_counts = _route_tables(
        idx, pos, counts, n_tok)
    xs = _dispatch(hp, dest, fill_blocks, n_rows)
    ys = _expert_ffn(xs, unit_e, unit_blk0, unit_nblk, unit_counts,
                     w_exp_in[l], b_exp_in[l], w_exp_out[l], b_exp_out[l])
    out = _combine(dest, x1, gate, norm_f_g[None, :], ys)
    return out.reshape(b_sz, s_len, d)
```

```python
import functools

import jax
import jax.numpy as jnp
from jax import lax
from jax.experimental import pallas as pl
from jax.experimental.pallas import tpu as pltpu

D_MODEL = 2048
SEQ = 8192
D_CONV = D_MODEL // 2
CONV_WIDTH = 31
HEAD_DIM = 128
HEADS_PER_GROUP = 4
ATTN_PATTERNS = ((128, 1), (512, 4), (2048, 16))
N_GROUPS = len(ATTN_PATTERNS)
ATTN_WIDTH = N_GROUPS * HEADS_PER_GROUP * HEAD_DIM
ATTN_OUT_WIDTH = HEADS_PER_GROUP * HEAD_DIM
ATTN_BLOCK = 128
D_IN = 2 * D_CONV + 3 * ATTN_WIDTH + 2 * D_MODEL
N_EXPERTS = 32
TOP_K = 4
D_EXPERT = D_MODEL
SWIGLU_LIMIT = 7.0
SWIGLU_ALPHA = 1.702
NORM_EPS = 1e-5

CB = 512
N_CB = D_IN // CB
CB_CONV_A, CB_CONV_G = 0, 2
CB_Q, CB_K, CB_V = 4, 7, 10
CB_GATE_C, CB_GATE_A = 13, 17
GATE_COL0 = CB_GATE_C * CB

VMEM_LIMIT = 56 * 1024 * 1024

F32 = jnp.float32
BF16 = jnp.bfloat16
U32 = jnp.uint32
I32 = jnp.int32

PACK_W = D_MODEL // 2
HI_MASK = 0xFFFF0000


def _sigmoid(z):
    return 1.0 / (1.0 + jnp.exp(-z))


def _pack_bf16_pair(lo, hi):
    lo_bits = lax.bitcast_convert_type(lo.astype(BF16).astype(F32), U32) >> 16
    hi_bits = lax.bitcast_convert_type(hi.astype(BF16).astype(F32), U32) & jnp.uint32(HI_MASK)
    return hi_bits | lo_bits


def _unpack_bf16_pair(w):
    lo = lax.bitcast_convert_type(w << 16, F32).astype(BF16)
    hi = lax.bitcast_convert_type(w & jnp.uint32(HI_MASK), F32).astype(BF16)
    return lo, hi


def _params(n_axes, vmem=VMEM_LIMIT):
    return pltpu.CompilerParams(
        dimension_semantics=("arbitrary",) * n_axes, vmem_limit_bytes=vmem)


IN_TM = 512
IN_TN = 1536


def _inproj_kernel(x_ref, g_ref, w_ref, bg_ref, o_ref, wbf_ref):
    j = pl.program_id(0)
    i = pl.program_id(1)

    @pl.when(i == 0)
    def _():
        wbf_ref[...] = w_ref[...].astype(BF16)

    x = x_ref[...]
    ms = jnp.mean(x * x, axis=-1, keepdims=True)
    h = ((x * lax.rsqrt(ms + NORM_EPS)) * g_ref[...]).astype(BF16)
    acc = jnp.dot(h, wbf_ref[...], preferred_element_type=F32)

    first_gate_tile = GATE_COL0 // IN_TN
    n_sub = IN_TN // CB

    def store(val):
        for c in range(n_sub):
            o_ref[c] = val[:, c * CB:(c + 1) * CB].astype(BF16)

    @pl.when(j < first_gate_tile)
    def _():
        store(acc)

    @pl.when(j == first_gate_tile)
    def _():
        col = j * IN_TN + lax.broadcasted_iota(I32, (1, IN_TN), 1)
        store(jnp.where(col >= GATE_COL0, _sigmoid(acc + bg_ref[...]), acc))

    @pl.when(j > first_gate_tile)
    def _():
        store(_sigmoid(acc + bg_ref[...]))


def _in_projection(x2, norm_g, w_in, b_gate_ext):
    s = x2.shape[0]
    grid = (D_IN // IN_TN, s // IN_TM)
    return pl.pallas_call(
        _inproj_kernel,
        grid=grid,
        in_specs=[
            pl.BlockSpec((IN_TM, D_MODEL), lambda j, i: (i, 0)),
            pl.BlockSpec((1, D_MODEL), lambda j, i: (0, 0)),
            pl.BlockSpec((D_MODEL, IN_TN), lambda j, i: (0, j)),
            pl.BlockSpec((1, IN_TN), lambda j, i: (0, j)),
        ],
        out_specs=pl.BlockSpec((IN_TN // CB, IN_TM, CB), lambda j, i: (j, i, 0)),
        out_shape=jax.ShapeDtypeStruct((N_CB, s, CB), BF16),
        scratch_shapes=[pltpu.VMEM((D_MODEL, IN_TN), BF16)],
        compiler_params=_params(2),
        name="in_projection",
    )(x2, norm_g, w_in, b_gate_ext)


CONV_TS = 256
CONV_HALO = 32
CONV_RC = 32
CONV_N = CONV_TS + CONV_HALO


def _conv_kernel(ac_ref, gc_ref, ah_ref, gh_ref, w_ref, cb_ref, lg_ref, lb_ref, o_ref, r_ref):
    i = pl.program_id(0)
    has_prev = i > 0
    half = D_CONV // 2
    for c in range(2):
        cs = slice(c * half, (c + 1) * half)
        uh = ah_ref[c].astype(F32) * _sigmoid(gh_ref[c].astype(F32))
        r_ref[0, 0:CONV_HALO, cs] = jnp.where(has_prev, uh, 0.0)
        r_ref[0, CONV_HALO:CONV_N, cs] = ac_ref[c].astype(F32) * _sigmoid(gc_ref[c].astype(F32))
    u_ext = r_ref[0]
    for b in range(1, 8):
        r_ref[b] = pltpu.roll(u_ext, CONV_N - b, axis=0)

    def chunk(ci, carry):
        r0 = pl.multiple_of(ci * CONV_RC, CONV_RC)
        acc = jnp.broadcast_to(cb_ref[...], (CONV_RC, D_CONV))
        for k in range(CONV_WIDTH):
            kp = k + (CONV_HALO - (CONV_WIDTH - 1))
            a, b = kp // 8, kp % 8
            acc = acc + w_ref[k:k + 1, :] * r_ref[b, pl.ds(r0 + 8 * a, CONV_RC), :]
        mu = jnp.mean(acc, axis=-1, keepdims=True)
        d = acc - mu
        var = jnp.mean(d * d, axis=-1, keepdims=True)
        y = (d * lax.rsqrt(var + NORM_EPS)) * lg_ref[...] + lb_ref[...]
        o_ref[pl.ds(r0, CONV_RC), :] = (y * _sigmoid(y)).astype(BF16)
        return carry

    lax.fori_loop(0, CONV_TS // CONV_RC, chunk, 0)


def _conv_branch(proj3, conv_w, conv_b, ln_g, ln_b):
    s = proj3.shape[1]
    hb = CONV_TS // CONV_HALO
    cur = lambda blk: pl.BlockSpec((2, CONV_TS, CB), lambda i: (blk // 2, i, 0))
    halo = lambda blk: pl.BlockSpec(
        (2, CONV_HALO, CB), lambda i: (blk // 2, jnp.maximum(i * hb - 1, 0), 0))
    vec = pl.BlockSpec((1, D_CONV), lambda i: (0, 0))
    return pl.pallas_call(
        _conv_kernel,
        grid=(s // CONV_TS,),
        in_specs=[cur(CB_CONV_A), cur(CB_CONV_G), halo(CB_CONV_A), halo(CB_CONV_G),
                  pl.BlockSpec((CONV_WIDTH, D_CONV), lambda i: (0, 0)), vec, vec, vec],
        out_specs=pl.BlockSpec((CONV_TS, D_CONV), lambda i: (i, 0)),
        out_shape=jax.ShapeDtypeStruct((s, D_CONV), BF16),
        scratch_shapes=[pltpu.VMEM((8, CONV_N, D_CONV), F32)],
        compiler_params=_params(1),
        name="conv_branch",
    )(proj3, proj3, proj3, proj3, conv_w, conv_b, ln_g, ln_b)


def _attn_kernel(q_ref, kc_ref, kp_ref, vc_ref, vp_ref, o_ref, lse_ref, *, tq):
    n = pl.program_id(0)
    blk = ATTN_BLOCK
    scale = HEAD_DIM ** -0.5
    qi = lax.broadcasted_iota(I32, (blk, 2 * blk), 0)
    ki = lax.broadcasted_iota(I32, (blk, 2 * blk), 1)
    band = (ki >= qi) & (ki <= qi + blk)
    lane = lax.broadcasted_iota(I32, (blk, HEADS_PER_GROUP), 1)
    for j in range(tq // blk):
        mask = band
        if j == 0:
            mask = band & ((ki >= blk) | (n > 0))
        rows = slice(j * blk, (j + 1) * blk)
        lse_tile = jnp.zeros((blk, HEADS_PER_GROUP), F32)
        for hh in range(HEADS_PER_GROUP):
            cs = slice(hh * HEAD_DIM, (hh + 1) * HEAD_DIM)
            q = q_ref[rows, cs]
            if j == 0:
                k = jnp.concatenate([kp_ref[:, cs], kc_ref[0:blk, cs]], axis=0)
                v = jnp.concatenate([vp_ref[:, cs], vc_ref[0:blk, cs]], axis=0)
            else:
                k = kc_ref[(j - 1) * blk:(j + 1) * blk, cs]
                v = vc_ref[(j - 1) * blk:(j + 1) * blk, cs]
            s = lax.dot_general(q, k, (((1,), (1,)), ((), ())), preferred_element_type=F32) * scale
            s = jnp.where(mask, s, -jnp.inf)
            m = jnp.max(s, axis=-1, keepdims=True)
            p = jnp.exp(s - m)
            l = jnp.sum(p, axis=-1, keepdims=True)
            o = jnp.dot(p.astype(BF16), v, preferred_element_type=F32) / l
            o_ref[rows, cs] = o.astype(BF16)
            lse_tile = jnp.where(lane == hh, m + jnp.log(l), lse_tile)
        lse_ref[0, rows, :] = lse_tile


def _dilated_attention(proj3, group):
    _, dil = ATTN_PATTERNS[group]
    s = proj3.shape[1]
    m_len = s // dil
    tq = min(512, m_len)
    if dil == 1:
        view, cbs = proj3, (CB_Q + group, CB_K + group, CB_V + group)
    else:
        qkv = jnp.stack([proj3[CB_Q + group], proj3[CB_K + group], proj3[CB_V + group]])
        view, cbs = qkv.reshape(3, m_len, dil * CB), (0, 1, 2)
    tb = tq // ATTN_BLOCK
    cur = lambda cb: pl.BlockSpec((None, tq, CB), lambda n, r: (cb, n, r))
    prev = lambda cb: pl.BlockSpec(
        (None, ATTN_BLOCK, CB), lambda n, r: (cb, jnp.maximum(n * tb - 1, 0), r))
    o, lse = pl.pallas_call(
        functools.partial(_attn_kernel, tq=tq),
        grid=(m_len // tq, dil),
        in_specs=[cur(cbs[0]), cur(cbs[1]), prev(cbs[1]), cur(cbs[2]), prev(cbs[2])],
        out_specs=[pl.BlockSpec((tq, CB), lambda n, r: (n, r)),
                   pl.BlockSpec((1, tq, HEADS_PER_GROUP), lambda n, r: (r, n, 0))],
        out_shape=[jax.ShapeDtypeStruct((m_len, dil * CB), BF16),
                   jax.ShapeDtypeStruct((dil, m_len, HEADS_PER_GROUP), F32)],
        compiler_params=_params(2),
        name=f"dilated_attention_g{group}",
    )(view, view, view, view, view)
    o = o.reshape(s, CB)
    lse = jnp.transpose(lse, (1, 0, 2)).reshape(s, HEADS_PER_GROUP)
    return o, lse


MIX_TM = 256


def _mixer_out_kernel(x_ref, o0_ref, o1_ref, o2_ref, l0_ref, l1_ref, l2_ref, u_ref,
                      gc0, gc1, gc2, gc3, ga0, ga1, ga2, ga3,
                      wa_ref, wc_ref, bc_ref, wo_ref, g2_ref, wrh_ref, wrl_ref, br_ref,
                      x1_ref, hp_ref, idx_ref, gate_ref, pos_ref, cnt_ref, carry_ref):
    i = pl.program_id(0)
    tm = MIX_TM

    @pl.when(i == 0)
    def _():
        carry_ref[...] = jnp.zeros_like(carry_ref)

    l0, l1, l2 = l0_ref[...], l1_ref[...], l2_ref[...]
    m = jnp.maximum(jnp.maximum(l0, l1), l2)
    e0, e1, e2 = jnp.exp(l0 - m), jnp.exp(l1 - m), jnp.exp(l2 - m)
    den = e0 + e1 + e2
    mix = (e0 / den, e1 / den, e2 / den)
    o_refs = (o0_ref, o1_ref, o2_ref)
    parts = []
    for hh in range(HEADS_PER_GROUP):
        cs = slice(hh * HEAD_DIM, (hh + 1) * HEAD_DIM)
        acc = mix[0][:, hh:hh + 1] * o_refs[0][:, cs].astype(F32)
        for g in range(1, N_GROUPS):
            acc = acc + mix[g][:, hh:hh + 1] * o_refs[g][:, cs].astype(F32)
        parts.append(acc)
    o = jnp.concatenate(parts, axis=1).astype(BF16)
    ya = jnp.dot(o, wa_ref[...], preferred_element_type=F32)
    yc = jnp.dot(u_ref[...], wc_ref[...], preferred_element_type=F32) + bc_ref[...]
    gcs = (gc0, gc1, gc2, gc3)
    gas = (ga0, ga1, ga2, ga3)
    merged = []
    for c in range(D_MODEL // CB):
        cs = slice(c * CB, (c + 1) * CB)
        merged.append((gcs[c][...].astype(F32) * yc[:, cs]
                       + gas[c][...].astype(F32) * ya[:, cs]).astype(BF16))
    merged = jnp.concatenate(merged, axis=1)
    x1 = x_ref[...] + jnp.dot(merged, wo_ref[...], preferred_element_type=F32)
    x1_ref[...] = x1
    ms = jnp.mean(x1 * x1, axis=-1, keepdims=True)
    h2 = (x1 * lax.rsqrt(ms + NORM_EPS)) * g2_ref[...]
    h2_hi = h2.astype(BF16)
    hp_ref[...] = _pack_bf16_pair(h2[:, 0:PACK_W], h2[:, PACK_W:D_MODEL])
    h2_lo = (h2 - h2_hi.astype(F32)).astype(BF16)
    logits = (jnp.dot(h2_hi, wrh_ref[...], preferred_element_type=F32)
              + jnp.dot(h2_hi, wrl_ref[...], preferred_element_type=F32)
              + jnp.dot(h2_lo, wrh_ref[...], preferred_element_type=F32)) + br_ref[...]

    e_iota = lax.broadcasted_iota(I32, (tm, N_EXPERTS), 1).astype(F32)
    k_lane = lax.broadcasted_iota(I32, (tm, TOP_K), 1)
    vals = logits
    sels = []
    idx_t = jnp.zeros((tm, TOP_K), F32)
    val_t = jnp.zeros((tm, TOP_K), F32)
    for k in range(TOP_K):
        mk = jnp.max(vals, axis=-1, keepdims=True)
        ik = jnp.min(jnp.where(vals == mk, e_iota, float(N_EXPERTS)), axis=-1, keepdims=True)
        sel = e_iota == ik
        sels.append(sel)
        vals = jnp.where(sel, -jnp.inf, vals)
        idx_t = jnp.where(k_lane == k, ik, idx_t)
        val_t = jnp.where(k_lane == k, mk, val_t)
    ex = jnp.exp(val_t - val_t[:, 0:1])
    gate_ref[...] = ex / jnp.sum(ex, axis=-1, keepdims=True)
    idx_ref[...] = idx_t.astype(I32)

    cnt = jnp.zeros((tm, N_EXPERTS), F32)
    for sel in sels:
        cnt = cnt + jnp.where(sel, 1.0, 0.0)
    ri = lax.broadcasted_iota(I32, (tm, tm), 0)
    ci = lax.broadcasted_iota(I32, (tm, tm), 1)
    lower = jnp.where(ci < ri, 1.0, 0.0).astype(BF16)
    prefix = jnp.dot(lower, cnt.astype(BF16), preferred_element_type=F32) + carry_ref[...]
    pos_t = jnp.zeros((tm, TOP_K), F32)
    for k, sel in enumerate(sels):
        pk = jnp.sum(jnp.where(sel, prefix, 0.0), axis=-1, keepdims=True)
        pos_t = jnp.where(k_lane == k, pk, pos_t)
    pos_ref[...] = pos_t.astype(I32)
    carry_ref[...] = carry_ref[...] + jnp.sum(cnt, axis=0, keepdims=True)
    cnt_ref[...] = carry_ref[...]


def _mixer_output(x2, outs, lses, u_ln, proj3, wa, wc, bc, wo, g2, wr_hi, wr_lo, br):
    s = x2.shape[0]
    row = lambda w: pl.BlockSpec((MIX_TM, w), lambda i: (i, 0))
    gate = lambda cb: pl.BlockSpec((None, MIX_TM, CB), lambda i: (cb, i, 0))
    full = lambda a: pl.BlockSpec(a.shape, lambda i: (0,) * a.ndim)
    in_specs = ([row(D_MODEL)] + [row(CB)] * 3 + [row(HEADS_PER_GROUP)] * 3 + [row(D_CONV)]
                + [gate(CB_GATE_C + c) for c in range(4)]
                + [gate(CB_GATE_A + c) for c in range(4)]
                + [full(a) for a in (wa, wc, bc, wo, g2, wr_hi, wr_lo, br)])
    return pl.pallas_call(
        _mixer_out_kernel,
        grid=(s // MIX_TM,),
        in_specs=in_specs,
        out_specs=[row(D_MODEL), row(PACK_W), row(TOP_K), row(TOP_K), row(TOP_K),
                   pl.BlockSpec((1, N_EXPERTS), lambda i: (0, 0))],
        out_shape=[jax.ShapeDtypeStruct((s, D_MODEL), F32),
                   jax.ShapeDtypeStruct((s, PACK_W), U32),
                   jax.ShapeDtypeStruct((s, TOP_K), I32),
                   jax.ShapeDtypeStruct((s, TOP_K), F32),
                   jax.ShapeDtypeStruct((s, TOP_K), I32),
                   jax.ShapeDtypeStruct((1, N_EXPERTS), F32)],
        scratch_shapes=[pltpu.VMEM((1, N_EXPERTS), F32)],
        compiler_params=_params(1),
        name="mixer_output",
    )(x2, *outs, *lses, u_ln, *([proj3] * 8), wa, wc, bc, wo, g2, wr_hi, wr_lo, br)


ROW_BLK = 256
DISP_TM = 256


def _dispatch_kernel(fill_ref, dest_ref, h_ref, xs_ref, zbuf, sem):
    i = pl.program_id(0)
    n_fill = fill_ref.shape[0]

    @pl.when(i == 0)
    def _():
        zbuf[...] = jnp.zeros_like(zbuf)

        def fill_copy(j):
            return pltpu.make_async_copy(
                zbuf, xs_ref.at[pl.ds(fill_ref[j] * ROW_BLK, ROW_BLK)], sem.at[0])

        def start(j, c):
            @pl.when(fill_ref[j] >= 0)
            def _():
                fill_copy(j).start()
            return c

        def wait(j, c):
            @pl.when(fill_ref[j] >= 0)
            def _():
                fill_copy(j).wait()
            return c

        lax.fori_loop(0, n_fill, start, 0)
        lax.fori_loop(0, n_fill, wait, 0)

    def rows8(tt, c):
        for si in range(8):
            for k in range(TOP_K):
                d = dest_ref[0, (tt * 8 + si) * TOP_K + k]
                pltpu.make_async_copy(h_ref.at[tt, pl.ds(si, 1)], xs_ref.at[pl.ds(d, 1)],
                                      sem.at[0]).start(priority=k % 2)
        return c

    lax.fori_loop(0, DISP_TM // 8, rows8, 0)
    for _ in range(TOP_K):
        pltpu.make_async_copy(h_ref, h_ref, sem.at[0]).wait()


def _dispatch(hp, dest, fill_blocks, n_rows):
    s = hp.shape[0]
    dest3 = dest.reshape(s // DISP_TM, 1, DISP_TM * TOP_K)
    grid_spec = pltpu.PrefetchScalarGridSpec(
        num_scalar_prefetch=1,
        grid=(s // DISP_TM,),
        in_specs=[
            pl.BlockSpec((None, 1, DISP_TM * TOP_K), lambda i, fr: (i, 0, 0),
                         memory_space=pltpu.SMEM),
            pl.BlockSpec((DISP_TM // 8, 8, PACK_W), lambda i, fr: (i, 0, 0)),
        ],
        out_specs=pl.BlockSpec(memory_space=pl.ANY),
        scratch_shapes=[pltpu.VMEM((ROW_BLK, PACK_W), U32), pltpu.SemaphoreType.DMA((1,))],
    )
    return pl.pallas_call(
        _dispatch_kernel,
        grid_spec=grid_spec,
        out_shape=jax.ShapeDtypeStruct((n_rows, PACK_W), U32),
        compiler_params=_params(1),
        name="moe_dispatch",
    )(fill_blocks, dest3, hp.reshape(s // 8, 8, PACK_W))


UNIT_BLOCKS = 6
UNIT_ROWS = ROW_BLK * UNIT_BLOCKS
EXP_TF = 256
EXP_NF = D_EXPERT // EXP_TF


def _expert_kernel(ue_ref, ub0_ref, unb_ref, nu_ref,
                   xs_ref, win_ref, wout_ref, bin_ref, bo_ref,
                   ys_ref,
                   xbuf, xb16, acc, wg_buf, wu_buf, wo_buf, sem_x, sem_y, sem_w):
    u = pl.program_id(0)
    n_units = nu_ref[0]

    def x_copy(unit, slot, b):
        src = xs_ref.at[pl.ds((ub0_ref[unit] + b) * ROW_BLK, ROW_BLK)]
        dst = xbuf.at[slot, pl.ds(b * ROW_BLK, ROW_BLK)]
        return pltpu.make_async_copy(src, dst, sem_x.at[slot])

    def y_copy(unit, b):
        src = acc.at[pl.ds(b * ROW_BLK, ROW_BLK)]
        dst = ys_ref.at[pl.ds((ub0_ref[unit] + b) * ROW_BLK, ROW_BLK)]
        return pltpu.make_async_copy(src, dst, sem_y.at[0])

    def w_copies(unit, f, slot):
        e = ue_ref[unit]
        c0 = pl.multiple_of(f * EXP_TF, EXP_TF)
        return (
            pltpu.make_async_copy(win_ref.at[e, :, pl.ds(c0, EXP_TF)], wg_buf.at[slot],
                                  sem_w.at[slot]),
            pltpu.make_async_copy(win_ref.at[e, :, pl.ds(D_EXPERT + c0, EXP_TF)], wu_buf.at[slot],
                                  sem_w.at[slot]),
            pltpu.make_async_copy(wout_ref.at[e, pl.ds(c0, EXP_TF), :], wo_buf.at[slot],
                                  sem_w.at[slot]),
        )

    def start_w(unit, f, slot):
        for cp in w_copies(unit, f, slot):
            cp.start()

    def wait_w(unit, f, slot):
        for cp in w_copies(unit, f, slot):
            cp.wait()

    def start_x(unit, slot):
        lax.fori_loop(0, unb_ref[unit], lambda b, c: (x_copy(unit, slot, b).start(), c)[1], 0)

    def wait_x(unit, slot):
        lax.fori_loop(0, unb_ref[unit], lambda b, c: (x_copy(unit, slot, b).wait(), c)[1], 0)

    def wait_y(unit):
        lax.fori_loop(0, unb_ref[unit], lambda b, c: (y_copy(unit, b).wait(), c)[1], 0)

    @pl.when(u < n_units)
    def _():
        xslot = lax.rem(u, 2)
        nblk = unb_ref[u]

        @pl.when(u == 0)
        def _():
            start_x(u, xslot)
            start_w(u, 0, 0)

        wait_x(u, xslot)

        @pl.when(u + 1 < n_units)
        def _():
            start_x(u + 1, 1 - xslot)

        @pl.when(u > 0)
        def _():
            wait_y(u - 1)

        def init(b, c):
            r0 = pl.multiple_of(b * ROW_BLK, ROW_BLK)
            lo, hi = _unpack_bf16_pair(xbuf[xslot, pl.ds(r0, ROW_BLK), :])
            xb16[pl.ds(r0, ROW_BLK), 0:PACK_W] = lo
            xb16[pl.ds(r0, ROW_BLK), PACK_W:D_MODEL] = hi
            acc[pl.ds(r0, ROW_BLK), :] = jnp.broadcast_to(bo_ref[...], (ROW_BLK, D_MODEL))
            return c

        lax.fori_loop(0, nblk, init, 0)

        def group(b0, n_blk, f, ws):
            rows = n_blk * ROW_BLK
            r0 = pl.multiple_of(b0 * ROW_BLK, ROW_BLK)
            c0 = pl.multiple_of(f * EXP_TF, EXP_TF)
            x = xb16[pl.ds(r0, rows), :]
            wgu = jnp.concatenate([wg_buf[ws].astype(BF16), wu_buf[ws].astype(BF16)], axis=1)
            gu = jnp.dot(x, wgu, preferred_element_type=F32)
            bg = bin_ref[:, pl.ds(c0, EXP_TF)]
            bu = bin_ref[:, pl.ds(D_EXPERT + c0, EXP_TF)]
            g = jnp.minimum(gu[:, 0:EXP_TF] + bg, SWIGLU_LIMIT)
            up = jnp.clip(gu[:, EXP_TF:2 * EXP_TF] + bu, -SWIGLU_LIMIT, SWIGLU_LIMIT)
            act = (up + 1.0) * (g * _sigmoid(SWIGLU_ALPHA * g))
            acc[pl.ds(r0, rows), :] += jnp.dot(
                act.astype(BF16), wo_buf[ws].astype(BF16), preferred_element_type=F32)

            @pl.when(f == EXP_NF - 1)
            def _():
                for j in range(n_blk):
                    y_copy(u, b0 + j).start()

        def f_tile(f, ws):
            n_quad = nblk // 4
            rem = nblk - 4 * n_quad
            lax.fori_loop(0, n_quad, lambda q, c: (group(4 * q, 4, f, ws), c)[1], 0)

            @pl.when(rem >= 2)
            def _():
                group(4 * n_quad, 2, f, ws)

            @pl.when(lax.rem(rem, 2) == 1)
            def _():
                group(nblk - 1, 1, f, ws)

        def f_step(f, c):
            ws = lax.rem(f, 2)

            @pl.when(f + 1 < EXP_NF)
            def _():
                start_w(u, f + 1, 1 - ws)

            @pl.when((f + 1 == EXP_NF) & (u + 1 < n_units))
            def _():
                start_w(u + 1, 0, 1 - ws)

            wait_w(u, f, ws)
            f_tile(f, ws)
            return c

        lax.fori_loop(0, EXP_NF, f_step, 0)

        @pl.when(u == n_units - 1)
        def _():
            wait_y(u)

    @pl.when(u == pl.num_programs(0) - 1)
    def _():
        n_blocks = ys_ref.shape[0] // ROW_BLK
        acc[0:ROW_BLK, :] = jnp.zeros((ROW_BLK, D_MODEL), F32)

        def tail_copy(b):
            return pltpu.make_async_copy(acc.at[pl.ds(0, ROW_BLK)],
                                         ys_ref.at[pl.ds(b * ROW_BLK, ROW_BLK)], sem_y.at[0])

        lax.fori_loop(nu_ref[1], n_blocks, lambda b, c: (tail_copy(b).start(), c)[1], 0)
        lax.fori_loop(nu_ref[1], n_blocks, lambda b, c: (tail_copy(b).wait(), c)[1], 0)


def _expert_ffn(xs, unit_e, unit_blk0, unit_nblk, unit_counts, w_in, b_in, w_out, b_out):
    n_rows = xs.shape[0]
    n_units_max = unit_e.shape[0]

    def e_eff(u, ue, nu):
        return ue[jnp.minimum(u, jnp.maximum(nu[0] - 1, 0))]

    any_spec = pl.BlockSpec(memory_space=pl.ANY)
    in_specs = [
        any_spec, any_spec, any_spec,
        pl.BlockSpec((None, 1, 2 * D_EXPERT), lambda u, ue, ub, un, nu: (e_eff(u, ue, nu), 0, 0)),
        pl.BlockSpec((None, 1, D_MODEL), lambda u, ue, ub, un, nu: (e_eff(u, ue, nu), 0, 0)),
    ]
    grid_spec = pltpu.PrefetchScalarGridSpec(
        num_scalar_prefetch=4,
        grid=(n_units_max,),
        in_specs=in_specs,
        out_specs=any_spec,
        scratch_shapes=[
            pltpu.VMEM((2, UNIT_ROWS, PACK_W), U32),
            pltpu.VMEM((UNIT_ROWS, D_MODEL), BF16),
            pltpu.VMEM((UNIT_ROWS, D_MODEL), F32),
            pltpu.VMEM((2, D_MODEL, EXP_TF), F32),
            pltpu.VMEM((2, D_MODEL, EXP_TF), F32),
            pltpu.VMEM((2, EXP_TF, D_MODEL), F32),
            pltpu.SemaphoreType.DMA((2,)),
            pltpu.SemaphoreType.DMA((1,)),
            pltpu.SemaphoreType.DMA((2,)),
        ],
    )
    return pl.pallas_call(
        _expert_kernel,
        grid_spec=grid_spec,
        out_shape=jax.ShapeDtypeStruct((n_rows, D_MODEL), F32),
        compiler_params=_params(1),
        name="expert_ffn",
    )(unit_e, unit_blk0, unit_nblk, unit_counts, xs,
      w_in, w_out, b_in[:, None, :], b_out[:, None, :])


COMB_TM = 256


def _combine_kernel(dcur_ref, dnxt_ref, x1_ref, gate_ref, g_ref, ys_ref, o_ref, ybuf, sem):
    i = pl.program_id(0)
    n = pl.num_programs(0)
    slot = lax.rem(i, 2)

    def issue(dref, s):
        def rows8(tt, c):
            for si in range(8):
                for k in range(TOP_K):
                    d = dref[0, (tt * 8 + si) * TOP_K + k]
                    pltpu.make_async_copy(ys_ref.at[pl.ds(d, 1)],
                                          ybuf.at[s, k, tt, pl.ds(si, 1)],
                                          sem.at[s]).start(priority=k % 2)
            return c

        lax.fori_loop(0, COMB_TM // 8, rows8, 0)

    @pl.when(i == 0)
    def _():
        issue(dcur_ref, slot)

    @pl.when(i + 1 < n)
    def _():
        issue(dnxt_ref, 1 - slot)

    for k in range(TOP_K):
        pltpu.make_async_copy(ybuf.at[slot, k], ybuf.at[slot, k], sem.at[slot]).wait()

    gate = gate_ref[...]
    y = x1_ref[...]
    for k in range(TOP_K):
        y = y + gate[:, k:k + 1] * ybuf[slot, k].reshape(COMB_TM, D_MODEL)
    ms = jnp.mean(y * y, axis=-1, keepdims=True)
    o_ref[...] = (y * lax.rsqrt(ms + NORM_EPS)) * g_ref[...]


def _combine(dest, x1, gate, norm_g, ys):
    s = x1.shape[0]
    nt = s // COMB_TM
    dest3 = dest.reshape(nt, 1, COMB_TM * TOP_K)
    smem = lambda imap: pl.BlockSpec((None, 1, COMB_TM * TOP_K), imap, memory_space=pltpu.SMEM)
    row = lambda w: pl.BlockSpec((COMB_TM, w), lambda i: (i, 0))
    return pl.pallas_call(
        _combine_kernel,
        grid=(nt,),
        in_specs=[smem(lambda i: (i, 0, 0)),
                  smem(lambda i: (jnp.minimum(i + 1, nt - 1), 0, 0)),
                  row(D_MODEL), row(TOP_K), pl.BlockSpec((1, D_MODEL), lambda i: (0, 0)),
                  pl.BlockSpec(memory_space=pl.ANY)],
        out_specs=row(D_MODEL),
        out_shape=jax.ShapeDtypeStruct((s, D_MODEL), F32),
        scratch_shapes=[pltpu.VMEM((2, TOP_K, COMB_TM // 8, 8, D_MODEL), F32),
                        pltpu.SemaphoreType.DMA((2,))],
        compiler_params=_params(1),
        name="moe_combine",
    )(dest3, dest3, x1, gate, norm_g, ys)


def _route_tables(idx, pos, counts, n_tok):
    n_assign = n_tok * TOP_K
    counts = counts.reshape(N_EXPERTS).astype(I32)
    nblk_e = (counts + ROW_BLK - 1) // ROW_BLK
    pad_start = (jnp.cumsum(nblk_e) - nblk_e) * ROW_BLK
    onehot = idx[..., None] == jnp.arange(N_EXPERTS, dtype=I32)
    dest = pos + jnp.sum(jnp.where(onehot, pad_start, 0), axis=-1)

    n_blocks = -(-(n_assign + N_EXPERTS * (ROW_BLK - 1)) // ROW_BLK)
    used = jnp.sum(nblk_e)
    blk_ids = jnp.arange(n_blocks, dtype=I32)
    last_blk = pad_start // ROW_BLK + nblk_e - 1
    fill_blocks = jnp.concatenate([jnp.where(nblk_e > 0, last_blk, -1),
                                   jnp.where(blk_ids >= used, blk_ids, -1)]).astype(I32)

    n_units_max = (n_blocks + (UNIT_BLOCKS - 1) * N_EXPERTS) // UNIT_BLOCKS
    chunks_e = (nblk_e + UNIT_BLOCKS - 1) // UNIT_BLOCKS
    chunk_end = jnp.cumsum(chunks_e)
    chunk_start = chunk_end - chunks_e
    n_units = chunk_end[-1]
    uid = jnp.arange(n_units_max, dtype=I32)
    unit_e = jnp.minimum(jnp.searchsorted(chunk_end, uid, side="right"), N_EXPERTS - 1).astype(I32)
    c_in_e = uid - chunk_start[unit_e]
    unit_blk0 = (pad_start[unit_e] // ROW_BLK + c_in_e * UNIT_BLOCKS).astype(I32)
    unit_nblk = jnp.clip(nblk_e[unit_e] - c_in_e * UNIT_BLOCKS, 0, UNIT_BLOCKS).astype(I32)
    active = uid < n_units
    unit_blk0 = jnp.where(active, unit_blk0, 0)
    unit_nblk = jnp.where(active, unit_nblk, 0)
    unit_counts = jnp.stack([n_units, used]).astype(I32)
    return dest.astype(I32), fill_blocks, n_blocks * ROW_BLK, unit_e, unit_blk0, unit_nblk, unit_counts


def kernel(x, norm1_g, w_in, b_gate, conv_w, conv_b, conv_ln_g, conv_ln_b, w_conv_out, b_conv_out,
           w_attn_out, w_out, norm2_g, w_router, b_router, w_exp_in, b_exp_in, w_exp_out, b_exp_out,
           norm_f_g):
    b_sz, s_len, d = x.shape
    n_tok = b_sz * s_len
    x2 = x.reshape(n_tok, d)
    l = 0
    b_gate_ext = jnp.concatenate([jnp.zeros((GATE_COL0,), F32), b_gate[l]])[None, :]
    proj3 = _in_projection(x2, norm1_g[l][None, :], w_in[l], b_gate_ext)
    u_ln = _conv_branch(proj3, conv_w[l], conv_b[l][None, :], conv_ln_g[l][None, :],
                        conv_ln_b[l][None, :])
    outs, lses = [], []
    for g in range(N_GROUPS):
        o_g, lse_g = _dilated_attention(proj3, g)
        outs.append(o_g)
        lses.append(lse_g)
    wr = w_router[l]
    wr_hi = wr.astype(BF16)
    wr_lo = (wr - wr_hi.astype(F32)).astype(BF16)
    x1, hp, idx, gate, pos, counts = _mixer_output(
        x2, outs, lses, u_ln, proj3,
        w_attn_out[l].astype(BF16), w_conv_out[l].astype(BF16), b_conv_out[l][None, :],
        w_out[l].astype(BF16), norm2_g[l][None, :], wr_hi, wr_lo, b_router[l][None, :])

    dest, fill_blocks, n_rows, unit_e, unit_blk0, unit_nblk, unit_counts = _route_tables(
        idx, pos, counts, n_tok)
    xs = _dispatch(hp, dest, fill_blocks, n_rows)
    ys = _expert_ffn(xs, unit_e, unit_blk0, unit_nblk, unit_counts,
                     w_exp_in[l], b_exp_in[l], w_exp_out[l], b_exp_out[l])
    out = _combine(dest, x1, gate, norm_f_g[None, :], ys)
    return out.reshape(b_sz, s_len, d)
```

```python
import functools

import jax
import jax.numpy as jnp
from jax import lax
from jax.experimental import pallas as pl
from jax.experimental.pallas import tpu as pltpu

D_MODEL = 2048
SEQ = 8192
D_CONV = D_MODEL // 2
CONV_WIDTH = 31
HEAD_DIM = 128
HEADS_PER_GROUP = 4
ATTN_PATTERNS = ((128, 1), (512, 4), (2048, 16))
N_GROUPS = len(ATTN_PATTERNS)
ATTN_WIDTH = N_GROUPS * HEADS_PER_GROUP * HEAD_DIM
ATTN_OUT_WIDTH = HEADS_PER_GROUP * HEAD_DIM
ATTN_BLOCK = 128
D_IN = 2 * D_CONV + 3 * ATTN_WIDTH + 2 * D_MODEL
N_EXPERTS = 32
TOP_K = 4
D_EXPERT = D_MODEL
SWIGLU_LIMIT = 7.0
SWIGLU_ALPHA = 1.702
NORM_EPS = 1e-5

CB = 512
N_CB = D_IN // CB
CB_CONV_A, CB_CONV_G = 0, 2
CB_Q, CB_K, CB_V = 4, 7, 10
CB_GATE_C, CB_GATE_A = 13, 17
GATE_COL0 = CB_GATE_C * CB

VMEM_LIMIT = 56 * 1024 * 1024

F32 = jnp.float32
BF16 = jnp.bfloat16
U32 = jnp.uint32
I32 = jnp.int32

PACK_W = D_MODEL // 2
HI_MASK = 0xFFFF0000


def _sigmoid(z):
    return 1.0 / (1.0 + jnp.exp(-z))


def _pack_bf16_pair(lo, hi):
    lo_bits = lax.bitcast_convert_type(lo.astype(BF16).astype(F32), U32) >> 16
    hi_bits = lax.bitcast_convert_type(hi.astype(BF16).astype(F32), U32) & jnp.uint32(HI_MASK)
    return hi_bits | lo_bits


def _unpack_bf16_pair(w):
    lo = lax.bitcast_convert_type(w << 16, F32).astype(BF16)
    hi = lax.bitcast_convert_type(w & jnp.uint32(HI_MASK), F32).astype(BF16)
    return lo, hi


def _params(n_axes, vmem=VMEM_LIMIT):
    return pltpu.CompilerParams(
        dimension_semantics=("arbitrary",) * n_axes, vmem_limit_bytes=vmem)


IN_TM = 512
IN_TN = 1536


def _inproj_kernel(x_ref, g_ref, w_ref, bg_ref, o_ref, wbf_ref):
    j = pl.program_id(0)
    i = pl.program_id(1)

    @pl.when(i == 0)
    def _():
        wbf_ref[...] = w_ref[...].astype(BF16)

    x = x_ref[...]
    ms = jnp.mean(x * x, axis=-1, keepdims=True)
    h = ((x * lax.rsqrt(ms + NORM_EPS)) * g_ref[...]).astype(BF16)
    acc = jnp.dot(h, wbf_ref[...], preferred_element_type=F32)

    first_gate_tile = GATE_COL0 // IN_TN
    n_sub = IN_TN // CB

    def store(val):
        for c in range(n_sub):
            o_ref[c] = val[:, c * CB:(c + 1) * CB].astype(BF16)

    @pl.when(j < first_gate_tile)
    def _():
        store(acc)

    @pl.when(j == first_gate_tile)
    def _():
        col = j * IN_TN + lax.broadcasted_iota(I32, (1, IN_TN), 1)
        store(jnp.where(col >= GATE_COL0, _sigmoid(acc + bg_ref[...]), acc))

    @pl.when(j > first_gate_tile)
    def _():
        store(_sigmoid(acc + bg_ref[...]))


def _in_projection(x2, norm_g, w_in, b_gate_ext):
    s = x2.shape[0]
    grid = (D_IN // IN_TN, s // IN_TM)
    return pl.pallas_call(
        _inproj_kernel,
        grid=grid,
        in_specs=[
            pl.BlockSpec((IN_TM, D_MODEL), lambda j, i: (i, 0)),
            pl.BlockSpec((1, D_MODEL), lambda j, i: (0, 0)),
            pl.BlockSpec((D_MODEL, IN_TN), lambda j, i: (0, j)),
            pl.BlockSpec((1, IN_TN), lambda j, i: (0, j)),
        ],
        out_specs=pl.BlockSpec((IN_TN // CB, IN_TM, CB), lambda j, i: (j, i, 0)),
        out_shape=jax.ShapeDtypeStruct((N_CB, s, CB), BF16),
        scratch_shapes=[pltpu.VMEM((D_MODEL, IN_TN), BF16)],
        compiler_params=_params(2),
        name="in_projection",
    )(x2, norm_g, w_in, b_gate_ext)


CONV_TS = 256
CONV_HALO = 32
CONV_RC = 32
CONV_N = CONV_TS + CONV_HALO


def _conv_kernel(ac_ref, gc_ref, ah_ref, gh_ref, w_ref, cb_ref, lg_ref, lb_ref, o_ref, r_ref):
    i = pl.program_id(0)
    has_prev = i > 0
    half = D_CONV // 2
    for c in range(2):
        cs = slice(c * half, (c + 1) * half)
        uh = ah_ref[c].astype(F32) * _sigmoid(gh_ref[c].astype(F32))
        r_ref[0, 0:CONV_HALO, cs] = jnp.where(has_prev, uh, 0.0)
        r_ref[0, CONV_HALO:CONV_N, cs] = ac_ref[c].astype(F32) * _sigmoid(gc_ref[c].astype(F32))
    u_ext = r_ref[0]
    for b in range(1, 8):
        r_ref[b] = pltpu.roll(u_ext, CONV_N - b, axis=0)

    def chunk(ci, carry):
        r0 = pl.multiple_of(ci * CONV_RC, CONV_RC)
        acc = jnp.broadcast_to(cb_ref[...], (CONV_RC, D_CONV))
        for k in range(CONV_WIDTH):
            kp = k + (CONV_HALO - (CONV_WIDTH - 1))
            a, b = kp // 8, kp % 8
            acc = acc + w_ref[k:k + 1, :] * r_ref[b, pl.ds(r0 + 8 * a, CONV_RC), :]
        mu = jnp.mean(acc, axis=-1, keepdims=True)
        d = acc - mu
        var = jnp.mean(d * d, axis=-1, keepdims=True)
        y = (d * lax.rsqrt(var + NORM_EPS)) * lg_ref[...] + lb_ref[...]
        o_ref[pl.ds(r0, CONV_RC), :] = (y * _sigmoid(y)).astype(BF16)
        return carry

    lax.fori_loop(0, CONV_TS // CONV_RC, chunk, 0)


def _conv_branch(proj3, conv_w, conv_b, ln_g, ln_b):
    s = proj3.shape[1]
    hb = CONV_TS // CONV_HALO
    cur = lambda blk: pl.BlockSpec((2, CONV_TS, CB), lambda i: (blk // 2, i, 0))
    halo = lambda blk: pl.BlockSpec(
        (2, CONV_HALO, CB), lambda i: (blk // 2, jnp.maximum(i * hb - 1, 0), 0))
    vec = pl.BlockSpec((1, D_CONV), lambda i: (0, 0))
    return pl.pallas_call(
        _conv_kernel,
        grid=(s // CONV_TS,),
        in_specs=[cur(CB_CONV_A), cur(CB_CONV_G), halo(CB_CONV_A), halo(CB_CONV_G),
                  pl.BlockSpec((CONV_WIDTH, D_CONV), lambda i: (0, 0)), vec, vec, vec],
        out_specs=pl.BlockSpec((CONV_TS, D_CONV), lambda i: (i, 0)),
        out_shape=jax.ShapeDtypeStruct((s, D_CONV), BF16),
        scratch_shapes=[pltpu.VMEM((8, CONV_N, D_CONV), F32)],
        compiler_params=_params(1),
        name="conv_branch",
    )(proj3, proj3, proj3, proj3, conv_w, conv_b, ln_g, ln_b)


LSE_W = 128
LSE_REP = LSE_W // HEADS_PER_GROUP


def _attn_rows(q, k, v, n):
    blk = ATTN_BLOCK
    scale = HEAD_DIM ** -0.5
    qi = lax.broadcasted_iota(I32, (blk, 2 * blk), 0)
    ki = lax.broadcasted_iota(I32, (blk, 2 * blk), 1)
    band = (ki >= qi) & (ki <= qi + blk)
    lane_head = lax.broadcasted_iota(I32, (blk, LSE_W), 1) // LSE_REP
    o_rows, lse_rows = [], []
    for j in range(q.shape[0] // blk):
        mask = band
        if j == 0:
            mask = band & ((ki >= blk) | (n > 0))
        o_heads = []
        lse_tile = jnp.zeros((blk, LSE_W), F32)
        for hh in range(HEADS_PER_GROUP):
            cs = slice(hh * HEAD_DIM, (hh + 1) * HEAD_DIM)
            qh = q[j * blk:(j + 1) * blk, cs]
            kh = k[j * blk:(j + 2) * blk, cs]
            vh = v[j * blk:(j + 2) * blk, cs]
            s = lax.dot_general(qh, kh, (((1,), (1,)), ((), ())),
                                preferred_element_type=F32) * scale
            s = jnp.where(mask, s, -jnp.inf)
            m = jnp.max(s, axis=-1, keepdims=True)
            p = jnp.exp(s - m)
            l = jnp.sum(p, axis=-1, keepdims=True)
            o_heads.append(jnp.dot(p.astype(BF16), vh, preferred_element_type=F32) / l)
            lse_tile = jnp.where(lane_head == hh, m + jnp.log(l), lse_tile)
        o_rows.append(jnp.concatenate(o_heads, axis=1))
        lse_rows.append(lse_tile)
    return jnp.concatenate(o_rows, axis=0), jnp.concatenate(lse_rows, axis=0)


def _attn_kernel(q_ref, kc_ref, kp_ref, vc_ref, vp_ref, o_ref, lse_ref):
    n = pl.program_id(0)
    k = jnp.concatenate([kp_ref[...], kc_ref[...]], axis=0)
    v = jnp.concatenate([vp_ref[...], vc_ref[...]], axis=0)
    o, lse = _attn_rows(q_ref[...], k, v, n)
    o_ref[...] = o.astype(BF16)
    lse_ref[...] = lse


ATTN_TILE = 2048


def _attn_dilated_kernel(q_ref, kc_ref, kp_ref, vc_ref, vp_ref, o_ref, lse_ref,
                         qs, ks, vs, os_, ls, *, dil):
    n = pl.program_id(0)
    tile = q_ref.shape[0]
    mb = tile // dil
    half = dil // 2
    halo_w = ATTN_BLOCK * half
    n_slab = CB // 128
    q32 = pltpu.bitcast(q_ref[...], U32)
    kp32, kc32 = pltpu.bitcast(kp_ref[...], U32), pltpu.bitcast(kc_ref[...], U32)
    vp32, vc32 = pltpu.bitcast(vp_ref[...], U32), pltpu.bitcast(vc_ref[...], U32)
    for c in range(n_slab):
        cs = slice(c * 128, (c + 1) * 128)
        qs[c] = q32[:, cs]
        ks[c, 0:halo_w] = kp32[:, cs]
        ks[c, halo_w:] = kc32[:, cs]
        vs[c, 0:halo_w] = vp32[:, cs]
        vs[c, halo_w:] = vc32[:, cs]

    def gather(slabs, rp, rows):
        return jnp.concatenate(
            [slabs[c, pl.ds(rp, rows, stride=half), :] for c in range(n_slab)], axis=1)

    def residue(w, par):
        bits = (w << 16) if par == 0 else (w & jnp.uint32(HI_MASK))
        return lax.bitcast_convert_type(bits, F32).astype(BF16)

    for rp in range(half):
        wq = gather(qs, rp, mb)
        wk = gather(ks, rp, mb + ATTN_BLOCK)
        wv = gather(vs, rp, mb + ATTN_BLOCK)
        outs = []
        for par in range(2):
            o, lse = _attn_rows(residue(wq, par), residue(wk, par), residue(wv, par), n)
            outs.append(o)
            ls[pl.ds(2 * rp + par, mb, stride=dil), :] = lse
        w_out = _pack_bf16_pair(outs[0], outs[1])
        for c in range(n_slab):
            os_[c, pl.ds(rp, mb, stride=half), :] = w_out[:, c * 128:(c + 1) * 128]
    o32 = jnp.concatenate([os_[c] for c in range(n_slab)], axis=1)
    o_ref[...] = pltpu.bitcast(o32, BF16)
    lse_ref[...] = ls[...]


def _dilated_attention(proj3, group):
    _, dil = ATTN_PATTERNS[group]
    s = proj3.shape[1]
    cbs = (CB_Q + group, CB_K + group, CB_V + group)
    if dil == 1:
        tile, halo = 512, ATTN_BLOCK
        body, scratch = _attn_kernel, []
    else:
        tile, halo = ATTN_TILE, ATTN_BLOCK * dil
        body = functools.partial(_attn_dilated_kernel, dil=dil)
        scratch = [pltpu.VMEM((CB // 128, tile // 2, 128), U32),
                   pltpu.VMEM((CB // 128, (tile + halo) // 2, 128), U32),
                   pltpu.VMEM((CB // 128, (tile + halo) // 2, 128), U32),
                   pltpu.VMEM((CB // 128, tile // 2, 128), U32),
                   pltpu.VMEM((tile, LSE_W), F32)]
    hb = tile // halo
    cur = lambda cb: pl.BlockSpec((None, tile, CB), lambda n: (cb, n, 0))
    prev = lambda cb: pl.BlockSpec((None, halo, CB), lambda n: (cb, jnp.maximum(n * hb - 1, 0), 0))
    return pl.pallas_call(
        body,
        grid=(s // tile,),
        in_specs=[cur(cbs[0]), cur(cbs[1]), prev(cbs[1]), cur(cbs[2]), prev(cbs[2])],
        out_specs=[pl.BlockSpec((tile, CB), lambda n: (n, 0)),
                   pl.BlockSpec((tile, LSE_W), lambda n: (n, 0))],
        out_shape=[jax.ShapeDtypeStruct((s, CB), BF16),
                   jax.ShapeDtypeStruct((s, LSE_W), F32)],
        scratch_shapes=scratch,
        compiler_params=_params(1),
        name=f"dilated_attention_g{group}",
    )(proj3, proj3, proj3, proj3, proj3)


MIX_TM = 256


def _mixer_out_kernel(x_ref, o0_ref, o1_ref, o2_ref, l0_ref, l1_ref, l2_ref, u_ref,
                      gc0, gc1, gc2, gc3, ga0, ga1, ga2, ga3,
                      wa_ref, wc_ref, bc_ref, wo_ref, g2_ref, wrh_ref, wrl_ref, br_ref,
                      x1_ref, hp_ref, idx_ref, gate_ref, pos_ref, cnt_ref, carry_ref):
    i = pl.program_id(0)
    tm = MIX_TM

    @pl.when(i == 0)
    def _():
        carry_ref[...] = jnp.zeros_like(carry_ref)

    l_refs = (l0_ref, l1_ref, l2_ref)
    o_refs = (o0_ref, o1_ref, o2_ref)
    parts = []
    for hh in range(HEADS_PER_GROUP):
        cs = slice(hh * HEAD_DIM, (hh + 1) * HEAD_DIM)
        lse = [r[:, hh * LSE_REP:hh * LSE_REP + 1] for r in l_refs]
        m = jnp.maximum(jnp.maximum(lse[0], lse[1]), lse[2])
        e = [jnp.exp(v - m) for v in lse]
        den = e[0] + e[1] + e[2]
        acc = (e[0] / den) * o_refs[0][:, cs].astype(F32)
        for g in range(1, N_GROUPS):
            acc = acc + (e[g] / den) * o_refs[g][:, cs].astype(F32)
        parts.append(acc)
    o = jnp.concatenate(parts, axis=1).astype(BF16)
    ya = jnp.dot(o, wa_ref[...], preferred_element_type=F32)
    yc = jnp.dot(u_ref[...], wc_ref[...], preferred_element_type=F32) + bc_ref[...]
    gcs = (gc0, gc1, gc2, gc3)
    gas = (ga0, ga1, ga2, ga3)
    merged = []
    for c in range(D_MODEL // CB):
        cs = slice(c * CB, (c + 1) * CB)
        merged.append((gcs[c][...].astype(F32) * yc[:, cs]
                       + gas[c][...].astype(F32) * ya[:, cs]).astype(BF16))
    merged = jnp.concatenate(merged, axis=1)
    x1 = x_ref[...] + jnp.dot(merged, wo_ref[...], preferred_element_type=F32)
    x1_ref[...] = x1
    ms = jnp.mean(x1 * x1, axis=-1, keepdims=True)
    h2 = (x1 * lax.rsqrt(ms + NORM_EPS)) * g2_ref[...]
    h2_hi = h2.astype(BF16)
    hp_ref[...] = _pack_bf16_pair(h2[:, 0:PACK_W], h2[:, PACK_W:D_MODEL])
    h2_lo = (h2 - h2_hi.astype(F32)).astype(BF16)
    logits = (jnp.dot(h2_hi, wrh_ref[...], preferred_element_type=F32)
              + jnp.dot(h2_hi, wrl_ref[...], preferred_element_type=F32)
              + jnp.dot(h2_lo, wrh_ref[...], preferred_element_type=F32)) + br_ref[...]

    e_iota = lax.broadcasted_iota(I32, (tm, N_EXPERTS), 1).astype(F32)
    k_lane = lax.broadcasted_iota(I32, (tm, TOP_K), 1)
    vals = logits
    sels = []
    idx_t = jnp.zeros((tm, TOP_K), F32)
    val_t = jnp.zeros((tm, TOP_K), F32)
    for k in range(TOP_K):
        mk = jnp.max(vals, axis=-1, keepdims=True)
        ik = jnp.min(jnp.where(vals == mk, e_iota, float(N_EXPERTS)), axis=-1, keepdims=True)
        sel = e_iota == ik
        sels.append(sel)
        vals = jnp.where(sel, -jnp.inf, vals)
        idx_t = jnp.where(k_lane == k, ik, idx_t)
        val_t = jnp.where(k_lane == k, mk, val_t)
    ex = jnp.exp(val_t - val_t[:, 0:1])
    gate_ref[...] = ex / jnp.sum(ex, axis=-1, keepdims=True)
    idx_ref[...] = idx_t.astype(I32)

    cnt = jnp.zeros((tm, N_EXPERTS), F32)
    for sel in sels:
        cnt = cnt + jnp.where(sel, 1.0, 0.0)
    ri = lax.broadcasted_iota(I32, (tm, tm), 0)
    ci = lax.broadcasted_iota(I32, (tm, tm), 1)
    lower = jnp.where(ci < ri, 1.0, 0.0).astype(BF16)
    prefix = jnp.dot(lower, cnt.astype(BF16), preferred_element_type=F32) + carry_ref[...]
    pos_t = jnp.zeros((tm, TOP_K), F32)
    for k, sel in enumerate(sels):
        pk = jnp.sum(jnp.where(sel, prefix, 0.0), axis=-1, keepdims=True)
        pos_t = jnp.where(k_lane == k, pk, pos_t)
    pos_ref[...] = pos_t.astype(I32)
    carry_ref[...] = carry_ref[...] + jnp.sum(cnt, axis=0, keepdims=True)
    cnt_ref[...] = carry_ref[...]


def _mixer_output(x2, outs, lses, u_ln, proj3, wa, wc, bc, wo, g2, wr_hi, wr_lo, br):
    s = x2.shape[0]
    row = lambda w: pl.BlockSpec((MIX_TM, w), lambda i: (i, 0))
    gate = lambda cb: pl.BlockSpec((None, MIX_TM, CB), lambda i: (cb, i, 0))
    full = lambda a: pl.BlockSpec(a.shape, lambda i: (0,) * a.ndim)
    in_specs = ([row(D_MODEL)] + [row(CB)] * 3 + [row(LSE_W)] * 3 + [row(D_CONV)]
                + [gate(CB_GATE_C + c) for c in range(4)]
                + [gate(CB_GATE_A + c) for c in range(4)]
                + [full(a) for a in (wa, wc, bc, wo, g2, wr_hi, wr_lo, br)])
    return pl.pallas_call(
        _mixer_out_kernel,
        grid=(s // MIX_TM,),
        in_specs=in_specs,
        out_specs=[row(D_MODEL), row(PACK_W), row(TOP_K), row(TOP_K), row(TOP_K),
                   pl.BlockSpec((1, N_EXPERTS), lambda i: (0, 0))],
        out_shape=[jax.ShapeDtypeStruct((s, D_MODEL), F32),
                   jax.ShapeDtypeStruct((s, PACK_W), U32),
                   jax.ShapeDtypeStruct((s, TOP_K), I32),
                   jax.ShapeDtypeStruct((s, TOP_K), F32),
                   jax.ShapeDtypeStruct((s, TOP_K), I32),
                   jax.ShapeDtypeStruct((1, N_EXPERTS), F32)],
        scratch_shapes=[pltpu.VMEM((1, N_EXPERTS), F32)],
        compiler_params=_params(1),
        name="mixer_output",
    )(x2, *outs, *lses, u_ln, *([proj3] * 8), wa, wc, bc, wo, g2, wr_hi, wr_lo, br)


ROW_BLK = 256
DISP_TM = 256


def _dispatch_kernel(fill_ref, dest_ref, h_ref, xs_ref, zbuf, sem):
    i = pl.program_id(0)
    n_fill = fill_ref.shape[0]

    @pl.when(i == 0)
    def _():
        zbuf[...] = jnp.zeros_like(zbuf)

        def fill_copy(j):
            return pltpu.make_async_copy(
                zbuf, xs_ref.at[pl.ds(fill_ref[j] * ROW_BLK, ROW_BLK)], sem.at[0])

        def start(j, c):
            @pl.when(fill_ref[j] >= 0)
            def _():
                fill_copy(j).start()
            return c

        def wait(j, c):
            @pl.when(fill_ref[j] >= 0)
            def _():
                fill_copy(j).wait()
            return c

        lax.fori_loop(0, n_fill, start, 0)
        lax.fori_loop(0, n_fill, wait, 0)

    def rows8(tt, c):
        for si in range(8):
            for k in range(TOP_K):
                d = dest_ref[0, (tt * 8 + si) * TOP_K + k]
                pltpu.make_async_copy(h_ref.at[tt, pl.ds(si, 1)], xs_ref.at[pl.ds(d, 1)],
                                      sem.at[0]).start(priority=k % 2)
        return c

    lax.fori_loop(0, DISP_TM // 8, rows8, 0)
    for _ in range(TOP_K):
        pltpu.make_async_copy(h_ref, h_ref, sem.at[0]).wait()


def _dispatch(hp, dest, fill_blocks, n_rows):
    s = hp.shape[0]
    dest3 = dest.reshape(s // DISP_TM, 1, DISP_TM * TOP_K)
    grid_spec = pltpu.PrefetchScalarGridSpec(
        num_scalar_prefetch=1,
        grid=(s // DISP_TM,),
        in_specs=[
            pl.BlockSpec((None, 1, DISP_TM * TOP_K), lambda i, fr: (i, 0, 0),
                         memory_space=pltpu.SMEM),
            pl.BlockSpec((DISP_TM // 8, 8, PACK_W), lambda i, fr: (i, 0, 0)),
        ],
        out_specs=pl.BlockSpec(memory_space=pl.ANY),
        scratch_shapes=[pltpu.VMEM((ROW_BLK, PACK_W), U32), pltpu.SemaphoreType.DMA((1,))],
    )
    return pl.pallas_call(
        _dispatch_kernel,
        grid_spec=grid_spec,
        out_shape=jax.ShapeDtypeStruct((n_rows, PACK_W), U32),
        compiler_params=_params(1),
        name="moe_dispatch",
    )(fill_blocks, dest3, hp.reshape(s // 8, 8, PACK_W))


UNIT_BLOCKS = 6
UNIT_ROWS = ROW_BLK * UNIT_BLOCKS
EXP_TF = 256
EXP_NF = D_EXPERT // EXP_TF


def _expert_kernel(ue_ref, ub0_ref, unb_ref, nu_ref,
                   xs_ref, win_ref, wout_ref, bin_ref, bo_ref,
                   ys_ref,
                   xbuf, xb16, acc, wg_buf, wu_buf, wo_buf, sem_x, sem_y, sem_w):
    u = pl.program_id(0)
    n_units = nu_ref[0]

    def x_copy(unit, slot, b):
        src = xs_ref.at[pl.ds((ub0_ref[unit] + b) * ROW_BLK, ROW_BLK)]
        dst = xbuf.at[slot, pl.ds(b * ROW_BLK, ROW_BLK)]
        return pltpu.make_async_copy(src, dst, sem_x.at[slot])

    def y_copy(unit, b):
        src = acc.at[pl.ds(b * ROW_BLK, ROW_BLK)]
        dst = ys_ref.at[pl.ds((ub0_ref[unit] + b) * ROW_BLK, ROW_BLK)]
        return pltpu.make_async_copy(src, dst, sem_y.at[0])

    def w_copies(unit, f, slot):
        e = ue_ref[unit]
        c0 = pl.multiple_of(f * EXP_TF, EXP_TF)
        return (
            pltpu.make_async_copy(win_ref.at[e, :, pl.ds(c0, EXP_TF)], wg_buf.at[slot],
                                  sem_w.at[slot]),
            pltpu.make_async_copy(win_ref.at[e, :, pl.ds(D_EXPERT + c0, EXP_TF)], wu_buf.at[slot],
                                  sem_w.at[slot]),
            pltpu.make_async_copy(wout_ref.at[e, pl.ds(c0, EXP_TF), :], wo_buf.at[slot],
                                  sem_w.at[slot]),
        )

    def start_w(unit, f, slot):
        for cp in w_copies(unit, f, slot):
            cp.start()

    def wait_w(unit, f, slot):
        for cp in w_copies(unit, f, slot):
            cp.wait()

    def start_x(unit, slot):
        lax.fori_loop(0, unb_ref[unit], lambda b, c: (x_copy(unit, slot, b).start(), c)[1], 0)

    def wait_x(unit, slot):
        lax.fori_loop(0, unb_ref[unit], lambda b, c: (x_copy(unit, slot, b).wait(), c)[1], 0)

    def wait_y(unit):
        lax.fori_loop(0, unb_ref[unit], lambda b, c: (y_copy(unit, b).wait(), c)[1], 0)

    @pl.when(u < n_units)
    def _():
        xslot = lax.rem(u, 2)
        nblk = unb_ref[u]

        @pl.when(u == 0)
        def _():
            start_x(u, xslot)
            start_w(u, 0, 0)

        wait_x(u, xslot)

        @pl.when(u + 1 < n_units)
        def _():
            start_x(u + 1, 1 - xslot)

        @pl.when(u > 0)
        def _():
            wait_y(u - 1)

        def init(b, c):
            r0 = pl.multiple_of(b * ROW_BLK, ROW_BLK)
            lo, hi = _unpack_bf16_pair(xbuf[xslot, pl.ds(r0, ROW_BLK), :])
            xb16[pl.ds(r0, ROW_BLK), 0:PACK_W] = lo
            xb16[pl.ds(r0, ROW_BLK), PACK_W:D_MODEL] = hi
            acc[pl.ds(r0, ROW_BLK), :] = jnp.broadcast_to(bo_ref[...], (ROW_BLK, D_MODEL))
            return c

        lax.fori_loop(0, nblk, init, 0)

        def group(b0, n_blk, f, ws):
            rows = n_blk * ROW_BLK
            r0 = pl.multiple_of(b0 * ROW_BLK, ROW_BLK)
            c0 = pl.multiple_of(f * EXP_TF, EXP_TF)
            x = xb16[pl.ds(r0, rows), :]
            wgu = jnp.concatenate([wg_buf[ws].astype(BF16), wu_buf[ws].astype(BF16)], axis=1)
            gu = jnp.dot(x, wgu, preferred_element_type=F32)
            bg = bin_ref[:, pl.ds(c0, EXP_TF)]
            bu = bin_ref[:, pl.ds(D_EXPERT + c0, EXP_TF)]
            g = jnp.minimum(gu[:, 0:EXP_TF] + bg, SWIGLU_LIMIT)
            up = jnp.clip(gu[:, EXP_TF:2 * EXP_TF] + bu, -SWIGLU_LIMIT, SWIGLU_LIMIT)
            act = (up + 1.0) * (g * _sigmoid(SWIGLU_ALPHA * g))
            acc[pl.ds(r0, rows), :] += jnp.dot(
                act.astype(BF16), wo_buf[ws].astype(BF16), preferred_element_type=F32)

            @pl.when(f == EXP_NF - 1)
            def _():
                for j in range(n_blk):
                    y_copy(u, b0 + j).start()

        def f_tile(f, ws):
            n_quad = nblk // 4
            rem = nblk - 4 * n_quad
            lax.fori_loop(0, n_quad, lambda q, c: (group(4 * q, 4, f, ws), c)[1], 0)

            @pl.when(rem >= 2)
            def _():
                group(4 * n_quad, 2, f, ws)

            @pl.when(lax.rem(rem, 2) == 1)
            def _():
                group(nblk - 1, 1, f, ws)

        def f_step(f, c):
            ws = lax.rem(f, 2)

            @pl.when(f + 1 < EXP_NF)
            def _():
                start_w(u, f + 1, 1 - ws)

            @pl.when((f + 1 == EXP_NF) & (u + 1 < n_units))
            def _():
                start_w(u + 1, 0, 1 - ws)

            wait_w(u, f, ws)
            f_tile(f, ws)
            return c

        lax.fori_loop(0, EXP_NF, f_step, 0)

        @pl.when(u == n_units - 1)
        def _():
            wait_y(u)

    @pl.when(u == pl.num_programs(0) - 1)
    def _():
        n_blocks = ys_ref.shape[0] // ROW_BLK
        acc[0:ROW_BLK, :] = jnp.zeros((ROW_BLK, D_MODEL), F32)

        def tail_copy(b):
            return pltpu.make_async_copy(acc.at[pl.ds(0, ROW_BLK)],
                                         ys_ref.at[pl.ds(b * ROW_BLK, ROW_BLK)], sem_y.at[0])

        lax.fori_loop(nu_ref[1], n_blocks, lambda b, c: (tail_copy(b).start(), c)[1], 0)
        lax.fori_loop(nu_ref[1], n_blocks, lambda b, c: (tail_copy(b).wait(), c)[1], 0)


def _expert_ffn(xs, unit_e, unit_blk0, unit_nblk, unit_counts, w_in, b_in, w_out, b_out):
    n_rows = xs.shape[0]
    n_units_max = unit_e.shape[0]

    def e_eff(u, ue, nu):
        return ue[jnp.minimum(u, jnp.maximum(nu[0] - 1, 0))]

    any_spec = pl.BlockSpec(memory_space=pl.ANY)
    in_specs = [
        any_spec, any_spec, any_spec,
        pl.BlockSpec((None, 1, 2 * D_EXPERT), lambda u, ue, ub, un, nu: (e_eff(u, ue, nu), 0, 0)),
        pl.BlockSpec((None, 1, D_MODEL), lambda u, ue, ub, un, nu: (e_eff(u, ue, nu), 0, 0)),
    ]
    grid_spec = pltpu.PrefetchScalarGridSpec(
        num_scalar_prefetch=4,
        grid=(n_units_max,),
        in_specs=in_specs,
        out_specs=any_spec,
        scratch_shapes=[
            pltpu.VMEM((2, UNIT_ROWS, PACK_W), U32),
            pltpu.VMEM((UNIT_ROWS, D_MODEL), BF16),
            pltpu.VMEM((UNIT_ROWS, D_MODEL), F32),
            pltpu.VMEM((2, D_MODEL, EXP_TF), F32),
            pltpu.VMEM((2, D_MODEL, EXP_TF), F32),
            pltpu.VMEM((2, EXP_TF, D_MODEL), F32),
            pltpu.SemaphoreType.DMA((2,)),
            pltpu.SemaphoreType.DMA((1,)),
            pltpu.SemaphoreType.DMA((2,)),
        ],
    )
    return pl.pallas_call(
        _expert_kernel,
        grid_spec=grid_spec,
        out_shape=jax.ShapeDtypeStruct((n_rows, D_MODEL), F32),
        compiler_params=_params(1),
        name="expert_ffn",
    )(unit_e, unit_blk0, unit_nblk, unit_counts, xs,
      w_in, w_out, b_in[:, None, :], b_out[:, None, :])


COMB_TM = 256


def _combine_kernel(dcur_ref, dnxt_ref, x1_ref, gate_ref, g_ref, ys_ref, o_ref, ybuf, sem):
    i = pl.program_id(0)
    n = pl.num_programs(0)
    slot = lax.rem(i, 2)

    def issue(dref, s):
        def rows8(tt, c):
            for si in range(8):
                for k in range(TOP_K):
                    d = dref[0, (tt * 8 + si) * TOP_K + k]
                    pltpu.make_async_copy(ys_ref.at[pl.ds(d, 1)],
                                          ybuf.at[s, k, tt, pl.ds(si, 1)],
                                          sem.at[s]).start(priority=k % 2)
            return c

        lax.fori_loop(0, COMB_TM // 8, rows8, 0)

    @pl.when(i == 0)
    def _():
        issue(dcur_ref, slot)

    @pl.when(i + 1 < n)
    def _():
        issue(dnxt_ref, 1 - slot)

    for k in range(TOP_K):
        pltpu.make_async_copy(ybuf.at[slot, k], ybuf.at[slot, k], sem.at[slot]).wait()

    gate = gate_ref[...]
    y = x1_ref[...]
    for k in range(TOP_K):
        y = y + gate[:, k:k + 1] * ybuf[slot, k].reshape(COMB_TM, D_MODEL)
    ms = jnp.mean(y * y, axis=-1, keepdims=True)
    o_ref[...] = (y * lax.rsqrt(ms + NORM_EPS)) * g_ref[...]


def _combine(dest, x1, gate, norm_g, ys):
    s = x1.shape[0]
    nt = s // COMB_TM
    dest3 = dest.reshape(nt, 1, COMB_TM * TOP_K)
    smem = lambda imap: pl.BlockSpec((None, 1, COMB_TM * TOP_K), imap, memory_space=pltpu.SMEM)
    row = lambda w: pl.BlockSpec((COMB_TM, w), lambda i: (i, 0))
    return pl.pallas_call(
        _combine_kernel,
        grid=(nt,),
        in_specs=[smem(lambda i: (i, 0, 0)),
                  smem(lambda i: (jnp.minimum(i + 1, nt - 1), 0, 0)),
                  row(D_MODEL), row(TOP_K), pl.BlockSpec((1, D_MODEL), lambda i: (0, 0)),
                  pl.BlockSpec(memory_space=pl.ANY)],
        out_specs=row(D_MODEL),
        out_shape=jax.ShapeDtypeStruct((s, D_MODEL), F32),
        scratch_shapes=[pltpu.VMEM((2, TOP_K, COMB_TM // 8, 8, D_MODEL), F32),
                        pltpu.SemaphoreType.DMA((2,))],
        compiler_params=_params(1),
        name="moe_combine",
    )(dest3, dest3, x1, gate, norm_g, ys)


def _route_tables(idx, pos, counts, n_tok):
    n_assign = n_tok * TOP_K
    counts = counts.reshape(N_EXPERTS).astype(I32)
    nblk_e = (counts + ROW_BLK - 1) // ROW_BLK
    pad_start = (jnp.cumsum(nblk_e) - nblk_e) * ROW_BLK
    onehot = idx[..., None] == jnp.arange(N_EXPERTS, dtype=I32)
    dest = pos + jnp.sum(jnp.where(onehot, pad_start, 0), axis=-1)

    n_blocks = -(-(n_assign + N_EXPERTS * (ROW_BLK - 1)) // ROW_BLK)
    used = jnp.sum(nblk_e)
    blk_ids = jnp.arange(n_blocks, dtype=I32)
    last_blk = pad_start // ROW_BLK + nblk_e - 1
    fill_blocks = jnp.concatenate([jnp.where(nblk_e > 0, last_blk, -1),
                                   jnp.where(blk_ids >= used, blk_ids, -1)]).astype(I32)

    n_units_max = (n_blocks + (UNIT_BLOCKS - 1) * N_EXPERTS) // UNIT_BLOCKS
    chunks_e = (nblk_e + UNIT_BLOCKS - 1) // UNIT_BLOCKS
    chunk_end = jnp.cumsum(chunks_e)
    chunk_start = chunk_end - chunks_e
    n_units = chunk_end[-1]
    uid = jnp.arange(n_units_max, dtype=I32)
    unit_e = jnp.minimum(jnp.searchsorted(chunk_end, uid, side="right"), N_EXPERTS - 1).astype(I32)
    c_in_e = uid - chunk_start[unit_e]
    unit_blk0 = (pad_start[unit_e] // ROW_BLK + c_in_e * UNIT_BLOCKS).astype(I32)
    unit_nblk = jnp.clip(nblk_e[unit_e] - c_in_e * UNIT_BLOCKS, 0, UNIT_BLOCKS).astype(I32)
    active = uid < n_units
    unit_blk0 = jnp.where(active, unit_blk0, 0)
    unit_nblk = jnp.where(active, unit_nblk, 0)
    unit_counts = jnp.stack([n_units, used]).astype(I32)
    return dest.astype(I32), fill_blocks, n_blocks * ROW_BLK, unit_e, unit_blk0, unit_nblk, unit_counts


def kernel(x, norm1_g, w_in, b_gate, conv_w, conv_b, conv_ln_g, conv_ln_b, w_conv_out, b_conv_out,
           w_attn_out, w_out, norm2_g, w_router, b_router, w_exp_in, b_exp_in, w_exp_out, b_exp_out,
           norm_f_g):
    b_sz, s_len, d = x.shape
    n_tok = b_sz * s_len
    x2 = x.reshape(n_tok, d)
    l = 0
    b_gate_ext = jnp.concatenate([jnp.zeros((GATE_COL0,), F32), b_gate[l]])[None, :]
    proj3 = _in_projection(x2, norm1_g[l][None, :], w_in[l], b_gate_ext)
    u_ln = _conv_branch(proj3, conv_w[l], conv_b[l][None, :], conv_ln_g[l][None, :],
                        conv_ln_b[l][None, :])
    outs, lses = [], []
    for g in range(N_GROUPS):
        o_g, lse_g = _dilated_attention(proj3, g)
        outs.append(o_g)
        lses.append(lse_g)
    wr = w_router[l]
    wr_hi = wr.astype(BF16)
    wr_lo = (wr - wr_hi.astype(F32)).astype(BF16)
    x1, hp, idx, gate, pos, counts = _mixer_output(
        x2, outs, lses, u_ln, proj3,
        w_attn_out[l].astype(BF16), w_conv_out[l].astype(BF16), b_conv_out[l][None, :],
        w_out[l].astype(BF16), norm2_g[l][None, :], wr_hi, wr_lo, b_router[l][None, :])

    dest, fill_blocks, n_rows, unit_e, unit_blk0, unit_nblk, unit_counts = _route_tables(
        idx, pos, counts, n_tok)
    xs = _dispatch(hp, dest, fill_blocks, n_rows)
    ys = _expert_ffn(xs, unit_e, unit_blk0, unit_nblk, unit_counts,
                     w_exp_in[l], b_exp_in[l], w_exp_out[l], b_exp_out[l])
    out = _combine(dest, x1, gate, norm_f_g[None, :], ys)
    return out.reshape(b_sz, s_len, d)
```

```python
import functools

import jax
import jax.numpy as jnp
from jax import lax
from jax.experimental import pallas as pl
from jax.experimental.pallas import tpu as pltpu

D_MODEL = 2048
SEQ = 8192
D_CONV = D_MODEL // 2
CONV_WIDTH = 31
HEAD_DIM = 128
HEADS_PER_GROUP = 4
ATTN_PATTERNS = ((128, 1), (512, 4), (2048, 16))
N_GROUPS = len(ATTN_PATTERNS)
ATTN_WIDTH = N_GROUPS * HEADS_PER_GROUP * HEAD_DIM
ATTN_OUT_WIDTH = HEADS_PER_GROUP * HEAD_DIM
ATTN_BLOCK = 128
D_IN = 2 * D_CONV + 3 * ATTN_WIDTH + 2 * D_MODEL
N_EXPERTS = 32
TOP_K = 4
D_EXPERT = D_MODEL
SWIGLU_LIMIT = 7.0
SWIGLU_ALPHA = 1.702
NORM_EPS = 1e-5

CB = 512
N_CB = D_IN // CB
CB_CONV_A, CB_CONV_G = 0, 2
CB_Q, CB_K, CB_V = 4, 7, 10
CB_GATE_C, CB_GATE_A = 13, 17
GATE_COL0 = CB_GATE_C * CB

VMEM_LIMIT = 56 * 1024 * 1024

F32 = jnp.float32
BF16 = jnp.bfloat16
U32 = jnp.uint32
I32 = jnp.int32

PACK_W = D_MODEL // 2
HI_MASK = 0xFFFF0000


def _sigmoid(z):
    return 1.0 / (1.0 + jnp.exp(-z))


def _pack_bf16_pair(lo, hi):
    lo_bits = lax.bitcast_convert_type(lo.astype(BF16).astype(F32), U32) >> 16
    hi_bits = lax.bitcast_convert_type(hi.astype(BF16).astype(F32), U32) & jnp.uint32(HI_MASK)
    return hi_bits | lo_bits


def _unpack_bf16_pair(w):
    lo = lax.bitcast_convert_type(w << 16, F32).astype(BF16)
    hi = lax.bitcast_convert_type(w & jnp.uint32(HI_MASK), F32).astype(BF16)
    return lo, hi


def _params(n_axes, vmem=VMEM_LIMIT):
    return pltpu.CompilerParams(
        dimension_semantics=("arbitrary",) * n_axes, vmem_limit_bytes=vmem)


IN_TM = 512
IN_TN = 1536


def _inproj_kernel(x_ref, g_ref, w_ref, bg_ref, o_ref, wbf_ref):
    j = pl.program_id(0)
    i = pl.program_id(1)

    @pl.when(i == 0)
    def _():
        wbf_ref[...] = w_ref[...].astype(BF16)

    x = x_ref[...]
    ms = jnp.mean(x * x, axis=-1, keepdims=True)
    h = ((x * lax.rsqrt(ms + NORM_EPS)) * g_ref[...]).astype(BF16)
    acc = jnp.dot(h, wbf_ref[...], preferred_element_type=F32)

    first_gate_tile = GATE_COL0 // IN_TN
    n_sub = IN_TN // CB

    def store(val):
        for c in range(n_sub):
            o_ref[c] = val[:, c * CB:(c + 1) * CB].astype(BF16)

    @pl.when(j < first_gate_tile)
    def _():
        store(acc)

    @pl.when(j == first_gate_tile)
    def _():
        col = j * IN_TN + lax.broadcasted_iota(I32, (1, IN_TN), 1)
        store(jnp.where(col >= GATE_COL0, _sigmoid(acc + bg_ref[...]), acc))

    @pl.when(j > first_gate_tile)
    def _():
        store(_sigmoid(acc + bg_ref[...]))


def _in_projection(x2, norm_g, w_in, b_gate_ext):
    s = x2.shape[0]
    grid = (D_IN // IN_TN, s // IN_TM)
    return pl.pallas_call(
        _inproj_kernel,
        grid=grid,
        in_specs=[
            pl.BlockSpec((IN_TM, D_MODEL), lambda j, i: (i, 0)),
            pl.BlockSpec((1, D_MODEL), lambda j, i: (0, 0)),
            pl.BlockSpec((D_MODEL, IN_TN), lambda j, i: (0, j)),
            pl.BlockSpec((1, IN_TN), lambda j, i: (0, j)),
        ],
        out_specs=pl.BlockSpec((IN_TN // CB, IN_TM, CB), lambda j, i: (j, i, 0)),
        out_shape=jax.ShapeDtypeStruct((N_CB, s, CB), BF16),
        scratch_shapes=[pltpu.VMEM((D_MODEL, IN_TN), BF16)],
        compiler_params=_params(2),
        name="in_projection",
    )(x2, norm_g, w_in, b_gate_ext)


CONV_TS = 256
CONV_HALO = 32
CONV_RC = 32
CONV_N = CONV_TS + CONV_HALO


def _conv_kernel(ac_ref, gc_ref, ah_ref, gh_ref, w_ref, cb_ref, lg_ref, lb_ref, o_ref, r_ref,
                 c_ref):
    i = pl.program_id(0)
    has_prev = i > 0
    half = D_CONV // 2
    for c in range(2):
        cs = slice(c * half, (c + 1) * half)
        uh = ah_ref[c].astype(F32) * _sigmoid(gh_ref[c].astype(F32))
        r_ref[0, 0:CONV_HALO, cs] = jnp.where(has_prev, uh, 0.0)
        r_ref[0, CONV_HALO:CONV_N, cs] = ac_ref[c].astype(F32) * _sigmoid(gc_ref[c].astype(F32))
    u_ext = r_ref[0]
    for b in range(1, 8):
        r_ref[b] = pltpu.roll(u_ext, CONV_N - b, axis=0)

    def chunk(ci, carry):
        r0 = pl.multiple_of(ci * CONV_RC, CONV_RC)
        acc = jnp.broadcast_to(cb_ref[...], (CONV_RC, D_CONV))
        for k in range(CONV_WIDTH):
            kp = k + (CONV_HALO - (CONV_WIDTH - 1))
            a, b = kp // 8, kp % 8
            acc = acc + w_ref[k:k + 1, :] * r_ref[b, pl.ds(r0 + 8 * a, CONV_RC), :]
        c_ref[pl.ds(r0, CONV_RC), :] = acc
        return carry

    lax.fori_loop(0, CONV_TS // CONV_RC, chunk, 0)
    conv = c_ref[...]
    mu = jnp.mean(conv, axis=-1, keepdims=True)
    d = conv - mu
    var = jnp.mean(d * d, axis=-1, keepdims=True)
    y = (d * lax.rsqrt(var + NORM_EPS)) * lg_ref[...] + lb_ref[...]
    o_ref[...] = (y * _sigmoid(y)).astype(BF16)


def _conv_branch(proj3, conv_w, conv_b, ln_g, ln_b):
    s = proj3.shape[1]
    hb = CONV_TS // CONV_HALO
    cur = lambda blk: pl.BlockSpec((2, CONV_TS, CB), lambda i: (blk // 2, i, 0))
    halo = lambda blk: pl.BlockSpec(
        (2, CONV_HALO, CB), lambda i: (blk // 2, jnp.maximum(i * hb - 1, 0), 0))
    vec = pl.BlockSpec((1, D_CONV), lambda i: (0, 0))
    return pl.pallas_call(
        _conv_kernel,
        grid=(s // CONV_TS,),
        in_specs=[cur(CB_CONV_A), cur(CB_CONV_G), halo(CB_CONV_A), halo(CB_CONV_G),
                  pl.BlockSpec((CONV_WIDTH, D_CONV), lambda i: (0, 0)), vec, vec, vec],
        out_specs=pl.BlockSpec((CONV_TS, D_CONV), lambda i: (i, 0)),
        out_shape=jax.ShapeDtypeStruct((s, D_CONV), BF16),
        scratch_shapes=[pltpu.VMEM((8, CONV_N, D_CONV), F32), pltpu.VMEM((CONV_TS, D_CONV), F32)],
        compiler_params=_params(1),
        name="conv_branch",
    )(proj3, proj3, proj3, proj3, conv_w, conv_b, ln_g, ln_b)


LSE_W = 128
LSE_REP = LSE_W // HEADS_PER_GROUP


def _attn_rows(q, k, v, n):
    blk = ATTN_BLOCK
    scale = HEAD_DIM ** -0.5
    qi = lax.broadcasted_iota(I32, (blk, 2 * blk), 0)
    ki = lax.broadcasted_iota(I32, (blk, 2 * blk), 1)
    band = (ki >= qi) & (ki <= qi + blk)
    lane_head = lax.broadcasted_iota(I32, (blk, LSE_W), 1) // LSE_REP
    o_rows, lse_rows = [], []
    for j in range(q.shape[0] // blk):
        mask = band
        if j == 0:
            mask = band & ((ki >= blk) | (n > 0))
        o_heads = []
        lse_tile = jnp.zeros((blk, LSE_W), F32)
        for hh in range(HEADS_PER_GROUP):
            cs = slice(hh * HEAD_DIM, (hh + 1) * HEAD_DIM)
            qh = q[j * blk:(j + 1) * blk, cs]
            kh = k[j * blk:(j + 2) * blk, cs]
            vh = v[j * blk:(j + 2) * blk, cs]
            s = lax.dot_general(qh, kh, (((1,), (1,)), ((), ())),
                                preferred_element_type=F32) * scale
            s = jnp.where(mask, s, -jnp.inf)
            m = jnp.max(s, axis=-1, keepdims=True)
            p = jnp.exp(s - m)
            l = jnp.sum(p, axis=-1, keepdims=True)
            o_heads.append(jnp.dot(p.astype(BF16), vh, preferred_element_type=F32) / l)
            lse_tile = jnp.where(lane_head == hh, m + jnp.log(l), lse_tile)
        o_rows.append(jnp.concatenate(o_heads, axis=1))
        lse_rows.append(lse_tile)
    return jnp.concatenate(o_rows, axis=0), jnp.concatenate(lse_rows, axis=0)


def _attn_kernel(q_ref, kc_ref, kp_ref, vc_ref, vp_ref, o_ref, lse_ref):
    n = pl.program_id(0)
    k = jnp.concatenate([kp_ref[...], kc_ref[...]], axis=0)
    v = jnp.concatenate([vp_ref[...], vc_ref[...]], axis=0)
    o, lse = _attn_rows(q_ref[...], k, v, n)
    o_ref[...] = o.astype(BF16)
    lse_ref[...] = lse


ATTN_TILE = 2048


def _attn_dilated_kernel(q_ref, kc_ref, kp_ref, vc_ref, vp_ref, o_ref, lse_ref,
                         qs, ks, vs, os_, ls, *, dil):
    n = pl.program_id(0)
    tile = q_ref.shape[0]
    mb = tile // dil
    half = dil // 2
    halo_w = ATTN_BLOCK * half
    n_slab = CB // 128
    q32 = pltpu.bitcast(q_ref[...], U32)
    kp32, kc32 = pltpu.bitcast(kp_ref[...], U32), pltpu.bitcast(kc_ref[...], U32)
    vp32, vc32 = pltpu.bitcast(vp_ref[...], U32), pltpu.bitcast(vc_ref[...], U32)
    for c in range(n_slab):
        cs = slice(c * 128, (c + 1) * 128)
        qs[c] = q32[:, cs]
        ks[c, 0:halo_w] = kp32[:, cs]
        ks[c, halo_w:] = kc32[:, cs]
        vs[c, 0:halo_w] = vp32[:, cs]
        vs[c, halo_w:] = vc32[:, cs]

    def gather(slabs, rp, rows):
        return jnp.concatenate(
            [slabs[c, pl.ds(rp, rows, stride=half), :] for c in range(n_slab)], axis=1)

    def residue(w, par):
        bits = (w << 16) if par == 0 else (w & jnp.uint32(HI_MASK))
        return lax.bitcast_convert_type(bits, F32).astype(BF16)

    for rp in range(half):
        wq = gather(qs, rp, mb)
        wk = gather(ks, rp, mb + ATTN_BLOCK)
        wv = gather(vs, rp, mb + ATTN_BLOCK)
        outs = []
        for par in range(2):
            o, lse = _attn_rows(residue(wq, par), residue(wk, par), residue(wv, par), n)
            outs.append(o)
            ls[pl.ds(2 * rp + par, mb, stride=dil), :] = lse
        w_out = _pack_bf16_pair(outs[0], outs[1])
        for c in range(n_slab):
            os_[c, pl.ds(rp, mb, stride=half), :] = w_out[:, c * 128:(c + 1) * 128]
    o32 = jnp.concatenate([os_[c] for c in range(n_slab)], axis=1)
    o_ref[...] = pltpu.bitcast(o32, BF16)
    lse_ref[...] = ls[...]


def _dilated_attention(proj3, group):
    _, dil = ATTN_PATTERNS[group]
    s = proj3.shape[1]
    cbs = (CB_Q + group, CB_K + group, CB_V + group)
    if dil == 1:
        tile, halo = 512, ATTN_BLOCK
        body, scratch = _attn_kernel, []
    else:
        tile, halo = ATTN_TILE, ATTN_BLOCK * dil
        body = functools.partial(_attn_dilated_kernel, dil=dil)
        scratch = [pltpu.VMEM((CB // 128, tile // 2, 128), U32),
                   pltpu.VMEM((CB // 128, (tile + halo) // 2, 128), U32),
                   pltpu.VMEM((CB // 128, (tile + halo) // 2, 128), U32),
                   pltpu.VMEM((CB // 128, tile // 2, 128), U32),
                   pltpu.VMEM((tile, LSE_W), F32)]
    hb = tile // halo
    cur = lambda cb: pl.BlockSpec((None, tile, CB), lambda n: (cb, n, 0))
    prev = lambda cb: pl.BlockSpec((None, halo, CB), lambda n: (cb, jnp.maximum(n * hb - 1, 0), 0))
    return pl.pallas_call(
        body,
        grid=(s // tile,),
        in_specs=[cur(cbs[0]), cur(cbs[1]), prev(cbs[1]), cur(cbs[2]), prev(cbs[2])],
        out_specs=[pl.BlockSpec((tile, CB), lambda n: (n, 0)),
                   pl.BlockSpec((tile, LSE_W), lambda n: (n, 0))],
        out_shape=[jax.ShapeDtypeStruct((s, CB), BF16),
                   jax.ShapeDtypeStruct((s, LSE_W), F32)],
        scratch_shapes=scratch,
        compiler_params=_params(1),
        name=f"dilated_attention_g{group}",
    )(proj3, proj3, proj3, proj3, proj3)


MIX_TM = 256


def _mixer_out_kernel(x_ref, o0_ref, o1_ref, o2_ref, l0_ref, l1_ref, l2_ref, u_ref,
                      gc0, gc1, gc2, gc3, ga0, ga1, ga2, ga3,
                      wa_ref, wc_ref, bc_ref, wo_ref, g2_ref, wrh_ref, wrl_ref, br_ref,
                      x1_ref, hp_ref, idx_ref, gate_ref, pos_ref, cnt_ref, carry_ref):
    i = pl.program_id(0)
    tm = MIX_TM

    @pl.when(i == 0)
    def _():
        carry_ref[...] = jnp.zeros_like(carry_ref)

    l_refs = (l0_ref, l1_ref, l2_ref)
    o_refs = (o0_ref, o1_ref, o2_ref)
    parts = []
    for hh in range(HEADS_PER_GROUP):
        cs = slice(hh * HEAD_DIM, (hh + 1) * HEAD_DIM)
        lse = [r[:, hh * LSE_REP:hh * LSE_REP + 1] for r in l_refs]
        m = jnp.maximum(jnp.maximum(lse[0], lse[1]), lse[2])
        e = [jnp.exp(v - m) for v in lse]
        den = e[0] + e[1] + e[2]
        acc = (e[0] / den) * o_refs[0][:, cs].astype(F32)
        for g in range(1, N_GROUPS):
            acc = acc + (e[g] / den) * o_refs[g][:, cs].astype(F32)
        parts.append(acc)
    o = jnp.concatenate(parts, axis=1).astype(BF16)
    ya = jnp.dot(o, wa_ref[...], preferred_element_type=F32)
    yc = jnp.dot(u_ref[...], wc_ref[...], preferred_element_type=F32) + bc_ref[...]
    gcs = (gc0, gc1, gc2, gc3)
    gas = (ga0, ga1, ga2, ga3)
    merged = []
    for c in range(D_MODEL // CB):
        cs = slice(c * CB, (c + 1) * CB)
        merged.append((gcs[c][...].astype(F32) * yc[:, cs]
                       + gas[c][...].astype(F32) * ya[:, cs]).astype(BF16))
    merged = jnp.concatenate(merged, axis=1)
    x1 = x_ref[...] + jnp.dot(merged, wo_ref[...], preferred_element_type=F32)
    x1_ref[...] = x1
    ms = jnp.mean(x1 * x1, axis=-1, keepdims=True)
    h2 = (x1 * lax.rsqrt(ms + NORM_EPS)) * g2_ref[...]
    h2_hi = h2.astype(BF16)
    hp_ref[...] = _pack_bf16_pair(h2[:, 0:PACK_W], h2[:, PACK_W:D_MODEL])
    h2_lo = (h2 - h2_hi.astype(F32)).astype(BF16)
    logits = (jnp.dot(h2_hi, wrh_ref[...], preferred_element_type=F32)
              + jnp.dot(h2_hi, wrl_ref[...], preferred_element_type=F32)
              + jnp.dot(h2_lo, wrh_ref[...], preferred_element_type=F32)) + br_ref[...]

    e_iota = lax.broadcasted_iota(I32, (tm, N_EXPERTS), 1).astype(F32)
    k_lane = lax.broadcasted_iota(I32, (tm, TOP_K), 1)
    vals = logits
    sels = []
    idx_t = jnp.zeros((tm, TOP_K), F32)
    val_t = jnp.zeros((tm, TOP_K), F32)
    for k in range(TOP_K):
        mk = jnp.max(vals, axis=-1, keepdims=True)
        ik = jnp.min(jnp.where(vals == mk, e_iota, float(N_EXPERTS)), axis=-1, keepdims=True)
        sel = e_iota == ik
        sels.append(sel)
        vals = jnp.where(sel, -jnp.inf, vals)
        idx_t = jnp.where(k_lane == k, ik, idx_t)
        val_t = jnp.where(k_lane == k, mk, val_t)
    ex = jnp.exp(val_t - val_t[:, 0:1])
    gate_ref[...] = ex / jnp.sum(ex, axis=-1, keepdims=True)
    idx_ref[...] = idx_t.astype(I32)

    cnt = jnp.zeros((tm, N_EXPERTS), F32)
    for sel in sels:
        cnt = cnt + jnp.where(sel, 1.0, 0.0)
    ri = lax.broadcasted_iota(I32, (tm, tm), 0)
    ci = lax.broadcasted_iota(I32, (tm, tm), 1)
    lower = jnp.where(ci < ri, 1.0, 0.0).astype(BF16)
    prefix = jnp.dot(lower, cnt.astype(BF16), preferred_element_type=F32) + carry_ref[...]
    pos_t = jnp.zeros((tm, TOP_K), F32)
    for k, sel in enumerate(sels):
        pk = jnp.sum(jnp.where(sel, prefix, 0.0), axis=-1, keepdims=True)
        pos_t = jnp.where(k_lane == k, pk, pos_t)
    pos_ref[...] = pos_t.astype(I32)
    carry_ref[...] = carry_ref[...] + jnp.sum(cnt, axis=0, keepdims=True)
    cnt_ref[...] = carry_ref[...]


def _mixer_output(x2, outs, lses, u_ln, proj3, wa, wc, bc, wo, g2, wr_hi, wr_lo, br):
    s = x2.shape[0]
    row = lambda w: pl.BlockSpec((MIX_TM, w), lambda i: (i, 0))
    gate = lambda cb: pl.BlockSpec((None, MIX_TM, CB), lambda i: (cb, i, 0))
    full = lambda a: pl.BlockSpec(a.shape, lambda i: (0,) * a.ndim)
    in_specs = ([row(D_MODEL)] + [row(CB)] * 3 + [row(LSE_W)] * 3 + [row(D_CONV)]
                + [gate(CB_GATE_C + c) for c in range(4)]
                + [gate(CB_GATE_A + c) for c in range(4)]
                + [full(a) for a in (wa, wc, bc, wo, g2, wr_hi, wr_lo, br)])
    return pl.pallas_call(
        _mixer_out_kernel,
        grid=(s // MIX_TM,),
        in_specs=in_specs,
        out_specs=[row(D_MODEL), row(PACK_W), row(TOP_K), row(TOP_K), row(TOP_K),
                   pl.BlockSpec((1, N_EXPERTS), lambda i: (0, 0))],
        out_shape=[jax.ShapeDtypeStruct((s, D_MODEL), F32),
                   jax.ShapeDtypeStruct((s, PACK_W), U32),
                   jax.ShapeDtypeStruct((s, TOP_K), I32),
                   jax.ShapeDtypeStruct((s, TOP_K), F32),
                   jax.ShapeDtypeStruct((s, TOP_K), I32),
                   jax.ShapeDtypeStruct((1, N_EXPERTS), F32)],
        scratch_shapes=[pltpu.VMEM((1, N_EXPERTS), F32)],
        compiler_params=_params(1),
        name="mixer_output",
    )(x2, *outs, *lses, u_ln, *([proj3] * 8), wa, wc, bc, wo, g2, wr_hi, wr_lo, br)


ROW_BLK = 128
DISP_TM = 256


def _dispatch_kernel(fill_ref, dest_ref, h_ref, xs_ref, zbuf, sem):
    i = pl.program_id(0)
    n_fill = fill_ref.shape[0]

    @pl.when(i == 0)
    def _():
        zbuf[...] = jnp.zeros_like(zbuf)

        def fill_copy(j):
            return pltpu.make_async_copy(
                zbuf, xs_ref.at[pl.ds(fill_ref[j] * ROW_BLK, ROW_BLK)], sem.at[0])

        def start(j, c):
            @pl.when(fill_ref[j] >= 0)
            def _():
                fill_copy(j).start()
            return c

        def wait(j, c):
            @pl.when(fill_ref[j] >= 0)
            def _():
                fill_copy(j).wait()
            return c

        lax.fori_loop(0, n_fill, start, 0)
        lax.fori_loop(0, n_fill, wait, 0)

    def rows8(tt, c):
        for si in range(8):
            for k in range(TOP_K):
                d = dest_ref[0, (tt * 8 + si) * TOP_K + k]
                pltpu.make_async_copy(h_ref.at[tt, pl.ds(si, 1)], xs_ref.at[pl.ds(d, 1)],
                                      sem.at[0]).start(priority=k % 2)
        return c

    lax.fori_loop(0, DISP_TM // 8, rows8, 0)
    for _ in range(TOP_K):
        pltpu.make_async_copy(h_ref, h_ref, sem.at[0]).wait()


def _dispatch(hp, dest, fill_blocks, n_rows):
    s = hp.shape[0]
    dest3 = dest.reshape(s // DISP_TM, 1, DISP_TM * TOP_K)
    grid_spec = pltpu.PrefetchScalarGridSpec(
        num_scalar_prefetch=1,
        grid=(s // DISP_TM,),
        in_specs=[
            pl.BlockSpec((None, 1, DISP_TM * TOP_K), lambda i, fr: (i, 0, 0),
                         memory_space=pltpu.SMEM),
            pl.BlockSpec((DISP_TM // 8, 8, PACK_W), lambda i, fr: (i, 0, 0)),
        ],
        out_specs=pl.BlockSpec(memory_space=pl.ANY),
        scratch_shapes=[pltpu.VMEM((ROW_BLK, PACK_W), U32), pltpu.SemaphoreType.DMA((1,))],
    )
    return pl.pallas_call(
        _dispatch_kernel,
        grid_spec=grid_spec,
        out_shape=jax.ShapeDtypeStruct((n_rows, PACK_W), U32),
        compiler_params=_params(1),
        name="moe_dispatch",
    )(fill_blocks, dest3, hp.reshape(s // 8, 8, PACK_W))


UNIT_BLOCKS = 12
UNIT_ROWS = ROW_BLK * UNIT_BLOCKS
GROUP_SIZES = (8, 4, 2, 1)
EXP_TF = 256
EXP_NF = D_EXPERT // EXP_TF


def _expert_kernel(ue_ref, ub0_ref, unb_ref, nu_ref,
                   xs_ref, win_ref, wout_ref, bin_ref, bo_ref,
                   ys_ref,
                   xbuf, xb16, acc, wg_buf, wu_buf, wo_buf, sem_x, sem_y, sem_w):
    u = pl.program_id(0)
    n_units = nu_ref[0]

    def x_copy(unit, slot, b):
        src = xs_ref.at[pl.ds((ub0_ref[unit] + b) * ROW_BLK, ROW_BLK)]
        dst = xbuf.at[slot, pl.ds(b * ROW_BLK, ROW_BLK)]
        return pltpu.make_async_copy(src, dst, sem_x.at[slot])

    def y_copy(unit, b):
        src = acc.at[pl.ds(b * ROW_BLK, ROW_BLK)]
        dst = ys_ref.at[pl.ds((ub0_ref[unit] + b) * ROW_BLK, ROW_BLK)]
        return pltpu.make_async_copy(src, dst, sem_y.at[0])

    def w_copies(unit, f, slot):
        e = ue_ref[unit]
        c0 = pl.multiple_of(f * EXP_TF, EXP_TF)
        return (
            pltpu.make_async_copy(win_ref.at[e, :, pl.ds(c0, EXP_TF)], wg_buf.at[slot],
                                  sem_w.at[slot]),
            pltpu.make_async_copy(win_ref.at[e, :, pl.ds(D_EXPERT + c0, EXP_TF)], wu_buf.at[slot],
                                  sem_w.at[slot]),
            pltpu.make_async_copy(wout_ref.at[e, pl.ds(c0, EXP_TF), :], wo_buf.at[slot],
                                  sem_w.at[slot]),
        )

    def start_w(unit, f, slot):
        for cp in w_copies(unit, f, slot):
            cp.start()

    def wait_w(unit, f, slot):
        for cp in w_copies(unit, f, slot):
            cp.wait()

    def start_x(unit, slot):
        lax.fori_loop(0, unb_ref[unit], lambda b, c: (x_copy(unit, slot, b).start(), c)[1], 0)

    def wait_x(unit, slot):
        lax.fori_loop(0, unb_ref[unit], lambda b, c: (x_copy(unit, slot, b).wait(), c)[1], 0)

    def wait_y(unit):
        lax.fori_loop(0, unb_ref[unit], lambda b, c: (y_copy(unit, b).wait(), c)[1], 0)

    @pl.when(u < n_units)
    def _():
        xslot = lax.rem(u, 2)
        nblk = unb_ref[u]

        @pl.when(u == 0)
        def _():
            start_x(u, xslot)
            start_w(u, 0, 0)

        wait_x(u, xslot)

        @pl.when(u + 1 < n_units)
        def _():
            start_x(u + 1, 1 - xslot)

        @pl.when(u > 0)
        def _():
            wait_y(u - 1)

        def init(b, c):
            r0 = pl.multiple_of(b * ROW_BLK, ROW_BLK)
            lo, hi = _unpack_bf16_pair(xbuf[xslot, pl.ds(r0, ROW_BLK), :])
            xb16[pl.ds(r0, ROW_BLK), 0:PACK_W] = lo
            xb16[pl.ds(r0, ROW_BLK), PACK_W:D_MODEL] = hi
            acc[pl.ds(r0, ROW_BLK), :] = jnp.broadcast_to(bo_ref[...], (ROW_BLK, D_MODEL))
            return c

        lax.fori_loop(0, nblk, init, 0)

        def group(b0, n_blk, f, ws):
            rows = n_blk * ROW_BLK
            r0 = pl.multiple_of(b0 * ROW_BLK, ROW_BLK)
            c0 = pl.multiple_of(f * EXP_TF, EXP_TF)
            x = xb16[pl.ds(r0, rows), :]
            wgu = jnp.concatenate([wg_buf[ws].astype(BF16), wu_buf[ws].astype(BF16)], axis=1)
            gu = jnp.dot(x, wgu, preferred_element_type=F32)
            bg = bin_ref[:, pl.ds(c0, EXP_TF)]
            bu = bin_ref[:, pl.ds(D_EXPERT + c0, EXP_TF)]
            g = jnp.minimum(gu[:, 0:EXP_TF] + bg, SWIGLU_LIMIT)
            up = jnp.clip(gu[:, EXP_TF:2 * EXP_TF] + bu, -SWIGLU_LIMIT, SWIGLU_LIMIT)
            act = (up + 1.0) * (g * _sigmoid(SWIGLU_ALPHA * g))
            acc[pl.ds(r0, rows), :] += jnp.dot(
                act.astype(BF16), wo_buf[ws].astype(BF16), preferred_element_type=F32)

            @pl.when(f == EXP_NF - 1)
            def _():
                for j in range(n_blk):
                    y_copy(u, b0 + j).start()

        def f_tile(f, ws):
            big = GROUP_SIZES[0]
            n_big = nblk // big
            lax.fori_loop(0, n_big, lambda q, c: (group(big * q, big, f, ws), c)[1], 0)
            for size in GROUP_SIZES[1:]:
                done = (nblk // (2 * size)) * (2 * size)

                @pl.when(lax.rem(nblk, 2 * size) >= size)
                def _():
                    group(done, size, f, ws)

        def f_step(f, c):
            ws = lax.rem(f, 2)

            @pl.when(f + 1 < EXP_NF)
            def _():
                start_w(u, f + 1, 1 - ws)

            @pl.when((f + 1 == EXP_NF) & (u + 1 < n_units))
            def _():
                start_w(u + 1, 0, 1 - ws)

            wait_w(u, f, ws)
            f_tile(f, ws)
            return c

        lax.fori_loop(0, EXP_NF, f_step, 0)

        @pl.when(u == n_units - 1)
        def _():
            wait_y(u)

    @pl.when(u == pl.num_programs(0) - 1)
    def _():
        n_blocks = ys_ref.shape[0] // ROW_BLK
        acc[0:ROW_BLK, :] = jnp.zeros((ROW_BLK, D_MODEL), F32)

        def tail_copy(b):
            return pltpu.make_async_copy(acc.at[pl.ds(0, ROW_BLK)],
                                         ys_ref.at[pl.ds(b * ROW_BLK, ROW_BLK)], sem_y.at[0])

        lax.fori_loop(nu_ref[1], n_blocks, lambda b, c: (tail_copy(b).start(), c)[1], 0)
        lax.fori_loop(nu_ref[1], n_blocks, lambda b, c: (tail_copy(b).wait(), c)[1], 0)


def _expert_ffn(xs, unit_e, unit_blk0, unit_nblk, unit_counts, w_in, b_in, w_out, b_out):
    n_rows = xs.shape[0]
    n_units_max = unit_e.shape[0]

    def e_eff(u, ue, nu):
        return ue[jnp.minimum(u, jnp.maximum(nu[0] - 1, 0))]

    any_spec = pl.BlockSpec(memory_space=pl.ANY)
    in_specs = [
        any_spec, any_spec, any_spec,
        pl.BlockSpec((None, 1, 2 * D_EXPERT), lambda u, ue, ub, un, nu: (e_eff(u, ue, nu), 0, 0)),
        pl.BlockSpec((None, 1, D_MODEL), lambda u, ue, ub, un, nu: (e_eff(u, ue, nu), 0, 0)),
    ]
    grid_spec = pltpu.PrefetchScalarGridSpec(
        num_scalar_prefetch=4,
        grid=(n_units_max,),
        in_specs=in_specs,
        out_specs=any_spec,
        scratch_shapes=[
            pltpu.VMEM((2, UNIT_ROWS, PACK_W), U32),
            pltpu.VMEM((UNIT_ROWS, D_MODEL), BF16),
            pltpu.VMEM((UNIT_ROWS, D_MODEL), F32),
            pltpu.VMEM((2, D_MODEL, EXP_TF), F32),
            pltpu.VMEM((2, D_MODEL, EXP_TF), F32),
            pltpu.VMEM((2, EXP_TF, D_MODEL), F32),
            pltpu.SemaphoreType.DMA((2,)),
            pltpu.SemaphoreType.DMA((1,)),
            pltpu.SemaphoreType.DMA((2,)),
        ],
    )
    return pl.pallas_call(
        _expert_kernel,
        grid_spec=grid_spec,
        out_shape=jax.ShapeDtypeStruct((n_rows, D_MODEL), F32),
        compiler_params=_params(1),
        name="expert_ffn",
    )(unit_e, unit_blk0, unit_nblk, unit_counts, xs,
      w_in, w_out, b_in[:, None, :], b_out[:, None, :])


COMB_TM = 256


def _combine_kernel(dcur_ref, dnxt_ref, x1_ref, gate_ref, g_ref, ys_ref, o_ref, ybuf, sem):
    i = pl.program_id(0)
    n = pl.num_programs(0)
    slot = lax.rem(i, 2)

    def issue(dref, s):
        def rows8(tt, c):
            for si in range(8):
                for k in range(TOP_K):
                    d = dref[0, (tt * 8 + si) * TOP_K + k]
                    pltpu.make_async_copy(ys_ref.at[pl.ds(d, 1)],
                                          ybuf.at[s, k, tt, pl.ds(si, 1)],
                                          sem.at[s]).start(priority=k % 2)
            return c

        lax.fori_loop(0, COMB_TM // 8, rows8, 0)

    @pl.when(i == 0)
    def _():
        issue(dcur_ref, slot)

    @pl.when(i + 1 < n)
    def _():
        issue(dnxt_ref, 1 - slot)

    for k in range(TOP_K):
        pltpu.make_async_copy(ybuf.at[slot, k], ybuf.at[slot, k], sem.at[slot]).wait()

    gate = gate_ref[...]
    y = x1_ref[...]
    for k in range(TOP_K):
        y = y + gate[:, k:k + 1] * ybuf[slot, k].reshape(COMB_TM, D_MODEL)
    ms = jnp.mean(y * y, axis=-1, keepdims=True)
    o_ref[...] = (y * lax.rsqrt(ms + NORM_EPS)) * g_ref[...]


def _combine(dest, x1, gate, norm_g, ys):
    s = x1.shape[0]
    nt = s // COMB_TM
    dest3 = dest.reshape(nt, 1, COMB_TM * TOP_K)
    smem = lambda imap: pl.BlockSpec((None, 1, COMB_TM * TOP_K), imap, memory_space=pltpu.SMEM)
    row = lambda w: pl.BlockSpec((COMB_TM, w), lambda i: (i, 0))
    return pl.pallas_call(
        _combine_kernel,
        grid=(nt,),
        in_specs=[smem(lambda i: (i, 0, 0)),
                  smem(lambda i: (jnp.minimum(i + 1, nt - 1), 0, 0)),
                  row(D_MODEL), row(TOP_K), pl.BlockSpec((1, D_MODEL), lambda i: (0, 0)),
                  pl.BlockSpec(memory_space=pl.ANY)],
        out_specs=row(D_MODEL),
        out_shape=jax.ShapeDtypeStruct((s, D_MODEL), F32),
        scratch_shapes=[pltpu.VMEM((2, TOP_K, COMB_TM // 8, 8, D_MODEL), F32),
                        pltpu.SemaphoreType.DMA((2,))],
        compiler_params=_params(1),
        name="moe_combine",
    )(dest3, dest3, x1, gate, norm_g, ys)


def _route_tables(idx, pos, counts, n_tok):
    n_assign = n_tok * TOP_K
    counts = counts.reshape(N_EXPERTS).astype(I32)
    nblk_e = (counts + ROW_BLK - 1) // ROW_BLK
    pad_start = (jnp.cumsum(nblk_e) - nblk_e) * ROW_BLK
    onehot = idx[..., None] == jnp.arange(N_EXPERTS, dtype=I32)
    dest = pos + jnp.sum(jnp.where(onehot, pad_start, 0), axis=-1)

    n_blocks = -(-(n_assign + N_EXPERTS * (ROW_BLK - 1)) // ROW_BLK)
    used = jnp.sum(nblk_e)
    blk_ids = jnp.arange(n_blocks, dtype=I32)
    last_blk = pad_start // ROW_BLK + nblk_e - 1
    fill_blocks = jnp.concatenate([jnp.where(nblk_e > 0, last_blk, -1),
                                   jnp.where(blk_ids >= used, blk_ids, -1)]).astype(I32)

    n_units_max = (n_blocks + (UNIT_BLOCKS - 1) * N_EXPERTS) // UNIT_BLOCKS
    chunks_e = (nblk_e + UNIT_BLOCKS - 1) // UNIT_BLOCKS
    chunk_end = jnp.cumsum(chunks_e)
    chunk_start = chunk_end - chunks_e
    n_units = chunk_end[-1]
    uid = jnp.arange(n_units_max, dtype=I32)
    unit_e = jnp.minimum(jnp.searchsorted(chunk_end, uid, side="right"), N_EXPERTS - 1).astype(I32)
    c_in_e = uid - chunk_start[unit_e]
    unit_blk0 = (pad_start[unit_e] // ROW_BLK + c_in_e * UNIT_BLOCKS).astype(I32)
    unit_nblk = jnp.clip(nblk_e[unit_e] - c_in_e * UNIT_BLOCKS, 0, UNIT_BLOCKS).astype(I32)
    active = uid < n_units
    unit_blk0 = jnp.where(active, unit_blk0, 0)
    unit_nblk = jnp.where(active, unit_nblk, 0)
    unit_counts = jnp.stack([n_units, used]).astype(I32)
    return dest.astype(I32), fill_blocks, n_blocks * ROW_BLK, unit_e, unit_blk0, unit_nblk, unit_counts


def kernel(x, norm1_g, w_in, b_gate, conv_w, conv_b, conv_ln_g, conv_ln_b, w_conv_out, b_conv_out,
           w_attn_out, w_out, norm2_g, w_router, b_router, w_exp_in, b_exp_in, w_exp_out, b_exp_out,
           norm_f_g):
    b_sz, s_len, d = x.shape
    n_tok = b_sz * s_len
    x2 = x.reshape(n_tok, d)
    l = 0
    b_gate_ext = jnp.concatenate([jnp.zeros((GATE_COL0,), F32), b_gate[l]])[None, :]
    proj3 = _in_projection(x2, norm1_g[l][None, :], w_in[l], b_gate_ext)
    u_ln = _conv_branch(proj3, conv_w[l], conv_b[l][None, :], conv_ln_g[l][None, :],
                        conv_ln_b[l][None, :])
    outs, lses = [], []
    for g in range(N_GROUPS):
        o_g, lse_g = _dilated_attention(proj3, g)
        outs.append(o_g)
        lses.append(lse_g)
    wr = w_router[l]
    wr_hi = wr.astype(BF16)
    wr_lo = (wr - wr_hi.astype(F32)).astype(BF16)
    x1, hp, idx, gate, pos, counts = _mixer_output(
        x2, outs, lses, u_ln, proj3,
        w_attn_out[l].astype(BF16), w_conv_out[l].astype(BF16), b_conv_out[l][None, :],
        w_out[l].astype(BF16), norm2_g[l][None, :], wr_hi, wr_lo, b_router[l][None, :])

    dest, fill_blocks, n_rows, unit_e, unit_blk0, unit_nblk, unit_counts = _route_tables(
        idx, pos, counts, n_tok)
    xs = _dispatch(hp, dest, fill_blocks, n_rows)
    ys = _expert_ffn(xs, unit_e, unit_blk0, unit_nblk, unit_counts,
                     w_exp_in[l], b_exp_in[l], w_exp_out[l], b_exp_out[l])
    out = _combine(dest, x1, gate, norm_f_g[None, :], ys)
    return out.reshape(b_sz, s_len, d)
```

```python
import functools

import jax
import jax.numpy as jnp
from jax import lax
from jax.experimental import pallas as pl
from jax.experimental.pallas import tpu as pltpu

D_MODEL = 2048
SEQ = 8192
D_CONV = D_MODEL // 2
CONV_WIDTH = 31
HEAD_DIM = 128
HEADS_PER_GROUP = 4
ATTN_PATTERNS = ((128, 1), (512, 4), (2048, 16))
N_GROUPS = len(ATTN_PATTERNS)
ATTN_WIDTH = N_GROUPS * HEADS_PER_GROUP * HEAD_DIM
ATTN_OUT_WIDTH = HEADS_PER_GROUP * HEAD_DIM
ATTN_BLOCK = 128
D_IN = 2 * D_CONV + 3 * ATTN_WIDTH + 2 * D_MODEL
N_EXPERTS = 32
TOP_K = 4
D_EXPERT = D_MODEL
SWIGLU_LIMIT = 7.0
SWIGLU_ALPHA = 1.702
NORM_EPS = 1e-5

CB = 512
N_CB = D_IN // CB
CB_CONV_A, CB_CONV_G = 0, 2
CB_Q, CB_K, CB_V = 4, 7, 10
CB_GATE_C, CB_GATE_A = 13, 17
GATE_COL0 = CB_GATE_C * CB

VMEM_LIMIT = 56 * 1024 * 1024

F32 = jnp.float32
BF16 = jnp.bfloat16
U32 = jnp.uint32
I32 = jnp.int32

PACK_W = D_MODEL // 2
HI_MASK = 0xFFFF0000


def _sigmoid(z):
    return 1.0 / (1.0 + jnp.exp(-z))


def _pack_bf16_pair(lo, hi):
    lo_bits = lax.bitcast_convert_type(lo.astype(BF16).astype(F32), U32) >> 16
    hi_bits = lax.bitcast_convert_type(hi.astype(BF16).astype(F32), U32) & jnp.uint32(HI_MASK)
    return hi_bits | lo_bits


def _unpack_bf16_pair(w):
    lo = lax.bitcast_convert_type(w << 16, F32).astype(BF16)
    hi = lax.bitcast_convert_type(w & jnp.uint32(HI_MASK), F32).astype(BF16)
    return lo, hi


def _params(n_axes, vmem=VMEM_LIMIT):
    return pltpu.CompilerParams(
        dimension_semantics=("arbitrary",) * n_axes, vmem_limit_bytes=vmem)


IN_TM = 512
IN_TN = 1536


def _inproj_kernel(x_ref, g_ref, w_ref, bg_ref, o_ref, wbf_ref):
    j = pl.program_id(0)
    i = pl.program_id(1)

    @pl.when(i == 0)
    def _():
        wbf_ref[...] = w_ref[...].astype(BF16)

    x = x_ref[...]
    ms = jnp.mean(x * x, axis=-1, keepdims=True)
    h = ((x * lax.rsqrt(ms + NORM_EPS)) * g_ref[...]).astype(BF16)
    acc = jnp.dot(h, wbf_ref[...], preferred_element_type=F32)

    first_gate_tile = GATE_COL0 // IN_TN
    n_sub = IN_TN // CB

    def store(val):
        for c in range(n_sub):
            o_ref[c] = val[:, c * CB:(c + 1) * CB].astype(BF16)

    @pl.when(j < first_gate_tile)
    def _():
        store(acc)

    @pl.when(j == first_gate_tile)
    def _():
        col = j * IN_TN + lax.broadcasted_iota(I32, (1, IN_TN), 1)
        store(jnp.where(col >= GATE_COL0, _sigmoid(acc + bg_ref[...]), acc))

    @pl.when(j > first_gate_tile)
    def _():
        store(_sigmoid(acc + bg_ref[...]))


def _in_projection(x2, norm_g, w_in, b_gate_ext):
    s = x2.shape[0]
    grid = (D_IN // IN_TN, s // IN_TM)
    return pl.pallas_call(
        _inproj_kernel,
        grid=grid,
        in_specs=[
            pl.BlockSpec((IN_TM, D_MODEL), lambda j, i: (i, 0)),
            pl.BlockSpec((1, D_MODEL), lambda j, i: (0, 0)),
            pl.BlockSpec((D_MODEL, IN_TN), lambda j, i: (0, j)),
            pl.BlockSpec((1, IN_TN), lambda j, i: (0, j)),
        ],
        out_specs=pl.BlockSpec((IN_TN // CB, IN_TM, CB), lambda j, i: (j, i, 0)),
        out_shape=jax.ShapeDtypeStruct((N_CB, s, CB), BF16),
        scratch_shapes=[pltpu.VMEM((D_MODEL, IN_TN), BF16)],
        compiler_params=_params(2),
        name="in_projection",
    )(x2, norm_g, w_in, b_gate_ext)


CONV_TS = 256
CONV_HALO = 32
CONV_RC = 32
CONV_N = CONV_TS + CONV_HALO


def _conv_kernel(ac_ref, gc_ref, ah_ref, gh_ref, w_ref, cb_ref, lg_ref, lb_ref, o_ref, r_ref,
                 c_ref):
    i = pl.program_id(0)
    has_prev = i > 0
    half = D_CONV // 2
    for c in range(2):
        cs = slice(c * half, (c + 1) * half)
        uh = ah_ref[c].astype(F32) * _sigmoid(gh_ref[c].astype(F32))
        r_ref[0, 0:CONV_HALO, cs] = jnp.where(has_prev, uh, 0.0)
        r_ref[0, CONV_HALO:CONV_N, cs] = ac_ref[c].astype(F32) * _sigmoid(gc_ref[c].astype(F32))
    u_ext = r_ref[0]
    for b in range(1, 8):
        r_ref[b] = pltpu.roll(u_ext, CONV_N - b, axis=0)

    def chunk(ci, carry):
        r0 = pl.multiple_of(ci * CONV_RC, CONV_RC)
        acc = jnp.broadcast_to(cb_ref[...], (CONV_RC, D_CONV))
        for k in range(CONV_WIDTH):
            kp = k + (CONV_HALO - (CONV_WIDTH - 1))
            a, b = kp // 8, kp % 8
            acc = acc + w_ref[k:k + 1, :] * r_ref[b, pl.ds(r0 + 8 * a, CONV_RC), :]
        c_ref[pl.ds(r0, CONV_RC), :] = acc
        return carry

    lax.fori_loop(0, CONV_TS // CONV_RC, chunk, 0)
    conv = c_ref[...]
    mu = jnp.mean(conv, axis=-1, keepdims=True)
    d = conv - mu
    var = jnp.mean(d * d, axis=-1, keepdims=True)
    y = (d * lax.rsqrt(var + NORM_EPS)) * lg_ref[...] + lb_ref[...]
    o_ref[...] = (y * _sigmoid(y)).astype(BF16)


def _conv_branch(proj3, conv_w, conv_b, ln_g, ln_b):
    s = proj3.shape[1]
    hb = CONV_TS // CONV_HALO
    cur = lambda blk: pl.BlockSpec((2, CONV_TS, CB), lambda i: (blk // 2, i, 0))
    halo = lambda blk: pl.BlockSpec(
        (2, CONV_HALO, CB), lambda i: (blk // 2, jnp.maximum(i * hb - 1, 0), 0))
    vec = pl.BlockSpec((1, D_CONV), lambda i: (0, 0))
    return pl.pallas_call(
        _conv_kernel,
        grid=(s // CONV_TS,),
        in_specs=[cur(CB_CONV_A), cur(CB_CONV_G), halo(CB_CONV_A), halo(CB_CONV_G),
                  pl.BlockSpec((CONV_WIDTH, D_CONV), lambda i: (0, 0)), vec, vec, vec],
        out_specs=pl.BlockSpec((CONV_TS, D_CONV), lambda i: (i, 0)),
        out_shape=jax.ShapeDtypeStruct((s, D_CONV), BF16),
        scratch_shapes=[pltpu.VMEM((8, CONV_N, D_CONV), F32), pltpu.VMEM((CONV_TS, D_CONV), F32)],
        compiler_params=_params(1),
        name="conv_branch",
    )(proj3, proj3, proj3, proj3, conv_w, conv_b, ln_g, ln_b)


LSE_W = 128
LSE_REP = LSE_W // HEADS_PER_GROUP


def _attn_rows(q, k, v, n):
    blk = ATTN_BLOCK
    scale = HEAD_DIM ** -0.5
    qi = lax.broadcasted_iota(I32, (blk, 2 * blk), 0)
    ki = lax.broadcasted_iota(I32, (blk, 2 * blk), 1)
    band = (ki >= qi) & (ki <= qi + blk)
    lane_head = lax.broadcasted_iota(I32, (blk, LSE_W), 1) // LSE_REP
    o_rows, lse_rows = [], []
    for j in range(q.shape[0] // blk):
        mask = band
        if j == 0:
            mask = band & ((ki >= blk) | (n > 0))
        o_heads = []
        lse_tile = jnp.zeros((blk, LSE_W), F32)
        for hh in range(HEADS_PER_GROUP):
            cs = slice(hh * HEAD_DIM, (hh + 1) * HEAD_DIM)
            qh = q[j * blk:(j + 1) * blk, cs]
            kh = k[j * blk:(j + 2) * blk, cs]
            vh = v[j * blk:(j + 2) * blk, cs]
            s = lax.dot_general(qh, kh, (((1,), (1,)), ((), ())),
                                preferred_element_type=F32) * scale
            s = jnp.where(mask, s, -jnp.inf)
            m = jnp.max(s, axis=-1, keepdims=True)
            p = jnp.exp(s - m)
            l = jnp.sum(p, axis=-1, keepdims=True)
            o_heads.append(jnp.dot(p.astype(BF16), vh, preferred_element_type=F32) / l)
            lse_tile = jnp.where(lane_head == hh, m + jnp.log(l), lse_tile)
        o_rows.append(jnp.concatenate(o_heads, axis=1))
        lse_rows.append(lse_tile)
    return jnp.concatenate(o_rows, axis=0), jnp.concatenate(lse_rows, axis=0)


def _attn_kernel(q_ref, kc_ref, kp_ref, vc_ref, vp_ref, o_ref, lse_ref):
    n = pl.program_id(0)
    k = jnp.concatenate([kp_ref[...], kc_ref[...]], axis=0)
    v = jnp.concatenate([vp_ref[...], vc_ref[...]], axis=0)
    o, lse = _attn_rows(q_ref[...], k, v, n)
    o_ref[...] = o.astype(BF16)
    lse_ref[...] = lse


ATTN_TILE = 2048


def _attn_dilated_kernel(q_ref, kc_ref, kp_ref, vc_ref, vp_ref, o_ref, lse_ref,
                         qs, ks, vs, os_, ls, *, dil):
    n = pl.program_id(0)
    tile = q_ref.shape[0]
    mb = tile // dil
    half = dil // 2
    halo_w = ATTN_BLOCK * half
    n_slab = CB // 128
    q32 = pltpu.bitcast(q_ref[...], U32)
    kp32, kc32 = pltpu.bitcast(kp_ref[...], U32), pltpu.bitcast(kc_ref[...], U32)
    vp32, vc32 = pltpu.bitcast(vp_ref[...], U32), pltpu.bitcast(vc_ref[...], U32)
    for c in range(n_slab):
        cs = slice(c * 128, (c + 1) * 128)
        qs[c] = q32[:, cs]
        ks[c, 0:halo_w] = kp32[:, cs]
        ks[c, halo_w:] = kc32[:, cs]
        vs[c, 0:halo_w] = vp32[:, cs]
        vs[c, halo_w:] = vc32[:, cs]

    def gather(slabs, rp, rows):
        return jnp.concatenate(
            [slabs[c, pl.ds(rp, rows, stride=half), :] for c in range(n_slab)], axis=1)

    def residue(w, par):
        bits = (w << 16) if par == 0 else (w & jnp.uint32(HI_MASK))
        return lax.bitcast_convert_type(bits, F32).astype(BF16)

    for rp in range(half):
        wq = gather(qs, rp, mb)
        wk = gather(ks, rp, mb + ATTN_BLOCK)
        wv = gather(vs, rp, mb + ATTN_BLOCK)
        outs = []
        for par in range(2):
            o, lse = _attn_rows(residue(wq, par), residue(wk, par), residue(wv, par), n)
            outs.append(o)
            ls[pl.ds(2 * rp + par, mb, stride=dil), :] = lse
        w_out = _pack_bf16_pair(outs[0], outs[1])
        for c in range(n_slab):
            os_[c, pl.ds(rp, mb, stride=half), :] = w_out[:, c * 128:(c + 1) * 128]
    o32 = jnp.concatenate([os_[c] for c in range(n_slab)], axis=1)
    o_ref[...] = pltpu.bitcast(o32, BF16)
    lse_ref[...] = ls[...]


def _dilated_attention(proj3, group):
    _, dil = ATTN_PATTERNS[group]
    s = proj3.shape[1]
    cbs = (CB_Q + group, CB_K + group, CB_V + group)
    if dil == 1:
        tile, halo = 512, ATTN_BLOCK
        body, scratch = _attn_kernel, []
    else:
        tile, halo = ATTN_TILE, ATTN_BLOCK * dil
        body = functools.partial(_attn_dilated_kernel, dil=dil)
        scratch = [pltpu.VMEM((CB // 128, tile // 2, 128), U32),
                   pltpu.VMEM((CB // 128, (tile + halo) // 2, 128), U32),
                   pltpu.VMEM((CB // 128, (tile + halo) // 2, 128), U32),
                   pltpu.VMEM((CB // 128, tile // 2, 128), U32),
                   pltpu.VMEM((tile, LSE_W), F32)]
    hb = tile // halo
    cur = lambda cb: pl.BlockSpec((None, tile, CB), lambda n: (cb, n, 0))
    prev = lambda cb: pl.BlockSpec((None, halo, CB), lambda n: (cb, jnp.maximum(n * hb - 1, 0), 0))
    return pl.pallas_call(
        body,
        grid=(s // tile,),
        in_specs=[cur(cbs[0]), cur(cbs[1]), prev(cbs[1]), cur(cbs[2]), prev(cbs[2])],
        out_specs=[pl.BlockSpec((tile, CB), lambda n: (n, 0)),
                   pl.BlockSpec((tile, LSE_W), lambda n: (n, 0))],
        out_shape=[jax.ShapeDtypeStruct((s, CB), BF16),
                   jax.ShapeDtypeStruct((s, LSE_W), F32)],
        scratch_shapes=scratch,
        compiler_params=_params(1),
        name=f"dilated_attention_g{group}",
    )(proj3, proj3, proj3, proj3, proj3)


MIX_TM = 256


def _mixer_out_kernel(x_ref, o0_ref, o1_ref, o2_ref, l0_ref, l1_ref, l2_ref, u_ref,
                      gc0, gc1, gc2, gc3, ga0, ga1, ga2, ga3,
                      wa_ref, wc_ref, bc_ref, wo_ref, g2_ref, wrh_ref, wrl_ref, br_ref,
                      x1_ref, hp_ref, idx_ref, gate_ref, pos_ref, cnt_ref, carry_ref):
    i = pl.program_id(0)
    tm = MIX_TM

    @pl.when(i == 0)
    def _():
        carry_ref[...] = jnp.zeros_like(carry_ref)

    l_refs = (l0_ref, l1_ref, l2_ref)
    o_refs = (o0_ref, o1_ref, o2_ref)
    parts = []
    for hh in range(HEADS_PER_GROUP):
        cs = slice(hh * HEAD_DIM, (hh + 1) * HEAD_DIM)
        lse = [r[:, hh * LSE_REP:hh * LSE_REP + 1] for r in l_refs]
        m = jnp.maximum(jnp.maximum(lse[0], lse[1]), lse[2])
        e = [jnp.exp(v - m) for v in lse]
        den = e[0] + e[1] + e[2]
        acc = (e[0] / den) * o_refs[0][:, cs].astype(F32)
        for g in range(1, N_GROUPS):
            acc = acc + (e[g] / den) * o_refs[g][:, cs].astype(F32)
        parts.append(acc)
    o = jnp.concatenate(parts, axis=1).astype(BF16)
    ya = jnp.dot(o, wa_ref[...], preferred_element_type=F32)
    yc = jnp.dot(u_ref[...], wc_ref[...], preferred_element_type=F32) + bc_ref[...]
    gcs = (gc0, gc1, gc2, gc3)
    gas = (ga0, ga1, ga2, ga3)
    merged = []
    for c in range(D_MODEL // CB):
        cs = slice(c * CB, (c + 1) * CB)
        merged.append((gcs[c][...].astype(F32) * yc[:, cs]
                       + gas[c][...].astype(F32) * ya[:, cs]).astype(BF16))
    merged = jnp.concatenate(merged, axis=1)
    x1 = x_ref[...] + jnp.dot(merged, wo_ref[...], preferred_element_type=F32)
    x1_ref[...] = x1
    ms = jnp.mean(x1 * x1, axis=-1, keepdims=True)
    h2 = (x1 * lax.rsqrt(ms + NORM_EPS)) * g2_ref[...]
    h2_hi = h2.astype(BF16)
    hp_ref[...] = _pack_bf16_pair(h2[:, 0:PACK_W], h2[:, PACK_W:D_MODEL])
    h2_lo = (h2 - h2_hi.astype(F32)).astype(BF16)
    logits = (jnp.dot(h2_hi, wrh_ref[...], preferred_element_type=F32)
              + jnp.dot(h2_hi, wrl_ref[...], preferred_element_type=F32)
              + jnp.dot(h2_lo, wrh_ref[...], preferred_element_type=F32)) + br_ref[...]

    e_iota = lax.broadcasted_iota(I32, (tm, N_EXPERTS), 1).astype(F32)
    k_lane = lax.broadcasted_iota(I32, (tm, TOP_K), 1)
    vals = logits
    sels = []
    idx_t = jnp.zeros((tm, TOP_K), F32)
    val_t = jnp.zeros((tm, TOP_K), F32)
    for k in range(TOP_K):
        mk = jnp.max(vals, axis=-1, keepdims=True)
        ik = jnp.min(jnp.where(vals == mk, e_iota, float(N_EXPERTS)), axis=-1, keepdims=True)
        sel = e_iota == ik
        sels.append(sel)
        vals = jnp.where(sel, -jnp.inf, vals)
        idx_t = jnp.where(k_lane == k, ik, idx_t)
        val_t = jnp.where(k_lane == k, mk, val_t)
    ex = jnp.exp(val_t - val_t[:, 0:1])
    gate_ref[...] = ex / jnp.sum(ex, axis=-1, keepdims=True)
    idx_ref[...] = idx_t.astype(I32)

    cnt = jnp.zeros((tm, N_EXPERTS), F32)
    for sel in sels:
        cnt = cnt + jnp.where(sel, 1.0, 0.0)
    ri = lax.broadcasted_iota(I32, (tm, tm), 0)
    ci = lax.broadcasted_iota(I32, (tm, tm), 1)
    lower = jnp.where(ci < ri, 1.0, 0.0).astype(BF16)
    prefix = jnp.dot(lower, cnt.astype(BF16), preferred_element_type=F32) + carry_ref[...]
    pos_t = jnp.zeros((tm, TOP_K), F32)
    for k, sel in enumerate(sels):
        pk = jnp.sum(jnp.where(sel, prefix, 0.0), axis=-1, keepdims=True)
        pos_t = jnp.where(k_lane == k, pk, pos_t)
    pos_ref[...] = pos_t.astype(I32)
    carry_ref[...] = carry_ref[...] + jnp.sum(cnt, axis=0, keepdims=True)
    cnt_ref[...] = carry_ref[...]


def _mixer_output(x2, outs, lses, u_ln, proj3, wa, wc, bc, wo, g2, wr_hi, wr_lo, br):
    s = x2.shape[0]
    row = lambda w: pl.BlockSpec((MIX_TM, w), lambda i: (i, 0))
    gate = lambda cb: pl.BlockSpec((None, MIX_TM, CB), lambda i: (cb, i, 0))
    full = lambda a: pl.BlockSpec(a.shape, lambda i: (0,) * a.ndim)
    in_specs = ([row(D_MODEL)] + [row(CB)] * 3 + [row(LSE_W)] * 3 + [row(D_CONV)]
                + [gate(CB_GATE_C + c) for c in range(4)]
                + [gate(CB_GATE_A + c) for c in range(4)]
                + [full(a) for a in (wa, wc, bc, wo, g2, wr_hi, wr_lo, br)])
    return pl.pallas_call(
        _mixer_out_kernel,
        grid=(s // MIX_TM,),
        in_specs=in_specs,
        out_specs=[row(D_MODEL), row(PACK_W), row(TOP_K), row(TOP_K), row(TOP_K),
                   pl.BlockSpec((1, N_EXPERTS), lambda i: (0, 0))],
        out_shape=[jax.ShapeDtypeStruct((s, D_MODEL), F32),
                   jax.ShapeDtypeStruct((s, PACK_W), U32),
                   jax.ShapeDtypeStruct((s, TOP_K), I32),
                   jax.ShapeDtypeStruct((s, TOP_K), F32),
                   jax.ShapeDtypeStruct((s, TOP_K), I32),
                   jax.ShapeDtypeStruct((1, N_EXPERTS), F32)],
        scratch_shapes=[pltpu.VMEM((1, N_EXPERTS), F32)],
        compiler_params=_params(1),
        name="mixer_output",
    )(x2, *outs, *lses, u_ln, *([proj3] * 8), wa, wc, bc, wo, g2, wr_hi, wr_lo, br)


ROW_BLK = 128
DISP_TM = 256


def _dispatch_kernel(fill_ref, dest_ref, h_ref, xs_ref, zbuf, sem):
    i = pl.program_id(0)
    n_fill = fill_ref.shape[0]

    @pl.when(i == 0)
    def _():
        zbuf[...] = jnp.zeros_like(zbuf)

        def fill_copy(j):
            return pltpu.make_async_copy(
                zbuf, xs_ref.at[pl.ds(fill_ref[j] * ROW_BLK, ROW_BLK)], sem.at[0])

        def start(j, c):
            @pl.when(fill_ref[j] >= 0)
            def _():
                fill_copy(j).start()
            return c

        def wait(j, c):
            @pl.when(fill_ref[j] >= 0)
            def _():
                fill_copy(j).wait()
            return c

        lax.fori_loop(0, n_fill, start, 0)
        lax.fori_loop(0, n_fill, wait, 0)

    def rows8(tt, c):
        for si in range(8):
            for k in range(TOP_K):
                d = dest_ref[0, (tt * 8 + si) * TOP_K + k]
                pltpu.make_async_copy(h_ref.at[tt, pl.ds(si, 1)], xs_ref.at[pl.ds(d, 1)],
                                      sem.at[0]).start(priority=k % 2)
        return c

    lax.fori_loop(0, DISP_TM // 8, rows8, 0)
    for _ in range(TOP_K):
        pltpu.make_async_copy(h_ref, h_ref, sem.at[0]).wait()


def _dispatch(hp, dest, fill_blocks, n_rows):
    s = hp.shape[0]
    dest3 = dest.reshape(s // DISP_TM, 1, DISP_TM * TOP_K)
    grid_spec = pltpu.PrefetchScalarGridSpec(
        num_scalar_prefetch=1,
        grid=(s // DISP_TM,),
        in_specs=[
            pl.BlockSpec((None, 1, DISP_TM * TOP_K), lambda i, fr: (i, 0, 0),
                         memory_space=pltpu.SMEM),
            pl.BlockSpec((DISP_TM // 8, 8, PACK_W), lambda i, fr: (i, 0, 0)),
        ],
        out_specs=pl.BlockSpec(memory_space=pl.ANY),
        scratch_shapes=[pltpu.VMEM((ROW_BLK, PACK_W), U32), pltpu.SemaphoreType.DMA((1,))],
    )
    return pl.pallas_call(
        _dispatch_kernel,
        grid_spec=grid_spec,
        out_shape=jax.ShapeDtypeStruct((n_rows, PACK_W), U32),
        compiler_params=_params(1),
        name="moe_dispatch",
    )(fill_blocks, dest3, hp.reshape(s // 8, 8, PACK_W))


UNIT_BLOCKS = 10
UNIT_ROWS = ROW_BLK * UNIT_BLOCKS
GROUP_SIZES = (8, 4, 2, 1)
EXP_TF = 512
EXP_NF = D_EXPERT // EXP_TF


def _expert_kernel(ue_ref, ub0_ref, unb_ref, nu_ref,
                   xs_ref, win_ref, wout_ref, bin_ref, bo_ref,
                   ys_ref,
                   xbuf, xb16, acc, wg_buf, wu_buf, wo_buf, sem_x, sem_y, sem_w):
    u = pl.program_id(0)
    n_units = nu_ref[0]

    def x_copy(unit, b):
        src = xs_ref.at[pl.ds((ub0_ref[unit] + b) * ROW_BLK, ROW_BLK)]
        dst = xbuf.at[pl.ds(b * ROW_BLK, ROW_BLK)]
        return pltpu.make_async_copy(src, dst, sem_x.at[0])

    def y_copy(unit, b):
        src = acc.at[pl.ds(b * ROW_BLK, ROW_BLK)]
        dst = ys_ref.at[pl.ds((ub0_ref[unit] + b) * ROW_BLK, ROW_BLK)]
        return pltpu.make_async_copy(src, dst, sem_y.at[0])

    def w_copies(unit, f, slot):
        e = ue_ref[unit]
        c0 = pl.multiple_of(f * EXP_TF, EXP_TF)
        return (
            pltpu.make_async_copy(win_ref.at[e, :, pl.ds(c0, EXP_TF)], wg_buf.at[slot],
                                  sem_w.at[slot]),
            pltpu.make_async_copy(win_ref.at[e, :, pl.ds(D_EXPERT + c0, EXP_TF)], wu_buf.at[slot],
                                  sem_w.at[slot]),
            pltpu.make_async_copy(wout_ref.at[e, pl.ds(c0, EXP_TF), :], wo_buf.at[slot],
                                  sem_w.at[slot]),
        )

    def start_w(unit, f, slot):
        for cp in w_copies(unit, f, slot):
            cp.start()

    def wait_w(unit, f, slot):
        for cp in w_copies(unit, f, slot):
            cp.wait()

    def start_x(unit):
        lax.fori_loop(0, unb_ref[unit], lambda b, c: (x_copy(unit, b).start(), c)[1], 0)

    def wait_x(unit):
        lax.fori_loop(0, unb_ref[unit], lambda b, c: (x_copy(unit, b).wait(), c)[1], 0)

    def wait_y(unit):
        lax.fori_loop(0, unb_ref[unit], lambda b, c: (y_copy(unit, b).wait(), c)[1], 0)

    @pl.when(u < n_units)
    def _():
        nblk = unb_ref[u]

        @pl.when(u == 0)
        def _():
            start_x(u)
            start_w(u, 0, 0)

        wait_x(u)

        @pl.when(u > 0)
        def _():
            wait_y(u - 1)

        def init(b, c):
            r0 = pl.multiple_of(b * ROW_BLK, ROW_BLK)
            lo, hi = _unpack_bf16_pair(xbuf[pl.ds(r0, ROW_BLK), :])
            xb16[pl.ds(r0, ROW_BLK), 0:PACK_W] = lo
            xb16[pl.ds(r0, ROW_BLK), PACK_W:D_MODEL] = hi
            acc[pl.ds(r0, ROW_BLK), :] = jnp.broadcast_to(bo_ref[...], (ROW_BLK, D_MODEL))
            return c

        lax.fori_loop(0, nblk, init, 0)

        @pl.when(u + 1 < n_units)
        def _():
            start_x(u + 1)

        def group(b0, n_blk, f, ws):
            rows = n_blk * ROW_BLK
            r0 = pl.multiple_of(b0 * ROW_BLK, ROW_BLK)
            c0 = pl.multiple_of(f * EXP_TF, EXP_TF)
            x = xb16[pl.ds(r0, rows), :]
            wgu = jnp.concatenate([wg_buf[ws].astype(BF16), wu_buf[ws].astype(BF16)], axis=1)
            gu = jnp.dot(x, wgu, preferred_element_type=F32)
            bg = bin_ref[:, pl.ds(c0, EXP_TF)]
            bu = bin_ref[:, pl.ds(D_EXPERT + c0, EXP_TF)]
            g = jnp.minimum(gu[:, 0:EXP_TF] + bg, SWIGLU_LIMIT)
            up = jnp.clip(gu[:, EXP_TF:2 * EXP_TF] + bu, -SWIGLU_LIMIT, SWIGLU_LIMIT)
            act = (up + 1.0) * (g * _sigmoid(SWIGLU_ALPHA * g))
            acc[pl.ds(r0, rows), :] += jnp.dot(
                act.astype(BF16), wo_buf[ws].astype(BF16), preferred_element_type=F32)

            @pl.when(f == EXP_NF - 1)
            def _():
                for j in range(n_blk):
                    y_copy(u, b0 + j).start()

        def f_tile(f, ws):
            big = GROUP_SIZES[0]
            n_big = nblk // big
            lax.fori_loop(0, n_big, lambda q, c: (group(big * q, big, f, ws), c)[1], 0)
            for size in GROUP_SIZES[1:]:
                done = (nblk // (2 * size)) * (2 * size)

                @pl.when(lax.rem(nblk, 2 * size) >= size)
                def _():
                    group(done, size, f, ws)

        def f_step(f, c):
            ws = lax.rem(f, 2)

            @pl.when(f + 1 < EXP_NF)
            def _():
                start_w(u, f + 1, 1 - ws)

            @pl.when((f + 1 == EXP_NF) & (u + 1 < n_units))
            def _():
                start_w(u + 1, 0, 1 - ws)

            wait_w(u, f, ws)
            f_tile(f, ws)
            return c

        lax.fori_loop(0, EXP_NF, f_step, 0)

        @pl.when(u == n_units - 1)
        def _():
            wait_y(u)

    @pl.when(u == pl.num_programs(0) - 1)
    def _():
        n_blocks = ys_ref.shape[0] // ROW_BLK
        acc[0:ROW_BLK, :] = jnp.zeros((ROW_BLK, D_MODEL), F32)

        def tail_copy(b):
            return pltpu.make_async_copy(acc.at[pl.ds(0, ROW_BLK)],
                                         ys_ref.at[pl.ds(b * ROW_BLK, ROW_BLK)], sem_y.at[0])

        lax.fori_loop(nu_ref[1], n_blocks, lambda b, c: (tail_copy(b).start(), c)[1], 0)
        lax.fori_loop(nu_ref[1], n_blocks, lambda b, c: (tail_copy(b).wait(), c)[1], 0)


def _expert_ffn(xs, unit_e, unit_blk0, unit_nblk, unit_counts, w_in, b_in, w_out, b_out):
    n_rows = xs.shape[0]
    n_units_max = unit_e.shape[0]

    def e_eff(u, ue, nu):
        return ue[jnp.minimum(u, jnp.maximum(nu[0] - 1, 0))]

    any_spec = pl.BlockSpec(memory_space=pl.ANY)
    in_specs = [
        any_spec, any_spec, any_spec,
        pl.BlockSpec((None, 1, 2 * D_EXPERT), lambda u, ue, ub, un, nu: (e_eff(u, ue, nu), 0, 0)),
        pl.BlockSpec((None, 1, D_MODEL), lambda u, ue, ub, un, nu: (e_eff(u, ue, nu), 0, 0)),
    ]
    grid_spec = pltpu.PrefetchScalarGridSpec(
        num_scalar_prefetch=4,
        grid=(n_units_max,),
        in_specs=in_specs,
        out_specs=any_spec,
        scratch_shapes=[
            pltpu.VMEM((UNIT_ROWS, PACK_W), U32),
            pltpu.VMEM((UNIT_ROWS, D_MODEL), BF16),
            pltpu.VMEM((UNIT_ROWS, D_MODEL), F32),
            pltpu.VMEM((2, D_MODEL, EXP_TF), F32),
            pltpu.VMEM((2, D_MODEL, EXP_TF), F32),
            pltpu.VMEM((2, EXP_TF, D_MODEL), F32),
            pltpu.SemaphoreType.DMA((1,)),
            pltpu.SemaphoreType.DMA((1,)),
            pltpu.SemaphoreType.DMA((2,)),
        ],
    )
    return pl.pallas_call(
        _expert_kernel,
        grid_spec=grid_spec,
        out_shape=jax.ShapeDtypeStruct((n_rows, D_MODEL), F32),
        compiler_params=_params(1),
        name="expert_ffn",
    )(unit_e, unit_blk0, unit_nblk, unit_counts, xs,
      w_in, w_out, b_in[:, None, :], b_out[:, None, :])


COMB_TM = 256


def _combine_kernel(dcur_ref, dnxt_ref, x1_ref, gate_ref, g_ref, ys_ref, o_ref, ybuf, sem):
    i = pl.program_id(0)
    n = pl.num_programs(0)
    slot = lax.rem(i, 2)

    def issue(dref, s):
        def rows8(tt, c):
            for si in range(8):
                for k in range(TOP_K):
                    d = dref[0, (tt * 8 + si) * TOP_K + k]
                    pltpu.make_async_copy(ys_ref.at[pl.ds(d, 1)],
                                          ybuf.at[s, k, tt, pl.ds(si, 1)],
                                          sem.at[s]).start(priority=k % 2)
            return c

        lax.fori_loop(0, COMB_TM // 8, rows8, 0)

    @pl.when(i == 0)
    def _():
        issue(dcur_ref, slot)

    @pl.when(i + 1 < n)
    def _():
        issue(dnxt_ref, 1 - slot)

    for k in range(TOP_K):
        pltpu.make_async_copy(ybuf.at[slot, k], ybuf.at[slot, k], sem.at[slot]).wait()

    gate = gate_ref[...]
    y = x1_ref[...]
    for k in range(TOP_K):
        y = y + gate[:, k:k + 1] * ybuf[slot, k].reshape(COMB_TM, D_MODEL)
    ms = jnp.mean(y * y, axis=-1, keepdims=True)
    o_ref[...] = (y * lax.rsqrt(ms + NORM_EPS)) * g_ref[...]


def _combine(dest, x1, gate, norm_g, ys):
    s = x1.shape[0]
    nt = s // COMB_TM
    dest3 = dest.reshape(nt, 1, COMB_TM * TOP_K)
    smem = lambda imap: pl.BlockSpec((None, 1, COMB_TM * TOP_K), imap, memory_space=pltpu.SMEM)
    row = lambda w: pl.BlockSpec((COMB_TM, w), lambda i: (i, 0))
    return pl.pallas_call(
        _combine_kernel,
        grid=(nt,),
        in_specs=[smem(lambda i: (i, 0, 0)),
                  smem(lambda i: (jnp.minimum(i + 1, nt - 1), 0, 0)),
                  row(D_MODEL), row(TOP_K), pl.BlockSpec((1, D_MODEL), lambda i: (0, 0)),
                  pl.BlockSpec(memory_space=pl.ANY)],
        out_specs=row(D_MODEL),
        out_shape=jax.ShapeDtypeStruct((s, D_MODEL), F32),
        scratch_shapes=[pltpu.VMEM((2, TOP_K, COMB_TM // 8, 8, D_MODEL), F32),
                        pltpu.SemaphoreType.DMA((2,))],
        compiler_params=_params(1),
        name="moe_combine",
    )(dest3, dest3, x1, gate, norm_g, ys)


def _route_tables(idx, pos, counts, n_tok):
    n_assign = n_tok * TOP_K
    counts = counts.reshape(N_EXPERTS).astype(I32)
    nblk_e = (counts + ROW_BLK - 1) // ROW_BLK
    pad_start = (jnp.cumsum(nblk_e) - nblk_e) * ROW_BLK
    onehot = idx[..., None] == jnp.arange(N_EXPERTS, dtype=I32)
    dest = pos + jnp.sum(jnp.where(onehot, pad_start, 0), axis=-1)

    n_blocks = -(-(n_assign + N_EXPERTS * (ROW_BLK - 1)) // ROW_BLK)
    used = jnp.sum(nblk_e)
    blk_ids = jnp.arange(n_blocks, dtype=I32)
    last_blk = pad_start // ROW_BLK + nblk_e - 1
    fill_blocks = jnp.concatenate([jnp.where(nblk_e > 0, last_blk, -1),
                                   jnp.where(blk_ids >= used, blk_ids, -1)]).astype(I32)

    n_units_max = (n_blocks + (UNIT_BLOCKS - 1) * N_EXPERTS) // UNIT_BLOCKS
    chunks_e = (nblk_e + UNIT_BLOCKS - 1) // UNIT_BLOCKS
    chunk_end = jnp.cumsum(chunks_e)
    chunk_start = chunk_end - chunks_e
    n_units = chunk_end[-1]
    uid = jnp.arange(n_units_max, dtype=I32)
    unit_e = jnp.minimum(jnp.searchsorted(chunk_end, uid, side="right"), N_EXPERTS - 1).astype(I32)
    c_in_e = uid - chunk_start[unit_e]
    unit_blk0 = (pad_start[unit_e] // ROW_BLK + c_in_e * UNIT_BLOCKS).astype(I32)
    unit_nblk = jnp.clip(nblk_e[unit_e] - c_in_e * UNIT_BLOCKS, 0, UNIT_BLOCKS).astype(I32)
    active = uid < n_units
    unit_blk0 = jnp.where(active, unit_blk0, 0)
    unit_nblk = jnp.where(active, unit_nblk, 0)
    unit_counts = jnp.stack([n_units, used]).astype(I32)
    return dest.astype(I32), fill_blocks, n_blocks * ROW_BLK, unit_e, unit_blk0, unit_nblk, unit_counts


def kernel(x, norm1_g, w_in, b_gate, conv_w, conv_b, conv_ln_g, conv_ln_b, w_conv_out, b_conv_out,
           w_attn_out, w_out, norm2_g, w_router, b_router, w_exp_in, b_exp_in, w_exp_out, b_exp_out,
           norm_f_g):
    b_sz, s_len, d = x.shape
    n_tok = b_sz * s_len
    x2 = x.reshape(n_tok, d)
    l = 0
    b_gate_ext = jnp.concatenate([jnp.zeros((GATE_COL0,), F32), b_gate[l]])[None, :]
    proj3 = _in_projection(x2, norm1_g[l][None, :], w_in[l], b_gate_ext)
    u_ln = _conv_branch(proj3, conv_w[l], conv_b[l][None, :], conv_ln_g[l][None, :],
                        conv_ln_b[l][None, :])
    outs, lses = [], []
    for g in range(N_GROUPS):
        o_g, lse_g = _dilated_attention(proj3, g)
        outs.append(o_g)
        lses.append(lse_g)
    wr = w_router[l]
    wr_hi = wr.astype(BF16)
    wr_lo = (wr - wr_hi.astype(F32)).astype(BF16)
    x1, hp, idx, gate, pos, counts = _mixer_output(
        x2, outs, lses, u_ln, proj3,
        w_attn_out[l].astype(BF16), w_conv_out[l].astype(BF16), b_conv_out[l][None, :],
        w_out[l].astype(BF16), norm2_g[l][None, :], wr_hi, wr_lo, b_router[l][None, :])

    dest, fill_blocks, n_rows, unit_e, unit_blk0, unit_nblk, unit_counts = _route_tables(
        idx, pos, counts, n_tok)
    xs = _dispatch(hp, dest, fill_blocks, n_rows)
    ys = _expert_ffn(xs, unit_e, unit_blk0, unit_nblk, unit_counts,
                     w_exp_in[l], b_exp_in[l], w_exp_out[l], b_exp_out[l])
    out = _combine(dest, x1, gate, norm_f_g[None, :], ys)
    return out.reshape(b_sz, s_len, d)
```

```python
import functools

import jax
import jax.numpy as jnp
from jax import lax
from jax.experimental import pallas as pl
from jax.experimental.pallas import tpu as pltpu

D_MODEL = 2048
SEQ = 8192
D_CONV = D_MODEL // 2
CONV_WIDTH = 31
HEAD_DIM = 128
HEADS_PER_GROUP = 4
ATTN_PATTERNS = ((128, 1), (512, 4), (2048, 16))
N_GROUPS = len(ATTN_PATTERNS)
ATTN_WIDTH = N_GROUPS * HEADS_PER_GROUP * HEAD_DIM
ATTN_OUT_WIDTH = HEADS_PER_GROUP * HEAD_DIM
ATTN_BLOCK = 128
D_IN = 2 * D_CONV + 3 * ATTN_WIDTH + 2 * D_MODEL
N_EXPERTS = 32
TOP_K = 4
D_EXPERT = D_MODEL
SWIGLU_LIMIT = 7.0
SWIGLU_ALPHA = 1.702
NORM_EPS = 1e-5

CB = 512
N_CB = D_IN // CB
CB_CONV_A, CB_CONV_G = 0, 2
CB_Q, CB_K, CB_V = 4, 7, 10
CB_GATE_C, CB_GATE_A = 13, 17
GATE_COL0 = CB_GATE_C * CB

VMEM_LIMIT = 56 * 1024 * 1024

F32 = jnp.float32
BF16 = jnp.bfloat16
U32 = jnp.uint32
I32 = jnp.int32

PACK_W = D_MODEL // 2
HI_MASK = 0xFFFF0000


def _sigmoid(z):
    return 0.5 * jnp.tanh(0.5 * z) + 0.5


def _pack_bf16_pair(lo, hi):
    lo_bits = lax.bitcast_convert_type(lo.astype(BF16).astype(F32), U32) >> 16
    hi_bits = lax.bitcast_convert_type(hi.astype(BF16).astype(F32), U32) & jnp.uint32(HI_MASK)
    return hi_bits | lo_bits


def _unpack_bf16_pair(w):
    lo = lax.bitcast_convert_type(w << 16, F32).astype(BF16)
    hi = lax.bitcast_convert_type(w & jnp.uint32(HI_MASK), F32).astype(BF16)
    return lo, hi


def _params(n_axes, vmem=VMEM_LIMIT):
    return pltpu.CompilerParams(
        dimension_semantics=("arbitrary",) * n_axes, vmem_limit_bytes=vmem)


NORM_TM = 512
IN_TM = 1024
IN_TN = 1536


def _rmsnorm_kernel(x_ref, g_ref, o_ref):
    x = x_ref[...]
    ms = jnp.mean(x * x, axis=-1, keepdims=True)
    o_ref[...] = ((x * lax.rsqrt(ms + NORM_EPS)) * g_ref[...]).astype(BF16)


def _rmsnorm_bf16(x2, norm_g):
    s = x2.shape[0]
    row = pl.BlockSpec((NORM_TM, D_MODEL), lambda i: (i, 0))
    return pl.pallas_call(
        _rmsnorm_kernel,
        grid=(s // NORM_TM,),
        in_specs=[row, pl.BlockSpec((1, D_MODEL), lambda i: (0, 0))],
        out_specs=row,
        out_shape=jax.ShapeDtypeStruct((s, D_MODEL), BF16),
        compiler_params=_params(1),
        name="rmsnorm_in",
    )(x2, norm_g)


def _inproj_kernel(h_ref, w_ref, bg_ref, o_ref, wbf_ref):
    j = pl.program_id(0)
    i = pl.program_id(1)

    @pl.when(i == 0)
    def _():
        wbf_ref[...] = w_ref[...].astype(BF16)

    acc = jnp.dot(h_ref[...], wbf_ref[...], preferred_element_type=F32)

    first_gate_tile = GATE_COL0 // IN_TN
    n_sub = IN_TN // CB

    def store(val):
        for c in range(n_sub):
            o_ref[c] = val[:, c * CB:(c + 1) * CB].astype(BF16)

    @pl.when(j < first_gate_tile)
    def _():
        store(acc)

    @pl.when(j == first_gate_tile)
    def _():
        col = j * IN_TN + lax.broadcasted_iota(I32, (1, IN_TN), 1)
        store(jnp.where(col >= GATE_COL0, _sigmoid(acc + bg_ref[...]), acc))

    @pl.when(j > first_gate_tile)
    def _():
        store(_sigmoid(acc + bg_ref[...]))


def _in_projection(x2, norm_g, w_in, b_gate_ext):
    s = x2.shape[0]
    h = _rmsnorm_bf16(x2, norm_g)
    grid = (D_IN // IN_TN, s // IN_TM)
    return pl.pallas_call(
        _inproj_kernel,
        grid=grid,
        in_specs=[
            pl.BlockSpec((IN_TM, D_MODEL), lambda j, i: (i, 0)),
            pl.BlockSpec((D_MODEL, IN_TN), lambda j, i: (0, j)),
            pl.BlockSpec((1, IN_TN), lambda j, i: (0, j)),
        ],
        out_specs=pl.BlockSpec((IN_TN // CB, IN_TM, CB), lambda j, i: (j, i, 0)),
        out_shape=jax.ShapeDtypeStruct((N_CB, s, CB), BF16),
        scratch_shapes=[pltpu.VMEM((D_MODEL, IN_TN), BF16)],
        compiler_params=_params(2),
        name="in_projection",
    )(h, w_in, b_gate_ext)


CONV_TS = 256
CONV_HALO = 32
CONV_RC = 32
CONV_N = CONV_TS + CONV_HALO


def _conv_kernel(ac_ref, gc_ref, ah_ref, gh_ref, w_ref, cb_ref, lg_ref, lb_ref, o_ref, r_ref,
                 c_ref):
    i = pl.program_id(0)
    has_prev = i > 0
    half = D_CONV // 2
    for c in range(2):
        cs = slice(c * half, (c + 1) * half)
        uh = ah_ref[c].astype(F32) * _sigmoid(gh_ref[c].astype(F32))
        r_ref[0, 0:CONV_HALO, cs] = jnp.where(has_prev, uh, 0.0)
        r_ref[0, CONV_HALO:CONV_N, cs] = ac_ref[c].astype(F32) * _sigmoid(gc_ref[c].astype(F32))
    u_ext = r_ref[0]
    for b in range(1, 8):
        r_ref[b] = pltpu.roll(u_ext, CONV_N - b, axis=0)

    def chunk(ci, carry):
        r0 = pl.multiple_of(ci * CONV_RC, CONV_RC)
        acc = jnp.broadcast_to(cb_ref[...], (CONV_RC, D_CONV))
        for k in range(CONV_WIDTH):
            kp = k + (CONV_HALO - (CONV_WIDTH - 1))
            a, b = kp // 8, kp % 8
            acc = acc + w_ref[k:k + 1, :] * r_ref[b, pl.ds(r0 + 8 * a, CONV_RC), :]
        c_ref[pl.ds(r0, CONV_RC), :] = acc
        return carry

    lax.fori_loop(0, CONV_TS // CONV_RC, chunk, 0)
    conv = c_ref[...]
    mu = jnp.mean(conv, axis=-1, keepdims=True)
    d = conv - mu
    var = jnp.mean(d * d, axis=-1, keepdims=True)
    y = (d * lax.rsqrt(var + NORM_EPS)) * lg_ref[...] + lb_ref[...]
    o_ref[...] = (y * _sigmoid(y)).astype(BF16)


def _conv_branch(proj3, conv_w, conv_b, ln_g, ln_b):
    s = proj3.shape[1]
    hb = CONV_TS // CONV_HALO
    cur = lambda blk: pl.BlockSpec((2, CONV_TS, CB), lambda i: (blk // 2, i, 0))
    halo = lambda blk: pl.BlockSpec(
        (2, CONV_HALO, CB), lambda i: (blk // 2, jnp.maximum(i * hb - 1, 0), 0))
    vec = pl.BlockSpec((1, D_CONV), lambda i: (0, 0))
    return pl.pallas_call(
        _conv_kernel,
        grid=(s // CONV_TS,),
        in_specs=[cur(CB_CONV_A), cur(CB_CONV_G), halo(CB_CONV_A), halo(CB_CONV_G),
                  pl.BlockSpec((CONV_WIDTH, D_CONV), lambda i: (0, 0)), vec, vec, vec],
        out_specs=pl.BlockSpec((CONV_TS, D_CONV), lambda i: (i, 0)),
        out_shape=jax.ShapeDtypeStruct((s, D_CONV), BF16),
        scratch_shapes=[pltpu.VMEM((8, CONV_N, D_CONV), F32), pltpu.VMEM((CONV_TS, D_CONV), F32)],
        compiler_params=_params(1),
        name="conv_branch",
    )(proj3, proj3, proj3, proj3, conv_w, conv_b, ln_g, ln_b)


LSE_W = 128
LSE_REP = LSE_W // HEADS_PER_GROUP


def _attn_rows(q, k, v, n):
    blk = ATTN_BLOCK
    scale = HEAD_DIM ** -0.5
    qi = lax.broadcasted_iota(I32, (blk, 2 * blk), 0)
    ki = lax.broadcasted_iota(I32, (blk, 2 * blk), 1)
    band = (ki >= qi) & (ki <= qi + blk)
    lane_head = lax.broadcasted_iota(I32, (blk, LSE_W), 1) // LSE_REP
    o_rows, lse_rows = [], []
    for j in range(q.shape[0] // blk):
        mask = band
        if j == 0:
            mask = band & ((ki >= blk) | (n > 0))
        o_heads = []
        lse_tile = jnp.zeros((blk, LSE_W), F32)
        for hh in range(HEADS_PER_GROUP):
            cs = slice(hh * HEAD_DIM, (hh + 1) * HEAD_DIM)
            qh = q[j * blk:(j + 1) * blk, cs]
            kh = k[j * blk:(j + 2) * blk, cs]
            vh = v[j * blk:(j + 2) * blk, cs]
            s = lax.dot_general(qh, kh, (((1,), (1,)), ((), ())),
                                preferred_element_type=F32) * scale
            s = jnp.where(mask, s, -jnp.inf)
            m = jnp.max(s, axis=-1, keepdims=True)
            p = jnp.exp(s - m)
            l = jnp.sum(p, axis=-1, keepdims=True)
            o_heads.append(jnp.dot(p.astype(BF16), vh, preferred_element_type=F32) / l)
            lse_tile = jnp.where(lane_head == hh, m + jnp.log(l), lse_tile)
        o_rows.append(jnp.concatenate(o_heads, axis=1))
        lse_rows.append(lse_tile)
    return jnp.concatenate(o_rows, axis=0), jnp.concatenate(lse_rows, axis=0)


def _attn_kernel(q_ref, kc_ref, kp_ref, vc_ref, vp_ref, o_ref, lse_ref):
    n = pl.program_id(0)
    k = jnp.concatenate([kp_ref[...], kc_ref[...]], axis=0)
    v = jnp.concatenate([vp_ref[...], vc_ref[...]], axis=0)
    o, lse = _attn_rows(q_ref[...], k, v, n)
    o_ref[...] = o.astype(BF16)
    lse_ref[...] = lse


ATTN_TILE = 2048


def _attn_dilated_kernel(q_ref, kc_ref, kp_ref, vc_ref, vp_ref, o_ref, lse_ref,
                         qs, ks, vs, os_, ls, *, dil):
    n = pl.program_id(0)
    tile = q_ref.shape[0]
    mb = tile // dil
    half = dil // 2
    halo_w = ATTN_BLOCK * half
    n_slab = CB // 128
    q32 = pltpu.bitcast(q_ref[...], U32)
    kp32, kc32 = pltpu.bitcast(kp_ref[...], U32), pltpu.bitcast(kc_ref[...], U32)
    vp32, vc32 = pltpu.bitcast(vp_ref[...], U32), pltpu.bitcast(vc_ref[...], U32)
    for c in range(n_slab):
        cs = slice(c * 128, (c + 1) * 128)
        qs[c] = q32[:, cs]
        ks[c, 0:halo_w] = kp32[:, cs]
        ks[c, halo_w:] = kc32[:, cs]
        vs[c, 0:halo_w] = vp32[:, cs]
        vs[c, halo_w:] = vc32[:, cs]

    def gather(slabs, rp, rows):
        return jnp.concatenate(
            [slabs[c, pl.ds(rp, rows, stride=half), :] for c in range(n_slab)], axis=1)

    def residue(w, par):
        bits = (w << 16) if par == 0 else (w & jnp.uint32(HI_MASK))
        return lax.bitcast_convert_type(bits, F32).astype(BF16)

    for rp in range(half):
        wq = gather(qs, rp, mb)
        wk = gather(ks, rp, mb + ATTN_BLOCK)
        wv = gather(vs, rp, mb + ATTN_BLOCK)
        outs = []
        for par in range(2):
            o, lse = _attn_rows(residue(wq, par), residue(wk, par), residue(wv, par), n)
            outs.append(o)
            ls[pl.ds(2 * rp + par, mb, stride=dil), :] = lse
        w_out = _pack_bf16_pair(outs[0], outs[1])
        for c in range(n_slab):
            os_[c, pl.ds(rp, mb, stride=half), :] = w_out[:, c * 128:(c + 1) * 128]
    o32 = jnp.concatenate([os_[c] for c in range(n_slab)], axis=1)
    o_ref[...] = pltpu.bitcast(o32, BF16)
    lse_ref[...] = ls[...]


def _dilated_attention(proj3, group):
    _, dil = ATTN_PATTERNS[group]
    s = proj3.shape[1]
    cbs = (CB_Q + group, CB_K + group, CB_V + group)
    if dil == 1:
        tile, halo = 512, ATTN_BLOCK
        body, scratch = _attn_kernel, []
    else:
        tile, halo = ATTN_TILE, ATTN_BLOCK * dil
        body = functools.partial(_attn_dilated_kernel, dil=dil)
        scratch = [pltpu.VMEM((CB // 128, tile // 2, 128), U32),
                   pltpu.VMEM((CB // 128, (tile + halo) // 2, 128), U32),
                   pltpu.VMEM((CB // 128, (tile + halo) // 2, 128), U32),
                   pltpu.VMEM((CB // 128, tile // 2, 128), U32),
                   pltpu.VMEM((tile, LSE_W), F32)]
    hb = tile // halo
    cur = lambda cb: pl.BlockSpec((None, tile, CB), lambda n: (cb, n, 0))
    prev = lambda cb: pl.BlockSpec((None, halo, CB), lambda n: (cb, jnp.maximum(n * hb - 1, 0), 0))
    return pl.pallas_call(
        body,
        grid=(s // tile,),
        in_specs=[cur(cbs[0]), cur(cbs[1]), prev(cbs[1]), cur(cbs[2]), prev(cbs[2])],
        out_specs=[pl.BlockSpec((tile, CB), lambda n: (n, 0)),
                   pl.BlockSpec((tile, LSE_W), lambda n: (n, 0))],
        out_shape=[jax.ShapeDtypeStruct((s, CB), BF16),
                   jax.ShapeDtypeStruct((s, LSE_W), F32)],
        scratch_shapes=scratch,
        compiler_params=_params(1),
        name=f"dilated_attention_g{group}",
    )(proj3, proj3, proj3, proj3, proj3)


MIX_TM = 256


def _mixer_out_kernel(x_ref, o0_ref, o1_ref, o2_ref, l0_ref, l1_ref, l2_ref, u_ref,
                      gc0, gc1, gc2, gc3, ga0, ga1, ga2, ga3,
                      wa_ref, wc_ref, bc_ref, wo_ref, g2_ref, wrh_ref, wrl_ref, br_ref,
                      x1_ref, hp_ref, idx_ref, gate_ref, pos_ref, cnt_ref, carry_ref):
    i = pl.program_id(0)
    tm = MIX_TM

    @pl.when(i == 0)
    def _():
        carry_ref[...] = jnp.zeros_like(carry_ref)

    l_refs = (l0_ref, l1_ref, l2_ref)
    o_refs = (o0_ref, o1_ref, o2_ref)
    parts = []
    for hh in range(HEADS_PER_GROUP):
        cs = slice(hh * HEAD_DIM, (hh + 1) * HEAD_DIM)
        lse = [r[:, hh * LSE_REP:hh * LSE_REP + 1] for r in l_refs]
        m = jnp.maximum(jnp.maximum(lse[0], lse[1]), lse[2])
        e = [jnp.exp(v - m) for v in lse]
        den = e[0] + e[1] + e[2]
        acc = (e[0] / den) * o_refs[0][:, cs].astype(F32)
        for g in range(1, N_GROUPS):
            acc = acc + (e[g] / den) * o_refs[g][:, cs].astype(F32)
        parts.append(acc)
    o = jnp.concatenate(parts, axis=1).astype(BF16)
    ya = jnp.dot(o, wa_ref[...], preferred_element_type=F32)
    yc = jnp.dot(u_ref[...], wc_ref[...], preferred_element_type=F32) + bc_ref[...]
    gcs = (gc0, gc1, gc2, gc3)
    gas = (ga0, ga1, ga2, ga3)
    merged = []
    for c in range(D_MODEL // CB):
        cs = slice(c * CB, (c + 1) * CB)
        merged.append((gcs[c][...].astype(F32) * yc[:, cs]
                       + gas[c][...].astype(F32) * ya[:, cs]).astype(BF16))
    merged = jnp.concatenate(merged, axis=1)
    x1 = x_ref[...] + jnp.dot(merged, wo_ref[...], preferred_element_type=F32)
    x1_ref[...] = x1
    ms = jnp.mean(x1 * x1, axis=-1, keepdims=True)
    h2 = (x1 * lax.rsqrt(ms + NORM_EPS)) * g2_ref[...]
    h2_hi = h2.astype(BF16)
    hp_ref[...] = _pack_bf16_pair(h2[:, 0:PACK_W], h2[:, PACK_W:D_MODEL])
    h2_lo = (h2 - h2_hi.astype(F32)).astype(BF16)
    hi_pass = jnp.dot(h2_hi, wrl_ref[...], preferred_element_type=F32)
    logits = (hi_pass[:, 0:N_EXPERTS] + hi_pass[:, N_EXPERTS:2 * N_EXPERTS]
              + jnp.dot(h2_lo, wrh_ref[...], preferred_element_type=F32)) + br_ref[...]

    e_iota = lax.broadcasted_iota(I32, (tm, N_EXPERTS), 1).astype(F32)
    k_lane = lax.broadcasted_iota(I32, (tm, TOP_K), 1)
    vals = logits
    sels = []
    idx_t = jnp.zeros((tm, TOP_K), F32)
    val_t = jnp.zeros((tm, TOP_K), F32)
    for k in range(TOP_K):
        mk = jnp.max(vals, axis=-1, keepdims=True)
        ik = jnp.min(jnp.where(vals == mk, e_iota, float(N_EXPERTS)), axis=-1, keepdims=True)
        sel = e_iota == ik
        sels.append(sel)
        vals = jnp.where(sel, -jnp.inf, vals)
        idx_t = jnp.where(k_lane == k, ik, idx_t)
        val_t = jnp.where(k_lane == k, mk, val_t)
    ex = jnp.exp(val_t - val_t[:, 0:1])
    gate_ref[...] = ex / jnp.sum(ex, axis=-1, keepdims=True)
    idx_ref[...] = idx_t.astype(I32)

    cnt = jnp.zeros((tm, N_EXPERTS), F32)
    for sel in sels:
        cnt = cnt + jnp.where(sel, 1.0, 0.0)
    ri = lax.broadcasted_iota(I32, (tm, tm), 0)
    ci = lax.broadcasted_iota(I32, (tm, tm), 1)
    lower = jnp.where(ci < ri, 1.0, 0.0).astype(BF16)
    prefix = jnp.dot(lower, cnt.astype(BF16), preferred_element_type=F32) + carry_ref[...]
    pos_t = jnp.zeros((tm, TOP_K), F32)
    for k, sel in enumerate(sels):
        pk = jnp.sum(jnp.where(sel, prefix, 0.0), axis=-1, keepdims=True)
        pos_t = jnp.where(k_lane == k, pk, pos_t)
    pos_ref[...] = pos_t.astype(I32)
    carry_ref[...] = carry_ref[...] + jnp.sum(cnt, axis=0, keepdims=True)
    cnt_ref[...] = carry_ref[...]


def _mixer_output(x2, outs, lses, u_ln, proj3, wa, wc, bc, wo, g2, wr_hi, wr_lo, br):
    s = x2.shape[0]
    row = lambda w: pl.BlockSpec((MIX_TM, w), lambda i: (i, 0))
    gate = lambda cb: pl.BlockSpec((None, MIX_TM, CB), lambda i: (cb, i, 0))
    full = lambda a: pl.BlockSpec(a.shape, lambda i: (0,) * a.ndim)
    in_specs = ([row(D_MODEL)] + [row(CB)] * 3 + [row(LSE_W)] * 3 + [row(D_CONV)]
                + [gate(CB_GATE_C + c) for c in range(4)]
                + [gate(CB_GATE_A + c) for c in range(4)]
                + [full(a) for a in (wa, wc, bc, wo, g2, wr_hi, wr_lo, br)])
    return pl.pallas_call(
        _mixer_out_kernel,
        grid=(s // MIX_TM,),
        in_specs=in_specs,
        out_specs=[row(D_MODEL), row(PACK_W), row(TOP_K), row(TOP_K), row(TOP_K),
                   pl.BlockSpec((1, N_EXPERTS), lambda i: (0, 0))],
        out_shape=[jax.ShapeDtypeStruct((s, D_MODEL), F32),
                   jax.ShapeDtypeStruct((s, PACK_W), U32),
                   jax.ShapeDtypeStruct((s, TOP_K), I32),
                   jax.ShapeDtypeStruct((s, TOP_K), F32),
                   jax.ShapeDtypeStruct((s, TOP_K), I32),
                   jax.ShapeDtypeStruct((1, N_EXPERTS), F32)],
        scratch_shapes=[pltpu.VMEM((1, N_EXPERTS), F32)],
        compiler_params=_params(1),
        name="mixer_output",
    )(x2, *outs, *lses, u_ln, *([proj3] * 8), wa, wc, bc, wo, g2, wr_hi, wr_lo, br)


ROW_BLK = 128
DISP_TM = 256


def _dispatch_kernel(fill_ref, dest_ref, h_ref, xs_ref, zbuf, sem):
    i = pl.program_id(0)
    n_fill = fill_ref.shape[0]

    @pl.when(i == 0)
    def _():
        zbuf[...] = jnp.zeros_like(zbuf)

        def fill_copy(j):
            return pltpu.make_async_copy(
                zbuf, xs_ref.at[pl.ds(fill_ref[j] * ROW_BLK, ROW_BLK)], sem.at[0])

        def start(j, c):
            @pl.when(fill_ref[j] >= 0)
            def _():
                fill_copy(j).start()
            return c

        def wait(j, c):
            @pl.when(fill_ref[j] >= 0)
            def _():
                fill_copy(j).wait()
            return c

        lax.fori_loop(0, n_fill, start, 0)
        lax.fori_loop(0, n_fill, wait, 0)

    def rows8(tt, c):
        for si in range(8):
            for k in range(TOP_K):
                d = dest_ref[0, (tt * 8 + si) * TOP_K + k]
                pltpu.make_async_copy(h_ref.at[tt, pl.ds(si, 1)], xs_ref.at[pl.ds(d, 1)],
                                      sem.at[0]).start(priority=k % 2)
        return c

    lax.fori_loop(0, DISP_TM // 8, rows8, 0)
    for _ in range(TOP_K):
        pltpu.make_async_copy(h_ref, h_ref, sem.at[0]).wait()


def _dispatch(hp, dest, fill_blocks, n_rows):
    s = hp.shape[0]
    dest3 = dest.reshape(s // DISP_TM, 1, DISP_TM * TOP_K)
    grid_spec = pltpu.PrefetchScalarGridSpec(
        num_scalar_prefetch=1,
        grid=(s // DISP_TM,),
        in_specs=[
            pl.BlockSpec((None, 1, DISP_TM * TOP_K), lambda i, fr: (i, 0, 0),
                         memory_space=pltpu.SMEM),
            pl.BlockSpec((DISP_TM // 8, 8, PACK_W), lambda i, fr: (i, 0, 0)),
        ],
        out_specs=pl.BlockSpec(memory_space=pl.ANY),
        scratch_shapes=[pltpu.VMEM((ROW_BLK, PACK_W), U32), pltpu.SemaphoreType.DMA((1,))],
    )
    return pl.pallas_call(
        _dispatch_kernel,
        grid_spec=grid_spec,
        out_shape=jax.ShapeDtypeStruct((n_rows, PACK_W), U32),
        compiler_params=_params(1),
        name="moe_dispatch",
    )(fill_blocks, dest3, hp.reshape(s // 8, 8, PACK_W))


UNIT_BLOCKS = 10
UNIT_ROWS = ROW_BLK * UNIT_BLOCKS
GROUP_SIZES = (8, 4, 2, 1)
EXP_TF = 512
EXP_NF = D_EXPERT // EXP_TF


def _expert_kernel(ue_ref, ub0_ref, unb_ref, nu_ref,
                   xs_ref, win_ref, wout_ref, bin_ref, bo_ref,
                   ys_ref,
                   xbuf, xb16, acc, wg_buf, wu_buf, wo_buf, sem_x, sem_y, sem_w):
    u = pl.program_id(0)
    n_units = nu_ref[0]

    def x_copy(unit, b):
        src = xs_ref.at[pl.ds((ub0_ref[unit] + b) * ROW_BLK, ROW_BLK)]
        dst = xbuf.at[pl.ds(b * ROW_BLK, ROW_BLK)]
        return pltpu.make_async_copy(src, dst, sem_x.at[0])

    def y_copy(unit, b):
        src = acc.at[pl.ds(b * ROW_BLK, ROW_BLK)]
        dst = ys_ref.at[pl.ds((ub0_ref[unit] + b) * ROW_BLK, ROW_BLK)]
        return pltpu.make_async_copy(src, dst, sem_y.at[0])

    def w_copies(unit, f, slot):
        e = ue_ref[unit]
        c0 = pl.multiple_of(f * EXP_TF, EXP_TF)
        return (
            pltpu.make_async_copy(win_ref.at[e, :, pl.ds(c0, EXP_TF)], wg_buf.at[slot],
                                  sem_w.at[slot]),
            pltpu.make_async_copy(win_ref.at[e, :, pl.ds(D_EXPERT + c0, EXP_TF)], wu_buf.at[slot],
                                  sem_w.at[slot]),
            pltpu.make_async_copy(wout_ref.at[e, pl.ds(c0, EXP_TF), :], wo_buf.at[slot],
                                  sem_w.at[slot]),
        )

    def start_w(unit, f, slot):
        for cp in w_copies(unit, f, slot):
            cp.start()

    def wait_w(unit, f, slot):
        for cp in w_copies(unit, f, slot):
            cp.wait()

    def start_x(unit):
        lax.fori_loop(0, unb_ref[unit], lambda b, c: (x_copy(unit, b).start(), c)[1], 0)

    def wait_x(unit):
        lax.fori_loop(0, unb_ref[unit], lambda b, c: (x_copy(unit, b).wait(), c)[1], 0)

    def wait_y(unit):
        lax.fori_loop(0, unb_ref[unit], lambda b, c: (y_copy(unit, b).wait(), c)[1], 0)

    @pl.when(u < n_units)
    def _():
        nblk = unb_ref[u]

        @pl.when(u == 0)
        def _():
            start_x(u)
            start_w(u, 0, 0)

        wait_x(u)

        @pl.when(u > 0)
        def _():
            wait_y(u - 1)

        def init(b, c):
            r0 = pl.multiple_of(b * ROW_BLK, ROW_BLK)
            lo, hi = _unpack_bf16_pair(xbuf[pl.ds(r0, ROW_BLK), :])
            xb16[pl.ds(r0, ROW_BLK), 0:PACK_W] = lo
            xb16[pl.ds(r0, ROW_BLK), PACK_W:D_MODEL] = hi
            acc[pl.ds(r0, ROW_BLK), :] = jnp.broadcast_to(bo_ref[...], (ROW_BLK, D_MODEL))
            return c

        lax.fori_loop(0, nblk, init, 0)

        @pl.when(u + 1 < n_units)
        def _():
            start_x(u + 1)

        def group(b0, n_blk, f, ws):
            rows = n_blk * ROW_BLK
            r0 = pl.multiple_of(b0 * ROW_BLK, ROW_BLK)
            c0 = pl.multiple_of(f * EXP_TF, EXP_TF)
            x = xb16[pl.ds(r0, rows), :]
            wgu = jnp.concatenate([wg_buf[ws].astype(BF16), wu_buf[ws].astype(BF16)], axis=1)
            gu = jnp.dot(x, wgu, preferred_element_type=F32)
            bg = bin_ref[:, pl.ds(c0, EXP_TF)]
            bu = bin_ref[:, pl.ds(D_EXPERT + c0, EXP_TF)]
            g = jnp.minimum(gu[:, 0:EXP_TF] + bg, SWIGLU_LIMIT)
            up = jnp.clip(gu[:, EXP_TF:2 * EXP_TF] + bu, -SWIGLU_LIMIT, SWIGLU_LIMIT)
            act = (up + 1.0) * (g * _sigmoid(SWIGLU_ALPHA * g))
            acc[pl.ds(r0, rows), :] += jnp.dot(
                act.astype(BF16), wo_buf[ws].astype(BF16), preferred_element_type=F32)

            @pl.when(f == EXP_NF - 1)
            def _():
                for j in range(n_blk):
                    y_copy(u, b0 + j).start()

        def f_tile(f, ws):
            big = GROUP_SIZES[0]
            n_big = nblk // big
            lax.fori_loop(0, n_big, lambda q, c: (group(big * q, big, f, ws), c)[1], 0)
            for size in GROUP_SIZES[1:]:
                done = (nblk // (2 * size)) * (2 * size)

                @pl.when(lax.rem(nblk, 2 * size) >= size)
                def _():
                    group(done, size, f, ws)

        def f_step(f, c):
            ws = lax.rem(f, 2)

            @pl.when(f + 1 < EXP_NF)
            def _():
                start_w(u, f + 1, 1 - ws)

            @pl.when((f + 1 == EXP_NF) & (u + 1 < n_units))
            def _():
                start_w(u + 1, 0, 1 - ws)

            wait_w(u, f, ws)
            f_tile(f, ws)
            return c

        lax.fori_loop(0, EXP_NF, f_step, 0)

        @pl.when(u == n_units - 1)
        def _():
            wait_y(u)

    @pl.when(u == pl.num_programs(0) - 1)
    def _():
        n_blocks = ys_ref.shape[0] // ROW_BLK
        acc[0:ROW_BLK, :] = jnp.zeros((ROW_BLK, D_MODEL), F32)

        def tail_copy(b):
            return pltpu.make_async_copy(acc.at[pl.ds(0, ROW_BLK)],
                                         ys_ref.at[pl.ds(b * ROW_BLK, ROW_BLK)], sem_y.at[0])

        lax.fori_loop(nu_ref[1], n_blocks, lambda b, c: (tail_copy(b).start(), c)[1], 0)
        lax.fori_loop(nu_ref[1], n_blocks, lambda b, c: (tail_copy(b).wait(), c)[1], 0)


def _expert_ffn(xs, unit_e, unit_blk0, unit_nblk, unit_counts, w_in, b_in, w_out, b_out):
    n_rows = xs.shape[0]
    n_units_max = unit_e.shape[0]

    def e_eff(u, ue, nu):
        return ue[jnp.minimum(u, jnp.maximum(nu[0] - 1, 0))]

    any_spec = pl.BlockSpec(memory_space=pl.ANY)
    in_specs = [
        any_spec, any_spec, any_spec,
        pl.BlockSpec((None, 1, 2 * D_EXPERT), lambda u, ue, ub, un, nu: (e_eff(u, ue, nu), 0, 0)),
        pl.BlockSpec((None, 1, D_MODEL), lambda u, ue, ub, un, nu: (e_eff(u, ue, nu), 0, 0)),
    ]
    grid_spec = pltpu.PrefetchScalarGridSpec(
        num_scalar_prefetch=4,
        grid=(n_units_max,),
        in_specs=in_specs,
        out_specs=any_spec,
        scratch_shapes=[
            pltpu.VMEM((UNIT_ROWS, PACK_W), U32),
            pltpu.VMEM((UNIT_ROWS, D_MODEL), BF16),
            pltpu.VMEM((UNIT_ROWS, D_MODEL), F32),
            pltpu.VMEM((2, D_MODEL, EXP_TF), F32),
            pltpu.VMEM((2, D_MODEL, EXP_TF), F32),
            pltpu.VMEM((2, EXP_TF, D_MODEL), F32),
            pltpu.SemaphoreType.DMA((1,)),
            pltpu.SemaphoreType.DMA((1,)),
            pltpu.SemaphoreType.DMA((2,)),
        ],
    )
    return pl.pallas_call(
        _expert_kernel,
        grid_spec=grid_spec,
        out_shape=jax.ShapeDtypeStruct((n_rows, D_MODEL), F32),
        compiler_params=_params(1),
        name="expert_ffn",
    )(unit_e, unit_blk0, unit_nblk, unit_counts, xs,
      w_in, w_out, b_in[:, None, :], b_out[:, None, :])


COMB_TM = 256


def _combine_kernel(dcur_ref, dnxt_ref, x1_ref, gate_ref, g_ref, ys_ref, o_ref, ybuf, sem):
    i = pl.program_id(0)
    n = pl.num_programs(0)
    slot = lax.rem(i, 2)

    def issue(dref, s):
        def rows8(tt, c):
            for si in range(8):
                for k in range(TOP_K):
                    d = dref[0, (tt * 8 + si) * TOP_K + k]
                    pltpu.make_async_copy(ys_ref.at[pl.ds(d, 1)],
                                          ybuf.at[s, k, tt, pl.ds(si, 1)],
                                          sem.at[s]).start(priority=k % 2)
            return c

        lax.fori_loop(0, COMB_TM // 8, rows8, 0)

    @pl.when(i == 0)
    def _():
        issue(dcur_ref, slot)

    @pl.when(i + 1 < n)
    def _():
        issue(dnxt_ref, 1 - slot)

    for k in range(TOP_K):
        pltpu.make_async_copy(ybuf.at[slot, k], ybuf.at[slot, k], sem.at[slot]).wait()

    gate = gate_ref[...]
    y = x1_ref[...]
    for k in range(TOP_K):
        y = y + gate[:, k:k + 1] * ybuf[slot, k].reshape(COMB_TM, D_MODEL)
    ms = jnp.mean(y * y, axis=-1, keepdims=True)
    o_ref[...] = (y * lax.rsqrt(ms + NORM_EPS)) * g_ref[...]


def _combine(dest, x1, gate, norm_g, ys):
    s = x1.shape[0]
    nt = s // COMB_TM
    dest3 = dest.reshape(nt, 1, COMB_TM * TOP_K)
    smem = lambda imap: pl.BlockSpec((None, 1, COMB_TM * TOP_K), imap, memory_space=pltpu.SMEM)
    row = lambda w: pl.BlockSpec((COMB_TM, w), lambda i: (i, 0))
    return pl.pallas_call(
        _combine_kernel,
        grid=(nt,),
        in_specs=[smem(lambda i: (i, 0, 0)),
                  smem(lambda i: (jnp.minimum(i + 1, nt - 1), 0, 0)),
                  row(D_MODEL), row(TOP_K), pl.BlockSpec((1, D_MODEL), lambda i: (0, 0)),
                  pl.BlockSpec(memory_space=pl.ANY)],
        out_specs=row(D_MODEL),
        out_shape=jax.ShapeDtypeStruct((s, D_MODEL), F32),
        scratch_shapes=[pltpu.VMEM((2, TOP_K, COMB_TM // 8, 8, D_MODEL), F32),
                        pltpu.SemaphoreType.DMA((2,))],
        compiler_params=_params(1),
        name="moe_combine",
    )(dest3, dest3, x1, gate, norm_g, ys)


def _route_tables(idx, pos, counts, n_tok):
    n_assign = n_tok * TOP_K
    counts = counts.reshape(N_EXPERTS).astype(I32)
    nblk_e = (counts + ROW_BLK - 1) // ROW_BLK
    pad_start = (jnp.cumsum(nblk_e) - nblk_e) * ROW_BLK
    onehot = idx[..., None] == jnp.arange(N_EXPERTS, dtype=I32)
    dest = pos + jnp.sum(jnp.where(onehot, pad_start, 0), axis=-1)

    n_blocks = -(-(n_assign + N_EXPERTS * (ROW_BLK - 1)) // ROW_BLK)
    used = jnp.sum(nblk_e)
    blk_ids = jnp.arange(n_blocks, dtype=I32)
    last_blk = pad_start // ROW_BLK + nblk_e - 1
    fill_blocks = jnp.concatenate([jnp.where(nblk_e > 0, last_blk, -1),
                                   jnp.where(blk_ids >= used, blk_ids, -1)]).astype(I32)

    n_units_max = (n_blocks + (UNIT_BLOCKS - 1) * N_EXPERTS) // UNIT_BLOCKS
    chunks_e = (nblk_e + UNIT_BLOCKS - 1) // UNIT_BLOCKS
    chunk_end = jnp.cumsum(chunks_e)
    chunk_start = chunk_end - chunks_e
    n_units = chunk_end[-1]
    uid = jnp.arange(n_units_max, dtype=I32)
    unit_e = jnp.minimum(jnp.searchsorted(chunk_end, uid, side="right"), N_EXPERTS - 1).astype(I32)
    c_in_e = uid - chunk_start[unit_e]
    unit_blk0 = (pad_start[unit_e] // ROW_BLK + c_in_e * UNIT_BLOCKS).astype(I32)
    unit_nblk = jnp.clip(nblk_e[unit_e] - c_in_e * UNIT_BLOCKS, 0, UNIT_BLOCKS).astype(I32)
    active = uid < n_units
    unit_blk0 = jnp.where(active, unit_blk0, 0)
    unit_nblk = jnp.where(active, unit_nblk, 0)
    unit_counts = jnp.stack([n_units, used]).astype(I32)
    return dest.astype(I32), fill_blocks, n_blocks * ROW_BLK, unit_e, unit_blk0, unit_nblk, unit_counts


def kernel(x, norm1_g, w_in, b_gate, conv_w, conv_b, conv_ln_g, conv_ln_b, w_conv_out, b_conv_out,
           w_attn_out, w_out, norm2_g, w_router, b_router, w_exp_in, b_exp_in, w_exp_out, b_exp_out,
           norm_f_g):
    b_sz, s_len, d = x.shape
    n_tok = b_sz * s_len
    x2 = x.reshape(n_tok, d)
    l = 0
    b_gate_ext = jnp.concatenate([jnp.zeros((GATE_COL0,), F32), b_gate[l]])[None, :]
    proj3 = _in_projection(x2, norm1_g[l][None, :], w_in[l], b_gate_ext)
    u_ln = _conv_branch(proj3, conv_w[l], conv_b[l][None, :], conv_ln_g[l][None, :],
                        conv_ln_b[l][None, :])
    outs, lses = [], []
    for g in range(N_GROUPS):
        o_g, lse_g = _dilated_attention(proj3, g)
        outs.append(o_g)
        lses.append(lse_g)
    wr = w_router[l]
    wr_hi = wr.astype(BF16)
    wr_lo = jnp.concatenate([wr_hi, (wr - wr_hi.astype(F32)).astype(BF16)], axis=1)
    x1, hp, idx, gate, pos, counts = _mixer_output(
        x2, outs, lses, u_ln, proj3,
        w_attn_out[l].astype(BF16), w_conv_out[l].astype(BF16), b_conv_out[l][None, :],
        w_out[l].astype(BF16), norm2_g[l][None, :], wr_hi, wr_lo, b_router[l][None, :])

    dest, fill_blocks, n_rows, unit_e, unit_blk0, unit_nblk, unit_counts = _route_tables(
        idx, pos, counts, n_tok)
    xs = _dispatch(hp, dest, fill_blocks, n_rows)
    ys = _expert_ffn(xs, unit_e, unit_blk0, unit_nblk, unit_counts,
                     w_exp_in[l], b_exp_in[l], w_exp_out[l], b_exp_out[l])
    out = _combine(dest, x1, gate, norm_f_g[None, :], ys)
    return out.reshape(b_sz, s_len, d)
```

```python
import functools

import jax
import jax.numpy as jnp
from jax import lax
from jax.experimental import pallas as pl
from jax.experimental.pallas import tpu as pltpu

D_MODEL = 2048
SEQ = 8192
D_CONV = D_MODEL // 2
CONV_WIDTH = 31
HEAD_DIM = 128
HEADS_PER_GROUP = 4
ATTN_PATTERNS = ((128, 1), (512, 4), (2048, 16))
N_GROUPS = len(ATTN_PATTERNS)
ATTN_WIDTH = N_GROUPS * HEADS_PER_GROUP * HEAD_DIM
ATTN_OUT_WIDTH = HEADS_PER_GROUP * HEAD_DIM
ATTN_BLOCK = 128
D_IN = 2 * D_CONV + 3 * ATTN_WIDTH + 2 * D_MODEL
N_EXPERTS = 32
TOP_K = 4
D_EXPERT = D_MODEL
SWIGLU_LIMIT = 7.0
SWIGLU_ALPHA = 1.702
NORM_EPS = 1e-5

CB = 512
N_CB = D_IN // CB
CB_CONV_A, CB_CONV_G = 0, 2
CB_Q, CB_K, CB_V = 4, 7, 10
CB_GATE_C, CB_GATE_A = 13, 17
GATE_COL0 = CB_GATE_C * CB

VMEM_LIMIT = 56 * 1024 * 1024

F32 = jnp.float32
BF16 = jnp.bfloat16
U32 = jnp.uint32
I32 = jnp.int32

PACK_W = D_MODEL // 2
HI_MASK = 0xFFFF0000


def _sigmoid(z):
    return 0.5 * jnp.tanh(0.5 * z) + 0.5


def _pack_bf16_pair(lo, hi):
    lo_bits = lax.bitcast_convert_type(lo.astype(BF16).astype(F32), U32) >> 16
    hi_bits = lax.bitcast_convert_type(hi.astype(BF16).astype(F32), U32) & jnp.uint32(HI_MASK)
    return hi_bits | lo_bits


def _unpack_bf16_pair(w):
    lo = lax.bitcast_convert_type(w << 16, F32).astype(BF16)
    hi = lax.bitcast_convert_type(w & jnp.uint32(HI_MASK), F32).astype(BF16)
    return lo, hi


def _params(n_axes, vmem=VMEM_LIMIT):
    return pltpu.CompilerParams(
        dimension_semantics=("arbitrary",) * n_axes, vmem_limit_bytes=vmem)


NORM_TM = 512
IN_TM = 1024
IN_TN = 1536


def _rmsnorm_kernel(x_ref, g_ref, o_ref):
    x = x_ref[...]
    ms = jnp.mean(x * x, axis=-1, keepdims=True)
    o_ref[...] = ((x * lax.rsqrt(ms + NORM_EPS)) * g_ref[...]).astype(BF16)


def _rmsnorm_bf16(x2, norm_g):
    s = x2.shape[0]
    row = pl.BlockSpec((NORM_TM, D_MODEL), lambda i: (i, 0))
    return pl.pallas_call(
        _rmsnorm_kernel,
        grid=(s // NORM_TM,),
        in_specs=[row, pl.BlockSpec((1, D_MODEL), lambda i: (0, 0))],
        out_specs=row,
        out_shape=jax.ShapeDtypeStruct((s, D_MODEL), BF16),
        compiler_params=_params(1),
        name="rmsnorm_in",
    )(x2, norm_g)


def _inproj_kernel(h_ref, w_ref, bg_ref, o_ref, wbf_ref):
    j = pl.program_id(0)
    i = pl.program_id(1)

    @pl.when(i == 0)
    def _():
        wbf_ref[...] = w_ref[...].astype(BF16)

    acc = jnp.dot(h_ref[...], wbf_ref[...], preferred_element_type=F32) + bg_ref[...]
    for c in range(IN_TN // CB):
        o_ref[c] = acc[:, c * CB:(c + 1) * CB].astype(BF16)


def _in_projection(x2, norm_g, w_in, b_gate_ext):
    s = x2.shape[0]
    h = _rmsnorm_bf16(x2, norm_g)
    grid = (D_IN // IN_TN, s // IN_TM)
    return pl.pallas_call(
        _inproj_kernel,
        grid=grid,
        in_specs=[
            pl.BlockSpec((IN_TM, D_MODEL), lambda j, i: (i, 0)),
            pl.BlockSpec((D_MODEL, IN_TN), lambda j, i: (0, j)),
            pl.BlockSpec((1, IN_TN), lambda j, i: (0, j)),
        ],
        out_specs=pl.BlockSpec((IN_TN // CB, IN_TM, CB), lambda j, i: (j, i, 0)),
        out_shape=jax.ShapeDtypeStruct((N_CB, s, CB), BF16),
        scratch_shapes=[pltpu.VMEM((D_MODEL, IN_TN), BF16)],
        compiler_params=_params(2),
        name="in_projection",
    )(h, w_in, b_gate_ext)


CONV_TS = 256
CONV_HALO = 32
CONV_RC = 32
CONV_N = CONV_TS + CONV_HALO


def _conv_kernel(ac_ref, gc_ref, ah_ref, gh_ref, w_ref, cb_ref, lg_ref, lb_ref, o_ref, r_ref,
                 c_ref):
    i = pl.program_id(0)
    has_prev = i > 0
    half = D_CONV // 2
    for c in range(2):
        cs = slice(c * half, (c + 1) * half)
        uh = ah_ref[c].astype(F32) * _sigmoid(gh_ref[c].astype(F32))
        r_ref[0, 0:CONV_HALO, cs] = jnp.where(has_prev, uh, 0.0)
        r_ref[0, CONV_HALO:CONV_N, cs] = ac_ref[c].astype(F32) * _sigmoid(gc_ref[c].astype(F32))
    u_ext = r_ref[0]
    for b in range(1, 8):
        r_ref[b] = pltpu.roll(u_ext, CONV_N - b, axis=0)

    def chunk(ci, carry):
        r0 = pl.multiple_of(ci * CONV_RC, CONV_RC)
        acc = jnp.broadcast_to(cb_ref[...], (CONV_RC, D_CONV))
        for k in range(CONV_WIDTH):
            kp = k + (CONV_HALO - (CONV_WIDTH - 1))
            a, b = kp // 8, kp % 8
            acc = acc + w_ref[k:k + 1, :] * r_ref[b, pl.ds(r0 + 8 * a, CONV_RC), :]
        c_ref[pl.ds(r0, CONV_RC), :] = acc
        return carry

    lax.fori_loop(0, CONV_TS // CONV_RC, chunk, 0)
    conv = c_ref[...]
    mu = jnp.mean(conv, axis=-1, keepdims=True)
    d = conv - mu
    var = jnp.mean(d * d, axis=-1, keepdims=True)
    y = (d * lax.rsqrt(var + NORM_EPS)) * lg_ref[...] + lb_ref[...]
    o_ref[...] = (y * _sigmoid(y)).astype(BF16)


def _conv_branch(proj3, conv_w, conv_b, ln_g, ln_b):
    s = proj3.shape[1]
    hb = CONV_TS // CONV_HALO
    cur = lambda blk: pl.BlockSpec((2, CONV_TS, CB), lambda i: (blk // 2, i, 0))
    halo = lambda blk: pl.BlockSpec(
        (2, CONV_HALO, CB), lambda i: (blk // 2, jnp.maximum(i * hb - 1, 0), 0))
    vec = pl.BlockSpec((1, D_CONV), lambda i: (0, 0))
    return pl.pallas_call(
        _conv_kernel,
        grid=(s // CONV_TS,),
        in_specs=[cur(CB_CONV_A), cur(CB_CONV_G), halo(CB_CONV_A), halo(CB_CONV_G),
                  pl.BlockSpec((CONV_WIDTH, D_CONV), lambda i: (0, 0)), vec, vec, vec],
        out_specs=pl.BlockSpec((CONV_TS, D_CONV), lambda i: (i, 0)),
        out_shape=jax.ShapeDtypeStruct((s, D_CONV), BF16),
        scratch_shapes=[pltpu.VMEM((8, CONV_N, D_CONV), F32), pltpu.VMEM((CONV_TS, D_CONV), F32)],
        compiler_params=_params(1),
        name="conv_branch",
    )(proj3, proj3, proj3, proj3, conv_w, conv_b, ln_g, ln_b)


LSE_W = 128
LSE_REP = LSE_W // HEADS_PER_GROUP


def _attn_rows(q, k, v, n):
    blk = ATTN_BLOCK
    scale = HEAD_DIM ** -0.5
    qi = lax.broadcasted_iota(I32, (blk, 2 * blk), 0)
    ki = lax.broadcasted_iota(I32, (blk, 2 * blk), 1)
    band = (ki >= qi) & (ki <= qi + blk)
    lane_head = lax.broadcasted_iota(I32, (blk, LSE_W), 1) // LSE_REP
    o_rows, lse_rows = [], []
    for j in range(q.shape[0] // blk):
        mask = band
        if j == 0:
            mask = band & ((ki >= blk) | (n > 0))
        o_heads = []
        lse_tile = jnp.zeros((blk, LSE_W), F32)
        for hh in range(HEADS_PER_GROUP):
            cs = slice(hh * HEAD_DIM, (hh + 1) * HEAD_DIM)
            qh = q[j * blk:(j + 1) * blk, cs]
            kh = k[j * blk:(j + 2) * blk, cs]
            vh = v[j * blk:(j + 2) * blk, cs]
            s = lax.dot_general(qh, kh, (((1,), (1,)), ((), ())),
                                preferred_element_type=F32) * scale
            s = jnp.where(mask, s, -jnp.inf)
            m = jnp.max(s, axis=-1, keepdims=True)
            p = jnp.exp(s - m)
            l = jnp.sum(p, axis=-1, keepdims=True)
            o_heads.append(jnp.dot(p.astype(BF16), vh, preferred_element_type=F32) / l)
            lse_tile = jnp.where(lane_head == hh, m + jnp.log(l), lse_tile)
        o_rows.append(jnp.concatenate(o_heads, axis=1))
        lse_rows.append(lse_tile)
    return jnp.concatenate(o_rows, axis=0), jnp.concatenate(lse_rows, axis=0)


def _attn_kernel(q_ref, kc_ref, kp_ref, vc_ref, vp_ref, o_ref, lse_ref):
    n = pl.program_id(0)
    k = jnp.concatenate([kp_ref[...], kc_ref[...]], axis=0)
    v = jnp.concatenate([vp_ref[...], vc_ref[...]], axis=0)
    o, lse = _attn_rows(q_ref[...], k, v, n)
    o_ref[...] = o.astype(BF16)
    lse_ref[...] = lse


ATTN_TILE = 2048


def _attn_dilated_kernel(q_ref, kc_ref, kp_ref, vc_ref, vp_ref, o_ref, lse_ref,
                         qs, ks, vs, os_, ls, *, dil):
    n = pl.program_id(0)
    tile = q_ref.shape[0]
    mb = tile // dil
    half = dil // 2
    halo_w = ATTN_BLOCK * half
    n_slab = CB // 128
    q32 = pltpu.bitcast(q_ref[...], U32)
    kp32, kc32 = pltpu.bitcast(kp_ref[...], U32), pltpu.bitcast(kc_ref[...], U32)
    vp32, vc32 = pltpu.bitcast(vp_ref[...], U32), pltpu.bitcast(vc_ref[...], U32)
    for c in range(n_slab):
        cs = slice(c * 128, (c + 1) * 128)
        qs[c] = q32[:, cs]
        ks[c, 0:halo_w] = kp32[:, cs]
        ks[c, halo_w:] = kc32[:, cs]
        vs[c, 0:halo_w] = vp32[:, cs]
        vs[c, halo_w:] = vc32[:, cs]

    def gather(slabs, rp, rows):
        return jnp.concatenate(
            [slabs[c, pl.ds(rp, rows, stride=half), :] for c in range(n_slab)], axis=1)

    def residue(w, par):
        bits = (w << 16) if par == 0 else (w & jnp.uint32(HI_MASK))
        return lax.bitcast_convert_type(bits, F32).astype(BF16)

    for rp in range(half):
        wq = gather(qs, rp, mb)
        wk = gather(ks, rp, mb + ATTN_BLOCK)
        wv = gather(vs, rp, mb + ATTN_BLOCK)
        outs = []
        for par in range(2):
            o, lse = _attn_rows(residue(wq, par), residue(wk, par), residue(wv, par), n)
            outs.append(o)
            ls[pl.ds(2 * rp + par, mb, stride=dil), :] = lse
        w_out = _pack_bf16_pair(outs[0], outs[1])
        for c in range(n_slab):
            os_[c, pl.ds(rp, mb, stride=half), :] = w_out[:, c * 128:(c + 1) * 128]
    o32 = jnp.concatenate([os_[c] for c in range(n_slab)], axis=1)
    o_ref[...] = pltpu.bitcast(o32, BF16)
    lse_ref[...] = ls[...]


def _dilated_attention(proj3, group):
    _, dil = ATTN_PATTERNS[group]
    s = proj3.shape[1]
    cbs = (CB_Q + group, CB_K + group, CB_V + group)
    if dil == 1:
        tile, halo = 512, ATTN_BLOCK
        body, scratch = _attn_kernel, []
    else:
        tile, halo = ATTN_TILE, ATTN_BLOCK * dil
        body = functools.partial(_attn_dilated_kernel, dil=dil)
        scratch = [pltpu.VMEM((CB // 128, tile // 2, 128), U32),
                   pltpu.VMEM((CB // 128, (tile + halo) // 2, 128), U32),
                   pltpu.VMEM((CB // 128, (tile + halo) // 2, 128), U32),
                   pltpu.VMEM((CB // 128, tile // 2, 128), U32),
                   pltpu.VMEM((tile, LSE_W), F32)]
    hb = tile // halo
    cur = lambda cb: pl.BlockSpec((None, tile, CB), lambda n: (cb, n, 0))
    prev = lambda cb: pl.BlockSpec((None, halo, CB), lambda n: (cb, jnp.maximum(n * hb - 1, 0), 0))
    return pl.pallas_call(
        body,
        grid=(s // tile,),
        in_specs=[cur(cbs[0]), cur(cbs[1]), prev(cbs[1]), cur(cbs[2]), prev(cbs[2])],
        out_specs=[pl.BlockSpec((tile, CB), lambda n: (n, 0)),
                   pl.BlockSpec((tile, LSE_W), lambda n: (n, 0))],
        out_shape=[jax.ShapeDtypeStruct((s, CB), BF16),
                   jax.ShapeDtypeStruct((s, LSE_W), F32)],
        scratch_shapes=scratch,
        compiler_params=_params(1),
        name=f"dilated_attention_g{group}",
    )(proj3, proj3, proj3, proj3, proj3)


MIX_TM = 256


def _mixer_out_kernel(x_ref, o0_ref, o1_ref, o2_ref, l0_ref, l1_ref, l2_ref, u_ref,
                      gc0, gc1, gc2, gc3, ga0, ga1, ga2, ga3,
                      wa_ref, wc_ref, bc_ref, wo_ref, g2_ref, wrh_ref, wrl_ref, br_ref,
                      x1_ref, hp_ref, logit_ref):
    l_refs = (l0_ref, l1_ref, l2_ref)
    o_refs = (o0_ref, o1_ref, o2_ref)
    parts = []
    for hh in range(HEADS_PER_GROUP):
        cs = slice(hh * HEAD_DIM, (hh + 1) * HEAD_DIM)
        lse = [r[:, hh * LSE_REP:hh * LSE_REP + 1] for r in l_refs]
        m = jnp.maximum(jnp.maximum(lse[0], lse[1]), lse[2])
        e = [jnp.exp(v - m) for v in lse]
        den = e[0] + e[1] + e[2]
        acc = (e[0] / den) * o_refs[0][:, cs].astype(F32)
        for g in range(1, N_GROUPS):
            acc = acc + (e[g] / den) * o_refs[g][:, cs].astype(F32)
        parts.append(acc)
    o = jnp.concatenate(parts, axis=1).astype(BF16)
    ya = jnp.dot(o, wa_ref[...], preferred_element_type=F32)
    yc = jnp.dot(u_ref[...], wc_ref[...], preferred_element_type=F32) + bc_ref[...]
    gcs = (gc0, gc1, gc2, gc3)
    gas = (ga0, ga1, ga2, ga3)
    merged = []
    for c in range(D_MODEL // CB):
        cs = slice(c * CB, (c + 1) * CB)
        merged.append((_sigmoid(gcs[c][...].astype(F32)) * yc[:, cs]
                       + _sigmoid(gas[c][...].astype(F32)) * ya[:, cs]).astype(BF16))
    merged = jnp.concatenate(merged, axis=1)
    x1 = x_ref[...] + jnp.dot(merged, wo_ref[...], preferred_element_type=F32)
    x1_ref[...] = x1
    ms = jnp.mean(x1 * x1, axis=-1, keepdims=True)
    h2 = (x1 * lax.rsqrt(ms + NORM_EPS)) * g2_ref[...]
    h2_hi = h2.astype(BF16)
    hp_ref[...] = _pack_bf16_pair(h2[:, 0:PACK_W], h2[:, PACK_W:D_MODEL])
    h2_lo = (h2 - h2_hi.astype(F32)).astype(BF16)
    hi_pass = jnp.dot(h2_hi, wrl_ref[...], preferred_element_type=F32)
    logits = (hi_pass[:, 0:N_EXPERTS] + hi_pass[:, N_EXPERTS:2 * N_EXPERTS]
              + jnp.dot(h2_lo, wrh_ref[...], preferred_element_type=F32)) + br_ref[...]
    logit_ref[...] = logits


def _mixer_output(x2, outs, lses, u_ln, proj3, wa, wc, bc, wo, g2, wr_hi, wr_lo, br):
    s = x2.shape[0]
    row = lambda w: pl.BlockSpec((MIX_TM, w), lambda i: (i, 0))
    gate = lambda cb: pl.BlockSpec((None, MIX_TM, CB), lambda i: (cb, i, 0))
    full = lambda a: pl.BlockSpec(a.shape, lambda i: (0,) * a.ndim)
    in_specs = ([row(D_MODEL)] + [row(CB)] * 3 + [row(LSE_W)] * 3 + [row(D_CONV)]
                + [gate(CB_GATE_C + c) for c in range(4)]
                + [gate(CB_GATE_A + c) for c in range(4)]
                + [full(a) for a in (wa, wc, bc, wo, g2, wr_hi, wr_lo, br)])
    return pl.pallas_call(
        _mixer_out_kernel,
        grid=(s // MIX_TM,),
        in_specs=in_specs,
        out_specs=[row(D_MODEL), row(PACK_W), row(N_EXPERTS)],
        out_shape=[jax.ShapeDtypeStruct((s, D_MODEL), F32),
                   jax.ShapeDtypeStruct((s, PACK_W), U32),
                   jax.ShapeDtypeStruct((s, N_EXPERTS), F32)],
        compiler_params=_params(1),
        name="mixer_output",
    )(x2, *outs, *lses, u_ln, *([proj3] * 8), wa, wc, bc, wo, g2, wr_hi, wr_lo, br)


ROUTE_TM = 1024


def _route_kernel(logit_ref, idx_ref, gate_ref, pos_ref, cnt_ref, carry_ref):
    i = pl.program_id(0)
    tm = ROUTE_TM

    @pl.when(i == 0)
    def _():
        carry_ref[...] = jnp.zeros_like(carry_ref)

    logits = logit_ref[...]
    e_iota = lax.broadcasted_iota(I32, (tm, N_EXPERTS), 1).astype(F32)
    k_lane = lax.broadcasted_iota(I32, (tm, TOP_K), 1)
    vals = logits
    sels = []
    idx_t = jnp.zeros((tm, TOP_K), F32)
    val_t = jnp.zeros((tm, TOP_K), F32)
    for k in range(TOP_K):
        mk = jnp.max(vals, axis=-1, keepdims=True)
        ik = jnp.min(jnp.where(vals == mk, e_iota, float(N_EXPERTS)), axis=-1, keepdims=True)
        sel = e_iota == ik
        sels.append(sel)
        vals = jnp.where(sel, -jnp.inf, vals)
        idx_t = jnp.where(k_lane == k, ik, idx_t)
        val_t = jnp.where(k_lane == k, mk, val_t)
    ex = jnp.exp(val_t - val_t[:, 0:1])
    gate_ref[...] = ex / jnp.sum(ex, axis=-1, keepdims=True)
    idx_ref[...] = idx_t.astype(I32)

    cnt = jnp.zeros((tm, N_EXPERTS), F32)
    for sel in sels:
        cnt = cnt + jnp.where(sel, 1.0, 0.0)
    ri = lax.broadcasted_iota(I32, (tm, tm), 0)
    ci = lax.broadcasted_iota(I32, (tm, tm), 1)
    lower = jnp.where(ci < ri, 1.0, 0.0).astype(BF16)
    prefix = jnp.dot(lower, cnt.astype(BF16), preferred_element_type=F32) + carry_ref[...]
    pos_t = jnp.zeros((tm, TOP_K), F32)
    for k, sel in enumerate(sels):
        pk = jnp.sum(jnp.where(sel, prefix, 0.0), axis=-1, keepdims=True)
        pos_t = jnp.where(k_lane == k, pk, pos_t)
    pos_ref[...] = pos_t.astype(I32)
    carry_ref[...] = carry_ref[...] + jnp.sum(cnt, axis=0, keepdims=True)
    cnt_ref[...] = carry_ref[...]


def _route(logits):
    s = logits.shape[0]
    row = lambda w: pl.BlockSpec((ROUTE_TM, w), lambda i: (i, 0))
    return pl.pallas_call(
        _route_kernel,
        grid=(s // ROUTE_TM,),
        in_specs=[row(N_EXPERTS)],
        out_specs=[row(TOP_K), row(TOP_K), row(TOP_K),
                   pl.BlockSpec((1, N_EXPERTS), lambda i: (0, 0))],
        out_shape=[jax.ShapeDtypeStruct((s, TOP_K), I32),
                   jax.ShapeDtypeStruct((s, TOP_K), F32),
                   jax.ShapeDtypeStruct((s, TOP_K), I32),
                   jax.ShapeDtypeStruct((1, N_EXPERTS), F32)],
        scratch_shapes=[pltpu.VMEM((1, N_EXPERTS), F32)],
        compiler_params=_params(1),
        name="moe_route",
    )(logits)


ROW_BLK = 128
DISP_TM = 256


def _dispatch_kernel(fill_ref, dest_ref, h_ref, xs_ref, zbuf, sem):
    i = pl.program_id(0)
    n_fill = fill_ref.shape[0]

    @pl.when(i == 0)
    def _():
        zbuf[...] = jnp.zeros_like(zbuf)

        def fill_copy(j):
            return pltpu.make_async_copy(
                zbuf, xs_ref.at[pl.ds(fill_ref[j] * ROW_BLK, ROW_BLK)], sem.at[0])

        def start(j, c):
            @pl.when(fill_ref[j] >= 0)
            def _():
                fill_copy(j).start()
            return c

        def wait(j, c):
            @pl.when(fill_ref[j] >= 0)
            def _():
                fill_copy(j).wait()
            return c

        lax.fori_loop(0, n_fill, start, 0)
        lax.fori_loop(0, n_fill, wait, 0)

    def rows8(tt, c):
        for si in range(8):
            for k in range(TOP_K):
                d = dest_ref[0, (tt * 8 + si) * TOP_K + k]
                pltpu.make_async_copy(h_ref.at[tt, pl.ds(si, 1)], xs_ref.at[pl.ds(d, 1)],
                                      sem.at[0]).start(priority=k % 2)
        return c

    lax.fori_loop(0, DISP_TM // 8, rows8, 0)
    for _ in range(TOP_K):
        pltpu.make_async_copy(h_ref, h_ref, sem.at[0]).wait()


def _dispatch(hp, dest, fill_blocks, n_rows):
    s = hp.shape[0]
    dest3 = dest.reshape(s // DISP_TM, 1, DISP_TM * TOP_K)
    grid_spec = pltpu.PrefetchScalarGridSpec(
        num_scalar_prefetch=1,
        grid=(s // DISP_TM,),
        in_specs=[
            pl.BlockSpec((None, 1, DISP_TM * TOP_K), lambda i, fr: (i, 0, 0),
                         memory_space=pltpu.SMEM),
            pl.BlockSpec((DISP_TM // 8, 8, PACK_W), lambda i, fr: (i, 0, 0)),
        ],
        out_specs=pl.BlockSpec(memory_space=pl.ANY),
        scratch_shapes=[pltpu.VMEM((ROW_BLK, PACK_W), U32), pltpu.SemaphoreType.DMA((1,))],
    )
    return pl.pallas_call(
        _dispatch_kernel,
        grid_spec=grid_spec,
        out_shape=jax.ShapeDtypeStruct((n_rows, PACK_W), U32),
        compiler_params=_params(1),
        name="moe_dispatch",
    )(fill_blocks, dest3, hp.reshape(s // 8, 8, PACK_W))


UNIT_BLOCKS = 10
UNIT_ROWS = ROW_BLK * UNIT_BLOCKS
GROUP_SIZES = (8, 4, 2, 1)
EXP_TF = 512
EXP_NF = D_EXPERT // EXP_TF


def _expert_kernel(ue_ref, ub0_ref, unb_ref, nu_ref,
                   xs_ref, win_ref, wout_ref, bin_ref, bo_ref,
                   ys_ref,
                   xbuf, xb16, acc, wg_buf, wu_buf, wo_buf, sem_x, sem_y, sem_w):
    u = pl.program_id(0)
    n_units = nu_ref[0]

    def x_copy(unit, b):
        src = xs_ref.at[pl.ds((ub0_ref[unit] + b) * ROW_BLK, ROW_BLK)]
        dst = xbuf.at[pl.ds(b * ROW_BLK, ROW_BLK)]
        return pltpu.make_async_copy(src, dst, sem_x.at[0])

    def y_copy(unit, b):
        src = acc.at[pl.ds(b * ROW_BLK, ROW_BLK)]
        dst = ys_ref.at[pl.ds((ub0_ref[unit] + b) * ROW_BLK, ROW_BLK)]
        return pltpu.make_async_copy(src, dst, sem_y.at[0])

    def w_copies(unit, f, slot):
        e = ue_ref[unit]
        c0 = pl.multiple_of(f * EXP_TF, EXP_TF)
        return (
            pltpu.make_async_copy(win_ref.at[e, :, pl.ds(c0, EXP_TF)], wg_buf.at[slot],
                                  sem_w.at[slot]),
            pltpu.make_async_copy(win_ref.at[e, :, pl.ds(D_EXPERT + c0, EXP_TF)], wu_buf.at[slot],
                                  sem_w.at[slot]),
            pltpu.make_async_copy(wout_ref.at[e, pl.ds(c0, EXP_TF), :], wo_buf.at[slot],
                                  sem_w.at[slot]),
        )

    def start_w(unit, f, slot):
        for cp in w_copies(unit, f, slot):
            cp.start()

    def wait_w(unit, f, slot):
        for cp in w_copies(unit, f, slot):
            cp.wait()

    def start_x(unit):
        lax.fori_loop(0, unb_ref[unit], lambda b, c: (x_copy(unit, b).start(), c)[1], 0)

    def wait_x(unit):
        lax.fori_loop(0, unb_ref[unit], lambda b, c: (x_copy(unit, b).wait(), c)[1], 0)

    def wait_y(unit):
        lax.fori_loop(0, unb_ref[unit], lambda b, c: (y_copy(unit, b).wait(), c)[1], 0)

    @pl.when(u < n_units)
    def _():
        nblk = unb_ref[u]

        @pl.when(u == 0)
        def _():
            start_x(u)
            start_w(u, 0, 0)

        wait_x(u)

        @pl.when(u > 0)
        def _():
            wait_y(u - 1)

        def init(b, c):
            r0 = pl.multiple_of(b * ROW_BLK, ROW_BLK)
            lo, hi = _unpack_bf16_pair(xbuf[pl.ds(r0, ROW_BLK), :])
            xb16[pl.ds(r0, ROW_BLK), 0:PACK_W] = lo
            xb16[pl.ds(r0, ROW_BLK), PACK_W:D_MODEL] = hi
            acc[pl.ds(r0, ROW_BLK), :] = jnp.broadcast_to(bo_ref[...], (ROW_BLK, D_MODEL))
            return c

        lax.fori_loop(0, nblk, init, 0)

        @pl.when(u + 1 < n_units)
        def _():
            start_x(u + 1)

        def group(b0, n_blk, f, ws):
            rows = n_blk * ROW_BLK
            r0 = pl.multiple_of(b0 * ROW_BLK, ROW_BLK)
            c0 = pl.multiple_of(f * EXP_TF, EXP_TF)
            x = xb16[pl.ds(r0, rows), :]
            wgu = jnp.concatenate([wg_buf[ws].astype(BF16), wu_buf[ws].astype(BF16)], axis=1)
            gu = jnp.dot(x, wgu, preferred_element_type=F32)
            bg = bin_ref[:, pl.ds(c0, EXP_TF)]
            bu = bin_ref[:, pl.ds(D_EXPERT + c0, EXP_TF)]
            g = jnp.minimum(gu[:, 0:EXP_TF] + bg, SWIGLU_LIMIT)
            up = jnp.clip(gu[:, EXP_TF:2 * EXP_TF] + bu, -SWIGLU_LIMIT, SWIGLU_LIMIT)
            act = (up + 1.0) * (g * _sigmoid(SWIGLU_ALPHA * g))
            acc[pl.ds(r0, rows), :] += jnp.dot(
                act.astype(BF16), wo_buf[ws].astype(BF16), preferred_element_type=F32)

            @pl.when(f == EXP_NF - 1)
            def _():
                for j in range(n_blk):
                    y_copy(u, b0 + j).start()

        def f_tile(f, ws):
            big = GROUP_SIZES[0]
            n_big = nblk // big
            lax.fori_loop(0, n_big, lambda q, c: (group(big * q, big, f, ws), c)[1], 0)
            for size in GROUP_SIZES[1:]:
                done = (nblk // (2 * size)) * (2 * size)

                @pl.when(lax.rem(nblk, 2 * size) >= size)
                def _():
                    group(done, size, f, ws)

        def f_step(f, c):
            ws = lax.rem(f, 2)

            @pl.when(f + 1 < EXP_NF)
            def _():
                start_w(u, f + 1, 1 - ws)

            @pl.when((f + 1 == EXP_NF) & (u + 1 < n_units))
            def _():
                start_w(u + 1, 0, 1 - ws)

            wait_w(u, f, ws)
            f_tile(f, ws)
            return c

        lax.fori_loop(0, EXP_NF, f_step, 0)

        @pl.when(u == n_units - 1)
        def _():
            wait_y(u)

    @pl.when(u == pl.num_programs(0) - 1)
    def _():
        n_blocks = ys_ref.shape[0] // ROW_BLK
        acc[0:ROW_BLK, :] = jnp.zeros((ROW_BLK, D_MODEL), F32)

        def tail_copy(b):
            return pltpu.make_async_copy(acc.at[pl.ds(0, ROW_BLK)],
                                         ys_ref.at[pl.ds(b * ROW_BLK, ROW_BLK)], sem_y.at[0])

        lax.fori_loop(nu_ref[1], n_blocks, lambda b, c: (tail_copy(b).start(), c)[1], 0)
        lax.fori_loop(nu_ref[1], n_blocks, lambda b, c: (tail_copy(b).wait(), c)[1], 0)


def _expert_ffn(xs, unit_e, unit_blk0, unit_nblk, unit_counts, w_in, b_in, w_out, b_out):
    n_rows = xs.shape[0]
    n_units_max = unit_e.shape[0]

    def e_eff(u, ue, nu):
        return ue[jnp.minimum(u, jnp.maximum(nu[0] - 1, 0))]

    any_spec = pl.BlockSpec(memory_space=pl.ANY)
    in_specs = [
        any_spec, any_spec, any_spec,
        pl.BlockSpec((None, 1, 2 * D_EXPERT), lambda u, ue, ub, un, nu: (e_eff(u, ue, nu), 0, 0)),
        pl.BlockSpec((None, 1, D_MODEL), lambda u, ue, ub, un, nu: (e_eff(u, ue, nu), 0, 0)),
    ]
    grid_spec = pltpu.PrefetchScalarGridSpec(
        num_scalar_prefetch=4,
        grid=(n_units_max,),
        in_specs=in_specs,
        out_specs=any_spec,
        scratch_shapes=[
            pltpu.VMEM((UNIT_ROWS, PACK_W), U32),
            pltpu.VMEM((UNIT_ROWS, D_MODEL), BF16),
            pltpu.VMEM((UNIT_ROWS, D_MODEL), F32),
            pltpu.VMEM((2, D_MODEL, EXP_TF), F32),
            pltpu.VMEM((2, D_MODEL, EXP_TF), F32),
            pltpu.VMEM((2, EXP_TF, D_MODEL), F32),
            pltpu.SemaphoreType.DMA((1,)),
            pltpu.SemaphoreType.DMA((1,)),
            pltpu.SemaphoreType.DMA((2,)),
        ],
    )
    return pl.pallas_call(
        _expert_kernel,
        grid_spec=grid_spec,
        out_shape=jax.ShapeDtypeStruct((n_rows, D_MODEL), F32),
        compiler_params=_params(1),
        name="expert_ffn",
    )(unit_e, unit_blk0, unit_nblk, unit_counts, xs,
      w_in, w_out, b_in[:, None, :], b_out[:, None, :])


COMB_TM = 256


def _combine_kernel(dcur_ref, dnxt_ref, x1_ref, gate_ref, g_ref, ys_ref, o_ref, ybuf, sem):
    i = pl.program_id(0)
    n = pl.num_programs(0)
    slot = lax.rem(i, 2)

    def issue(dref, s):
        def rows8(tt, c):
            for si in range(8):
                for k in range(TOP_K):
                    d = dref[0, (tt * 8 + si) * TOP_K + k]
                    pltpu.make_async_copy(ys_ref.at[pl.ds(d, 1)],
                                          ybuf.at[s, k, tt, pl.ds(si, 1)],
                                          sem.at[s]).start(priority=k % 2)
            return c

        lax.fori_loop(0, COMB_TM // 8, rows8, 0)

    @pl.when(i == 0)
    def _():
        issue(dcur_ref, slot)

    @pl.when(i + 1 < n)
    def _():
        issue(dnxt_ref, 1 - slot)

    for k in range(TOP_K):
        pltpu.make_async_copy(ybuf.at[slot, k], ybuf.at[slot, k], sem.at[slot]).wait()

    gate = gate_ref[...]
    y = x1_ref[...]
    for k in range(TOP_K):
        y = y + gate[:, k:k + 1] * ybuf[slot, k].reshape(COMB_TM, D_MODEL)
    ms = jnp.mean(y * y, axis=-1, keepdims=True)
    o_ref[...] = (y * lax.rsqrt(ms + NORM_EPS)) * g_ref[...]


def _combine(dest, x1, gate, norm_g, ys):
    s = x1.shape[0]
    nt = s // COMB_TM
    dest3 = dest.reshape(nt, 1, COMB_TM * TOP_K)
    smem = lambda imap: pl.BlockSpec((None, 1, COMB_TM * TOP_K), imap, memory_space=pltpu.SMEM)
    row = lambda w: pl.BlockSpec((COMB_TM, w), lambda i: (i, 0))
    return pl.pallas_call(
        _combine_kernel,
        grid=(nt,),
        in_specs=[smem(lambda i: (i, 0, 0)),
                  smem(lambda i: (jnp.minimum(i + 1, nt - 1), 0, 0)),
                  row(D_MODEL), row(TOP_K), pl.BlockSpec((1, D_MODEL), lambda i: (0, 0)),
                  pl.BlockSpec(memory_space=pl.ANY)],
        out_specs=row(D_MODEL),
        out_shape=jax.ShapeDtypeStruct((s, D_MODEL), F32),
        scratch_shapes=[pltpu.VMEM((2, TOP_K, COMB_TM // 8, 8, D_MODEL), F32),
                        pltpu.SemaphoreType.DMA((2,))],
        compiler_params=_params(1),
        name="moe_combine",
    )(dest3, dest3, x1, gate, norm_g, ys)


def _route_tables(idx, pos, counts, n_tok):
    n_assign = n_tok * TOP_K
    counts = counts.reshape(N_EXPERTS).astype(I32)
    nblk_e = (counts + ROW_BLK - 1) // ROW_BLK
    pad_start = (jnp.cumsum(nblk_e) - nblk_e) * ROW_BLK
    onehot = idx[..., None] == jnp.arange(N_EXPERTS, dtype=I32)
    dest = pos + jnp.sum(jnp.where(onehot, pad_start, 0), axis=-1)

    n_blocks = -(-(n_assign + N_EXPERTS * (ROW_BLK - 1)) // ROW_BLK)
    used = jnp.sum(nblk_e)
    blk_ids = jnp.arange(n_blocks, dtype=I32)
    last_blk = pad_start // ROW_BLK + nblk_e - 1
    fill_blocks = jnp.concatenate([jnp.where(nblk_e > 0, last_blk, -1),
                                   jnp.where(blk_ids >= used, blk_ids, -1)]).astype(I32)

    n_units_max = (n_blocks + (UNIT_BLOCKS - 1) * N_EXPERTS) // UNIT_BLOCKS
    chunks_e = (nblk_e + UNIT_BLOCKS - 1) // UNIT_BLOCKS
    chunk_end = jnp.cumsum(chunks_e)
    chunk_start = chunk_end - chunks_e
    n_units = chunk_end[-1]
    uid = jnp.arange(n_units_max, dtype=I32)
    unit_e = jnp.minimum(jnp.searchsorted(chunk_end, uid, side="right"), N_EXPERTS - 1).astype(I32)
    c_in_e = uid - chunk_start[unit_e]
    unit_blk0 = (pad_start[unit_e] // ROW_BLK + c_in_e * UNIT_BLOCKS).astype(I32)
    unit_nblk = jnp.clip(nblk_e[unit_e] - c_in_e * UNIT_BLOCKS, 0, UNIT_BLOCKS).astype(I32)
    active = uid < n_units
    unit_blk0 = jnp.where(active, unit_blk0, 0)
    unit_nblk = jnp.where(active, unit_nblk, 0)
    unit_counts = jnp.stack([n_units, used]).astype(I32)
    return dest.astype(I32), fill_blocks, n_blocks * ROW_BLK, unit_e, unit_blk0, unit_nblk, unit_counts


def kernel(x, norm1_g, w_in, b_gate, conv_w, conv_b, conv_ln_g, conv_ln_b, w_conv_out, b_conv_out,
           w_attn_out, w_out, norm2_g, w_router, b_router, w_exp_in, b_exp_in, w_exp_out, b_exp_out,
           norm_f_g):
    b_sz, s_len, d = x.shape
    n_tok = b_sz * s_len
    x2 = x.reshape(n_tok, d)
    l = 0
    b_gate_ext = jnp.concatenate([jnp.zeros((GATE_COL0,), F32), b_gate[l]])[None, :]
    proj3 = _in_projection(x2, norm1_g[l][None, :], w_in[l], b_gate_ext)
    u_ln = _conv_branch(proj3, conv_w[l], conv_b[l][None, :], conv_ln_g[l][None, :],
                        conv_ln_b[l][None, :])
    outs, lses = [], []
    for g in range(N_GROUPS):
        o_g, lse_g = _dilated_attention(proj3, g)
        outs.append(o_g)
        lses.append(lse_g)
    wr = w_router[l]
    wr_hi = wr.astype(BF16)
    wr_lo = jnp.concatenate([wr_hi, (wr - wr_hi.astype(F32)).astype(BF16)], axis=1)
    x1, hp, logits = _mixer_output(
        x2, outs, lses, u_ln, proj3,
        w_attn_out[l].astype(BF16), w_conv_out[l].astype(BF16), b_conv_out[l][None, :],
        w_out[l].astype(BF16), norm2_g[l][None, :], wr_hi, wr_lo, b_router[l][None, :])

    idx, gate, pos, counts = _route(logits)
    dest, fill_blocks, n_rows, unit_e, unit_blk0, unit_nblk, unit_counts = _route_tables(
        idx, pos, counts, n_tok)
    xs = _dispatch(hp, dest, fill_blocks, n_rows)
    ys = _expert_ffn(xs, unit_e, unit_blk0, unit_nblk, unit_counts,
                     w_exp_in[l], b_exp_in[l], w_exp_out[l], b_exp_out[l])
    out = _combine(dest, x1, gate, norm_f_g[None, :], ys)
    return out.reshape(b_sz, s_len, d)
```

```python
import functools

import jax
import jax.numpy as jnp
from jax import lax
from jax.experimental import pallas as pl
from jax.experimental.pallas import tpu as pltpu

D_MODEL = 2048
SEQ = 8192
D_CONV = D_MODEL // 2
CONV_WIDTH = 31
HEAD_DIM = 128
HEADS_PER_GROUP = 4
ATTN_PATTERNS = ((128, 1), (512, 4), (2048, 16))
N_GROUPS = len(ATTN_PATTERNS)
ATTN_WIDTH = N_GROUPS * HEADS_PER_GROUP * HEAD_DIM
ATTN_OUT_WIDTH = HEADS_PER_GROUP * HEAD_DIM
ATTN_BLOCK = 128
D_IN = 2 * D_CONV + 3 * ATTN_WIDTH + 2 * D_MODEL
N_EXPERTS = 32
TOP_K = 4
D_EXPERT = D_MODEL
SWIGLU_LIMIT = 7.0
SWIGLU_ALPHA = 1.702
NORM_EPS = 1e-5

CB = 512
N_CB = D_IN // CB
CB_CONV_A, CB_CONV_G = 0, 2
CB_Q, CB_K, CB_V = 4, 7, 10
CB_GATE_C, CB_GATE_A = 13, 17
GATE_COL0 = CB_GATE_C * CB

VMEM_LIMIT = 56 * 1024 * 1024

F32 = jnp.float32
BF16 = jnp.bfloat16
U32 = jnp.uint32
I32 = jnp.int32

PACK_W = D_MODEL // 2
HI_MASK = 0xFFFF0000


def _sigmoid(z):
    return 0.5 * jnp.tanh(0.5 * z) + 0.5


def _pack_bf16_pair(lo, hi):
    lo_bits = lax.bitcast_convert_type(lo.astype(BF16).astype(F32), U32) >> 16
    hi_bits = lax.bitcast_convert_type(hi.astype(BF16).astype(F32), U32) & jnp.uint32(HI_MASK)
    return hi_bits | lo_bits


def _unpack_bf16_pair(w):
    lo = lax.bitcast_convert_type(w << 16, F32).astype(BF16)
    hi = lax.bitcast_convert_type(w & jnp.uint32(HI_MASK), F32).astype(BF16)
    return lo, hi


def _params(n_axes, vmem=VMEM_LIMIT):
    return pltpu.CompilerParams(
        dimension_semantics=("arbitrary",) * n_axes, vmem_limit_bytes=vmem)


NORM_TM = 512
IN_TM = 1024
IN_TN = 1536


def _rmsnorm_kernel(x_ref, g_ref, o_ref):
    x = x_ref[...]
    ms = jnp.mean(x * x, axis=-1, keepdims=True)
    o_ref[...] = ((x * lax.rsqrt(ms + NORM_EPS)) * g_ref[...]).astype(BF16)


def _rmsnorm_bf16(x2, norm_g):
    s = x2.shape[0]
    row = pl.BlockSpec((NORM_TM, D_MODEL), lambda i: (i, 0))
    return pl.pallas_call(
        _rmsnorm_kernel,
        grid=(s // NORM_TM,),
        in_specs=[row, pl.BlockSpec((1, D_MODEL), lambda i: (0, 0))],
        out_specs=row,
        out_shape=jax.ShapeDtypeStruct((s, D_MODEL), BF16),
        compiler_params=_params(1),
        name="rmsnorm_in",
    )(x2, norm_g)


def _inproj_kernel(h_ref, w_ref, bg_ref, o_ref, wbf_ref):
    j = pl.program_id(0)
    i = pl.program_id(1)

    @pl.when(i == 0)
    def _():
        wbf_ref[...] = w_ref[...].astype(BF16)

    acc = jnp.dot(h_ref[...], wbf_ref[...], preferred_element_type=F32) + bg_ref[...]
    for c in range(IN_TN // CB):
        o_ref[c] = acc[:, c * CB:(c + 1) * CB].astype(BF16)


def _in_projection(x2, norm_g, w_in, b_gate_ext):
    s = x2.shape[0]
    h = _rmsnorm_bf16(x2, norm_g)
    grid = (D_IN // IN_TN, s // IN_TM)
    return pl.pallas_call(
        _inproj_kernel,
        grid=grid,
        in_specs=[
            pl.BlockSpec((IN_TM, D_MODEL), lambda j, i: (i, 0)),
            pl.BlockSpec((D_MODEL, IN_TN), lambda j, i: (0, j)),
            pl.BlockSpec((1, IN_TN), lambda j, i: (0, j)),
        ],
        out_specs=pl.BlockSpec((IN_TN // CB, IN_TM, CB), lambda j, i: (j, i, 0)),
        out_shape=jax.ShapeDtypeStruct((N_CB, s, CB), BF16),
        scratch_shapes=[pltpu.VMEM((D_MODEL, IN_TN), BF16)],
        compiler_params=_params(2),
        name="in_projection",
    )(h, w_in, b_gate_ext)


CONV_TS = 256
CONV_HALO = 32
CONV_RC = 32
CONV_N = CONV_TS + CONV_HALO


def _conv_kernel(ac_ref, gc_ref, ah_ref, gh_ref, w_ref, cb_ref, lg_ref, lb_ref, o_ref, r_ref,
                 c_ref):
    i = pl.program_id(0)
    has_prev = i > 0
    half = D_CONV // 2
    for c in range(2):
        cs = slice(c * half, (c + 1) * half)
        uh = ah_ref[c].astype(F32) * _sigmoid(gh_ref[c].astype(F32))
        r_ref[0, 0:CONV_HALO, cs] = jnp.where(has_prev, uh, 0.0)
        r_ref[0, CONV_HALO:CONV_N, cs] = ac_ref[c].astype(F32) * _sigmoid(gc_ref[c].astype(F32))
    u_ext = r_ref[0]
    for b in range(1, 8):
        r_ref[b] = pltpu.roll(u_ext, CONV_N - b, axis=0)

    def chunk(ci, carry):
        r0 = pl.multiple_of(ci * CONV_RC, CONV_RC)
        acc = jnp.broadcast_to(cb_ref[...], (CONV_RC, D_CONV))
        for k in range(CONV_WIDTH):
            kp = k + (CONV_HALO - (CONV_WIDTH - 1))
            a, b = kp // 8, kp % 8
            acc = acc + w_ref[k:k + 1, :] * r_ref[b, pl.ds(r0 + 8 * a, CONV_RC), :]
        c_ref[pl.ds(r0, CONV_RC), :] = acc
        return carry

    lax.fori_loop(0, CONV_TS // CONV_RC, chunk, 0)
    conv = c_ref[...]
    mu = jnp.mean(conv, axis=-1, keepdims=True)
    d = conv - mu
    var = jnp.mean(d * d, axis=-1, keepdims=True)
    y = (d * lax.rsqrt(var + NORM_EPS)) * lg_ref[...] + lb_ref[...]
    o_ref[...] = (y * _sigmoid(y)).astype(BF16)


def _conv_branch(proj3, conv_w, conv_b, ln_g, ln_b):
    s = proj3.shape[1]
    hb = CONV_TS // CONV_HALO
    cur = lambda blk: pl.BlockSpec((2, CONV_TS, CB), lambda i: (blk // 2, i, 0))
    halo = lambda blk: pl.BlockSpec(
        (2, CONV_HALO, CB), lambda i: (blk // 2, jnp.maximum(i * hb - 1, 0), 0))
    vec = pl.BlockSpec((1, D_CONV), lambda i: (0, 0))
    return pl.pallas_call(
        _conv_kernel,
        grid=(s // CONV_TS,),
        in_specs=[cur(CB_CONV_A), cur(CB_CONV_G), halo(CB_CONV_A), halo(CB_CONV_G),
                  pl.BlockSpec((CONV_WIDTH, D_CONV), lambda i: (0, 0)), vec, vec, vec],
        out_specs=pl.BlockSpec((CONV_TS, D_CONV), lambda i: (i, 0)),
        out_shape=jax.ShapeDtypeStruct((s, D_CONV), BF16),
        scratch_shapes=[pltpu.VMEM((8, CONV_N, D_CONV), F32), pltpu.VMEM((CONV_TS, D_CONV), F32)],
        compiler_params=_params(1),
        name="conv_branch",
    )(proj3, proj3, proj3, proj3, conv_w, conv_b, ln_g, ln_b)


LSE_W = 128
LSE_REP = LSE_W // HEADS_PER_GROUP


def _attn_rows(q, k, v, n):
    blk = ATTN_BLOCK
    scale = HEAD_DIM ** -0.5
    qi = lax.broadcasted_iota(I32, (blk, 2 * blk), 0)
    ki = lax.broadcasted_iota(I32, (blk, 2 * blk), 1)
    band = (ki >= qi) & (ki <= qi + blk)
    lane_head = lax.broadcasted_iota(I32, (blk, LSE_W), 1) // LSE_REP
    o_rows, lse_rows = [], []
    for j in range(q.shape[0] // blk):
        mask = band
        if j == 0:
            mask = band & ((ki >= blk) | (n > 0))
        o_heads = []
        lse_tile = jnp.zeros((blk, LSE_W), F32)
        for hh in range(HEADS_PER_GROUP):
            cs = slice(hh * HEAD_DIM, (hh + 1) * HEAD_DIM)
            qh = q[j * blk:(j + 1) * blk, cs]
            kh = k[j * blk:(j + 2) * blk, cs]
            vh = v[j * blk:(j + 2) * blk, cs]
            s = lax.dot_general(qh, kh, (((1,), (1,)), ((), ())),
                                preferred_element_type=F32) * scale
            s = jnp.where(mask, s, -jnp.inf)
            m = jnp.max(s, axis=-1, keepdims=True)
            p = jnp.exp(s - m)
            l = jnp.sum(p, axis=-1, keepdims=True)
            o_heads.append(jnp.dot(p.astype(BF16), vh, preferred_element_type=F32) / l)
            lse_tile = jnp.where(lane_head == hh, m + jnp.log(l), lse_tile)
        o_rows.append(jnp.concatenate(o_heads, axis=1))
        lse_rows.append(lse_tile)
    return jnp.concatenate(o_rows, axis=0), jnp.concatenate(lse_rows, axis=0)


def _attn_kernel(q_ref, kc_ref, kp_ref, vc_ref, vp_ref, o_ref, lse_ref):
    n = pl.program_id(0)
    k = jnp.concatenate([kp_ref[...], kc_ref[...]], axis=0)
    v = jnp.concatenate([vp_ref[...], vc_ref[...]], axis=0)
    o, lse = _attn_rows(q_ref[...], k, v, n)
    o_ref[...] = o.astype(BF16)
    lse_ref[...] = lse


ATTN_TILE = 2048


def _attn_dilated_kernel(q_ref, kc_ref, kp_ref, vc_ref, vp_ref, o_ref, lse_ref,
                         qs, ks, vs, os_, ls, *, dil):
    n = pl.program_id(0)
    tile = q_ref.shape[0]
    mb = tile // dil
    half = dil // 2
    halo_w = ATTN_BLOCK * half
    n_slab = CB // 128
    q32 = pltpu.bitcast(q_ref[...], U32)
    kp32, kc32 = pltpu.bitcast(kp_ref[...], U32), pltpu.bitcast(kc_ref[...], U32)
    vp32, vc32 = pltpu.bitcast(vp_ref[...], U32), pltpu.bitcast(vc_ref[...], U32)
    for c in range(n_slab):
        cs = slice(c * 128, (c + 1) * 128)
        qs[c] = q32[:, cs]
        ks[c, 0:halo_w] = kp32[:, cs]
        ks[c, halo_w:] = kc32[:, cs]
        vs[c, 0:halo_w] = vp32[:, cs]
        vs[c, halo_w:] = vc32[:, cs]

    def gather(slabs, rp, rows):
        return jnp.concatenate(
            [slabs[c, pl.ds(rp, rows, stride=half), :] for c in range(n_slab)], axis=1)

    def residue(w, par):
        bits = (w << 16) if par == 0 else (w & jnp.uint32(HI_MASK))
        return lax.bitcast_convert_type(bits, F32).astype(BF16)

    for rp in range(half):
        wq = gather(qs, rp, mb)
        wk = gather(ks, rp, mb + ATTN_BLOCK)
        wv = gather(vs, rp, mb + ATTN_BLOCK)
        outs = []
        for par in range(2):
            o, lse = _attn_rows(residue(wq, par), residue(wk, par), residue(wv, par), n)
            outs.append(o)
            ls[pl.ds(2 * rp + par, mb, stride=dil), :] = lse
        w_out = _pack_bf16_pair(outs[0], outs[1])
        for c in range(n_slab):
            os_[c, pl.ds(rp, mb, stride=half), :] = w_out[:, c * 128:(c + 1) * 128]
    o32 = jnp.concatenate([os_[c] for c in range(n_slab)], axis=1)
    o_ref[...] = pltpu.bitcast(o32, BF16)
    lse_ref[...] = ls[...]


def _dilated_attention(proj3, group):
    _, dil = ATTN_PATTERNS[group]
    s = proj3.shape[1]
    cbs = (CB_Q + group, CB_K + group, CB_V + group)
    if dil == 1:
        tile, halo = 1024, ATTN_BLOCK
        body, scratch = _attn_kernel, []
    else:
        tile, halo = ATTN_TILE, ATTN_BLOCK * dil
        body = functools.partial(_attn_dilated_kernel, dil=dil)
        scratch = [pltpu.VMEM((CB // 128, tile // 2, 128), U32),
                   pltpu.VMEM((CB // 128, (tile + halo) // 2, 128), U32),
                   pltpu.VMEM((CB // 128, (tile + halo) // 2, 128), U32),
                   pltpu.VMEM((CB // 128, tile // 2, 128), U32),
                   pltpu.VMEM((tile, LSE_W), F32)]
    hb = tile // halo
    cur = lambda cb: pl.BlockSpec((None, tile, CB), lambda n: (cb, n, 0))
    prev = lambda cb: pl.BlockSpec((None, halo, CB), lambda n: (cb, jnp.maximum(n * hb - 1, 0), 0))
    return pl.pallas_call(
        body,
        grid=(s // tile,),
        in_specs=[cur(cbs[0]), cur(cbs[1]), prev(cbs[1]), cur(cbs[2]), prev(cbs[2])],
        out_specs=[pl.BlockSpec((tile, CB), lambda n: (n, 0)),
                   pl.BlockSpec((tile, LSE_W), lambda n: (n, 0))],
        out_shape=[jax.ShapeDtypeStruct((s, CB), BF16),
                   jax.ShapeDtypeStruct((s, LSE_W), F32)],
        scratch_shapes=scratch,
        compiler_params=_params(1),
        name=f"dilated_attention_g{group}",
    )(proj3, proj3, proj3, proj3, proj3)


MIX_TM = 256


def _mixer_out_kernel(x_ref, o0_ref, o1_ref, o2_ref, l0_ref, l1_ref, l2_ref, u_ref,
                      gc0, gc1, gc2, gc3, ga0, ga1, ga2, ga3,
                      wa_ref, wc_ref, bc_ref, wo_ref, g2_ref, wrh_ref, wrl_ref, br_ref,
                      x1_ref, hp_ref, logit_ref):
    l_refs = (l0_ref, l1_ref, l2_ref)
    o_refs = (o0_ref, o1_ref, o2_ref)
    parts = []
    for hh in range(HEADS_PER_GROUP):
        cs = slice(hh * HEAD_DIM, (hh + 1) * HEAD_DIM)
        lse = [r[:, hh * LSE_REP:hh * LSE_REP + 1] for r in l_refs]
        m = jnp.maximum(jnp.maximum(lse[0], lse[1]), lse[2])
        e = [jnp.exp(v - m) for v in lse]
        den = e[0] + e[1] + e[2]
        acc = (e[0] / den) * o_refs[0][:, cs].astype(F32)
        for g in range(1, N_GROUPS):
            acc = acc + (e[g] / den) * o_refs[g][:, cs].astype(F32)
        parts.append(acc)
    o = jnp.concatenate(parts, axis=1).astype(BF16)
    ya = jnp.dot(o, wa_ref[...], preferred_element_type=F32)
    yc = jnp.dot(u_ref[...], wc_ref[...], preferred_element_type=F32) + bc_ref[...]
    gcs = (gc0, gc1, gc2, gc3)
    gas = (ga0, ga1, ga2, ga3)
    merged = []
    for c in range(D_MODEL // CB):
        cs = slice(c * CB, (c + 1) * CB)
        merged.append((_sigmoid(gcs[c][...].astype(F32)) * yc[:, cs]
                       + _sigmoid(gas[c][...].astype(F32)) * ya[:, cs]).astype(BF16))
    merged = jnp.concatenate(merged, axis=1)
    x1 = x_ref[...] + jnp.dot(merged, wo_ref[...], preferred_element_type=F32)
    x1_ref[...] = x1
    ms = jnp.mean(x1 * x1, axis=-1, keepdims=True)
    h2 = (x1 * lax.rsqrt(ms + NORM_EPS)) * g2_ref[...]
    h2_hi = h2.astype(BF16)
    hp_ref[...] = _pack_bf16_pair(h2[:, 0:PACK_W], h2[:, PACK_W:D_MODEL])
    h2_lo = (h2 - h2_hi.astype(F32)).astype(BF16)
    hi_pass = jnp.dot(h2_hi, wrl_ref[...], preferred_element_type=F32)
    logits = (hi_pass[:, 0:N_EXPERTS] + hi_pass[:, N_EXPERTS:2 * N_EXPERTS]
              + jnp.dot(h2_lo, wrh_ref[...], preferred_element_type=F32)) + br_ref[...]
    logit_ref[...] = logits


def _mixer_output(x2, outs, lses, u_ln, proj3, wa, wc, bc, wo, g2, wr_hi, wr_lo, br):
    s = x2.shape[0]
    row = lambda w: pl.BlockSpec((MIX_TM, w), lambda i: (i, 0))
    gate = lambda cb: pl.BlockSpec((None, MIX_TM, CB), lambda i: (cb, i, 0))
    full = lambda a: pl.BlockSpec(a.shape, lambda i: (0,) * a.ndim)
    in_specs = ([row(D_MODEL)] + [row(CB)] * 3 + [row(LSE_W)] * 3 + [row(D_CONV)]
                + [gate(CB_GATE_C + c) for c in range(4)]
                + [gate(CB_GATE_A + c) for c in range(4)]
                + [full(a) for a in (wa, wc, bc, wo, g2, wr_hi, wr_lo, br)])
    return pl.pallas_call(
        _mixer_out_kernel,
        grid=(s // MIX_TM,),
        in_specs=in_specs,
        out_specs=[row(D_MODEL), row(PACK_W), row(N_EXPERTS)],
        out_shape=[jax.ShapeDtypeStruct((s, D_MODEL), F32),
                   jax.ShapeDtypeStruct((s, PACK_W), U32),
                   jax.ShapeDtypeStruct((s, N_EXPERTS), F32)],
        compiler_params=_params(1),
        name="mixer_output",
    )(x2, *outs, *lses, u_ln, *([proj3] * 8), wa, wc, bc, wo, g2, wr_hi, wr_lo, br)


ROUTE_TM = 1024


def _route_kernel(logit_ref, idx_ref, gate_ref, pos_ref, cnt_ref, carry_ref):
    i = pl.program_id(0)
    tm = ROUTE_TM

    @pl.when(i == 0)
    def _():
        carry_ref[...] = jnp.zeros_like(carry_ref)

    logits = logit_ref[...]
    e_iota = lax.broadcasted_iota(I32, (tm, N_EXPERTS), 1).astype(F32)
    k_lane = lax.broadcasted_iota(I32, (tm, TOP_K), 1)
    vals = logits
    sels = []
    idx_t = jnp.zeros((tm, TOP_K), F32)
    val_t = jnp.zeros((tm, TOP_K), F32)
    for k in range(TOP_K):
        mk = jnp.max(vals, axis=-1, keepdims=True)
        ik = jnp.min(jnp.where(vals == mk, e_iota, float(N_EXPERTS)), axis=-1, keepdims=True)
        sel = e_iota == ik
        sels.append(sel)
        vals = jnp.where(sel, -jnp.inf, vals)
        idx_t = jnp.where(k_lane == k, ik, idx_t)
        val_t = jnp.where(k_lane == k, mk, val_t)
    ex = jnp.exp(val_t - val_t[:, 0:1])
    gate_ref[...] = ex / jnp.sum(ex, axis=-1, keepdims=True)
    idx_ref[...] = idx_t.astype(I32)

    cnt = jnp.zeros((tm, N_EXPERTS), F32)
    for sel in sels:
        cnt = cnt + jnp.where(sel, 1.0, 0.0)
    ri = lax.broadcasted_iota(I32, (tm, tm), 0)
    ci = lax.broadcasted_iota(I32, (tm, tm), 1)
    lower = jnp.where(ci < ri, 1.0, 0.0).astype(BF16)
    prefix = jnp.dot(lower, cnt.astype(BF16), preferred_element_type=F32) + carry_ref[...]
    pos_t = jnp.zeros((tm, TOP_K), F32)
    for k, sel in enumerate(sels):
        pk = jnp.sum(jnp.where(sel, prefix, 0.0), axis=-1, keepdims=True)
        pos_t = jnp.where(k_lane == k, pk, pos_t)
    pos_ref[...] = pos_t.astype(I32)
    carry_ref[...] = carry_ref[...] + jnp.sum(cnt, axis=0, keepdims=True)
    cnt_ref[...] = carry_ref[...]


def _route(logits):
    s = logits.shape[0]
    row = lambda w: pl.BlockSpec((ROUTE_TM, w), lambda i: (i, 0))
    return pl.pallas_call(
        _route_kernel,
        grid=(s // ROUTE_TM,),
        in_specs=[row(N_EXPERTS)],
        out_specs=[row(TOP_K), row(TOP_K), row(TOP_K),
                   pl.BlockSpec((1, N_EXPERTS), lambda i: (0, 0))],
        out_shape=[jax.ShapeDtypeStruct((s, TOP_K), I32),
                   jax.ShapeDtypeStruct((s, TOP_K), F32),
                   jax.ShapeDtypeStruct((s, TOP_K), I32),
                   jax.ShapeDtypeStruct((1, N_EXPERTS), F32)],
        scratch_shapes=[pltpu.VMEM((1, N_EXPERTS), F32)],
        compiler_params=_params(1),
        name="moe_route",
    )(logits)


ROW_BLK = 128
DISP_TM = 256


def _dispatch_kernel(fill_ref, dest_ref, h_ref, xs_ref, zbuf, sem):
    i = pl.program_id(0)
    n_fill = fill_ref.shape[0]

    @pl.when(i == 0)
    def _():
        zbuf[...] = jnp.zeros_like(zbuf)

        def fill_copy(j):
            return pltpu.make_async_copy(
                zbuf, xs_ref.at[pl.ds(fill_ref[j] * ROW_BLK, ROW_BLK)], sem.at[0])

        def start(j, c):
            @pl.when(fill_ref[j] >= 0)
            def _():
                fill_copy(j).start()
            return c

        def wait(j, c):
            @pl.when(fill_ref[j] >= 0)
            def _():
                fill_copy(j).wait()
            return c

        lax.fori_loop(0, n_fill, start, 0)
        lax.fori_loop(0, n_fill, wait, 0)

    def rows8(tt, c):
        for si in range(8):
            for k in range(TOP_K):
                d = dest_ref[0, (tt * 8 + si) * TOP_K + k]
                pltpu.make_async_copy(h_ref.at[tt, pl.ds(si, 1)], xs_ref.at[pl.ds(d, 1)],
                                      sem.at[0]).start(priority=k % 2)
        return c

    lax.fori_loop(0, DISP_TM // 8, rows8, 0)
    for _ in range(TOP_K):
        pltpu.make_async_copy(h_ref, h_ref, sem.at[0]).wait()


def _dispatch(hp, dest, fill_blocks, n_rows):
    s = hp.shape[0]
    dest3 = dest.reshape(s // DISP_TM, 1, DISP_TM * TOP_K)
    grid_spec = pltpu.PrefetchScalarGridSpec(
        num_scalar_prefetch=1,
        grid=(s // DISP_TM,),
        in_specs=[
            pl.BlockSpec((None, 1, DISP_TM * TOP_K), lambda i, fr: (i, 0, 0),
                         memory_space=pltpu.SMEM),
            pl.BlockSpec((DISP_TM // 8, 8, PACK_W), lambda i, fr: (i, 0, 0)),
        ],
        out_specs=pl.BlockSpec(memory_space=pl.ANY),
        scratch_shapes=[pltpu.VMEM((ROW_BLK, PACK_W), U32), pltpu.SemaphoreType.DMA((1,))],
    )
    return pl.pallas_call(
        _dispatch_kernel,
        grid_spec=grid_spec,
        out_shape=jax.ShapeDtypeStruct((n_rows, PACK_W), U32),
        compiler_params=_params(1),
        name="moe_dispatch",
    )(fill_blocks, dest3, hp.reshape(s // 8, 8, PACK_W))


UNIT_BLOCKS = 10
UNIT_ROWS = ROW_BLK * UNIT_BLOCKS
GROUP_SIZES = (8, 4, 2, 1)
EXP_TF = 512
EXP_NF = D_EXPERT // EXP_TF


def _expert_kernel(ue_ref, ub0_ref, unb_ref, nu_ref,
                   xs_ref, win_ref, wout_ref, bin_ref, bo_ref,
                   ys_ref,
                   xbuf, xb16, acc, wg_buf, wu_buf, wo_buf, sem_x, sem_y, sem_w):
    u = pl.program_id(0)
    n_units = nu_ref[0]

    def x_copy(unit, b):
        src = xs_ref.at[pl.ds((ub0_ref[unit] + b) * ROW_BLK, ROW_BLK)]
        dst = xbuf.at[pl.ds(b * ROW_BLK, ROW_BLK)]
        return pltpu.make_async_copy(src, dst, sem_x.at[0])

    def y_copy(unit, b):
        src = acc.at[pl.ds(b * ROW_BLK, ROW_BLK)]
        dst = ys_ref.at[pl.ds((ub0_ref[unit] + b) * ROW_BLK, ROW_BLK)]
        return pltpu.make_async_copy(src, dst, sem_y.at[0])

    def w_copies(unit, f, slot):
        e = ue_ref[unit]
        c0 = pl.multiple_of(f * EXP_TF, EXP_TF)
        return (
            pltpu.make_async_copy(win_ref.at[e, :, pl.ds(c0, EXP_TF)], wg_buf.at[slot],
                                  sem_w.at[slot]),
            pltpu.make_async_copy(win_ref.at[e, :, pl.ds(D_EXPERT + c0, EXP_TF)], wu_buf.at[slot],
                                  sem_w.at[slot]),
            pltpu.make_async_copy(wout_ref.at[e, pl.ds(c0, EXP_TF), :], wo_buf.at[slot],
                                  sem_w.at[slot]),
        )

    def start_w(unit, f, slot):
        for cp in w_copies(unit, f, slot):
            cp.start()

    def wait_w(unit, f, slot):
        for cp in w_copies(unit, f, slot):
            cp.wait()

    def start_x(unit):
        lax.fori_loop(0, unb_ref[unit], lambda b, c: (x_copy(unit, b).start(), c)[1], 0)

    def wait_x(unit):
        lax.fori_loop(0, unb_ref[unit], lambda b, c: (x_copy(unit, b).wait(), c)[1], 0)

    def wait_y(unit):
        lax.fori_loop(0, unb_ref[unit], lambda b, c: (y_copy(unit, b).wait(), c)[1], 0)

    @pl.when(u < n_units)
    def _():
        nblk = unb_ref[u]

        @pl.when(u == 0)
        def _():
            start_x(u)
            start_w(u, 0, 0)

        wait_x(u)

        @pl.when(u > 0)
        def _():
            wait_y(u - 1)

        def init(b, c):
            r0 = pl.multiple_of(b * ROW_BLK, ROW_BLK)
            lo, hi = _unpack_bf16_pair(xbuf[pl.ds(r0, ROW_BLK), :])
            xb16[pl.ds(r0, ROW_BLK), 0:PACK_W] = lo
            xb16[pl.ds(r0, ROW_BLK), PACK_W:D_MODEL] = hi
            acc[pl.ds(r0, ROW_BLK), :] = jnp.broadcast_to(bo_ref[...], (ROW_BLK, D_MODEL))
            return c

        lax.fori_loop(0, nblk, init, 0)

        @pl.when(u + 1 < n_units)
        def _():
            start_x(u + 1)

        def group(b0, n_blk, f, ws):
            rows = n_blk * ROW_BLK
            r0 = pl.multiple_of(b0 * ROW_BLK, ROW_BLK)
            c0 = pl.multiple_of(f * EXP_TF, EXP_TF)
            x = xb16[pl.ds(r0, rows), :]
            wgu = jnp.concatenate([wg_buf[ws].astype(BF16), wu_buf[ws].astype(BF16)], axis=1)
            gu = jnp.dot(x, wgu, preferred_element_type=F32)
            bg = bin_ref[:, pl.ds(c0, EXP_TF)]
            bu = bin_ref[:, pl.ds(D_EXPERT + c0, EXP_TF)]
            g = jnp.minimum(gu[:, 0:EXP_TF] + bg, SWIGLU_LIMIT)
            up = jnp.clip(gu[:, EXP_TF:2 * EXP_TF] + bu, -SWIGLU_LIMIT, SWIGLU_LIMIT)
            act = (up + 1.0) * (g * _sigmoid(SWIGLU_ALPHA * g))
            acc[pl.ds(r0, rows), :] += jnp.dot(
                act.astype(BF16), wo_buf[ws].astype(BF16), preferred_element_type=F32)

            @pl.when(f == EXP_NF - 1)
            def _():
                for j in range(n_blk):
                    y_copy(u, b0 + j).start()

        def f_tile(f, ws):
            big = GROUP_SIZES[0]

            @pl.when(nblk == big + 1)
            def _():
                group(0, big + 1, f, ws)

            @pl.when(nblk != big + 1)
            def _():
                n_big = nblk // big
                lax.fori_loop(0, n_big, lambda q, c: (group(big * q, big, f, ws), c)[1], 0)
                for size in GROUP_SIZES[1:]:
                    done = (nblk // (2 * size)) * (2 * size)

                    @pl.when(lax.rem(nblk, 2 * size) >= size)
                    def _():
                        group(done, size, f, ws)

        def f_step(f, c):
            ws = lax.rem(f, 2)

            @pl.when(f + 1 < EXP_NF)
            def _():
                start_w(u, f + 1, 1 - ws)

            @pl.when((f + 1 == EXP_NF) & (u + 1 < n_units))
            def _():
                start_w(u + 1, 0, 1 - ws)

            wait_w(u, f, ws)
            f_tile(f, ws)
            return c

        lax.fori_loop(0, EXP_NF, f_step, 0)

        @pl.when(u == n_units - 1)
        def _():
            wait_y(u)

    @pl.when(u == pl.num_programs(0) - 1)
    def _():
        n_blocks = ys_ref.shape[0] // ROW_BLK
        acc[0:ROW_BLK, :] = jnp.zeros((ROW_BLK, D_MODEL), F32)

        def tail_copy(b):
            return pltpu.make_async_copy(acc.at[pl.ds(0, ROW_BLK)],
                                         ys_ref.at[pl.ds(b * ROW_BLK, ROW_BLK)], sem_y.at[0])

        lax.fori_loop(nu_ref[1], n_blocks, lambda b, c: (tail_copy(b).start(), c)[1], 0)
        lax.fori_loop(nu_ref[1], n_blocks, lambda b, c: (tail_copy(b).wait(), c)[1], 0)


def _expert_ffn(xs, unit_e, unit_blk0, unit_nblk, unit_counts, w_in, b_in, w_out, b_out):
    n_rows = xs.shape[0]
    n_units_max = unit_e.shape[0]

    def e_eff(u, ue, nu):
        return ue[jnp.minimum(u, jnp.maximum(nu[0] - 1, 0))]

    any_spec = pl.BlockSpec(memory_space=pl.ANY)
    in_specs = [
        any_spec, any_spec, any_spec,
        pl.BlockSpec((None, 1, 2 * D_EXPERT), lambda u, ue, ub, un, nu: (e_eff(u, ue, nu), 0, 0)),
        pl.BlockSpec((None, 1, D_MODEL), lambda u, ue, ub, un, nu: (e_eff(u, ue, nu), 0, 0)),
    ]
    grid_spec = pltpu.PrefetchScalarGridSpec(
        num_scalar_prefetch=4,
        grid=(n_units_max,),
        in_specs=in_specs,
        out_specs=any_spec,
        scratch_shapes=[
            pltpu.VMEM((UNIT_ROWS, PACK_W), U32),
            pltpu.VMEM((UNIT_ROWS, D_MODEL), BF16),
            pltpu.VMEM((UNIT_ROWS, D_MODEL), F32),
            pltpu.VMEM((2, D_MODEL, EXP_TF), F32),
            pltpu.VMEM((2, D_MODEL, EXP_TF), F32),
            pltpu.VMEM((2, EXP_TF, D_MODEL), F32),
            pltpu.SemaphoreType.DMA((1,)),
            pltpu.SemaphoreType.DMA((1,)),
            pltpu.SemaphoreType.DMA((2,)),
        ],
    )
    return pl.pallas_call(
        _expert_kernel,
        grid_spec=grid_spec,
        out_shape=jax.ShapeDtypeStruct((n_rows, D_MODEL), F32),
        compiler_params=_params(1),
        name="expert_ffn",
    )(unit_e, unit_blk0, unit_nblk, unit_counts, xs,
      w_in, w_out, b_in[:, None, :], b_out[:, None, :])


COMB_TM = 256


def _combine_kernel(dcur_ref, dnxt_ref, x1_ref, gate_ref, g_ref, ys_ref, o_ref, ybuf, sem):
    i = pl.program_id(0)
    n = pl.num_programs(0)
    slot = lax.rem(i, 2)

    def issue(dref, s):
        def rows8(tt, c):
            for si in range(8):
                for k in range(TOP_K):
                    d = dref[0, (tt * 8 + si) * TOP_K + k]
                    pltpu.make_async_copy(ys_ref.at[pl.ds(d, 1)],
                                          ybuf.at[s, k, tt, pl.ds(si, 1)],
                                          sem.at[s]).start(priority=k % 2)
            return c

        lax.fori_loop(0, COMB_TM // 8, rows8, 0)

    @pl.when(i == 0)
    def _():
        issue(dcur_ref, slot)

    @pl.when(i + 1 < n)
    def _():
        issue(dnxt_ref, 1 - slot)

    for k in range(TOP_K):
        pltpu.make_async_copy(ybuf.at[slot, k], ybuf.at[slot, k], sem.at[slot]).wait()

    gate = gate_ref[...]
    y = x1_ref[...]
    for k in range(TOP_K):
        y = y + gate[:, k:k + 1] * ybuf[slot, k].reshape(COMB_TM, D_MODEL)
    ms = jnp.mean(y * y, axis=-1, keepdims=True)
    o_ref[...] = (y * lax.rsqrt(ms + NORM_EPS)) * g_ref[...]


def _combine(dest, x1, gate, norm_g, ys):
    s = x1.shape[0]
    nt = s // COMB_TM
    dest3 = dest.reshape(nt, 1, COMB_TM * TOP_K)
    smem = lambda imap: pl.BlockSpec((None, 1, COMB_TM * TOP_K), imap, memory_space=pltpu.SMEM)
    row = lambda w: pl.BlockSpec((COMB_TM, w), lambda i: (i, 0))
    return pl.pallas_call(
        _combine_kernel,
        grid=(nt,),
        in_specs=[smem(lambda i: (i, 0, 0)),
                  smem(lambda i: (jnp.minimum(i + 1, nt - 1), 0, 0)),
                  row(D_MODEL), row(TOP_K), pl.BlockSpec((1, D_MODEL), lambda i: (0, 0)),
                  pl.BlockSpec(memory_space=pl.ANY)],
        out_specs=row(D_MODEL),
        out_shape=jax.ShapeDtypeStruct((s, D_MODEL), F32),
        scratch_shapes=[pltpu.VMEM((2, TOP_K, COMB_TM // 8, 8, D_MODEL), F32),
                        pltpu.SemaphoreType.DMA((2,))],
        compiler_params=_params(1),
        name="moe_combine",
    )(dest3, dest3, x1, gate, norm_g, ys)


def _route_tables(idx, pos, counts, n_tok):
    n_assign = n_tok * TOP_K
    counts = counts.reshape(N_EXPERTS).astype(I32)
    nblk_e = (counts + ROW_BLK - 1) // ROW_BLK
    pad_start = (jnp.cumsum(nblk_e) - nblk_e) * ROW_BLK
    onehot = idx[..., None] == jnp.arange(N_EXPERTS, dtype=I32)
    dest = pos + jnp.sum(jnp.where(onehot, pad_start, 0), axis=-1)

    n_blocks = -(-(n_assign + N_EXPERTS * (ROW_BLK - 1)) // ROW_BLK)
    used = jnp.sum(nblk_e)
    blk_ids = jnp.arange(n_blocks, dtype=I32)
    last_blk = pad_start // ROW_BLK + nblk_e - 1
    fill_blocks = jnp.concatenate([jnp.where(nblk_e > 0, last_blk, -1),
                                   jnp.where(blk_ids >= used, blk_ids, -1)]).astype(I32)

    n_units_max = (n_blocks + (UNIT_BLOCKS - 1) * N_EXPERTS) // UNIT_BLOCKS
    chunks_e = (nblk_e + UNIT_BLOCKS - 1) // UNIT_BLOCKS
    chunk_end = jnp.cumsum(chunks_e)
    chunk_start = chunk_end - chunks_e
    n_units = chunk_end[-1]
    uid = jnp.arange(n_units_max, dtype=I32)
    unit_e = jnp.minimum(jnp.searchsorted(chunk_end, uid, side="right"), N_EXPERTS - 1).astype(I32)
    c_in_e = uid - chunk_start[unit_e]
    unit_blk0 = (pad_start[unit_e] // ROW_BLK + c_in_e * UNIT_BLOCKS).astype(I32)
    unit_nblk = jnp.clip(nblk_e[unit_e] - c_in_e * UNIT_BLOCKS, 0, UNIT_BLOCKS).astype(I32)
    active = uid < n_units
    unit_blk0 = jnp.where(active, unit_blk0, 0)
    unit_nblk = jnp.where(active, unit_nblk, 0)
    unit_counts = jnp.stack([n_units, used]).astype(I32)
    return dest.astype(I32), fill_blocks, n_blocks * ROW_BLK, unit_e, unit_blk0, unit_nblk, unit_counts


def kernel(x, norm1_g, w_in, b_gate, conv_w, conv_b, conv_ln_g, conv_ln_b, w_conv_out, b_conv_out,
           w_attn_out, w_out, norm2_g, w_router, b_router, w_exp_in, b_exp_in, w_exp_out, b_exp_out,
           norm_f_g):
    b_sz, s_len, d = x.shape
    n_tok = b_sz * s_len
    x2 = x.reshape(n_tok, d)
    l = 0
    b_gate_ext = jnp.concatenate([jnp.zeros((GATE_COL0,), F32), b_gate[l]])[None, :]
    proj3 = _in_projection(x2, norm1_g[l][None, :], w_in[l], b_gate_ext)
    u_ln = _conv_branch(proj3, conv_w[l], conv_b[l][None, :], conv_ln_g[l][None, :],
                        conv_ln_b[l][None, :])
    outs, lses = [], []
    for g in range(N_GROUPS):
        o_g, lse_g = _dilated_attention(proj3, g)
        outs.append(o_g)
        lses.append(lse_g)
    wr = w_router[l]
    wr_hi = wr.astype(BF16)
    wr_lo = jnp.concatenate([wr_hi, (wr - wr_hi.astype(F32)).astype(BF16)], axis=1)
    x1, hp, logits = _mixer_output(
        x2, outs, lses, u_ln, proj3,
        w_attn_out[l].astype(BF16), w_conv_out[l].astype(BF16), b_conv_out[l][None, :],
        w_out[l].astype(BF16), norm2_g[l][None, :], wr_hi, wr_lo, b_router[l][None, :])

    idx, gate, pos, counts = _route(logits)
    dest, fill_blocks, n_rows, unit_e, unit_blk0, unit_nblk, unit_counts = _route_tables(
        idx, pos, counts, n_tok)
    xs = _dispatch(hp, dest, fill_blocks, n_rows)
    ys = _expert_ffn(xs, unit_e, unit_blk0, unit_nblk, unit_counts,
                     w_exp_in[l], b_exp_in[l], w_exp_out[l], b_exp_out[l])
    out = _combine(dest, x1, gate, norm_f_g[None, :], ys)
    return out.reshape(b_sz, s_len, d)
```

```python
import functools

import jax
import jax.numpy as jnp
from jax import lax
from jax.experimental import pallas as pl
from jax.experimental.pallas import tpu as pltpu

D_MODEL = 2048
SEQ = 8192
D_CONV = D_MODEL // 2
CONV_WIDTH = 31
HEAD_DIM = 128
HEADS_PER_GROUP = 4
ATTN_PATTERNS = ((128, 1), (512, 4), (2048, 16))
N_GROUPS = len(ATTN_PATTERNS)
ATTN_WIDTH = N_GROUPS * HEADS_PER_GROUP * HEAD_DIM
ATTN_OUT_WIDTH = HEADS_PER_GROUP * HEAD_DIM
ATTN_BLOCK = 128
D_IN = 2 * D_CONV + 3 * ATTN_WIDTH + 2 * D_MODEL
N_EXPERTS = 32
TOP_K = 4
D_EXPERT = D_MODEL
SWIGLU_LIMIT = 7.0
SWIGLU_ALPHA = 1.702
NORM_EPS = 1e-5

CB = 512
N_CB = D_IN // CB
CB_CONV_A, CB_CONV_G = 0, 2
CB_Q, CB_K, CB_V = 4, 7, 10
CB_GATE_C, CB_GATE_A = 13, 17
GATE_COL0 = CB_GATE_C * CB

V7X_VMEM_BYTES = 64 * 1024 * 1024
VMEM_LIMIT = V7X_VMEM_BYTES - 8 * 1024 * 1024

F32 = jnp.float32
BF16 = jnp.bfloat16
U32 = jnp.uint32
I32 = jnp.int32

PACK_W = D_MODEL // 2
HI_MASK = 0xFFFF0000


def _sigmoid(z):
    return 0.5 * jnp.tanh(0.5 * z) + 0.5


def _pack_bf16_pair(lo, hi):
    lo_bits = lax.bitcast_convert_type(lo.astype(BF16).astype(F32), U32) >> 16
    hi_bits = lax.bitcast_convert_type(hi.astype(BF16).astype(F32), U32) & jnp.uint32(HI_MASK)
    return hi_bits | lo_bits


def _unpack_bf16_pair(w):
    lo = lax.bitcast_convert_type(w << 16, F32).astype(BF16)
    hi = lax.bitcast_convert_type(w & jnp.uint32(HI_MASK), F32).astype(BF16)
    return lo, hi


def _params(n_axes, vmem=VMEM_LIMIT):
    return pltpu.CompilerParams(
        dimension_semantics=("arbitrary",) * n_axes, vmem_limit_bytes=vmem)


NORM_TM = 512
IN_TM = 1024
IN_TN = 1536


def _rmsnorm_kernel(x_ref, g_ref, o_ref):
    x = x_ref[...]
    ms = jnp.mean(x * x, axis=-1, keepdims=True)
    o_ref[...] = ((x * lax.rsqrt(ms + NORM_EPS)) * g_ref[...]).astype(BF16)


def _rmsnorm_bf16(x2, norm_g):
    s = x2.shape[0]
    row = pl.BlockSpec((NORM_TM, D_MODEL), lambda i: (i, 0))
    return pl.pallas_call(
        _rmsnorm_kernel,
        grid=(s // NORM_TM,),
        in_specs=[row, pl.BlockSpec((1, D_MODEL), lambda i: (0, 0))],
        out_specs=row,
        out_shape=jax.ShapeDtypeStruct((s, D_MODEL), BF16),
        compiler_params=_params(1),
        name="rmsnorm_in",
    )(x2, norm_g)


def _inproj_kernel(h_ref, w_ref, bg_ref, o_ref, wbf_ref):
    j = pl.program_id(0)
    i = pl.program_id(1)

    @pl.when(i == 0)
    def _():
        wbf_ref[...] = w_ref[...].astype(BF16)

    acc = jnp.dot(h_ref[...], wbf_ref[...], preferred_element_type=F32) + bg_ref[...]
    for c in range(IN_TN // CB):
        o_ref[c] = acc[:, c * CB:(c + 1) * CB].astype(BF16)


def _in_projection(x2, norm_g, w_in, b_gate_ext):
    s = x2.shape[0]
    h = _rmsnorm_bf16(x2, norm_g)
    grid = (D_IN // IN_TN, s // IN_TM)
    return pl.pallas_call(
        _inproj_kernel,
        grid=grid,
        in_specs=[
            pl.BlockSpec((IN_TM, D_MODEL), lambda j, i: (i, 0)),
            pl.BlockSpec((D_MODEL, IN_TN), lambda j, i: (0, j)),
            pl.BlockSpec((1, IN_TN), lambda j, i: (0, j)),
        ],
        out_specs=pl.BlockSpec((IN_TN // CB, IN_TM, CB), lambda j, i: (j, i, 0)),
        out_shape=jax.ShapeDtypeStruct((N_CB, s, CB), BF16),
        scratch_shapes=[pltpu.VMEM((D_MODEL, IN_TN), BF16)],
        compiler_params=_params(2),
        name="in_projection",
    )(h, w_in, b_gate_ext)


CONV_TS = 256
CONV_HALO = 32
CONV_RC = 32
CONV_N = CONV_TS + CONV_HALO


def _conv_kernel(ac_ref, gc_ref, ah_ref, gh_ref, w_ref, cb_ref, lg_ref, lb_ref, o_ref, r_ref,
                 c_ref):
    i = pl.program_id(0)
    has_prev = i > 0
    half = D_CONV // 2
    for c in range(2):
        cs = slice(c * half, (c + 1) * half)
        uh = ah_ref[c].astype(F32) * _sigmoid(gh_ref[c].astype(F32))
        r_ref[0, 0:CONV_HALO, cs] = jnp.where(has_prev, uh, 0.0)
        r_ref[0, CONV_HALO:CONV_N, cs] = ac_ref[c].astype(F32) * _sigmoid(gc_ref[c].astype(F32))
    u_ext = r_ref[0]
    for b in range(1, 8):
        r_ref[b] = pltpu.roll(u_ext, CONV_N - b, axis=0)

    def chunk(ci, carry):
        r0 = pl.multiple_of(ci * CONV_RC, CONV_RC)
        acc = jnp.broadcast_to(cb_ref[...], (CONV_RC, D_CONV))
        for k in range(CONV_WIDTH):
            kp = k + (CONV_HALO - (CONV_WIDTH - 1))
            a, b = kp // 8, kp % 8
            acc = acc + w_ref[k:k + 1, :] * r_ref[b, pl.ds(r0 + 8 * a, CONV_RC), :]
        c_ref[pl.ds(r0, CONV_RC), :] = acc
        return carry

    lax.fori_loop(0, CONV_TS // CONV_RC, chunk, 0)
    conv = c_ref[...]
    mu = jnp.mean(conv, axis=-1, keepdims=True)
    d = conv - mu
    var = jnp.mean(d * d, axis=-1, keepdims=True)
    y = (d * lax.rsqrt(var + NORM_EPS)) * lg_ref[...] + lb_ref[...]
    o_ref[...] = (y * _sigmoid(y)).astype(BF16)


def _conv_branch(proj3, conv_w, conv_b, ln_g, ln_b):
    s = proj3.shape[1]
    hb = CONV_TS // CONV_HALO
    cur = lambda blk: pl.BlockSpec((2, CONV_TS, CB), lambda i: (blk // 2, i, 0))
    halo = lambda blk: pl.BlockSpec(
        (2, CONV_HALO, CB), lambda i: (blk // 2, jnp.maximum(i * hb - 1, 0), 0))
    vec = pl.BlockSpec((1, D_CONV), lambda i: (0, 0))
    return pl.pallas_call(
        _conv_kernel,
        grid=(s // CONV_TS,),
        in_specs=[cur(CB_CONV_A), cur(CB_CONV_G), halo(CB_CONV_A), halo(CB_CONV_G),
                  pl.BlockSpec((CONV_WIDTH, D_CONV), lambda i: (0, 0)), vec, vec, vec],
        out_specs=pl.BlockSpec((CONV_TS, D_CONV), lambda i: (i, 0)),
        out_shape=jax.ShapeDtypeStruct((s, D_CONV), BF16),
        scratch_shapes=[pltpu.VMEM((8, CONV_N, D_CONV), F32), pltpu.VMEM((CONV_TS, D_CONV), F32)],
        compiler_params=_params(1),
        name="conv_branch",
    )(proj3, proj3, proj3, proj3, conv_w, conv_b, ln_g, ln_b)


LSE_W = 128
LSE_REP = LSE_W // HEADS_PER_GROUP


def _attn_rows(q, k, v, n):
    blk = ATTN_BLOCK
    scale = HEAD_DIM ** -0.5
    qi = lax.broadcasted_iota(I32, (blk, 2 * blk), 0)
    ki = lax.broadcasted_iota(I32, (blk, 2 * blk), 1)
    band = (ki >= qi) & (ki <= qi + blk)
    lane_head = lax.broadcasted_iota(I32, (blk, LSE_W), 1) // LSE_REP
    o_rows, lse_rows = [], []
    for j in range(q.shape[0] // blk):
        mask = band
        if j == 0:
            mask = band & ((ki >= blk) | (n > 0))
        o_heads = []
        lse_tile = jnp.zeros((blk, LSE_W), F32)
        for hh in range(HEADS_PER_GROUP):
            cs = slice(hh * HEAD_DIM, (hh + 1) * HEAD_DIM)
            qh = q[j * blk:(j + 1) * blk, cs]
            kh = k[j * blk:(j + 2) * blk, cs]
            vh = v[j * blk:(j + 2) * blk, cs]
            s = lax.dot_general(qh, kh, (((1,), (1,)), ((), ())),
                                preferred_element_type=F32) * scale
            s = jnp.where(mask, s, -jnp.inf)
            m = jnp.max(s, axis=-1, keepdims=True)
            p = jnp.exp(s - m)
            l = jnp.sum(p, axis=-1, keepdims=True)
            o_heads.append(jnp.dot(p.astype(BF16), vh, preferred_element_type=F32) / l)
            lse_tile = jnp.where(lane_head == hh, m + jnp.log(l), lse_tile)
        o_rows.append(jnp.concatenate(o_heads, axis=1))
        lse_rows.append(lse_tile)
    return jnp.concatenate(o_rows, axis=0), jnp.concatenate(lse_rows, axis=0)


def _attn_kernel(q_ref, kc_ref, kp_ref, vc_ref, vp_ref, o_ref, lse_ref):
    n = pl.program_id(0)
    k = jnp.concatenate([kp_ref[...], kc_ref[...]], axis=0)
    v = jnp.concatenate([vp_ref[...], vc_ref[...]], axis=0)
    o, lse = _attn_rows(q_ref[...], k, v, n)
    o_ref[...] = o.astype(BF16)
    lse_ref[...] = lse


ATTN_TILE = 2048


def _attn_dilated_kernel(q_ref, kc_ref, kp_ref, vc_ref, vp_ref, o_ref, lse_ref,
                         qs, ks, vs, os_, ls, *, dil):
    n = pl.program_id(0)
    tile = q_ref.shape[0]
    mb = tile // dil
    half = dil // 2
    halo_w = ATTN_BLOCK * half
    n_slab = CB // 128
    q32 = pltpu.bitcast(q_ref[...], U32)
    kp32, kc32 = pltpu.bitcast(kp_ref[...], U32), pltpu.bitcast(kc_ref[...], U32)
    vp32, vc32 = pltpu.bitcast(vp_ref[...], U32), pltpu.bitcast(vc_ref[...], U32)
    for c in range(n_slab):
        cs = slice(c * 128, (c + 1) * 128)
        qs[c] = q32[:, cs]
        ks[c, 0:halo_w] = kp32[:, cs]
        ks[c, halo_w:] = kc32[:, cs]
        vs[c, 0:halo_w] = vp32[:, cs]
        vs[c, halo_w:] = vc32[:, cs]

    def gather(slabs, rp, rows):
        return jnp.concatenate(
            [slabs[c, pl.ds(rp, rows, stride=half), :] for c in range(n_slab)], axis=1)

    def residue(w, par):
        bits = (w << 16) if par == 0 else (w & jnp.uint32(HI_MASK))
        return lax.bitcast_convert_type(bits, F32).astype(BF16)

    for rp in range(half):
        wq = gather(qs, rp, mb)
        wk = gather(ks, rp, mb + ATTN_BLOCK)
        wv = gather(vs, rp, mb + ATTN_BLOCK)
        outs = []
        for par in range(2):
            o, lse = _attn_rows(residue(wq, par), residue(wk, par), residue(wv, par), n)
            outs.append(o)
            ls[pl.ds(2 * rp + par, mb, stride=dil), :] = lse
        w_out = _pack_bf16_pair(outs[0], outs[1])
        for c in range(n_slab):
            os_[c, pl.ds(rp, mb, stride=half), :] = w_out[:, c * 128:(c + 1) * 128]
    o32 = jnp.concatenate([os_[c] for c in range(n_slab)], axis=1)
    o_ref[...] = pltpu.bitcast(o32, BF16)
    lse_ref[...] = ls[...]


def _dilated_attention(proj3, group):
    _, dil = ATTN_PATTERNS[group]
    s = proj3.shape[1]
    cbs = (CB_Q + group, CB_K + group, CB_V + group)
    if dil == 1:
        tile, halo = 1024, ATTN_BLOCK
        body, scratch = _attn_kernel, []
    else:
        tile, halo = ATTN_TILE, ATTN_BLOCK * dil
        body = functools.partial(_attn_dilated_kernel, dil=dil)
        scratch = [pltpu.VMEM((CB // 128, tile // 2, 128), U32),
                   pltpu.VMEM((CB // 128, (tile + halo) // 2, 128), U32),
                   pltpu.VMEM((CB // 128, (tile + halo) // 2, 128), U32),
                   pltpu.VMEM((CB // 128, tile // 2, 128), U32),
                   pltpu.VMEM((tile, LSE_W), F32)]
    hb = tile // halo
    cur = lambda cb: pl.BlockSpec((None, tile, CB), lambda n: (cb, n, 0))
    prev = lambda cb: pl.BlockSpec((None, halo, CB), lambda n: (cb, jnp.maximum(n * hb - 1, 0), 0))
    return pl.pallas_call(
        body,
        grid=(s // tile,),
        in_specs=[cur(cbs[0]), cur(cbs[1]), prev(cbs[1]), cur(cbs[2]), prev(cbs[2])],
        out_specs=[pl.BlockSpec((tile, CB), lambda n: (n, 0)),
                   pl.BlockSpec((tile, LSE_W), lambda n: (n, 0))],
        out_shape=[jax.ShapeDtypeStruct((s, CB), BF16),
                   jax.ShapeDtypeStruct((s, LSE_W), F32)],
        scratch_shapes=scratch,
        compiler_params=_params(1),
        name=f"dilated_attention_g{group}",
    )(proj3, proj3, proj3, proj3, proj3)


MIX_TM = 256


def _mixer_out_kernel(x_ref, o0_ref, o1_ref, o2_ref, l0_ref, l1_ref, l2_ref, u_ref,
                      gc0, gc1, gc2, gc3, ga0, ga1, ga2, ga3,
                      wa_ref, wc_ref, bc_ref, wo_ref, g2_ref, wrh_ref, wrl_ref, br_ref,
                      x1_ref, hp_ref, logit_ref):
    l_refs = (l0_ref, l1_ref, l2_ref)
    o_refs = (o0_ref, o1_ref, o2_ref)
    parts = []
    for hh in range(HEADS_PER_GROUP):
        cs = slice(hh * HEAD_DIM, (hh + 1) * HEAD_DIM)
        lse = [r[:, hh * LSE_REP:hh * LSE_REP + 1] for r in l_refs]
        m = jnp.maximum(jnp.maximum(lse[0], lse[1]), lse[2])
        e = [jnp.exp(v - m) for v in lse]
        den = e[0] + e[1] + e[2]
        acc = (e[0] / den) * o_refs[0][:, cs].astype(F32)
        for g in range(1, N_GROUPS):
            acc = acc + (e[g] / den) * o_refs[g][:, cs].astype(F32)
        parts.append(acc)
    o = jnp.concatenate(parts, axis=1).astype(BF16)
    ya = jnp.dot(o, wa_ref[...], preferred_element_type=F32)
    yc = jnp.dot(u_ref[...], wc_ref[...], preferred_element_type=F32) + bc_ref[...]
    gcs = (gc0, gc1, gc2, gc3)
    gas = (ga0, ga1, ga2, ga3)
    merged = []
    for c in range(D_MODEL // CB):
        cs = slice(c * CB, (c + 1) * CB)
        merged.append((_sigmoid(gcs[c][...].astype(F32)) * yc[:, cs]
                       + _sigmoid(gas[c][...].astype(F32)) * ya[:, cs]).astype(BF16))
    merged = jnp.concatenate(merged, axis=1)
    x1 = x_ref[...] + jnp.dot(merged, wo_ref[...], preferred_element_type=F32)
    x1_ref[...] = x1
    ms = jnp.mean(x1 * x1, axis=-1, keepdims=True)
    h2 = (x1 * lax.rsqrt(ms + NORM_EPS)) * g2_ref[...]
    h2_hi = h2.astype(BF16)
    hp_ref[...] = _pack_bf16_pair(h2[:, 0:PACK_W], h2[:, PACK_W:D_MODEL])
    h2_lo = (h2 - h2_hi.astype(F32)).astype(BF16)
    hi_pass = jnp.dot(h2_hi, wrl_ref[...], preferred_element_type=F32)
    logits = (hi_pass[:, 0:N_EXPERTS] + hi_pass[:, N_EXPERTS:2 * N_EXPERTS]
              + jnp.dot(h2_lo, wrh_ref[...], preferred_element_type=F32)) + br_ref[...]
    logit_ref[...] = logits


def _mixer_output(x2, outs, lses, u_ln, proj3, wa, wc, bc, wo, g2, wr_hi, wr_lo, br):
    s = x2.shape[0]
    row = lambda w: pl.BlockSpec((MIX_TM, w), lambda i: (i, 0))
    gate = lambda cb: pl.BlockSpec((None, MIX_TM, CB), lambda i: (cb, i, 0))
    full = lambda a: pl.BlockSpec(a.shape, lambda i: (0,) * a.ndim)
    in_specs = ([row(D_MODEL)] + [row(CB)] * 3 + [row(LSE_W)] * 3 + [row(D_CONV)]
                + [gate(CB_GATE_C + c) for c in range(4)]
                + [gate(CB_GATE_A + c) for c in range(4)]
                + [full(a) for a in (wa, wc, bc, wo, g2, wr_hi, wr_lo, br)])
    return pl.pallas_call(
        _mixer_out_kernel,
        grid=(s // MIX_TM,),
        in_specs=in_specs,
        out_specs=[row(D_MODEL), row(PACK_W), row(N_EXPERTS)],
        out_shape=[jax.ShapeDtypeStruct((s, D_MODEL), F32),
                   jax.ShapeDtypeStruct((s, PACK_W), U32),
                   jax.ShapeDtypeStruct((s, N_EXPERTS), F32)],
        compiler_params=_params(1),
        name="mixer_output",
    )(x2, *outs, *lses, u_ln, *([proj3] * 8), wa, wc, bc, wo, g2, wr_hi, wr_lo, br)


ROUTE_TM = 1024


def _route_kernel(logit_ref, idx_ref, gate_ref, pos_ref, cnt_ref, carry_ref):
    i = pl.program_id(0)
    tm = ROUTE_TM

    @pl.when(i == 0)
    def _():
        carry_ref[...] = jnp.zeros_like(carry_ref)

    logits = logit_ref[...]
    e_iota = lax.broadcasted_iota(I32, (tm, N_EXPERTS), 1).astype(F32)
    k_lane = lax.broadcasted_iota(I32, (tm, TOP_K), 1)
    vals = logits
    sels = []
    idx_t = jnp.zeros((tm, TOP_K), F32)
    val_t = jnp.zeros((tm, TOP_K), F32)
    for k in range(TOP_K):
        mk = jnp.max(vals, axis=-1, keepdims=True)
        ik = jnp.min(jnp.where(vals == mk, e_iota, float(N_EXPERTS)), axis=-1, keepdims=True)
        sel = e_iota == ik
        sels.append(sel)
        vals = jnp.where(sel, -jnp.inf, vals)
        idx_t = jnp.where(k_lane == k, ik, idx_t)
        val_t = jnp.where(k_lane == k, mk, val_t)
    ex = jnp.exp(val_t - val_t[:, 0:1])
    gate_ref[...] = ex / jnp.sum(ex, axis=-1, keepdims=True)
    idx_ref[...] = idx_t.astype(I32)

    cnt = jnp.zeros((tm, N_EXPERTS), F32)
    for sel in sels:
        cnt = cnt + jnp.where(sel, 1.0, 0.0)
    ri = lax.broadcasted_iota(I32, (tm, tm), 0)
    ci = lax.broadcasted_iota(I32, (tm, tm), 1)
    lower = jnp.where(ci < ri, 1.0, 0.0).astype(BF16)
    prefix = jnp.dot(lower, cnt.astype(BF16), preferred_element_type=F32) + carry_ref[...]
    pos_t = jnp.zeros((tm, TOP_K), F32)
    for k, sel in enumerate(sels):
        pk = jnp.sum(jnp.where(sel, prefix, 0.0), axis=-1, keepdims=True)
        pos_t = jnp.where(k_lane == k, pk, pos_t)
    pos_ref[...] = pos_t.astype(I32)
    carry_ref[...] = carry_ref[...] + jnp.sum(cnt, axis=0, keepdims=True)
    cnt_ref[...] = carry_ref[...]


def _route(logits):
    s = logits.shape[0]
    row = lambda w: pl.BlockSpec((ROUTE_TM, w), lambda i: (i, 0))
    return pl.pallas_call(
        _route_kernel,
        grid=(s // ROUTE_TM,),
        in_specs=[row(N_EXPERTS)],
        out_specs=[row(TOP_K), row(TOP_K), row(TOP_K),
                   pl.BlockSpec((1, N_EXPERTS), lambda i: (0, 0))],
        out_shape=[jax.ShapeDtypeStruct((s, TOP_K), I32),
                   jax.ShapeDtypeStruct((s, TOP_K), F32),
                   jax.ShapeDtypeStruct((s, TOP_K), I32),
                   jax.ShapeDtypeStruct((1, N_EXPERTS), F32)],
        scratch_shapes=[pltpu.VMEM((1, N_EXPERTS), F32)],
        compiler_params=_params(1),
        name="moe_route",
    )(logits)


ROW_BLK = 128
DISP_TM = 256


def _dispatch_kernel(fill_ref, dest_ref, h_ref, xs_ref, zbuf, sem):
    i = pl.program_id(0)
    n_fill = fill_ref.shape[0]

    @pl.when(i == 0)
    def _():
        zbuf[...] = jnp.zeros_like(zbuf)

        def fill_copy(j):
            return pltpu.make_async_copy(
                zbuf, xs_ref.at[pl.ds(fill_ref[j] * ROW_BLK, ROW_BLK)], sem.at[0])

        def start(j, c):
            @pl.when(fill_ref[j] >= 0)
            def _():
                fill_copy(j).start()
            return c

        def wait(j, c):
            @pl.when(fill_ref[j] >= 0)
            def _():
                fill_copy(j).wait()
            return c

        lax.fori_loop(0, n_fill, start, 0)
        lax.fori_loop(0, n_fill, wait, 0)

    def rows8(tt, c):
        for si in range(8):
            for k in range(TOP_K):
                d = dest_ref[0, (tt * 8 + si) * TOP_K + k]
                pltpu.make_async_copy(h_ref.at[tt, pl.ds(si, 1)], xs_ref.at[pl.ds(d, 1)],
                                      sem.at[0]).start(priority=k % 2)
        return c

    lax.fori_loop(0, DISP_TM // 8, rows8, 0)
    for _ in range(TOP_K):
        pltpu.make_async_copy(h_ref, h_ref, sem.at[0]).wait()


def _dispatch(hp, dest, fill_blocks, n_rows):
    s = hp.shape[0]
    dest3 = dest.reshape(s // DISP_TM, 1, DISP_TM * TOP_K)
    grid_spec = pltpu.PrefetchScalarGridSpec(
        num_scalar_prefetch=1,
        grid=(s // DISP_TM,),
        in_specs=[
            pl.BlockSpec((None, 1, DISP_TM * TOP_K), lambda i, fr: (i, 0, 0),
                         memory_space=pltpu.SMEM),
            pl.BlockSpec((DISP_TM // 8, 8, PACK_W), lambda i, fr: (i, 0, 0)),
        ],
        out_specs=pl.BlockSpec(memory_space=pl.ANY),
        scratch_shapes=[pltpu.VMEM((ROW_BLK, PACK_W), U32), pltpu.SemaphoreType.DMA((1,))],
    )
    return pl.pallas_call(
        _dispatch_kernel,
        grid_spec=grid_spec,
        out_shape=jax.ShapeDtypeStruct((n_rows, PACK_W), U32),
        compiler_params=_params(1),
        name="moe_dispatch",
    )(fill_blocks, dest3, hp.reshape(s // 8, 8, PACK_W))


UNIT_BLOCKS = 10
UNIT_ROWS = ROW_BLK * UNIT_BLOCKS
GROUP_SIZES = (8, 4, 2, 1)
EXP_TF = 512
EXP_NF = D_EXPERT // EXP_TF


def _expert_kernel(ue_ref, ub0_ref, unb_ref, nu_ref,
                   xs_ref, win_ref, wout_ref, bin_ref, bo_ref,
                   ys_ref,
                   xbuf, xb16, acc, wg_buf, wu_buf, wo_buf, sem_x, sem_y, sem_w):
    u = pl.program_id(0)
    n_units = nu_ref[0]

    def x_copy(unit, b):
        src = xs_ref.at[pl.ds((ub0_ref[unit] + b) * ROW_BLK, ROW_BLK)]
        dst = xbuf.at[pl.ds(b * ROW_BLK, ROW_BLK)]
        return pltpu.make_async_copy(src, dst, sem_x.at[0])

    def y_copy(unit, b):
        src = acc.at[pl.ds(b * ROW_BLK, ROW_BLK)]
        dst = ys_ref.at[pl.ds((ub0_ref[unit] + b) * ROW_BLK, ROW_BLK)]
        return pltpu.make_async_copy(src, dst, sem_y.at[0])

    def w_copies(unit, f, slot):
        e = ue_ref[unit]
        c0 = pl.multiple_of(f * EXP_TF, EXP_TF)
        return (
            pltpu.make_async_copy(win_ref.at[e, :, pl.ds(c0, EXP_TF)], wg_buf.at[slot],
                                  sem_w.at[slot]),
            pltpu.make_async_copy(win_ref.at[e, :, pl.ds(D_EXPERT + c0, EXP_TF)], wu_buf.at[slot],
                                  sem_w.at[slot]),
            pltpu.make_async_copy(wout_ref.at[e, pl.ds(c0, EXP_TF), :], wo_buf.at[slot],
                                  sem_w.at[slot]),
        )

    def start_w(unit, f, slot):
        for cp in w_copies(unit, f, slot):
            cp.start()

    def wait_w(unit, f, slot):
        for cp in w_copies(unit, f, slot):
            cp.wait()

    def start_x(unit):
        lax.fori_loop(0, unb_ref[unit], lambda b, c: (x_copy(unit, b).start(), c)[1], 0)

    def wait_x(unit):
        lax.fori_loop(0, unb_ref[unit], lambda b, c: (x_copy(unit, b).wait(), c)[1], 0)

    def wait_y(unit):
        lax.fori_loop(0, unb_ref[unit], lambda b, c: (y_copy(unit, b).wait(), c)[1], 0)

    @pl.when(u < n_units)
    def _():
        nblk = unb_ref[u]

        @pl.when(u == 0)
        def _():
            start_x(u)
            start_w(u, 0, 0)

        wait_x(u)

        @pl.when(u > 0)
        def _():
            wait_y(u - 1)

        def init(b, c):
            r0 = pl.multiple_of(b * ROW_BLK, ROW_BLK)
            lo, hi = _unpack_bf16_pair(xbuf[pl.ds(r0, ROW_BLK), :])
            xb16[pl.ds(r0, ROW_BLK), 0:PACK_W] = lo
            xb16[pl.ds(r0, ROW_BLK), PACK_W:D_MODEL] = hi
            acc[pl.ds(r0, ROW_BLK), :] = jnp.broadcast_to(bo_ref[...], (ROW_BLK, D_MODEL))
            return c

        lax.fori_loop(0, nblk, init, 0)

        @pl.when(u + 1 < n_units)
        def _():
            start_x(u + 1)

        def group(b0, n_blk, f, ws):
            rows = n_blk * ROW_BLK
            r0 = pl.multiple_of(b0 * ROW_BLK, ROW_BLK)
            c0 = pl.multiple_of(f * EXP_TF, EXP_TF)
            x = xb16[pl.ds(r0, rows), :]
            wgu = jnp.concatenate([wg_buf[ws].astype(BF16), wu_buf[ws].astype(BF16)], axis=1)
            gu = jnp.dot(x, wgu, preferred_element_type=F32)
            bg = bin_ref[:, pl.ds(c0, EXP_TF)]
            bu = bin_ref[:, pl.ds(D_EXPERT + c0, EXP_TF)]
            g = jnp.minimum(gu[:, 0:EXP_TF] + bg, SWIGLU_LIMIT)
            up = jnp.clip(gu[:, EXP_TF:2 * EXP_TF] + bu, -SWIGLU_LIMIT, SWIGLU_LIMIT)
            act = (up + 1.0) * (g * _sigmoid(SWIGLU_ALPHA * g))
            acc[pl.ds(r0, rows), :] += jnp.dot(
                act.astype(BF16), wo_buf[ws].astype(BF16), preferred_element_type=F32)

            @pl.when(f == EXP_NF - 1)
            def _():
                for j in range(n_blk):
                    y_copy(u, b0 + j).start()

        def f_tile(f, ws):
            big = GROUP_SIZES[0]
            n_big = nblk // big
            lax.fori_loop(0, n_big, lambda q, c: (group(big * q, big, f, ws), c)[1], 0)
            for size in GROUP_SIZES[1:]:
                done = (nblk // (2 * size)) * (2 * size)

                @pl.when(lax.rem(nblk, 2 * size) >= size)
                def _():
                    group(done, size, f, ws)

        def f_step(f, c):
            ws = lax.rem(f, 2)

            @pl.when(f + 1 < EXP_NF)
            def _():
                start_w(u, f + 1, 1 - ws)

            @pl.when((f + 1 == EXP_NF) & (u + 1 < n_units))
            def _():
                start_w(u + 1, 0, 1 - ws)

            wait_w(u, f, ws)
            f_tile(f, ws)
            return c

        lax.fori_loop(0, EXP_NF, f_step, 0)

        @pl.when(u == n_units - 1)
        def _():
            wait_y(u)

    @pl.when(u == pl.num_programs(0) - 1)
    def _():
        n_blocks = ys_ref.shape[0] // ROW_BLK
        acc[0:ROW_BLK, :] = jnp.zeros((ROW_BLK, D_MODEL), F32)

        def tail_copy(b):
            return pltpu.make_async_copy(acc.at[pl.ds(0, ROW_BLK)],
                                         ys_ref.at[pl.ds(b * ROW_BLK, ROW_BLK)], sem_y.at[0])

        lax.fori_loop(nu_ref[1], n_blocks, lambda b, c: (tail_copy(b).start(), c)[1], 0)
        lax.fori_loop(nu_ref[1], n_blocks, lambda b, c: (tail_copy(b).wait(), c)[1], 0)


def _expert_ffn(xs, unit_e, unit_blk0, unit_nblk, unit_counts, w_in, b_in, w_out, b_out):
    n_rows = xs.shape[0]
    n_units_max = unit_e.shape[0]

    def e_eff(u, ue, nu):
        return ue[jnp.minimum(u, jnp.maximum(nu[0] - 1, 0))]

    any_spec = pl.BlockSpec(memory_space=pl.ANY)
    in_specs = [
        any_spec, any_spec, any_spec,
        pl.BlockSpec((None, 1, 2 * D_EXPERT), lambda u, ue, ub, un, nu: (e_eff(u, ue, nu), 0, 0)),
        pl.BlockSpec((None, 1, D_MODEL), lambda u, ue, ub, un, nu: (e_eff(u, ue, nu), 0, 0)),
    ]
    grid_spec = pltpu.PrefetchScalarGridSpec(
        num_scalar_prefetch=4,
        grid=(n_units_max,),
        in_specs=in_specs,
        out_specs=any_spec,
        scratch_shapes=[
            pltpu.VMEM((UNIT_ROWS, PACK_W), U32),
            pltpu.VMEM((UNIT_ROWS, D_MODEL), BF16),
            pltpu.VMEM((UNIT_ROWS, D_MODEL), F32),
            pltpu.VMEM((2, D_MODEL, EXP_TF), F32),
            pltpu.VMEM((2, D_MODEL, EXP_TF), F32),
            pltpu.VMEM((2, EXP_TF, D_MODEL), F32),
            pltpu.SemaphoreType.DMA((1,)),
            pltpu.SemaphoreType.DMA((1,)),
            pltpu.SemaphoreType.DMA((2,)),
        ],
    )
    return pl.pallas_call(
        _expert_kernel,
        grid_spec=grid_spec,
        out_shape=jax.ShapeDtypeStruct((n_rows, D_MODEL), F32),
        compiler_params=_params(1),
        name="expert_ffn",
    )(unit_e, unit_blk0, unit_nblk, unit_counts, xs,
      w_in, w_out, b_in[:, None, :], b_out[:, None, :])


COMB_TM = 256


def _combine_kernel(dcur_ref, dnxt_ref, x1_ref, gate_ref, g_ref, ys_ref, o_ref, ybuf, sem):
    i = pl.program_id(0)
    n = pl.num_programs(0)
    slot = lax.rem(i, 2)

    def issue(dref, s):
        def rows8(tt, c):
            for si in range(8):
                for k in range(TOP_K):
                    d = dref[0, (tt * 8 + si) * TOP_K + k]
                    pltpu.make_async_copy(ys_ref.at[pl.ds(d, 1)],
                                          ybuf.at[s, k, tt, pl.ds(si, 1)],
                                          sem.at[s]).start(priority=k % 2)
            return c

        lax.fori_loop(0, COMB_TM // 8, rows8, 0)

    @pl.when(i == 0)
    def _():
        issue(dcur_ref, slot)

    @pl.when(i + 1 < n)
    def _():
        issue(dnxt_ref, 1 - slot)

    for k in range(TOP_K):
        pltpu.make_async_copy(ybuf.at[slot, k], ybuf.at[slot, k], sem.at[slot]).wait()

    gate = gate_ref[...]
    y = x1_ref[...]
    for k in range(TOP_K):
        y = y + gate[:, k:k + 1] * ybuf[slot, k].reshape(COMB_TM, D_MODEL)
    ms = jnp.mean(y * y, axis=-1, keepdims=True)
    o_ref[...] = (y * lax.rsqrt(ms + NORM_EPS)) * g_ref[...]


def _combine(dest, x1, gate, norm_g, ys):
    s = x1.shape[0]
    nt = s // COMB_TM
    dest3 = dest.reshape(nt, 1, COMB_TM * TOP_K)
    smem = lambda imap: pl.BlockSpec((None, 1, COMB_TM * TOP_K), imap, memory_space=pltpu.SMEM)
    row = lambda w: pl.BlockSpec((COMB_TM, w), lambda i: (i, 0))
    return pl.pallas_call(
        _combine_kernel,
        grid=(nt,),
        in_specs=[smem(lambda i: (i, 0, 0)),
                  smem(lambda i: (jnp.minimum(i + 1, nt - 1), 0, 0)),
                  row(D_MODEL), row(TOP_K), pl.BlockSpec((1, D_MODEL), lambda i: (0, 0)),
                  pl.BlockSpec(memory_space=pl.ANY)],
        out_specs=row(D_MODEL),
        out_shape=jax.ShapeDtypeStruct((s, D_MODEL), F32),
        scratch_shapes=[pltpu.VMEM((2, TOP_K, COMB_TM // 8, 8, D_MODEL), F32),
                        pltpu.SemaphoreType.DMA((2,))],
        compiler_params=_params(1),
        name="moe_combine",
    )(dest3, dest3, x1, gate, norm_g, ys)


def _route_tables(idx, pos, counts, n_tok):
    n_assign = n_tok * TOP_K
    counts = counts.reshape(N_EXPERTS).astype(I32)
    nblk_e = (counts + ROW_BLK - 1) // ROW_BLK
    pad_start = (jnp.cumsum(nblk_e) - nblk_e) * ROW_BLK
    onehot = idx[..., None] == jnp.arange(N_EXPERTS, dtype=I32)
    dest = pos + jnp.sum(jnp.where(onehot, pad_start, 0), axis=-1)

    n_blocks = -(-(n_assign + N_EXPERTS * (ROW_BLK - 1)) // ROW_BLK)
    used = jnp.sum(nblk_e)
    blk_ids = jnp.arange(n_blocks, dtype=I32)
    last_blk = pad_start // ROW_BLK + nblk_e - 1
    fill_blocks = jnp.concatenate([jnp.where(nblk_e > 0, last_blk, -1),
                                   jnp.where(blk_ids >= used, blk_ids, -1)]).astype(I32)

    n_units_max = (n_blocks + (UNIT_BLOCKS - 1) * N_EXPERTS) // UNIT_BLOCKS
    chunks_e = (nblk_e + UNIT_BLOCKS - 1) // UNIT_BLOCKS
    chunk_end = jnp.cumsum(chunks_e)
    chunk_start = chunk_end - chunks_e
    n_units = chunk_end[-1]
    uid = jnp.arange(n_units_max, dtype=I32)
    unit_e = jnp.minimum(jnp.searchsorted(chunk_end, uid, side="right"), N_EXPERTS - 1).astype(I32)
    c_in_e = uid - chunk_start[unit_e]
    unit_blk0 = (pad_start[unit_e] // ROW_BLK + c_in_e * UNIT_BLOCKS).astype(I32)
    unit_nblk = jnp.clip(nblk_e[unit_e] - c_in_e * UNIT_BLOCKS, 0, UNIT_BLOCKS).astype(I32)
    active = uid < n_units
    unit_blk0 = jnp.where(active, unit_blk0, 0)
    unit_nblk = jnp.where(active, unit_nblk, 0)
    unit_counts = jnp.stack([n_units, used]).astype(I32)
    return dest.astype(I32), fill_blocks, n_blocks * ROW_BLK, unit_e, unit_blk0, unit_nblk, unit_counts


def kernel(x, norm1_g, w_in, b_gate, conv_w, conv_b, conv_ln_g, conv_ln_b, w_conv_out, b_conv_out,
           w_attn_out, w_out, norm2_g, w_router, b_router, w_exp_in, b_exp_in, w_exp_out, b_exp_out,
           norm_f_g):
    b_sz, s_len, d = x.shape
    n_tok = b_sz * s_len
    x2 = x.reshape(n_tok, d)
    l = 0
    b_gate_ext = jnp.concatenate([jnp.zeros((GATE_COL0,), F32), b_gate[l]])[None, :]
    proj3 = _in_projection(x2, norm1_g[l][None, :], w_in[l], b_gate_ext)
    u_ln = _conv_branch(proj3, conv_w[l], conv_b[l][None, :], conv_ln_g[l][None, :],
                        conv_ln_b[l][None, :])
    outs, lses = [], []
    for g in range(N_GROUPS):
        o_g, lse_g = _dilated_attention(proj3, g)
        outs.append(o_g)
        lses.append(lse_g)
    wr = w_router[l]
    wr_hi = wr.astype(BF16)
    wr_lo = jnp.concatenate([wr_hi, (wr - wr_hi.astype(F32)).astype(BF16)], axis=1)
    x1, hp, logits = _mixer_output(
        x2, outs, lses, u_ln, proj3,
        w_attn_out[l].astype(BF16), w_conv_out[l].astype(BF16), b_conv_out[l][None, :],
        w_out[l].astype(BF16), norm2_g[l][None, :], wr_hi, wr_lo, b_router[l][None, :])

    idx, gate, pos, counts = _route(logits)
    dest, fill_blocks, n_rows, unit_e, unit_blk0, unit_nblk, unit_counts = _route_tables(
        idx, pos, counts, n_tok)
    xs = _dispatch(hp, dest, fill_blocks, n_rows)
    ys = _expert_ffn(xs, unit_e, unit_blk0, unit_nblk, unit_counts,
                     w_exp_in[l], b_exp_in[l], w_exp_out[l], b_exp_out[l])
    out = _combine(dest, x1, gate, norm_f_g[None, :], ys)
    return out.reshape(b_sz, s_len, d)
```

```python
import functools

import jax
import jax.numpy as jnp
from jax import lax
from jax.experimental import pallas as pl
from jax.experimental.pallas import tpu as pltpu

D_MODEL = 2048
SEQ = 8192
D_CONV = D_MODEL // 2
CONV_WIDTH = 31
HEAD_DIM = 128
HEADS_PER_GROUP = 4
ATTN_PATTERNS = ((128, 1), (512, 4), (2048, 16))
N_GROUPS = len(ATTN_PATTERNS)
ATTN_WIDTH = N_GROUPS * HEADS_PER_GROUP * HEAD_DIM
ATTN_OUT_WIDTH = HEADS_PER_GROUP * HEAD_DIM
ATTN_BLOCK = 128
D_IN = 2 * D_CONV + 3 * ATTN_WIDTH + 2 * D_MODEL
N_EXPERTS = 32
TOP_K = 4
D_EXPERT = D_MODEL
SWIGLU_LIMIT = 7.0
SWIGLU_ALPHA = 1.702
NORM_EPS = 1e-5

CB = 512
N_CB = D_IN // CB
CB_CONV_A, CB_CONV_G = 0, 2
CB_Q, CB_K, CB_V = 4, 7, 10
CB_GATE_C, CB_GATE_A = 13, 17
GATE_COL0 = CB_GATE_C * CB

V7X_VMEM_BYTES = 64 * 1024 * 1024
VMEM_LIMIT = V7X_VMEM_BYTES - 8 * 1024 * 1024

F32 = jnp.float32
BF16 = jnp.bfloat16
U32 = jnp.uint32
I32 = jnp.int32

PACK_W = D_MODEL // 2
HI_MASK = 0xFFFF0000


def _sigmoid(z):
    return 0.5 * jnp.tanh(0.5 * z) + 0.5


def _pack_bf16_pair(lo, hi):
    lo_bits = lax.bitcast_convert_type(lo.astype(BF16).astype(F32), U32) >> 16
    hi_bits = lax.bitcast_convert_type(hi.astype(BF16).astype(F32), U32) & jnp.uint32(HI_MASK)
    return hi_bits | lo_bits


def _unpack_bf16_pair(w):
    lo = lax.bitcast_convert_type(w << 16, F32).astype(BF16)
    hi = lax.bitcast_convert_type(w & jnp.uint32(HI_MASK), F32).astype(BF16)
    return lo, hi


def _params(n_axes, vmem=VMEM_LIMIT):
    return pltpu.CompilerParams(
        dimension_semantics=("arbitrary",) * n_axes, vmem_limit_bytes=vmem)


NORM_TM = 512
IN_TM = 1024
IN_TN = 1536


def _rmsnorm_kernel(x_ref, g_ref, o_ref):
    x = x_ref[...]
    ms = jnp.mean(x * x, axis=-1, keepdims=True)
    o_ref[...] = ((x * lax.rsqrt(ms + NORM_EPS)) * g_ref[...]).astype(BF16)


def _rmsnorm_bf16(x2, norm_g):
    s = x2.shape[0]
    row = pl.BlockSpec((NORM_TM, D_MODEL), lambda i: (i, 0))
    return pl.pallas_call(
        _rmsnorm_kernel,
        grid=(s // NORM_TM,),
        in_specs=[row, pl.BlockSpec((1, D_MODEL), lambda i: (0, 0))],
        out_specs=row,
        out_shape=jax.ShapeDtypeStruct((s, D_MODEL), BF16),
        compiler_params=_params(1),
        name="rmsnorm_in",
    )(x2, norm_g)


def _inproj_kernel(h_ref, w_ref, bg_ref, o_ref, wbf_ref):
    j = pl.program_id(0)
    i = pl.program_id(1)

    @pl.when(i == 0)
    def _():
        wbf_ref[...] = w_ref[...].astype(BF16)

    acc = jnp.dot(h_ref[...], wbf_ref[...], preferred_element_type=F32) + bg_ref[...]
    for c in range(IN_TN // CB):
        o_ref[c] = acc[:, c * CB:(c + 1) * CB].astype(BF16)


def _in_projection(x2, norm_g, w_in, b_gate_ext):
    s = x2.shape[0]
    h = _rmsnorm_bf16(x2, norm_g)
    grid = (D_IN // IN_TN, s // IN_TM)
    return pl.pallas_call(
        _inproj_kernel,
        grid=grid,
        in_specs=[
            pl.BlockSpec((IN_TM, D_MODEL), lambda j, i: (i, 0)),
            pl.BlockSpec((D_MODEL, IN_TN), lambda j, i: (0, j)),
            pl.BlockSpec((1, IN_TN), lambda j, i: (0, j)),
        ],
        out_specs=pl.BlockSpec((IN_TN // CB, IN_TM, CB), lambda j, i: (j, i, 0)),
        out_shape=jax.ShapeDtypeStruct((N_CB, s, CB), BF16),
        scratch_shapes=[pltpu.VMEM((D_MODEL, IN_TN), BF16)],
        compiler_params=_params(2),
        name="in_projection",
    )(h, w_in, b_gate_ext)


CONV_TS = 512
CONV_HALO = 32
CONV_RC = 32
CONV_N = CONV_TS + CONV_HALO


def _conv_kernel(ac_ref, gc_ref, ah_ref, gh_ref, w_ref, cb_ref, lg_ref, lb_ref, o_ref, r_ref,
                 c_ref):
    i = pl.program_id(0)
    has_prev = i > 0
    half = D_CONV // 2
    for c in range(2):
        cs = slice(c * half, (c + 1) * half)
        uh = ah_ref[c].astype(F32) * _sigmoid(gh_ref[c].astype(F32))
        r_ref[0, 0:CONV_HALO, cs] = jnp.where(has_prev, uh, 0.0)
        r_ref[0, CONV_HALO:CONV_N, cs] = ac_ref[c].astype(F32) * _sigmoid(gc_ref[c].astype(F32))
    u_ext = r_ref[0]
    for b in range(1, 8):
        r_ref[b] = pltpu.roll(u_ext, CONV_N - b, axis=0)

    def chunk(ci, carry):
        r0 = pl.multiple_of(ci * CONV_RC, CONV_RC)
        acc = jnp.broadcast_to(cb_ref[...], (CONV_RC, D_CONV))
        for k in range(CONV_WIDTH):
            kp = k + (CONV_HALO - (CONV_WIDTH - 1))
            a, b = kp // 8, kp % 8
            acc = acc + w_ref[k:k + 1, :] * r_ref[b, pl.ds(r0 + 8 * a, CONV_RC), :]
        c_ref[pl.ds(r0, CONV_RC), :] = acc
        return carry

    lax.fori_loop(0, CONV_TS // CONV_RC, chunk, 0)
    conv = c_ref[...]
    mu = jnp.mean(conv, axis=-1, keepdims=True)
    d = conv - mu
    var = jnp.mean(d * d, axis=-1, keepdims=True)
    y = (d * lax.rsqrt(var + NORM_EPS)) * lg_ref[...] + lb_ref[...]
    o_ref[...] = (y * _sigmoid(y)).astype(BF16)


def _conv_branch(proj3, conv_w, conv_b, ln_g, ln_b):
    s = proj3.shape[1]
    hb = CONV_TS // CONV_HALO
    cur = lambda blk: pl.BlockSpec((2, CONV_TS, CB), lambda i: (blk // 2, i, 0))
    halo = lambda blk: pl.BlockSpec(
        (2, CONV_HALO, CB), lambda i: (blk // 2, jnp.maximum(i * hb - 1, 0), 0))
    vec = pl.BlockSpec((1, D_CONV), lambda i: (0, 0))
    return pl.pallas_call(
        _conv_kernel,
        grid=(s // CONV_TS,),
        in_specs=[cur(CB_CONV_A), cur(CB_CONV_G), halo(CB_CONV_A), halo(CB_CONV_G),
                  pl.BlockSpec((CONV_WIDTH, D_CONV), lambda i: (0, 0)), vec, vec, vec],
        out_specs=pl.BlockSpec((CONV_TS, D_CONV), lambda i: (i, 0)),
        out_shape=jax.ShapeDtypeStruct((s, D_CONV), BF16),
        scratch_shapes=[pltpu.VMEM((8, CONV_N, D_CONV), F32), pltpu.VMEM((CONV_TS, D_CONV), F32)],
        compiler_params=_params(1),
        name="conv_branch",
    )(proj3, proj3, proj3, proj3, conv_w, conv_b, ln_g, ln_b)


LSE_W = 128
LSE_REP = LSE_W // HEADS_PER_GROUP


def _attn_rows(q, k, v, n):
    blk = ATTN_BLOCK
    scale = HEAD_DIM ** -0.5
    qi = lax.broadcasted_iota(I32, (blk, 2 * blk), 0)
    ki = lax.broadcasted_iota(I32, (blk, 2 * blk), 1)
    band = (ki >= qi) & (ki <= qi + blk)
    lane_head = lax.broadcasted_iota(I32, (blk, LSE_W), 1) // LSE_REP
    o_rows, lse_rows = [], []
    for j in range(q.shape[0] // blk):
        mask = band
        if j == 0:
            mask = band & ((ki >= blk) | (n > 0))
        o_heads = []
        lse_tile = jnp.zeros((blk, LSE_W), F32)
        for hh in range(HEADS_PER_GROUP):
            cs = slice(hh * HEAD_DIM, (hh + 1) * HEAD_DIM)
            qh = q[j * blk:(j + 1) * blk, cs]
            kh = k[j * blk:(j + 2) * blk, cs]
            vh = v[j * blk:(j + 2) * blk, cs]
            s = lax.dot_general(qh, kh, (((1,), (1,)), ((), ())),
                                preferred_element_type=F32) * scale
            s = jnp.where(mask, s, -jnp.inf)
            m = jnp.max(s, axis=-1, keepdims=True)
            p = jnp.exp(s - m)
            l = jnp.sum(p, axis=-1, keepdims=True)
            o_heads.append(jnp.dot(p.astype(BF16), vh, preferred_element_type=F32) / l)
            lse_tile = jnp.where(lane_head == hh, m + jnp.log(l), lse_tile)
        o_rows.append(jnp.concatenate(o_heads, axis=1))
        lse_rows.append(lse_tile)
    return jnp.concatenate(o_rows, axis=0), jnp.concatenate(lse_rows, axis=0)


def _attn_kernel(q_ref, kc_ref, kp_ref, vc_ref, vp_ref, o_ref, lse_ref):
    n = pl.program_id(0)
    k = jnp.concatenate([kp_ref[...], kc_ref[...]], axis=0)
    v = jnp.concatenate([vp_ref[...], vc_ref[...]], axis=0)
    o, lse = _attn_rows(q_ref[...], k, v, n)
    o_ref[...] = o.astype(BF16)
    lse_ref[...] = lse


ATTN_TILE = 2048


def _attn_dilated_kernel(q_ref, kc_ref, kp_ref, vc_ref, vp_ref, o_ref, lse_ref,
                         qs, ks, vs, os_, ls, *, dil):
    n = pl.program_id(0)
    tile = q_ref.shape[0]
    mb = tile // dil
    half = dil // 2
    halo_w = ATTN_BLOCK * half
    n_slab = CB // 128
    q32 = pltpu.bitcast(q_ref[...], U32)
    kp32, kc32 = pltpu.bitcast(kp_ref[...], U32), pltpu.bitcast(kc_ref[...], U32)
    vp32, vc32 = pltpu.bitcast(vp_ref[...], U32), pltpu.bitcast(vc_ref[...], U32)
    for c in range(n_slab):
        cs = slice(c * 128, (c + 1) * 128)
        qs[c] = q32[:, cs]
        ks[c, 0:halo_w] = kp32[:, cs]
        ks[c, halo_w:] = kc32[:, cs]
        vs[c, 0:halo_w] = vp32[:, cs]
        vs[c, halo_w:] = vc32[:, cs]

    def gather(slabs, rp, rows):
        return jnp.concatenate(
            [slabs[c, pl.ds(rp, rows, stride=half), :] for c in range(n_slab)], axis=1)

    def residue(w, par):
        bits = (w << 16) if par == 0 else (w & jnp.uint32(HI_MASK))
        return lax.bitcast_convert_type(bits, F32).astype(BF16)

    for rp in range(half):
        wq = gather(qs, rp, mb)
        wk = gather(ks, rp, mb + ATTN_BLOCK)
        wv = gather(vs, rp, mb + ATTN_BLOCK)
        outs = []
        for par in range(2):
            o, lse = _attn_rows(residue(wq, par), residue(wk, par), residue(wv, par), n)
            outs.append(o)
            ls[pl.ds(2 * rp + par, mb, stride=dil), :] = lse
        w_out = _pack_bf16_pair(outs[0], outs[1])
        for c in range(n_slab):
            os_[c, pl.ds(rp, mb, stride=half), :] = w_out[:, c * 128:(c + 1) * 128]
    o32 = jnp.concatenate([os_[c] for c in range(n_slab)], axis=1)
    o_ref[...] = pltpu.bitcast(o32, BF16)
    lse_ref[...] = ls[...]


def _dilated_attention(proj3, group):
    _, dil = ATTN_PATTERNS[group]
    s = proj3.shape[1]
    cbs = (CB_Q + group, CB_K + group, CB_V + group)
    if dil == 1:
        tile, halo = 1024, ATTN_BLOCK
        body, scratch = _attn_kernel, []
    else:
        tile, halo = ATTN_TILE, ATTN_BLOCK * dil
        body = functools.partial(_attn_dilated_kernel, dil=dil)
        scratch = [pltpu.VMEM((CB // 128, tile // 2, 128), U32),
                   pltpu.VMEM((CB // 128, (tile + halo) // 2, 128), U32),
                   pltpu.VMEM((CB // 128, (tile + halo) // 2, 128), U32),
                   pltpu.VMEM((CB // 128, tile // 2, 128), U32),
                   pltpu.VMEM((tile, LSE_W), F32)]
    hb = tile // halo
    cur = lambda cb: pl.BlockSpec((None, tile, CB), lambda n: (cb, n, 0))
    prev = lambda cb: pl.BlockSpec((None, halo, CB), lambda n: (cb, jnp.maximum(n * hb - 1, 0), 0))
    return pl.pallas_call(
        body,
        grid=(s // tile,),
        in_specs=[cur(cbs[0]), cur(cbs[1]), prev(cbs[1]), cur(cbs[2]), prev(cbs[2])],
        out_specs=[pl.BlockSpec((tile, CB), lambda n: (n, 0)),
                   pl.BlockSpec((tile, LSE_W), lambda n: (n, 0))],
        out_shape=[jax.ShapeDtypeStruct((s, CB), BF16),
                   jax.ShapeDtypeStruct((s, LSE_W), F32)],
        scratch_shapes=scratch,
        compiler_params=_params(1),
        name=f"dilated_attention_g{group}",
    )(proj3, proj3, proj3, proj3, proj3)


MIX_TM = 256


def _mixer_out_kernel(x_ref, o0_ref, o1_ref, o2_ref, l0_ref, l1_ref, l2_ref, u_ref,
                      gc0, gc1, gc2, gc3, ga0, ga1, ga2, ga3,
                      wa_ref, wc_ref, bc_ref, wo_ref, g2_ref, wrh_ref, wrl_ref, br_ref,
                      x1_ref, hp_ref, logit_ref):
    l_refs = (l0_ref, l1_ref, l2_ref)
    o_refs = (o0_ref, o1_ref, o2_ref)
    parts = []
    for hh in range(HEADS_PER_GROUP):
        cs = slice(hh * HEAD_DIM, (hh + 1) * HEAD_DIM)
        lse = [r[:, hh * LSE_REP:hh * LSE_REP + 1] for r in l_refs]
        m = jnp.maximum(jnp.maximum(lse[0], lse[1]), lse[2])
        e = [jnp.exp(v - m) for v in lse]
        den = e[0] + e[1] + e[2]
        acc = (e[0] / den) * o_refs[0][:, cs].astype(F32)
        for g in range(1, N_GROUPS):
            acc = acc + (e[g] / den) * o_refs[g][:, cs].astype(F32)
        parts.append(acc)
    o = jnp.concatenate(parts, axis=1).astype(BF16)
    ya = jnp.dot(o, wa_ref[...], preferred_element_type=F32)
    yc = jnp.dot(u_ref[...], wc_ref[...], preferred_element_type=F32) + bc_ref[...]
    gcs = (gc0, gc1, gc2, gc3)
    gas = (ga0, ga1, ga2, ga3)
    merged = []
    for c in range(D_MODEL // CB):
        cs = slice(c * CB, (c + 1) * CB)
        merged.append((_sigmoid(gcs[c][...].astype(F32)) * yc[:, cs]
                       + _sigmoid(gas[c][...].astype(F32)) * ya[:, cs]).astype(BF16))
    merged = jnp.concatenate(merged, axis=1)
    x1 = x_ref[...] + jnp.dot(merged, wo_ref[...], preferred_element_type=F32)
    x1_ref[...] = x1
    ms = jnp.mean(x1 * x1, axis=-1, keepdims=True)
    h2 = (x1 * lax.rsqrt(ms + NORM_EPS)) * g2_ref[...]
    h2_hi = h2.astype(BF16)
    hp_ref[...] = _pack_bf16_pair(h2[:, 0:PACK_W], h2[:, PACK_W:D_MODEL])
    h2_lo = (h2 - h2_hi.astype(F32)).astype(BF16)
    hi_pass = jnp.dot(h2_hi, wrl_ref[...], preferred_element_type=F32)
    logits = (hi_pass[:, 0:N_EXPERTS] + hi_pass[:, N_EXPERTS:2 * N_EXPERTS]
              + jnp.dot(h2_lo, wrh_ref[...], preferred_element_type=F32)) + br_ref[...]
    logit_ref[...] = logits


def _mixer_output(x2, outs, lses, u_ln, proj3, wa, wc, bc, wo, g2, wr_hi, wr_lo, br):
    s = x2.shape[0]
    row = lambda w: pl.BlockSpec((MIX_TM, w), lambda i: (i, 0))
    gate = lambda cb: pl.BlockSpec((None, MIX_TM, CB), lambda i: (cb, i, 0))
    full = lambda a: pl.BlockSpec(a.shape, lambda i: (0,) * a.ndim)
    in_specs = ([row(D_MODEL)] + [row(CB)] * 3 + [row(LSE_W)] * 3 + [row(D_CONV)]
                + [gate(CB_GATE_C + c) for c in range(4)]
                + [gate(CB_GATE_A + c) for c in range(4)]
                + [full(a) for a in (wa, wc, bc, wo, g2, wr_hi, wr_lo, br)])
    return pl.pallas_call(
        _mixer_out_kernel,
        grid=(s // MIX_TM,),
        in_specs=in_specs,
        out_specs=[row(D_MODEL), row(PACK_W), row(N_EXPERTS)],
        out_shape=[jax.ShapeDtypeStruct((s, D_MODEL), F32),
                   jax.ShapeDtypeStruct((s, PACK_W), U32),
                   jax.ShapeDtypeStruct((s, N_EXPERTS), F32)],
        compiler_params=_params(1),
        name="mixer_output",
    )(x2, *outs, *lses, u_ln, *([proj3] * 8), wa, wc, bc, wo, g2, wr_hi, wr_lo, br)


ROUTE_TM = 1024


def _route_kernel(logit_ref, idx_ref, gate_ref, pos_ref, cnt_ref, carry_ref):
    i = pl.program_id(0)
    tm = ROUTE_TM

    @pl.when(i == 0)
    def _():
        carry_ref[...] = jnp.zeros_like(carry_ref)

    logits = logit_ref[...]
    e_iota = lax.broadcasted_iota(I32, (tm, N_EXPERTS), 1).astype(F32)
    k_lane = lax.broadcasted_iota(I32, (tm, TOP_K), 1)
    vals = logits
    sels = []
    idx_t = jnp.zeros((tm, TOP_K), F32)
    val_t = jnp.zeros((tm, TOP_K), F32)
    for k in range(TOP_K):
        mk = jnp.max(vals, axis=-1, keepdims=True)
        ik = jnp.min(jnp.where(vals == mk, e_iota, float(N_EXPERTS)), axis=-1, keepdims=True)
        sel = e_iota == ik
        sels.append(sel)
        vals = jnp.where(sel, -jnp.inf, vals)
        idx_t = jnp.where(k_lane == k, ik, idx_t)
        val_t = jnp.where(k_lane == k, mk, val_t)
    ex = jnp.exp(val_t - val_t[:, 0:1])
    gate_ref[...] = ex / jnp.sum(ex, axis=-1, keepdims=True)
    idx_ref[...] = idx_t.astype(I32)

    cnt = jnp.zeros((tm, N_EXPERTS), F32)
    for sel in sels:
        cnt = cnt + jnp.where(sel, 1.0, 0.0)
    ri = lax.broadcasted_iota(I32, (tm, tm), 0)
    ci = lax.broadcasted_iota(I32, (tm, tm), 1)
    lower = jnp.where(ci < ri, 1.0, 0.0).astype(BF16)
    prefix = jnp.dot(lower, cnt.astype(BF16), preferred_element_type=F32) + carry_ref[...]
    pos_t = jnp.zeros((tm, TOP_K), F32)
    for k, sel in enumerate(sels):
        pk = jnp.sum(jnp.where(sel, prefix, 0.0), axis=-1, keepdims=True)
        pos_t = jnp.where(k_lane == k, pk, pos_t)
    pos_ref[...] = pos_t.astype(I32)
    carry_ref[...] = carry_ref[...] + jnp.sum(cnt, axis=0, keepdims=True)
    cnt_ref[...] = carry_ref[...]


def _route(logits):
    s = logits.shape[0]
    row = lambda w: pl.BlockSpec((ROUTE_TM, w), lambda i: (i, 0))
    return pl.pallas_call(
        _route_kernel,
        grid=(s // ROUTE_TM,),
        in_specs=[row(N_EXPERTS)],
        out_specs=[row(TOP_K), row(TOP_K), row(TOP_K),
                   pl.BlockSpec((1, N_EXPERTS), lambda i: (0, 0))],
        out_shape=[jax.ShapeDtypeStruct((s, TOP_K), I32),
                   jax.ShapeDtypeStruct((s, TOP_K), F32),
                   jax.ShapeDtypeStruct((s, TOP_K), I32),
                   jax.ShapeDtypeStruct((1, N_EXPERTS), F32)],
        scratch_shapes=[pltpu.VMEM((1, N_EXPERTS), F32)],
        compiler_params=_params(1),
        name="moe_route",
    )(logits)


ROW_BLK = 128
DISP_TM = 1024


def _dispatch_kernel(fill_ref, dest_ref, h_ref, xs_ref, zbuf, sem):
    i = pl.program_id(0)
    n_fill = fill_ref.shape[0]

    @pl.when(i == 0)
    def _():
        zbuf[...] = jnp.zeros_like(zbuf)

        def fill_copy(j):
            return pltpu.make_async_copy(
                zbuf, xs_ref.at[pl.ds(fill_ref[j] * ROW_BLK, ROW_BLK)], sem.at[0])

        def start(j, c):
            @pl.when(fill_ref[j] >= 0)
            def _():
                fill_copy(j).start()
            return c

        def wait(j, c):
            @pl.when(fill_ref[j] >= 0)
            def _():
                fill_copy(j).wait()
            return c

        lax.fori_loop(0, n_fill, start, 0)
        lax.fori_loop(0, n_fill, wait, 0)

    def rows8(tt, c):
        for si in range(8):
            for k in range(TOP_K):
                d = dest_ref[0, (tt * 8 + si) * TOP_K + k]
                pltpu.make_async_copy(h_ref.at[tt, pl.ds(si, 1)], xs_ref.at[pl.ds(d, 1)],
                                      sem.at[0]).start(priority=k % 2)
        return c

    lax.fori_loop(0, DISP_TM // 8, rows8, 0)
    for _ in range(TOP_K):
        pltpu.make_async_copy(h_ref, h_ref, sem.at[0]).wait()


def _dispatch(hp, dest, fill_blocks, n_rows):
    s = hp.shape[0]
    dest3 = dest.reshape(s // DISP_TM, 1, DISP_TM * TOP_K)
    grid_spec = pltpu.PrefetchScalarGridSpec(
        num_scalar_prefetch=1,
        grid=(s // DISP_TM,),
        in_specs=[
            pl.BlockSpec((None, 1, DISP_TM * TOP_K), lambda i, fr: (i, 0, 0),
                         memory_space=pltpu.SMEM),
            pl.BlockSpec((DISP_TM // 8, 8, PACK_W), lambda i, fr: (i, 0, 0)),
        ],
        out_specs=pl.BlockSpec(memory_space=pl.ANY),
        scratch_shapes=[pltpu.VMEM((ROW_BLK, PACK_W), U32), pltpu.SemaphoreType.DMA((1,))],
    )
    return pl.pallas_call(
        _dispatch_kernel,
        grid_spec=grid_spec,
        out_shape=jax.ShapeDtypeStruct((n_rows, PACK_W), U32),
        compiler_params=_params(1),
        name="moe_dispatch",
    )(fill_blocks, dest3, hp.reshape(s // 8, 8, PACK_W))


UNIT_BLOCKS = 10
UNIT_ROWS = ROW_BLK * UNIT_BLOCKS
GROUP_SIZES = (8, 4, 2, 1)
EXP_TF = 512
EXP_NF = D_EXPERT // EXP_TF


def _expert_kernel(ue_ref, ub0_ref, unb_ref, nu_ref,
                   xs_ref, win_ref, wout_ref, bin_ref, bo_ref,
                   ys_ref,
                   xbuf, xb16, acc, wg_buf, wu_buf, wo_buf, sem_x, sem_y, sem_w):
    u = pl.program_id(0)
    n_units = nu_ref[0]

    def x_copy(unit, b):
        src = xs_ref.at[pl.ds((ub0_ref[unit] + b) * ROW_BLK, ROW_BLK)]
        dst = xbuf.at[pl.ds(b * ROW_BLK, ROW_BLK)]
        return pltpu.make_async_copy(src, dst, sem_x.at[0])

    def y_copy(unit, b):
        src = acc.at[pl.ds(b * ROW_BLK, ROW_BLK)]
        dst = ys_ref.at[pl.ds((ub0_ref[unit] + b) * ROW_BLK, ROW_BLK)]
        return pltpu.make_async_copy(src, dst, sem_y.at[0])

    def w_copies(unit, f, slot):
        e = ue_ref[unit]
        c0 = pl.multiple_of(f * EXP_TF, EXP_TF)
        return (
            pltpu.make_async_copy(win_ref.at[e, :, pl.ds(c0, EXP_TF)], wg_buf.at[slot],
                                  sem_w.at[slot]),
            pltpu.make_async_copy(win_ref.at[e, :, pl.ds(D_EXPERT + c0, EXP_TF)], wu_buf.at[slot],
                                  sem_w.at[slot]),
            pltpu.make_async_copy(wout_ref.at[e, pl.ds(c0, EXP_TF), :], wo_buf.at[slot],
                                  sem_w.at[slot]),
        )

    def start_w(unit, f, slot):
        for cp in w_copies(unit, f, slot):
            cp.start()

    def wait_w(unit, f, slot):
        for cp in w_copies(unit, f, slot):
            cp.wait()

    def start_x(unit):
        lax.fori_loop(0, unb_ref[unit], lambda b, c: (x_copy(unit, b).start(), c)[1], 0)

    def wait_x(unit):
        lax.fori_loop(0, unb_ref[unit], lambda b, c: (x_copy(unit, b).wait(), c)[1], 0)

    def wait_y(unit):
        lax.fori_loop(0, unb_ref[unit], lambda b, c: (y_copy(unit, b).wait(), c)[1], 0)

    @pl.when(u < n_units)
    def _():
        nblk = unb_ref[u]

        @pl.when(u == 0)
        def _():
            start_x(u)
            start_w(u, 0, 0)

        wait_x(u)

        @pl.when(u > 0)
        def _():
            wait_y(u - 1)

        def init(b, c):
            r0 = pl.multiple_of(b * ROW_BLK, ROW_BLK)
            lo, hi = _unpack_bf16_pair(xbuf[pl.ds(r0, ROW_BLK), :])
            xb16[pl.ds(r0, ROW_BLK), 0:PACK_W] = lo
            xb16[pl.ds(r0, ROW_BLK), PACK_W:D_MODEL] = hi
            acc[pl.ds(r0, ROW_BLK), :] = jnp.broadcast_to(bo_ref[...], (ROW_BLK, D_MODEL))
            return c

        lax.fori_loop(0, nblk, init, 0)

        @pl.when(u + 1 < n_units)
        def _():
            start_x(u + 1)

        def group(b0, n_blk, f, ws):
            rows = n_blk * ROW_BLK
            r0 = pl.multiple_of(b0 * ROW_BLK, ROW_BLK)
            c0 = pl.multiple_of(f * EXP_TF, EXP_TF)
            x = xb16[pl.ds(r0, rows), :]
            wgu = jnp.concatenate([wg_buf[ws].astype(BF16), wu_buf[ws].astype(BF16)], axis=1)
            gu = jnp.dot(x, wgu, preferred_element_type=F32)
            bg = bin_ref[:, pl.ds(c0, EXP_TF)]
            bu = bin_ref[:, pl.ds(D_EXPERT + c0, EXP_TF)]
            g = jnp.minimum(gu[:, 0:EXP_TF] + bg, SWIGLU_LIMIT)
            up = jnp.clip(gu[:, EXP_TF:2 * EXP_TF] + bu, -SWIGLU_LIMIT, SWIGLU_LIMIT)
            act = (up + 1.0) * (g * _sigmoid(SWIGLU_ALPHA * g))
            acc[pl.ds(r0, rows), :] += jnp.dot(
                act.astype(BF16), wo_buf[ws].astype(BF16), preferred_element_type=F32)

            @pl.when(f == EXP_NF - 1)
            def _():
                for j in range(n_blk):
                    y_copy(u, b0 + j).start()

        def f_tile(f, ws):
            big = GROUP_SIZES[0]
            n_big = nblk // big
            lax.fori_loop(0, n_big, lambda q, c: (group(big * q, big, f, ws), c)[1], 0)
            for size in GROUP_SIZES[1:]:
                done = (nblk // (2 * size)) * (2 * size)

                @pl.when(lax.rem(nblk, 2 * size) >= size)
                def _():
                    group(done, size, f, ws)

        def f_step(f, c):
            ws = lax.rem(f, 2)

            @pl.when(f + 1 < EXP_NF)
            def _():
                start_w(u, f + 1, 1 - ws)

            @pl.when((f + 1 == EXP_NF) & (u + 1 < n_units))
            def _():
                start_w(u + 1, 0, 1 - ws)

            wait_w(u, f, ws)
            f_tile(f, ws)
            return c

        lax.fori_loop(0, EXP_NF, f_step, 0)

        @pl.when(u == n_units - 1)
        def _():
            wait_y(u)

    @pl.when(u == pl.num_programs(0) - 1)
    def _():
        n_blocks = ys_ref.shape[0] // ROW_BLK
        acc[0:ROW_BLK, :] = jnp.zeros((ROW_BLK, D_MODEL), F32)

        def tail_copy(b):
            return pltpu.make_async_copy(acc.at[pl.ds(0, ROW_BLK)],
                                         ys_ref.at[pl.ds(b * ROW_BLK, ROW_BLK)], sem_y.at[0])

        lax.fori_loop(nu_ref[1], n_blocks, lambda b, c: (tail_copy(b).start(), c)[1], 0)
        lax.fori_loop(nu_ref[1], n_blocks, lambda b, c: (tail_copy(b).wait(), c)[1], 0)


def _expert_ffn(xs, unit_e, unit_blk0, unit_nblk, unit_counts, w_in, b_in, w_out, b_out):
    n_rows = xs.shape[0]
    n_units_max = unit_e.shape[0]

    def e_eff(u, ue, nu):
        return ue[jnp.minimum(u, jnp.maximum(nu[0] - 1, 0))]

    any_spec = pl.BlockSpec(memory_space=pl.ANY)
    in_specs = [
        any_spec, any_spec, any_spec,
        pl.BlockSpec((None, 1, 2 * D_EXPERT), lambda u, ue, ub, un, nu: (e_eff(u, ue, nu), 0, 0)),
        pl.BlockSpec((None, 1, D_MODEL), lambda u, ue, ub, un, nu: (e_eff(u, ue, nu), 0, 0)),
    ]
    grid_spec = pltpu.PrefetchScalarGridSpec(
        num_scalar_prefetch=4,
        grid=(n_units_max,),
        in_specs=in_specs,
        out_specs=any_spec,
        scratch_shapes=[
            pltpu.VMEM((UNIT_ROWS, PACK_W), U32),
            pltpu.VMEM((UNIT_ROWS, D_MODEL), BF16),
            pltpu.VMEM((UNIT_ROWS, D_MODEL), F32),
            pltpu.VMEM((2, D_MODEL, EXP_TF), F32),
            pltpu.VMEM((2, D_MODEL, EXP_TF), F32),
            pltpu.VMEM((2, EXP_TF, D_MODEL), F32),
            pltpu.SemaphoreType.DMA((1,)),
            pltpu.SemaphoreType.DMA((1,)),
            pltpu.SemaphoreType.DMA((2,)),
        ],
    )
    return pl.pallas_call(
        _expert_kernel,
        grid_spec=grid_spec,
        out_shape=jax.ShapeDtypeStruct((n_rows, D_MODEL), F32),
        compiler_params=_params(1),
        name="expert_ffn",
    )(unit_e, unit_blk0, unit_nblk, unit_counts, xs,
      w_in, w_out, b_in[:, None, :], b_out[:, None, :])


COMB_TM = 256


def _combine_kernel(dcur_ref, dnxt_ref, x1_ref, gate_ref, g_ref, ys_ref, o_ref, ybuf, sem):
    i = pl.program_id(0)
    n = pl.num_programs(0)
    slot = lax.rem(i, 2)

    def issue(dref, s):
        def rows8(tt, c):
            for si in range(8):
                for k in range(TOP_K):
                    d = dref[0, (tt * 8 + si) * TOP_K + k]
                    pltpu.make_async_copy(ys_ref.at[pl.ds(d, 1)],
                                          ybuf.at[s, k, tt, pl.ds(si, 1)],
                                          sem.at[s]).start(priority=k % 2)
            return c

        lax.fori_loop(0, COMB_TM // 8, rows8, 0)

    @pl.when(i == 0)
    def _():
        issue(dcur_ref, slot)

    @pl.when(i + 1 < n)
    def _():
        issue(dnxt_ref, 1 - slot)

    for k in range(TOP_K):
        pltpu.make_async_copy(ybuf.at[slot, k], ybuf.at[slot, k], sem.at[slot]).wait()

    gate = gate_ref[...]
    y = x1_ref[...]
    for k in range(TOP_K):
        y = y + gate[:, k:k + 1] * ybuf[slot, k].reshape(COMB_TM, D_MODEL)
    ms = jnp.mean(y * y, axis=-1, keepdims=True)
    o_ref[...] = (y * lax.rsqrt(ms + NORM_EPS)) * g_ref[...]


def _combine(dest, x1, gate, norm_g, ys):
    s = x1.shape[0]
    nt = s // COMB_TM
    dest3 = dest.reshape(nt, 1, COMB_TM * TOP_K)
    smem = lambda imap: pl.BlockSpec((None, 1, COMB_TM * TOP_K), imap, memory_space=pltpu.SMEM)
    row = lambda w: pl.BlockSpec((COMB_TM, w), lambda i: (i, 0))
    return pl.pallas_call(
        _combine_kernel,
        grid=(nt,),
        in_specs=[smem(lambda i: (i, 0, 0)),
                  smem(lambda i: (jnp.minimum(i + 1, nt - 1), 0, 0)),
                  row(D_MODEL), row(TOP_K), pl.BlockSpec((1, D_MODEL), lambda i: (0, 0)),
                  pl.BlockSpec(memory_space=pl.ANY)],
        out_specs=row(D_MODEL),
        out_shape=jax.ShapeDtypeStruct((s, D_MODEL), F32),
        scratch_shapes=[pltpu.VMEM((2, TOP_K, COMB_TM // 8, 8, D_MODEL), F32),
                        pltpu.SemaphoreType.DMA((2,))],
        compiler_params=_params(1),
        name="moe_combine",
    )(dest3, dest3, x1, gate, norm_g, ys)


def _route_tables(idx, pos, counts, n_tok):
    n_assign = n_tok * TOP_K
    counts = counts.reshape(N_EXPERTS).astype(I32)
    nblk_e = (counts + ROW_BLK - 1) // ROW_BLK
    pad_start = (jnp.cumsum(nblk_e) - nblk_e) * ROW_BLK
    onehot = idx[..., None] == jnp.arange(N_EXPERTS, dtype=I32)
    dest = pos + jnp.sum(jnp.where(onehot, pad_start, 0), axis=-1)

    n_blocks = -(-(n_assign + N_EXPERTS * (ROW_BLK - 1)) // ROW_BLK)
    used = jnp.sum(nblk_e)
    blk_ids = jnp.arange(n_blocks, dtype=I32)
    last_blk = pad_start // ROW_BLK + nblk_e - 1
    fill_blocks = jnp.concatenate([jnp.where(nblk_e > 0, last_blk, -1),
                                   jnp.where(blk_ids >= used, blk_ids, -1)]).astype(I32)

    n_units_max = (n_blocks + (UNIT_BLOCKS - 1) * N_EXPERTS) // UNIT_BLOCKS
    chunks_e = (nblk_e + UNIT_BLOCKS - 1) // UNIT_BLOCKS
    chunk_end = jnp.cumsum(chunks_e)
    chunk_start = chunk_end - chunks_e
    n_units = chunk_end[-1]
    uid = jnp.arange(n_units_max, dtype=I32)
    unit_e = jnp.minimum(jnp.searchsorted(chunk_end, uid, side="right"), N_EXPERTS - 1).astype(I32)
    c_in_e = uid - chunk_start[unit_e]
    unit_blk0 = (pad_start[unit_e] // ROW_BLK + c_in_e * UNIT_BLOCKS).astype(I32)
    unit_nblk = jnp.clip(nblk_e[unit_e] - c_in_e * UNIT_BLOCKS, 0, UNIT_BLOCKS).astype(I32)
    active = uid < n_units
    unit_blk0 = jnp.where(active, unit_blk0, 0)
    unit_nblk = jnp.where(active, unit_nblk, 0)
    unit_counts = jnp.stack([n_units, used]).astype(I32)
    return dest.astype(I32), fill_blocks, n_blocks * ROW_BLK, unit_e, unit_blk0, unit_nblk, unit_counts


def kernel(x, norm1_g, w_in, b_gate, conv_w, conv_b, conv_ln_g, conv_ln_b, w_conv_out, b_conv_out,
           w_attn_out, w_out, norm2_g, w_router, b_router, w_exp_in, b_exp_in, w_exp_out, b_exp_out,
           norm_f_g):
    b_sz, s_len, d = x.shape
    n_tok = b_sz * s_len
    x2 = x.reshape(n_tok, d)
    l = 0
    b_gate_ext = jnp.concatenate([jnp.zeros((GATE_COL0,), F32), b_gate[l]])[None, :]
    proj3 = _in_projection(x2, norm1_g[l][None, :], w_in[l], b_gate_ext)
    u_ln = _conv_branch(proj3, conv_w[l], conv_b[l][None, :], conv_ln_g[l][None, :],
                        conv_ln_b[l][None, :])
    outs, lses = [], []
    for g in range(N_GROUPS):
        o_g, lse_g = _dilated_attention(proj3, g)
        outs.append(o_g)
        lses.append(lse_g)
    wr = w_router[l]
    wr_hi = wr.astype(BF16)
    wr_lo = jnp.concatenate([wr_hi, (wr - wr_hi.astype(F32)).astype(BF16)], axis=1)
    x1, hp, logits = _mixer_output(
        x2, outs, lses, u_ln, proj3,
        w_attn_out[l].astype(BF16), w_conv_out[l].astype(BF16), b_conv_out[l][None, :],
        w_out[l].astype(BF16), norm2_g[l][None, :], wr_hi, wr_lo, b_router[l][None, :])

    idx, gate, pos, counts = _route(logits)
    dest, fill_blocks, n_rows, unit_e, unit_blk0, unit_nblk, unit_counts = _route_tables(
        idx, pos, counts, n_tok)
    xs = _dispatch(hp, dest, fill_blocks, n_rows)
    ys = _expert_ffn(xs, unit_e, unit_blk0, unit_nblk, unit_counts,
                     w_exp_in[l], b_exp_in[l], w_exp_out[l], b_exp_out[l])
    out = _combine(dest, x1, gate, norm_f_g[None, :], ys)
    return out.reshape(b_sz, s_len, d)
```

```python
import functools

import jax
import jax.numpy as jnp
from jax import lax
from jax.experimental import pallas as pl
from jax.experimental.pallas import tpu as pltpu

D_MODEL = 2048
SEQ = 8192
D_CONV = D_MODEL // 2
CONV_WIDTH = 31
HEAD_DIM = 128
HEADS_PER_GROUP = 4
ATTN_PATTERNS = ((128, 1), (512, 4), (2048, 16))
N_GROUPS = len(ATTN_PATTERNS)
ATTN_WIDTH = N_GROUPS * HEADS_PER_GROUP * HEAD_DIM
ATTN_OUT_WIDTH = HEADS_PER_GROUP * HEAD_DIM
ATTN_BLOCK = 128
D_IN = 2 * D_CONV + 3 * ATTN_WIDTH + 2 * D_MODEL
N_EXPERTS = 32
TOP_K = 4
D_EXPERT = D_MODEL
SWIGLU_LIMIT = 7.0
SWIGLU_ALPHA = 1.702
NORM_EPS = 1e-5

CB = 512
N_CB = D_IN // CB
CB_CONV_A, CB_CONV_G = 0, 2
CB_Q, CB_K, CB_V = 4, 7, 10
CB_GATE_C, CB_GATE_A = 13, 17
GATE_COL0 = CB_GATE_C * CB

V7X_VMEM_BYTES = 64 * 1024 * 1024
VMEM_LIMIT = V7X_VMEM_BYTES - 8 * 1024 * 1024

F32 = jnp.float32
BF16 = jnp.bfloat16
U32 = jnp.uint32
I32 = jnp.int32

PACK_W = D_MODEL // 2
HI_MASK = 0xFFFF0000


def _sigmoid(z):
    return 0.5 * jnp.tanh(0.5 * z) + 0.5


def _pack_bf16_pair(lo, hi):
    lo_bits = lax.bitcast_convert_type(lo.astype(BF16).astype(F32), U32) >> 16
    hi_bits = lax.bitcast_convert_type(hi.astype(BF16).astype(F32), U32) & jnp.uint32(HI_MASK)
    return hi_bits | lo_bits


def _unpack_bf16_pair(w):
    lo = lax.bitcast_convert_type(w << 16, F32).astype(BF16)
    hi = lax.bitcast_convert_type(w & jnp.uint32(HI_MASK), F32).astype(BF16)
    return lo, hi


def _params(n_axes, vmem=VMEM_LIMIT):
    return pltpu.CompilerParams(
        dimension_semantics=("arbitrary",) * n_axes, vmem_limit_bytes=vmem)


NORM_TM = 512
IN_TM = 1024
IN_TN = 1536


def _rmsnorm_kernel(x_ref, g_ref, o_ref):
    x = x_ref[...]
    ms = jnp.mean(x * x, axis=-1, keepdims=True)
    o_ref[...] = ((x * lax.rsqrt(ms + NORM_EPS)) * g_ref[...]).astype(BF16)


def _rmsnorm_bf16(x2, norm_g):
    s = x2.shape[0]
    row = pl.BlockSpec((NORM_TM, D_MODEL), lambda i: (i, 0))
    return pl.pallas_call(
        _rmsnorm_kernel,
        grid=(s // NORM_TM,),
        in_specs=[row, pl.BlockSpec((1, D_MODEL), lambda i: (0, 0))],
        out_specs=row,
        out_shape=jax.ShapeDtypeStruct((s, D_MODEL), BF16),
        compiler_params=_params(1),
        name="rmsnorm_in",
    )(x2, norm_g)


def _inproj_kernel(h_ref, w_ref, bg_ref, o_ref, wbf_ref):
    j = pl.program_id(0)
    i = pl.program_id(1)

    @pl.when(i == 0)
    def _():
        wbf_ref[...] = w_ref[...].astype(BF16)

    acc = jnp.dot(h_ref[...], wbf_ref[...], preferred_element_type=F32) + bg_ref[...]
    for c in range(IN_TN // CB):
        o_ref[c] = acc[:, c * CB:(c + 1) * CB].astype(BF16)


def _in_projection(x2, norm_g, w_in, b_gate_ext):
    s = x2.shape[0]
    h = _rmsnorm_bf16(x2, norm_g)
    grid = (D_IN // IN_TN, s // IN_TM)
    return pl.pallas_call(
        _inproj_kernel,
        grid=grid,
        in_specs=[
            pl.BlockSpec((IN_TM, D_MODEL), lambda j, i: (i, 0)),
            pl.BlockSpec((D_MODEL, IN_TN), lambda j, i: (0, j)),
            pl.BlockSpec((1, IN_TN), lambda j, i: (0, j)),
        ],
        out_specs=pl.BlockSpec((IN_TN // CB, IN_TM, CB), lambda j, i: (j, i, 0)),
        out_shape=jax.ShapeDtypeStruct((N_CB, s, CB), BF16),
        scratch_shapes=[pltpu.VMEM((D_MODEL, IN_TN), BF16)],
        compiler_params=_params(2),
        name="in_projection",
    )(h, w_in, b_gate_ext)


CONV_TS = 512
CONV_HALO = 32
CONV_RC = 32
CONV_N = CONV_TS + CONV_HALO


def _conv_kernel(ac_ref, gc_ref, ah_ref, gh_ref, w_ref, cb_ref, lg_ref, lb_ref, o_ref, r_ref,
                 c_ref):
    i = pl.program_id(0)
    has_prev = i > 0
    half = D_CONV // 2
    for c in range(2):
        cs = slice(c * half, (c + 1) * half)
        uh = ah_ref[c].astype(F32) * _sigmoid(gh_ref[c].astype(F32))
        r_ref[0, 0:CONV_HALO, cs] = jnp.where(has_prev, uh, 0.0)
        r_ref[0, CONV_HALO:CONV_N, cs] = ac_ref[c].astype(F32) * _sigmoid(gc_ref[c].astype(F32))
    u_ext = r_ref[0]
    for b in range(1, 8):
        r_ref[b] = pltpu.roll(u_ext, CONV_N - b, axis=0)

    def chunk(ci, carry):
        r0 = pl.multiple_of(ci * CONV_RC, CONV_RC)
        acc = jnp.broadcast_to(cb_ref[...], (CONV_RC, D_CONV))
        for k in range(CONV_WIDTH):
            kp = k + (CONV_HALO - (CONV_WIDTH - 1))
            a, b = kp // 8, kp % 8
            acc = acc + w_ref[k:k + 1, :] * r_ref[b, pl.ds(r0 + 8 * a, CONV_RC), :]
        c_ref[pl.ds(r0, CONV_RC), :] = acc
        return carry

    lax.fori_loop(0, CONV_TS // CONV_RC, chunk, 0)
    conv = c_ref[...]
    mu = jnp.mean(conv, axis=-1, keepdims=True)
    d = conv - mu
    var = jnp.mean(d * d, axis=-1, keepdims=True)
    y = (d * lax.rsqrt(var + NORM_EPS)) * lg_ref[...] + lb_ref[...]
    o_ref[...] = (y * _sigmoid(y)).astype(BF16)


def _conv_branch(proj3, conv_w, conv_b, ln_g, ln_b):
    s = proj3.shape[1]
    hb = CONV_TS // CONV_HALO
    cur = lambda blk: pl.BlockSpec((2, CONV_TS, CB), lambda i: (blk // 2, i, 0))
    halo = lambda blk: pl.BlockSpec(
        (2, CONV_HALO, CB), lambda i: (blk // 2, jnp.maximum(i * hb - 1, 0), 0))
    vec = pl.BlockSpec((1, D_CONV), lambda i: (0, 0))
    return pl.pallas_call(
        _conv_kernel,
        grid=(s // CONV_TS,),
        in_specs=[cur(CB_CONV_A), cur(CB_CONV_G), halo(CB_CONV_A), halo(CB_CONV_G),
                  pl.BlockSpec((CONV_WIDTH, D_CONV), lambda i: (0, 0)), vec, vec, vec],
        out_specs=pl.BlockSpec((CONV_TS, D_CONV), lambda i: (i, 0)),
        out_shape=jax.ShapeDtypeStruct((s, D_CONV), BF16),
        scratch_shapes=[pltpu.VMEM((8, CONV_N, D_CONV), F32), pltpu.VMEM((CONV_TS, D_CONV), F32)],
        compiler_params=_params(1),
        name="conv_branch",
    )(proj3, proj3, proj3, proj3, conv_w, conv_b, ln_g, ln_b)


LSE_W = 128
LSE_REP = LSE_W // HEADS_PER_GROUP


def _attn_rows(q, k, v, n):
    blk = ATTN_BLOCK
    scale = HEAD_DIM ** -0.5
    qi = lax.broadcasted_iota(I32, (blk, 2 * blk), 0)
    ki = lax.broadcasted_iota(I32, (blk, 2 * blk), 1)
    band = (ki >= qi) & (ki <= qi + blk)
    lane_head = lax.broadcasted_iota(I32, (blk, LSE_W), 1) // LSE_REP
    o_rows, lse_rows = [], []
    for j in range(q.shape[0] // blk):
        mask = band
        if j == 0:
            mask = band & ((ki >= blk) | (n > 0))
        o_heads = []
        lse_tile = jnp.zeros((blk, LSE_W), F32)
        for hh in range(HEADS_PER_GROUP):
            cs = slice(hh * HEAD_DIM, (hh + 1) * HEAD_DIM)
            qh = q[j * blk:(j + 1) * blk, cs]
            kh = k[j * blk:(j + 2) * blk, cs]
            vh = v[j * blk:(j + 2) * blk, cs]
            s = lax.dot_general(qh, kh, (((1,), (1,)), ((), ())),
                                preferred_element_type=F32) * scale
            s = jnp.where(mask, s, -jnp.inf)
            m = jnp.max(s, axis=-1, keepdims=True)
            p = jnp.exp(s - m)
            l = jnp.sum(p, axis=-1, keepdims=True)
            o_heads.append(jnp.dot(p.astype(BF16), vh, preferred_element_type=F32) / l)
            lse_tile = jnp.where(lane_head == hh, m + jnp.log(l), lse_tile)
        o_rows.append(jnp.concatenate(o_heads, axis=1))
        lse_rows.append(lse_tile)
    return jnp.concatenate(o_rows, axis=0), jnp.concatenate(lse_rows, axis=0)


def _attn_kernel(q_ref, kc_ref, kp_ref, vc_ref, vp_ref, o_ref, lse_ref):
    n = pl.program_id(0)
    k = jnp.concatenate([kp_ref[...], kc_ref[...]], axis=0)
    v = jnp.concatenate([vp_ref[...], vc_ref[...]], axis=0)
    o, lse = _attn_rows(q_ref[...], k, v, n)
    o_ref[...] = o.astype(BF16)
    lse_ref[...] = lse


ATTN_TILE = 2048


def _attn_dilated_kernel(q_ref, kc_ref, kp_ref, vc_ref, vp_ref, o_ref, lse_ref,
                         qs, ks, vs, os_, ls, *, dil):
    n = pl.program_id(0)
    tile = q_ref.shape[0]
    mb = tile // dil
    half = dil // 2
    halo_w = ATTN_BLOCK * half
    n_slab = CB // 128
    q32 = pltpu.bitcast(q_ref[...], U32)
    kp32, kc32 = pltpu.bitcast(kp_ref[...], U32), pltpu.bitcast(kc_ref[...], U32)
    vp32, vc32 = pltpu.bitcast(vp_ref[...], U32), pltpu.bitcast(vc_ref[...], U32)
    for c in range(n_slab):
        cs = slice(c * 128, (c + 1) * 128)
        qs[c] = q32[:, cs]
        ks[c, 0:halo_w] = kp32[:, cs]
        ks[c, halo_w:] = kc32[:, cs]
        vs[c, 0:halo_w] = vp32[:, cs]
        vs[c, halo_w:] = vc32[:, cs]

    def gather(slabs, rp, rows):
        return jnp.concatenate(
            [slabs[c, pl.ds(rp, rows, stride=half), :] for c in range(n_slab)], axis=1)

    def residue(w, par):
        bits = (w << 16) if par == 0 else (w & jnp.uint32(HI_MASK))
        return lax.bitcast_convert_type(bits, F32).astype(BF16)

    for rp in range(half):
        wq = gather(qs, rp, mb)
        wk = gather(ks, rp, mb + ATTN_BLOCK)
        wv = gather(vs, rp, mb + ATTN_BLOCK)
        outs = []
        for par in range(2):
            o, lse = _attn_rows(residue(wq, par), residue(wk, par), residue(wv, par), n)
            outs.append(o)
            ls[pl.ds(2 * rp + par, mb, stride=dil), :] = lse
        w_out = _pack_bf16_pair(outs[0], outs[1])
        for c in range(n_slab):
            os_[c, pl.ds(rp, mb, stride=half), :] = w_out[:, c * 128:(c + 1) * 128]
    o32 = jnp.concatenate([os_[c] for c in range(n_slab)], axis=1)
    o_ref[...] = pltpu.bitcast(o32, BF16)
    lse_ref[...] = ls[...]


def _dilated_attention(proj3, group):
    _, dil = ATTN_PATTERNS[group]
    s = proj3.shape[1]
    cbs = (CB_Q + group, CB_K + group, CB_V + group)
    if dil == 1:
        tile, halo = 1024, ATTN_BLOCK
        body, scratch = _attn_kernel, []
    else:
        tile, halo = ATTN_TILE, ATTN_BLOCK * dil
        body = functools.partial(_attn_dilated_kernel, dil=dil)
        scratch = [pltpu.VMEM((CB // 128, tile // 2, 128), U32),
                   pltpu.VMEM((CB // 128, (tile + halo) // 2, 128), U32),
                   pltpu.VMEM((CB // 128, (tile + halo) // 2, 128), U32),
                   pltpu.VMEM((CB // 128, tile // 2, 128), U32),
                   pltpu.VMEM((tile, LSE_W), F32)]
    hb = tile // halo
    cur = lambda cb: pl.BlockSpec((None, tile, CB), lambda n: (cb, n, 0))
    prev = lambda cb: pl.BlockSpec((None, halo, CB), lambda n: (cb, jnp.maximum(n * hb - 1, 0), 0))
    return pl.pallas_call(
        body,
        grid=(s // tile,),
        in_specs=[cur(cbs[0]), cur(cbs[1]), prev(cbs[1]), cur(cbs[2]), prev(cbs[2])],
        out_specs=[pl.BlockSpec((tile, CB), lambda n: (n, 0)),
                   pl.BlockSpec((tile, LSE_W), lambda n: (n, 0))],
        out_shape=[jax.ShapeDtypeStruct((s, CB), BF16),
                   jax.ShapeDtypeStruct((s, LSE_W), F32)],
        scratch_shapes=scratch,
        compiler_params=_params(1),
        name=f"dilated_attention_g{group}",
    )(proj3, proj3, proj3, proj3, proj3)


MIX_TM = 256


def _mixer_out_kernel(x_ref, o0_ref, o1_ref, o2_ref, l0_ref, l1_ref, l2_ref, u_ref,
                      gc0, gc1, gc2, gc3, ga0, ga1, ga2, ga3,
                      wa_ref, wc_ref, bc_ref, wo_ref, g2_ref, wrh_ref, wrl_ref, br_ref,
                      x1_ref, hp_ref, logit_ref):
    l_refs = (l0_ref, l1_ref, l2_ref)
    o_refs = (o0_ref, o1_ref, o2_ref)
    parts = []
    for hh in range(HEADS_PER_GROUP):
        cs = slice(hh * HEAD_DIM, (hh + 1) * HEAD_DIM)
        lse = [r[:, hh * LSE_REP:hh * LSE_REP + 1] for r in l_refs]
        m = jnp.maximum(jnp.maximum(lse[0], lse[1]), lse[2])
        e = [jnp.exp(v - m) for v in lse]
        den = e[0] + e[1] + e[2]
        acc = (e[0] / den) * o_refs[0][:, cs].astype(F32)
        for g in range(1, N_GROUPS):
            acc = acc + (e[g] / den) * o_refs[g][:, cs].astype(F32)
        parts.append(acc)
    o = jnp.concatenate(parts, axis=1).astype(BF16)
    ya = jnp.dot(o, wa_ref[...], preferred_element_type=F32)
    yc = jnp.dot(u_ref[...], wc_ref[...], preferred_element_type=F32) + bc_ref[...]
    gcs = (gc0, gc1, gc2, gc3)
    gas = (ga0, ga1, ga2, ga3)
    merged = []
    for c in range(D_MODEL // CB):
        cs = slice(c * CB, (c + 1) * CB)
        merged.append((_sigmoid(gcs[c][...].astype(F32)) * yc[:, cs]
                       + _sigmoid(gas[c][...].astype(F32)) * ya[:, cs]).astype(BF16))
    merged = jnp.concatenate(merged, axis=1)
    x1 = x_ref[...] + jnp.dot(merged, wo_ref[...], preferred_element_type=F32)
    x1_ref[...] = x1
    ms = jnp.mean(x1 * x1, axis=-1, keepdims=True)
    h2 = (x1 * lax.rsqrt(ms + NORM_EPS)) * g2_ref[...]
    h2_hi = h2.astype(BF16)
    hp_ref[...] = _pack_bf16_pair(h2[:, 0:PACK_W], h2[:, PACK_W:D_MODEL])
    h2_lo = (h2 - h2_hi.astype(F32)).astype(BF16)
    hi_pass = jnp.dot(h2_hi, wrl_ref[...], preferred_element_type=F32)
    logits = (hi_pass[:, 0:N_EXPERTS] + hi_pass[:, N_EXPERTS:2 * N_EXPERTS]
              + jnp.dot(h2_lo, wrh_ref[...], preferred_element_type=F32)) + br_ref[...]
    logit_ref[...] = logits


def _mixer_output(x2, outs, lses, u_ln, proj3, wa, wc, bc, wo, g2, wr_hi, wr_lo, br):
    s = x2.shape[0]
    row = lambda w: pl.BlockSpec((MIX_TM, w), lambda i: (i, 0))
    gate = lambda cb: pl.BlockSpec((None, MIX_TM, CB), lambda i: (cb, i, 0))
    full = lambda a: pl.BlockSpec(a.shape, lambda i: (0,) * a.ndim)
    in_specs = ([row(D_MODEL)] + [row(CB)] * 3 + [row(LSE_W)] * 3 + [row(D_CONV)]
                + [gate(CB_GATE_C + c) for c in range(4)]
                + [gate(CB_GATE_A + c) for c in range(4)]
                + [full(a) for a in (wa, wc, bc, wo, g2, wr_hi, wr_lo, br)])
    return pl.pallas_call(
        _mixer_out_kernel,
        grid=(s // MIX_TM,),
        in_specs=in_specs,
        out_specs=[row(D_MODEL), row(PACK_W), row(N_EXPERTS)],
        out_shape=[jax.ShapeDtypeStruct((s, D_MODEL), F32),
                   jax.ShapeDtypeStruct((s, PACK_W), U32),
                   jax.ShapeDtypeStruct((s, N_EXPERTS), F32)],
        compiler_params=_params(1),
        name="mixer_output",
    )(x2, *outs, *lses, u_ln, *([proj3] * 8), wa, wc, bc, wo, g2, wr_hi, wr_lo, br)


ROUTE_TM = 1024


def _route_kernel(logit_ref, idx_ref, gate_ref, pos_ref, cnt_ref, carry_ref):
    i = pl.program_id(0)
    tm = ROUTE_TM

    @pl.when(i == 0)
    def _():
        carry_ref[...] = jnp.zeros_like(carry_ref)

    logits = logit_ref[...]
    e_iota = lax.broadcasted_iota(I32, (tm, N_EXPERTS), 1).astype(F32)
    k_lane = lax.broadcasted_iota(I32, (tm, TOP_K), 1)
    vals = logits
    sels = []
    idx_t = jnp.zeros((tm, TOP_K), F32)
    val_t = jnp.zeros((tm, TOP_K), F32)
    for k in range(TOP_K):
        mk = jnp.max(vals, axis=-1, keepdims=True)
        ik = jnp.min(jnp.where(vals == mk, e_iota, float(N_EXPERTS)), axis=-1, keepdims=True)
        sel = e_iota == ik
        sels.append(sel)
        vals = jnp.where(sel, -jnp.inf, vals)
        idx_t = jnp.where(k_lane == k, ik, idx_t)
        val_t = jnp.where(k_lane == k, mk, val_t)
    ex = jnp.exp(val_t - val_t[:, 0:1])
    gate_ref[...] = ex / jnp.sum(ex, axis=-1, keepdims=True)
    idx_ref[...] = idx_t.astype(I32)

    cnt = jnp.zeros((tm, N_EXPERTS), F32)
    for sel in sels:
        cnt = cnt + jnp.where(sel, 1.0, 0.0)
    ri = lax.broadcasted_iota(I32, (tm, tm), 0)
    ci = lax.broadcasted_iota(I32, (tm, tm), 1)
    lower = jnp.where(ci < ri, 1.0, 0.0).astype(BF16)
    prefix = jnp.dot(lower, cnt.astype(BF16), preferred_element_type=F32) + carry_ref[...]
    pos_t = jnp.zeros((tm, TOP_K), F32)
    for k, sel in enumerate(sels):
        pk = jnp.sum(jnp.where(sel, prefix, 0.0), axis=-1, keepdims=True)
        pos_t = jnp.where(k_lane == k, pk, pos_t)
    pos_ref[...] = pos_t.astype(I32)
    carry_ref[...] = carry_ref[...] + jnp.sum(cnt, axis=0, keepdims=True)
    cnt_ref[...] = carry_ref[...]


def _route(logits):
    s = logits.shape[0]
    row = lambda w: pl.BlockSpec((ROUTE_TM, w), lambda i: (i, 0))
    return pl.pallas_call(
        _route_kernel,
        grid=(s // ROUTE_TM,),
        in_specs=[row(N_EXPERTS)],
        out_specs=[row(TOP_K), row(TOP_K), row(TOP_K),
                   pl.BlockSpec((1, N_EXPERTS), lambda i: (0, 0))],
        out_shape=[jax.ShapeDtypeStruct((s, TOP_K), I32),
                   jax.ShapeDtypeStruct((s, TOP_K), F32),
                   jax.ShapeDtypeStruct((s, TOP_K), I32),
                   jax.ShapeDtypeStruct((1, N_EXPERTS), F32)],
        scratch_shapes=[pltpu.VMEM((1, N_EXPERTS), F32)],
        compiler_params=_params(1),
        name="moe_route",
    )(logits)


ROW_BLK = 128
DISP_TM = 1024


def _dispatch_kernel(fill_ref, dest_ref, h_ref, xs_ref, zbuf, sem):
    i = pl.program_id(0)
    n_fill = fill_ref.shape[0]

    @pl.when(i == 0)
    def _():
        zbuf[...] = jnp.zeros_like(zbuf)

        def fill_copy(j):
            return pltpu.make_async_copy(
                zbuf, xs_ref.at[pl.ds(fill_ref[j] * ROW_BLK, ROW_BLK)], sem.at[0])

        def start(j, c):
            @pl.when(fill_ref[j] >= 0)
            def _():
                fill_copy(j).start()
            return c

        def wait(j, c):
            @pl.when(fill_ref[j] >= 0)
            def _():
                fill_copy(j).wait()
            return c

        lax.fori_loop(0, n_fill, start, 0)
        lax.fori_loop(0, n_fill, wait, 0)

    def rows8(tt, c):
        for si in range(8):
            for k in range(TOP_K):
                d = dest_ref[0, (tt * 8 + si) * TOP_K + k]
                pltpu.make_async_copy(h_ref.at[tt, pl.ds(si, 1)], xs_ref.at[pl.ds(d, 1)],
                                      sem.at[0]).start(priority=k % 2)
        return c

    lax.fori_loop(0, DISP_TM // 8, rows8, 0)
    for _ in range(TOP_K):
        pltpu.make_async_copy(h_ref, h_ref, sem.at[0]).wait()


def _dispatch(hp, dest, fill_blocks, n_rows):
    s = hp.shape[0]
    dest3 = dest.reshape(s // DISP_TM, 1, DISP_TM * TOP_K)
    grid_spec = pltpu.PrefetchScalarGridSpec(
        num_scalar_prefetch=1,
        grid=(s // DISP_TM,),
        in_specs=[
            pl.BlockSpec((None, 1, DISP_TM * TOP_K), lambda i, fr: (i, 0, 0),
                         memory_space=pltpu.SMEM),
            pl.BlockSpec((DISP_TM // 8, 8, PACK_W), lambda i, fr: (i, 0, 0)),
        ],
        out_specs=pl.BlockSpec(memory_space=pl.ANY),
        scratch_shapes=[pltpu.VMEM((ROW_BLK, PACK_W), U32), pltpu.SemaphoreType.DMA((1,))],
    )
    return pl.pallas_call(
        _dispatch_kernel,
        grid_spec=grid_spec,
        out_shape=jax.ShapeDtypeStruct((n_rows, PACK_W), U32),
        compiler_params=_params(1),
        name="moe_dispatch",
    )(fill_blocks, dest3, hp.reshape(s // 8, 8, PACK_W))


UNIT_BLOCKS = 10
UNIT_ROWS = ROW_BLK * UNIT_BLOCKS
GROUP_SIZES = (8, 4, 2, 1)
EXP_TF = 512
EXP_NF = D_EXPERT // EXP_TF


def _expert_kernel(ue_ref, ub0_ref, unb_ref, nu_ref,
                   xs_ref, win_ref, wout_ref, bin_ref, bo_ref,
                   ys_ref,
                   xbuf, xb16, acc, wg_buf, wu_buf, wo_buf, sem_x, sem_y, sem_w):
    u = pl.program_id(0)
    n_units = nu_ref[0]

    def x_copy(unit, b):
        src = xs_ref.at[pl.ds((ub0_ref[unit] + b) * ROW_BLK, ROW_BLK)]
        dst = xbuf.at[pl.ds(b * ROW_BLK, ROW_BLK)]
        return pltpu.make_async_copy(src, dst, sem_x.at[0])

    def y_copy(unit, b):
        src = acc.at[pl.ds(b * ROW_BLK, ROW_BLK)]
        dst = ys_ref.at[pl.ds((ub0_ref[unit] + b) * ROW_BLK, ROW_BLK)]
        return pltpu.make_async_copy(src, dst, sem_y.at[0])

    def w_copies(unit, f, slot):
        e = ue_ref[unit]
        c0 = pl.multiple_of(f * EXP_TF, EXP_TF)
        return (
            pltpu.make_async_copy(win_ref.at[e, :, pl.ds(c0, EXP_TF)], wg_buf.at[slot],
                                  sem_w.at[slot]),
            pltpu.make_async_copy(win_ref.at[e, :, pl.ds(D_EXPERT + c0, EXP_TF)], wu_buf.at[slot],
                                  sem_w.at[slot]),
            pltpu.make_async_copy(wout_ref.at[e, pl.ds(c0, EXP_TF), :], wo_buf.at[slot],
                                  sem_w.at[slot]),
        )

    def start_w(unit, f, slot):
        for cp in w_copies(unit, f, slot):
            cp.start()

    def wait_w(unit, f, slot):
        for cp in w_copies(unit, f, slot):
            cp.wait()

    def start_x(unit):
        lax.fori_loop(0, unb_ref[unit], lambda b, c: (x_copy(unit, b).start(), c)[1], 0)

    def wait_x(unit):
        lax.fori_loop(0, unb_ref[unit], lambda b, c: (x_copy(unit, b).wait(), c)[1], 0)

    def wait_y(unit):
        lax.fori_loop(0, unb_ref[unit], lambda b, c: (y_copy(unit, b).wait(), c)[1], 0)

    @pl.when(u < n_units)
    def _():
        nblk = unb_ref[u]

        @pl.when(u == 0)
        def _():
            start_x(u)
            start_w(u, 0, 0)
            acc[...] = jnp.zeros_like(acc)

        wait_x(u)

        def unpack(b, c):
            r0 = pl.multiple_of(b * ROW_BLK, ROW_BLK)
            lo, hi = _unpack_bf16_pair(xbuf[pl.ds(r0, ROW_BLK), :])
            xb16[pl.ds(r0, ROW_BLK), 0:PACK_W] = lo
            xb16[pl.ds(r0, ROW_BLK), PACK_W:D_MODEL] = hi
            return c

        lax.fori_loop(0, nblk, unpack, 0)

        @pl.when(u + 1 < n_units)
        def _():
            start_x(u + 1)

        @pl.when(u > 0)
        def _():
            wait_y(u - 1)

        def group(b0, n_blk, f, ws):
            rows = n_blk * ROW_BLK
            r0 = pl.multiple_of(b0 * ROW_BLK, ROW_BLK)
            c0 = pl.multiple_of(f * EXP_TF, EXP_TF)
            x = xb16[pl.ds(r0, rows), :]
            wgu = jnp.concatenate([wg_buf[ws].astype(BF16), wu_buf[ws].astype(BF16)], axis=1)
            gu = jnp.dot(x, wgu, preferred_element_type=F32)
            bg = bin_ref[:, pl.ds(c0, EXP_TF)]
            bu = bin_ref[:, pl.ds(D_EXPERT + c0, EXP_TF)]
            g = jnp.minimum(gu[:, 0:EXP_TF] + bg, SWIGLU_LIMIT)
            up = jnp.clip(gu[:, EXP_TF:2 * EXP_TF] + bu, -SWIGLU_LIMIT, SWIGLU_LIMIT)
            act = (up + 1.0) * (g * _sigmoid(SWIGLU_ALPHA * g))
            start = jnp.where(f == 0, jnp.broadcast_to(bo_ref[...], (rows, D_MODEL)),
                              acc[pl.ds(r0, rows), :])
            acc[pl.ds(r0, rows), :] = start + jnp.dot(
                act.astype(BF16), wo_buf[ws].astype(BF16), preferred_element_type=F32)

            @pl.when(f == EXP_NF - 1)
            def _():
                for j in range(n_blk):
                    y_copy(u, b0 + j).start()

        def f_tile(f, ws):
            big = GROUP_SIZES[0]
            n_big = nblk // big
            lax.fori_loop(0, n_big, lambda q, c: (group(big * q, big, f, ws), c)[1], 0)
            for size in GROUP_SIZES[1:]:
                done = (nblk // (2 * size)) * (2 * size)

                @pl.when(lax.rem(nblk, 2 * size) >= size)
                def _():
                    group(done, size, f, ws)

        def f_step(f, c):
            ws = lax.rem(f, 2)

            @pl.when(f + 1 < EXP_NF)
            def _():
                start_w(u, f + 1, 1 - ws)

            @pl.when((f + 1 == EXP_NF) & (u + 1 < n_units))
            def _():
                start_w(u + 1, 0, 1 - ws)

            wait_w(u, f, ws)
            f_tile(f, ws)
            return c

        lax.fori_loop(0, EXP_NF, f_step, 0)

        @pl.when(u == n_units - 1)
        def _():
            wait_y(u)

    @pl.when(u == pl.num_programs(0) - 1)
    def _():
        n_blocks = ys_ref.shape[0] // ROW_BLK
        acc[0:ROW_BLK, :] = jnp.zeros((ROW_BLK, D_MODEL), F32)

        def tail_copy(b):
            return pltpu.make_async_copy(acc.at[pl.ds(0, ROW_BLK)],
                                         ys_ref.at[pl.ds(b * ROW_BLK, ROW_BLK)], sem_y.at[0])

        lax.fori_loop(nu_ref[1], n_blocks, lambda b, c: (tail_copy(b).start(), c)[1], 0)
        lax.fori_loop(nu_ref[1], n_blocks, lambda b, c: (tail_copy(b).wait(), c)[1], 0)


def _expert_ffn(xs, unit_e, unit_blk0, unit_nblk, unit_counts, w_in, b_in, w_out, b_out):
    n_rows = xs.shape[0]
    n_units_max = unit_e.shape[0]

    def e_eff(u, ue, nu):
        return ue[jnp.minimum(u, jnp.maximum(nu[0] - 1, 0))]

    any_spec = pl.BlockSpec(memory_space=pl.ANY)
    in_specs = [
        any_spec, any_spec, any_spec,
        pl.BlockSpec((None, 1, 2 * D_EXPERT), lambda u, ue, ub, un, nu: (e_eff(u, ue, nu), 0, 0)),
        pl.BlockSpec((None, 1, D_MODEL), lambda u, ue, ub, un, nu: (e_eff(u, ue, nu), 0, 0)),
    ]
    grid_spec = pltpu.PrefetchScalarGridSpec(
        num_scalar_prefetch=4,
        grid=(n_units_max,),
        in_specs=in_specs,
        out_specs=any_spec,
        scratch_shapes=[
            pltpu.VMEM((UNIT_ROWS, PACK_W), U32),
            pltpu.VMEM((UNIT_ROWS, D_MODEL), BF16),
            pltpu.VMEM((UNIT_ROWS, D_MODEL), F32),
            pltpu.VMEM((2, D_MODEL, EXP_TF), F32),
            pltpu.VMEM((2, D_MODEL, EXP_TF), F32),
            pltpu.VMEM((2, EXP_TF, D_MODEL), F32),
            pltpu.SemaphoreType.DMA((1,)),
            pltpu.SemaphoreType.DMA((1,)),
            pltpu.SemaphoreType.DMA((2,)),
        ],
    )
    return pl.pallas_call(
        _expert_kernel,
        grid_spec=grid_spec,
        out_shape=jax.ShapeDtypeStruct((n_rows, D_MODEL), F32),
        compiler_params=_params(1),
        name="expert_ffn",
    )(unit_e, unit_blk0, unit_nblk, unit_counts, xs,
      w_in, w_out, b_in[:, None, :], b_out[:, None, :])


COMB_TM = 256


def _combine_kernel(dcur_ref, dnxt_ref, x1_ref, gate_ref, g_ref, ys_ref, o_ref, ybuf, sem):
    i = pl.program_id(0)
    n = pl.num_programs(0)
    slot = lax.rem(i, 2)

    def issue(dref, s):
        def rows8(tt, c):
            for si in range(8):
                for k in range(TOP_K):
                    d = dref[0, (tt * 8 + si) * TOP_K + k]
                    pltpu.make_async_copy(ys_ref.at[pl.ds(d, 1)],
                                          ybuf.at[s, k, tt, pl.ds(si, 1)],
                                          sem.at[s]).start(priority=k % 2)
            return c

        lax.fori_loop(0, COMB_TM // 8, rows8, 0)

    @pl.when(i == 0)
    def _():
        issue(dcur_ref, slot)

    @pl.when(i + 1 < n)
    def _():
        issue(dnxt_ref, 1 - slot)

    for k in range(TOP_K):
        pltpu.make_async_copy(ybuf.at[slot, k], ybuf.at[slot, k], sem.at[slot]).wait()

    gate = gate_ref[...]
    y = x1_ref[...]
    for k in range(TOP_K):
        y = y + gate[:, k:k + 1] * ybuf[slot, k].reshape(COMB_TM, D_MODEL)
    ms = jnp.mean(y * y, axis=-1, keepdims=True)
    o_ref[...] = (y * lax.rsqrt(ms + NORM_EPS)) * g_ref[...]


def _combine(dest, x1, gate, norm_g, ys):
    s = x1.shape[0]
    nt = s // COMB_TM
    dest3 = dest.reshape(nt, 1, COMB_TM * TOP_K)
    smem = lambda imap: pl.BlockSpec((None, 1, COMB_TM * TOP_K), imap, memory_space=pltpu.SMEM)
    row = lambda w: pl.BlockSpec((COMB_TM, w), lambda i: (i, 0))
    return pl.pallas_call(
        _combine_kernel,
        grid=(nt,),
        in_specs=[smem(lambda i: (i, 0, 0)),
                  smem(lambda i: (jnp.minimum(i + 1, nt - 1), 0, 0)),
                  row(D_MODEL), row(TOP_K), pl.BlockSpec((1, D_MODEL), lambda i: (0, 0)),
                  pl.BlockSpec(memory_space=pl.ANY)],
        out_specs=row(D_MODEL),
        out_shape=jax.ShapeDtypeStruct((s, D_MODEL), F32),
        scratch_shapes=[pltpu.VMEM((2, TOP_K, COMB_TM // 8, 8, D_MODEL), F32),
                        pltpu.SemaphoreType.DMA((2,))],
        compiler_params=_params(1),
        name="moe_combine",
    )(dest3, dest3, x1, gate, norm_g, ys)


def _route_tables(idx, pos, counts, n_tok):
    n_assign = n_tok * TOP_K
    counts = counts.reshape(N_EXPERTS).astype(I32)
    nblk_e = (counts + ROW_BLK - 1) // ROW_BLK
    pad_start = (jnp.cumsum(nblk_e) - nblk_e) * ROW_BLK
    onehot = idx[..., None] == jnp.arange(N_EXPERTS, dtype=I32)
    dest = pos + jnp.sum(jnp.where(onehot, pad_start, 0), axis=-1)

    n_blocks = -(-(n_assign + N_EXPERTS * (ROW_BLK - 1)) // ROW_BLK)
    used = jnp.sum(nblk_e)
    blk_ids = jnp.arange(n_blocks, dtype=I32)
    last_blk = pad_start // ROW_BLK + nblk_e - 1
    fill_blocks = jnp.concatenate([jnp.where(nblk_e > 0, last_blk, -1),
                                   jnp.where(blk_ids >= used, blk_ids, -1)]).astype(I32)

    n_units_max = (n_blocks + (UNIT_BLOCKS - 1) * N_EXPERTS) // UNIT_BLOCKS
    chunks_e = (nblk_e + UNIT_BLOCKS - 1) // UNIT_BLOCKS
    chunk_end = jnp.cumsum(chunks_e)
    chunk_start = chunk_end - chunks_e
    n_units = chunk_end[-1]
    uid = jnp.arange(n_units_max, dtype=I32)
    unit_e = jnp.minimum(jnp.searchsorted(chunk_end, uid, side="right"), N_EXPERTS - 1).astype(I32)
    c_in_e = uid - chunk_start[unit_e]
    unit_blk0 = (pad_start[unit_e] // ROW_BLK + c_in_e * UNIT_BLOCKS).astype(I32)
    unit_nblk = jnp.clip(nblk_e[unit_e] - c_in_e * UNIT_BLOCKS, 0, UNIT_BLOCKS).astype(I32)
    active = uid < n_units
    unit_blk0 = jnp.where(active, unit_blk0, 0)
    unit_nblk = jnp.where(active, unit_nblk, 0)
    unit_counts = jnp.stack([n_units, used]).astype(I32)
    return dest.astype(I32), fill_blocks, n_blocks * ROW_BLK, unit_e, unit_blk0, unit_nblk, unit_counts


def kernel(x, norm1_g, w_in, b_gate, conv_w, conv_b, conv_ln_g, conv_ln_b, w_conv_out, b_conv_out,
           w_attn_out, w_out, norm2_g, w_router, b_router, w_exp_in, b_exp_in, w_exp_out, b_exp_out,
           norm_f_g):
    b_sz, s_len, d = x.shape
    n_tok = b_sz * s_len
    x2 = x.reshape(n_tok, d)
    l = 0
    b_gate_ext = jnp.concatenate([jnp.zeros((GATE_COL0,), F32), b_gate[l]])[None, :]
    proj3 = _in_projection(x2, norm1_g[l][None, :], w_in[l], b_gate_ext)
    u_ln = _conv_branch(proj3, conv_w[l], conv_b[l][None, :], conv_ln_g[l][None, :],
                        conv_ln_b[l][None, :])
    outs, lses = [], []
    for g in range(N_GROUPS):
        o_g, lse_g = _dilated_attention(proj3, g)
        outs.append(o_g)
        lses.append(lse_g)
    wr = w_router[l]
    wr_hi = wr.astype(BF16)
    wr_lo = jnp.concatenate([wr_hi, (wr - wr_hi.astype(F32)).astype(BF16)], axis=1)
    x1, hp, logits = _mixer_output(
        x2, outs, lses, u_ln, proj3,
        w_attn_out[l].astype(BF16), w_conv_out[l].astype(BF16), b_conv_out[l][None, :],
        w_out[l].astype(BF16), norm2_g[l][None, :], wr_hi, wr_lo, b_router[l][None, :])

    idx, gate, pos, counts = _route(logits)
    dest, fill_blocks, n_rows, unit_e, unit_blk0, unit_nblk, unit_counts = _route_tables(
        idx, pos, counts, n_tok)
    xs = _dispatch(hp, dest, fill_blocks, n_rows)
    ys = _expert_ffn(xs, unit_e, unit_blk0, unit_nblk, unit_counts,
                     w_exp_in[l], b_exp_in[l], w_exp_out[l], b_exp_out[l])
    out = _combine(dest, x1, gate, norm_f_g[None, :], ys)
    return out.reshape(b_sz, s_len, d)
```

```python
import functools

import jax
import jax.numpy as jnp
from jax import lax
from jax.experimental import pallas as pl
from jax.experimental.pallas import tpu as pltpu

D_MODEL = 2048
D_CONV = D_MODEL // 2
CONV_WIDTH = 31
HEAD_DIM = 128
HEADS_PER_GROUP = 4
ATTN_PATTERNS = ((128, 1), (512, 4), (2048, 16))
N_GROUPS = len(ATTN_PATTERNS)
ATTN_WIDTH = N_GROUPS * HEADS_PER_GROUP * HEAD_DIM
ATTN_BLOCK = 128
D_IN = 2 * D_CONV + 3 * ATTN_WIDTH + 2 * D_MODEL
N_EXPERTS = 32
TOP_K = 4
D_EXPERT = D_MODEL
SWIGLU_LIMIT = 7.0
SWIGLU_ALPHA = 1.702
NORM_EPS = 1e-5

CB = 512
N_CB = D_IN // CB
CB_CONV_A, CB_CONV_G = 0, 2
CB_Q, CB_K, CB_V = 4, 7, 10
CB_GATE_C, CB_GATE_A = 13, 17
GATE_COL0 = CB_GATE_C * CB

V7X_VMEM_BYTES = 64 * 1024 * 1024
VMEM_LIMIT = V7X_VMEM_BYTES - 8 * 1024 * 1024

F32 = jnp.float32
BF16 = jnp.bfloat16
U32 = jnp.uint32
I32 = jnp.int32

PACK_W = D_MODEL // 2
HI_MASK = 0xFFFF0000


def _sigmoid(z):
    return 0.5 * jnp.tanh(0.5 * z) + 0.5


def _pack_bf16_pair(lo, hi):
    lo_bits = lax.bitcast_convert_type(lo.astype(BF16).astype(F32), U32) >> 16
    hi_bits = lax.bitcast_convert_type(hi.astype(BF16).astype(F32), U32) & jnp.uint32(HI_MASK)
    return hi_bits | lo_bits


def _unpack_bf16_pair(w):
    lo = lax.bitcast_convert_type(w << 16, F32).astype(BF16)
    hi = lax.bitcast_convert_type(w & jnp.uint32(HI_MASK), F32).astype(BF16)
    return lo, hi


def _params(n_axes, vmem=VMEM_LIMIT):
    return pltpu.CompilerParams(
        dimension_semantics=("arbitrary",) * n_axes, vmem_limit_bytes=vmem)


NORM_TM = 512
IN_TM = 1024
IN_TN = 1536


def _rmsnorm_kernel(x_ref, g_ref, o_ref):
    x = x_ref[...]
    ms = jnp.mean(x * x, axis=-1, keepdims=True)
    o_ref[...] = ((x * lax.rsqrt(ms + NORM_EPS)) * g_ref[...]).astype(BF16)


def _rmsnorm_bf16(x2, norm_g):
    s = x2.shape[0]
    row = pl.BlockSpec((NORM_TM, D_MODEL), lambda i: (i, 0))
    return pl.pallas_call(
        _rmsnorm_kernel,
        grid=(s // NORM_TM,),
        in_specs=[row, pl.BlockSpec((1, D_MODEL), lambda i: (0, 0))],
        out_specs=row,
        out_shape=jax.ShapeDtypeStruct((s, D_MODEL), BF16),
        compiler_params=_params(1),
        name="rmsnorm_in",
    )(x2, norm_g)


def _inproj_kernel(h_ref, w_ref, bg_ref, o_ref, wbf_ref):
    @pl.when(pl.program_id(1) == 0)
    def _():
        wbf_ref[...] = w_ref[...].astype(BF16)

    acc = jnp.dot(h_ref[...], wbf_ref[...], preferred_element_type=F32) + bg_ref[...]
    for c in range(IN_TN // CB):
        o_ref[c] = acc[:, c * CB:(c + 1) * CB].astype(BF16)


def _in_projection(x2, norm_g, w_in, b_gate_ext):
    s = x2.shape[0]
    h = _rmsnorm_bf16(x2, norm_g)
    grid = (D_IN // IN_TN, s // IN_TM)
    return pl.pallas_call(
        _inproj_kernel,
        grid=grid,
        in_specs=[
            pl.BlockSpec((IN_TM, D_MODEL), lambda j, i: (i, 0)),
            pl.BlockSpec((D_MODEL, IN_TN), lambda j, i: (0, j)),
            pl.BlockSpec((1, IN_TN), lambda j, i: (0, j)),
        ],
        out_specs=pl.BlockSpec((IN_TN // CB, IN_TM, CB), lambda j, i: (j, i, 0)),
        out_shape=jax.ShapeDtypeStruct((N_CB, s, CB), BF16),
        scratch_shapes=[pltpu.VMEM((D_MODEL, IN_TN), BF16)],
        compiler_params=_params(2),
        name="in_projection",
    )(h, w_in, b_gate_ext)


CONV_TS = 512
CONV_HALO = 32
CONV_RC = 32
CONV_N = CONV_TS + CONV_HALO


def _conv_kernel(ac_ref, gc_ref, ah_ref, gh_ref, w_ref, cb_ref, lg_ref, lb_ref, o_ref, r_ref,
                 c_ref):
    i = pl.program_id(0)
    has_prev = i > 0
    half = D_CONV // 2
    for c in range(2):
        cs = slice(c * half, (c + 1) * half)
        uh = ah_ref[c].astype(F32) * _sigmoid(gh_ref[c].astype(F32))
        r_ref[0, 0:CONV_HALO, cs] = jnp.where(has_prev, uh, 0.0)
        r_ref[0, CONV_HALO:CONV_N, cs] = ac_ref[c].astype(F32) * _sigmoid(gc_ref[c].astype(F32))
    u_ext = r_ref[0]
    for b in range(1, 8):
        r_ref[b] = pltpu.roll(u_ext, CONV_N - b, axis=0)

    def chunk(ci, carry):
        r0 = pl.multiple_of(ci * CONV_RC, CONV_RC)
        acc = jnp.broadcast_to(cb_ref[...], (CONV_RC, D_CONV))
        for k in range(CONV_WIDTH):
            kp = k + (CONV_HALO - (CONV_WIDTH - 1))
            a, b = kp // 8, kp % 8
            acc = acc + w_ref[k:k + 1, :] * r_ref[b, pl.ds(r0 + 8 * a, CONV_RC), :]
        c_ref[pl.ds(r0, CONV_RC), :] = acc
        return carry

    lax.fori_loop(0, CONV_TS // CONV_RC, chunk, 0)
    conv = c_ref[...]
    mu = jnp.mean(conv, axis=-1, keepdims=True)
    d = conv - mu
    var = jnp.mean(d * d, axis=-1, keepdims=True)
    y = (d * lax.rsqrt(var + NORM_EPS)) * lg_ref[...] + lb_ref[...]
    o_ref[...] = (y * _sigmoid(y)).astype(BF16)


def _conv_branch(proj3, conv_w, conv_b, ln_g, ln_b):
    s = proj3.shape[1]
    hb = CONV_TS // CONV_HALO
    cur = lambda blk: pl.BlockSpec((2, CONV_TS, CB), lambda i: (blk // 2, i, 0))
    halo = lambda blk: pl.BlockSpec(
        (2, CONV_HALO, CB), lambda i: (blk // 2, jnp.maximum(i * hb - 1, 0), 0))
    vec = pl.BlockSpec((1, D_CONV), lambda i: (0, 0))
    return pl.pallas_call(
        _conv_kernel,
        grid=(s // CONV_TS,),
        in_specs=[cur(CB_CONV_A), cur(CB_CONV_G), halo(CB_CONV_A), halo(CB_CONV_G),
                  pl.BlockSpec((CONV_WIDTH, D_CONV), lambda i: (0, 0)), vec, vec, vec],
        out_specs=pl.BlockSpec((CONV_TS, D_CONV), lambda i: (i, 0)),
        out_shape=jax.ShapeDtypeStruct((s, D_CONV), BF16),
        scratch_shapes=[pltpu.VMEM((8, CONV_N, D_CONV), F32), pltpu.VMEM((CONV_TS, D_CONV), F32)],
        compiler_params=_params(1),
        name="conv_branch",
    )(proj3, proj3, proj3, proj3, conv_w, conv_b, ln_g, ln_b)


LSE_W = 128
LSE_REP = LSE_W // HEADS_PER_GROUP


def _attn_rows(q, k, v, n):
    blk = ATTN_BLOCK
    scale = HEAD_DIM ** -0.5
    qi = lax.broadcasted_iota(I32, (blk, 2 * blk), 0)
    ki = lax.broadcasted_iota(I32, (blk, 2 * blk), 1)
    band = (ki >= qi) & (ki <= qi + blk)
    lane_head = lax.broadcasted_iota(I32, (blk, LSE_W), 1) // LSE_REP
    o_rows, lse_rows = [], []
    for j in range(q.shape[0] // blk):
        mask = band
        if j == 0:
            mask = band & ((ki >= blk) | (n > 0))
        o_heads = []
        lse_tile = jnp.zeros((blk, LSE_W), F32)
        for hh in range(HEADS_PER_GROUP):
            cs = slice(hh * HEAD_DIM, (hh + 1) * HEAD_DIM)
            qh = q[j * blk:(j + 1) * blk, cs]
            kh = k[j * blk:(j + 2) * blk, cs]
            vh = v[j * blk:(j + 2) * blk, cs]
            s = lax.dot_general(qh, kh, (((1,), (1,)), ((), ())),
                                preferred_element_type=F32) * scale
            s = jnp.where(mask, s, -jnp.inf)
            m = jnp.max(s, axis=-1, keepdims=True)
            p = jnp.exp(s - m)
            l = jnp.sum(p, axis=-1, keepdims=True)
            o_heads.append(jnp.dot(p.astype(BF16), vh, preferred_element_type=F32) / l)
            lse_tile = jnp.where(lane_head == hh, m + jnp.log(l), lse_tile)
        o_rows.append(jnp.concatenate(o_heads, axis=1))
        lse_rows.append(lse_tile)
    return jnp.concatenate(o_rows, axis=0), jnp.concatenate(lse_rows, axis=0)


def _attn_kernel(q_ref, kc_ref, kp_ref, vc_ref, vp_ref, o_ref, lse_ref):
    n = pl.program_id(0)
    k = jnp.concatenate([kp_ref[...], kc_ref[...]], axis=0)
    v = jnp.concatenate([vp_ref[...], vc_ref[...]], axis=0)
    o, lse = _attn_rows(q_ref[...], k, v, n)
    o_ref[...] = o.astype(BF16)
    lse_ref[...] = lse


ATTN_TILE = 2048


def _attn_dilated_kernel(q_ref, kc_ref, kp_ref, vc_ref, vp_ref, o_ref, lse_ref,
                         qs, ks, vs, os_, ls, *, dil):
    n = pl.program_id(0)
    tile = q_ref.shape[0]
    mb = tile // dil
    half = dil // 2
    halo_w = ATTN_BLOCK * half
    n_slab = CB // 128
    q32 = pltpu.bitcast(q_ref[...], U32)
    kp32, kc32 = pltpu.bitcast(kp_ref[...], U32), pltpu.bitcast(kc_ref[...], U32)
    vp32, vc32 = pltpu.bitcast(vp_ref[...], U32), pltpu.bitcast(vc_ref[...], U32)
    for c in range(n_slab):
        cs = slice(c * 128, (c + 1) * 128)
        qs[c] = q32[:, cs]
        ks[c, 0:halo_w] = kp32[:, cs]
        ks[c, halo_w:] = kc32[:, cs]
        vs[c, 0:halo_w] = vp32[:, cs]
        vs[c, halo_w:] = vc32[:, cs]

    def gather(slabs, rp, rows):
        return jnp.concatenate(
            [slabs[c, pl.ds(rp, rows, stride=half), :] for c in range(n_slab)], axis=1)

    def residue(w, par):
        bits = (w << 16) if par == 0 else (w & jnp.uint32(HI_MASK))
        return lax.bitcast_convert_type(bits, F32).astype(BF16)

    for rp in range(half):
        wq = gather(qs, rp, mb)
        wk = gather(ks, rp, mb + ATTN_BLOCK)
        wv = gather(vs, rp, mb + ATTN_BLOCK)
        outs = []
        for par in range(2):
            o, lse = _attn_rows(residue(wq, par), residue(wk, par), residue(wv, par), n)
            outs.append(o)
            ls[pl.ds(2 * rp + par, mb, stride=dil), :] = lse
        w_out = _pack_bf16_pair(outs[0], outs[1])
        for c in range(n_slab):
            os_[c, pl.ds(rp, mb, stride=half), :] = w_out[:, c * 128:(c + 1) * 128]
    o32 = jnp.concatenate([os_[c] for c in range(n_slab)], axis=1)
    o_ref[...] = pltpu.bitcast(o32, BF16)
    lse_ref[...] = ls[...]


def _dilated_attention(proj3, group):
    _, dil = ATTN_PATTERNS[group]
    s = proj3.shape[1]
    cbs = (CB_Q + group, CB_K + group, CB_V + group)
    if dil == 1:
        tile, halo = 1024, ATTN_BLOCK
        body, scratch = _attn_kernel, []
    else:
        tile, halo = ATTN_TILE, ATTN_BLOCK * dil
        body = functools.partial(_attn_dilated_kernel, dil=dil)
        scratch = [pltpu.VMEM((CB // 128, tile // 2, 128), U32),
                   pltpu.VMEM((CB // 128, (tile + halo) // 2, 128), U32),
                   pltpu.VMEM((CB // 128, (tile + halo) // 2, 128), U32),
                   pltpu.VMEM((CB // 128, tile // 2, 128), U32),
                   pltpu.VMEM((tile, LSE_W), F32)]
    hb = tile // halo
    cur = lambda cb: pl.BlockSpec((None, tile, CB), lambda n: (cb, n, 0))
    prev = lambda cb: pl.BlockSpec((None, halo, CB), lambda n: (cb, jnp.maximum(n * hb - 1, 0), 0))
    return pl.pallas_call(
        body,
        grid=(s // tile,),
        in_specs=[cur(cbs[0]), cur(cbs[1]), prev(cbs[1]), cur(cbs[2]), prev(cbs[2])],
        out_specs=[pl.BlockSpec((tile, CB), lambda n: (n, 0)),
                   pl.BlockSpec((tile, LSE_W), lambda n: (n, 0))],
        out_shape=[jax.ShapeDtypeStruct((s, CB), BF16),
                   jax.ShapeDtypeStruct((s, LSE_W), F32)],
        scratch_shapes=scratch,
        compiler_params=_params(1),
        name=f"dilated_attention_g{group}",
    )(proj3, proj3, proj3, proj3, proj3)


MIX_TM = 256


def _mixer_out_kernel(x_ref, o0_ref, o1_ref, o2_ref, l0_ref, l1_ref, l2_ref, u_ref,
                      gc0, gc1, gc2, gc3, ga0, ga1, ga2, ga3,
                      wa_ref, wc_ref, bc_ref, wo_ref, g2_ref, wrh_ref, wrl_ref, br_ref,
                      x1_ref, hp_ref, logit_ref):
    l_refs = (l0_ref, l1_ref, l2_ref)
    o_refs = (o0_ref, o1_ref, o2_ref)
    parts = []
    for hh in range(HEADS_PER_GROUP):
        cs = slice(hh * HEAD_DIM, (hh + 1) * HEAD_DIM)
        lse = [r[:, hh * LSE_REP:hh * LSE_REP + 1] for r in l_refs]
        m = jnp.maximum(jnp.maximum(lse[0], lse[1]), lse[2])
        e = [jnp.exp(v - m) for v in lse]
        den = e[0] + e[1] + e[2]
        acc = (e[0] / den) * o_refs[0][:, cs].astype(F32)
        for g in range(1, N_GROUPS):
            acc = acc + (e[g] / den) * o_refs[g][:, cs].astype(F32)
        parts.append(acc)
    o = jnp.concatenate(parts, axis=1).astype(BF16)
    ya = jnp.dot(o, wa_ref[...], preferred_element_type=F32)
    yc = jnp.dot(u_ref[...], wc_ref[...], preferred_element_type=F32) + bc_ref[...]
    gcs = (gc0, gc1, gc2, gc3)
    gas = (ga0, ga1, ga2, ga3)
    merged = []
    for c in range(D_MODEL // CB):
        cs = slice(c * CB, (c + 1) * CB)
        merged.append((_sigmoid(gcs[c][...].astype(F32)) * yc[:, cs]
                       + _sigmoid(gas[c][...].astype(F32)) * ya[:, cs]).astype(BF16))
    merged = jnp.concatenate(merged, axis=1)
    x1 = x_ref[...] + jnp.dot(merged, wo_ref[...], preferred_element_type=F32)
    x1_ref[...] = x1
    ms = jnp.mean(x1 * x1, axis=-1, keepdims=True)
    h2 = (x1 * lax.rsqrt(ms + NORM_EPS)) * g2_ref[...]
    h2_hi = h2.astype(BF16)
    hp_ref[...] = _pack_bf16_pair(h2[:, 0:PACK_W], h2[:, PACK_W:D_MODEL])
    h2_lo = (h2 - h2_hi.astype(F32)).astype(BF16)
    hi_pass = jnp.dot(h2_hi, wrl_ref[...], preferred_element_type=F32)
    logits = (hi_pass[:, 0:N_EXPERTS] + hi_pass[:, N_EXPERTS:2 * N_EXPERTS]
              + jnp.dot(h2_lo, wrh_ref[...], preferred_element_type=F32)) + br_ref[...]
    logit_ref[...] = logits


def _mixer_output(x2, outs, lses, u_ln, proj3, wa, wc, bc, wo, g2, wr_hi, wr_lo, br):
    s = x2.shape[0]
    row = lambda w: pl.BlockSpec((MIX_TM, w), lambda i: (i, 0))
    gate = lambda cb: pl.BlockSpec((None, MIX_TM, CB), lambda i: (cb, i, 0))
    full = lambda a: pl.BlockSpec(a.shape, lambda i: (0,) * a.ndim)
    in_specs = ([row(D_MODEL)] + [row(CB)] * 3 + [row(LSE_W)] * 3 + [row(D_CONV)]
                + [gate(CB_GATE_C + c) for c in range(4)]
                + [gate(CB_GATE_A + c) for c in range(4)]
                + [full(a) for a in (wa, wc, bc, wo, g2, wr_hi, wr_lo, br)])
    return pl.pallas_call(
        _mixer_out_kernel,
        grid=(s // MIX_TM,),
        in_specs=in_specs,
        out_specs=[row(D_MODEL), row(PACK_W), row(N_EXPERTS)],
        out_shape=[jax.ShapeDtypeStruct((s, D_MODEL), F32),
                   jax.ShapeDtypeStruct((s, PACK_W), U32),
                   jax.ShapeDtypeStruct((s, N_EXPERTS), F32)],
        compiler_params=_params(1),
        name="mixer_output",
    )(x2, *outs, *lses, u_ln, *([proj3] * 8), wa, wc, bc, wo, g2, wr_hi, wr_lo, br)


ROUTE_TM = 1024


def _route_kernel(logit_ref, idx_ref, gate_ref, pos_ref, cnt_ref, carry_ref):
    i = pl.program_id(0)
    tm = ROUTE_TM

    @pl.when(i == 0)
    def _():
        carry_ref[...] = jnp.zeros_like(carry_ref)

    logits = logit_ref[...]
    e_iota = lax.broadcasted_iota(I32, (tm, N_EXPERTS), 1).astype(F32)
    k_lane = lax.broadcasted_iota(I32, (tm, TOP_K), 1)
    vals = logits
    sels = []
    idx_t = jnp.zeros((tm, TOP_K), F32)
    val_t = jnp.zeros((tm, TOP_K), F32)
    for k in range(TOP_K):
        mk = jnp.max(vals, axis=-1, keepdims=True)
        ik = jnp.min(jnp.where(vals == mk, e_iota, float(N_EXPERTS)), axis=-1, keepdims=True)
        sel = e_iota == ik
        sels.append(sel)
        vals = jnp.where(sel, -jnp.inf, vals)
        idx_t = jnp.where(k_lane == k, ik, idx_t)
        val_t = jnp.where(k_lane == k, mk, val_t)
    ex = jnp.exp(val_t - val_t[:, 0:1])
    gate_ref[...] = ex / jnp.sum(ex, axis=-1, keepdims=True)
    idx_ref[...] = idx_t.astype(I32)

    cnt = jnp.zeros((tm, N_EXPERTS), F32)
    for sel in sels:
        cnt = cnt + jnp.where(sel, 1.0, 0.0)
    ri = lax.broadcasted_iota(I32, (tm, tm), 0)
    ci = lax.broadcasted_iota(I32, (tm, tm), 1)
    lower = jnp.where(ci < ri, 1.0, 0.0).astype(BF16)
    prefix = jnp.dot(lower, cnt.astype(BF16), preferred_element_type=F32) + carry_ref[...]
    pos_t = jnp.zeros((tm, TOP_K), F32)
    for k, sel in enumerate(sels):
        pk = jnp.sum(jnp.where(sel, prefix, 0.0), axis=-1, keepdims=True)
        pos_t = jnp.where(k_lane == k, pk, pos_t)
    pos_ref[...] = pos_t.astype(I32)
    carry_ref[...] = carry_ref[...] + jnp.sum(cnt, axis=0, keepdims=True)
    cnt_ref[...] = carry_ref[...]


def _route(logits):
    s = logits.shape[0]
    row = lambda w: pl.BlockSpec((ROUTE_TM, w), lambda i: (i, 0))
    return pl.pallas_call(
        _route_kernel,
        grid=(s // ROUTE_TM,),
        in_specs=[row(N_EXPERTS)],
        out_specs=[row(TOP_K), row(TOP_K), row(TOP_K),
                   pl.BlockSpec((1, N_EXPERTS), lambda i: (0, 0))],
        out_shape=[jax.ShapeDtypeStruct((s, TOP_K), I32),
                   jax.ShapeDtypeStruct((s, TOP_K), F32),
                   jax.ShapeDtypeStruct((s, TOP_K), I32),
                   jax.ShapeDtypeStruct((1, N_EXPERTS), F32)],
        scratch_shapes=[pltpu.VMEM((1, N_EXPERTS), F32)],
        compiler_params=_params(1),
        name="moe_route",
    )(logits)


ROW_BLK = 128
DISP_TM = 2048


def _dispatch_kernel(fill_ref, dest_ref, h_ref, xs_ref, zbuf, sem):
    i = pl.program_id(0)
    n_fill = fill_ref.shape[0]

    @pl.when(i == 0)
    def _():
        zbuf[...] = jnp.zeros_like(zbuf)

        def fill_copy(j):
            return pltpu.make_async_copy(
                zbuf, xs_ref.at[pl.ds(fill_ref[j] * ROW_BLK, ROW_BLK)], sem.at[0])

        def start(j, c):
            @pl.when(fill_ref[j] >= 0)
            def _():
                fill_copy(j).start()
            return c

        def wait(j, c):
            @pl.when(fill_ref[j] >= 0)
            def _():
                fill_copy(j).wait()
            return c

        lax.fori_loop(0, n_fill, start, 0)
        lax.fori_loop(0, n_fill, wait, 0)

    def rows8(tt, c):
        for si in range(8):
            for k in range(TOP_K):
                d = dest_ref[0, (tt * 8 + si) * TOP_K + k]
                pltpu.make_async_copy(h_ref.at[tt, pl.ds(si, 1)], xs_ref.at[pl.ds(d, 1)],
                                      sem.at[0]).start(priority=k % 2)
        return c

    lax.fori_loop(0, DISP_TM // 8, rows8, 0)
    for _ in range(TOP_K):
        pltpu.make_async_copy(h_ref, h_ref, sem.at[0]).wait()


def _dispatch(hp, dest, fill_blocks, n_rows):
    s = hp.shape[0]
    dest3 = dest.reshape(s // DISP_TM, 1, DISP_TM * TOP_K)
    grid_spec = pltpu.PrefetchScalarGridSpec(
        num_scalar_prefetch=1,
        grid=(s // DISP_TM,),
        in_specs=[
            pl.BlockSpec((None, 1, DISP_TM * TOP_K), lambda i, fr: (i, 0, 0),
                         memory_space=pltpu.SMEM),
            pl.BlockSpec((DISP_TM // 8, 8, PACK_W), lambda i, fr: (i, 0, 0)),
        ],
        out_specs=pl.BlockSpec(memory_space=pl.ANY),
        scratch_shapes=[pltpu.VMEM((ROW_BLK, PACK_W), U32), pltpu.SemaphoreType.DMA((1,))],
    )
    return pl.pallas_call(
        _dispatch_kernel,
        grid_spec=grid_spec,
        out_shape=jax.ShapeDtypeStruct((n_rows, PACK_W), U32),
        compiler_params=_params(1),
        name="moe_dispatch",
    )(fill_blocks, dest3, hp.reshape(s // 8, 8, PACK_W))


UNIT_BLOCKS = 10
UNIT_ROWS = ROW_BLK * UNIT_BLOCKS
GROUP_SIZES = (8, 4, 2, 1)
EXP_TF = 512
EXP_NF = D_EXPERT // EXP_TF


def _expert_kernel(ue_ref, ub0_ref, unb_ref, nu_ref,
                   xs_ref, win_ref, wout_ref, bin_ref, bo_ref,
                   ys_ref,
                   xbuf, xb16, acc, wg_buf, wu_buf, wo_buf, sem_x, sem_y, sem_w):
    u = pl.program_id(0)
    n_units = nu_ref[0]

    def x_copy(unit, b):
        src = xs_ref.at[pl.ds((ub0_ref[unit] + b) * ROW_BLK, ROW_BLK)]
        dst = xbuf.at[pl.ds(b * ROW_BLK, ROW_BLK)]
        return pltpu.make_async_copy(src, dst, sem_x.at[0])

    def y_copy(unit, b):
        src = acc.at[pl.ds(b * ROW_BLK, ROW_BLK)]
        dst = ys_ref.at[pl.ds((ub0_ref[unit] + b) * ROW_BLK, ROW_BLK)]
        return pltpu.make_async_copy(src, dst, sem_y.at[0])

    def w_copies(unit, f, slot):
        e = ue_ref[unit]
        c0 = pl.multiple_of(f * EXP_TF, EXP_TF)
        return (
            pltpu.make_async_copy(win_ref.at[e, :, pl.ds(c0, EXP_TF)], wg_buf.at[slot],
                                  sem_w.at[slot]),
            pltpu.make_async_copy(win_ref.at[e, :, pl.ds(D_EXPERT + c0, EXP_TF)], wu_buf.at[slot],
                                  sem_w.at[slot]),
            pltpu.make_async_copy(wout_ref.at[e, pl.ds(c0, EXP_TF), :], wo_buf.at[slot],
                                  sem_w.at[slot]),
        )

    def start_w(unit, f, slot):
        for cp in w_copies(unit, f, slot):
            cp.start()

    def wait_w(unit, f, slot):
        for cp in w_copies(unit, f, slot):
            cp.wait()

    def start_x(unit):
        lax.fori_loop(0, unb_ref[unit], lambda b, c: (x_copy(unit, b).start(), c)[1], 0)

    def wait_x(unit):
        lax.fori_loop(0, unb_ref[unit], lambda b, c: (x_copy(unit, b).wait(), c)[1], 0)

    def wait_y(unit):
        lax.fori_loop(0, unb_ref[unit], lambda b, c: (y_copy(unit, b).wait(), c)[1], 0)

    @pl.when(u < n_units)
    def _():
        nblk = unb_ref[u]

        @pl.when(u == 0)
        def _():
            start_x(u)
            start_w(u, 0, 0)
            acc[...] = jnp.zeros_like(acc)

        wait_x(u)

        def unpack(b, c):
            r0 = pl.multiple_of(b * ROW_BLK, ROW_BLK)
            lo, hi = _unpack_bf16_pair(xbuf[pl.ds(r0, ROW_BLK), :])
            xb16[pl.ds(r0, ROW_BLK), 0:PACK_W] = lo
            xb16[pl.ds(r0, ROW_BLK), PACK_W:D_MODEL] = hi
            return c

        lax.fori_loop(0, nblk, unpack, 0)

        @pl.when(u + 1 < n_units)
        def _():
            start_x(u + 1)

        @pl.when(u > 0)
        def _():
            wait_y(u - 1)

        def group(b0, n_blk, f, ws):
            rows = n_blk * ROW_BLK
            r0 = pl.multiple_of(b0 * ROW_BLK, ROW_BLK)
            c0 = pl.multiple_of(f * EXP_TF, EXP_TF)
            x = xb16[pl.ds(r0, rows), :]
            wgu = jnp.concatenate([wg_buf[ws].astype(BF16), wu_buf[ws].astype(BF16)], axis=1)
            gu = jnp.dot(x, wgu, preferred_element_type=F32)
            bg = bin_ref[:, pl.ds(c0, EXP_TF)]
            bu = bin_ref[:, pl.ds(D_EXPERT + c0, EXP_TF)]
            g = jnp.minimum(gu[:, 0:EXP_TF] + bg, SWIGLU_LIMIT)
            up = jnp.clip(gu[:, EXP_TF:2 * EXP_TF] + bu, -SWIGLU_LIMIT, SWIGLU_LIMIT)
            act = (up + 1.0) * (g * _sigmoid(SWIGLU_ALPHA * g))
            start = jnp.where(f == 0, jnp.broadcast_to(bo_ref[...], (rows, D_MODEL)),
                              acc[pl.ds(r0, rows), :])
            acc[pl.ds(r0, rows), :] = start + jnp.dot(
                act.astype(BF16), wo_buf[ws].astype(BF16), preferred_element_type=F32)

            @pl.when(f == EXP_NF - 1)
            def _():
                for j in range(n_blk):
                    y_copy(u, b0 + j).start()

        def f_tile(f, ws):
            big = GROUP_SIZES[0]
            n_big = nblk // big
            lax.fori_loop(0, n_big, lambda q, c: (group(big * q, big, f, ws), c)[1], 0)
            for size in GROUP_SIZES[1:]:
                done = (nblk // (2 * size)) * (2 * size)

                @pl.when(lax.rem(nblk, 2 * size) >= size)
                def _():
                    group(done, size, f, ws)

        def f_step(f, c):
            ws = lax.rem(f, 2)

            @pl.when(f + 1 < EXP_NF)
            def _():
                start_w(u, f + 1, 1 - ws)

            @pl.when((f + 1 == EXP_NF) & (u + 1 < n_units))
            def _():
                start_w(u + 1, 0, 1 - ws)

            wait_w(u, f, ws)
            f_tile(f, ws)
            return c

        lax.fori_loop(0, EXP_NF, f_step, 0)

        @pl.when(u == n_units - 1)
        def _():
            wait_y(u)

    @pl.when(u == pl.num_programs(0) - 1)
    def _():
        n_blocks = ys_ref.shape[0] // ROW_BLK
        acc[0:ROW_BLK, :] = jnp.zeros((ROW_BLK, D_MODEL), F32)

        def tail_copy(b):
            return pltpu.make_async_copy(acc.at[pl.ds(0, ROW_BLK)],
                                         ys_ref.at[pl.ds(b * ROW_BLK, ROW_BLK)], sem_y.at[0])

        lax.fori_loop(nu_ref[1], n_blocks, lambda b, c: (tail_copy(b).start(), c)[1], 0)
        lax.fori_loop(nu_ref[1], n_blocks, lambda b, c: (tail_copy(b).wait(), c)[1], 0)


def _expert_ffn(xs, unit_e, unit_blk0, unit_nblk, unit_counts, w_in, b_in, w_out, b_out):
    n_rows = xs.shape[0]
    n_units_max = unit_e.shape[0]

    def e_eff(u, ue, nu):
        return ue[jnp.minimum(u, jnp.maximum(nu[0] - 1, 0))]

    any_spec = pl.BlockSpec(memory_space=pl.ANY)
    in_specs = [
        any_spec, any_spec, any_spec,
        pl.BlockSpec((None, 1, 2 * D_EXPERT), lambda u, ue, ub, un, nu: (e_eff(u, ue, nu), 0, 0)),
        pl.BlockSpec((None, 1, D_MODEL), lambda u, ue, ub, un, nu: (e_eff(u, ue, nu), 0, 0)),
    ]
    grid_spec = pltpu.PrefetchScalarGridSpec(
        num_scalar_prefetch=4,
        grid=(n_units_max,),
        in_specs=in_specs,
        out_specs=any_spec,
        scratch_shapes=[
            pltpu.VMEM((UNIT_ROWS, PACK_W), U32),
            pltpu.VMEM((UNIT_ROWS, D_MODEL), BF16),
            pltpu.VMEM((UNIT_ROWS, D_MODEL), F32),
            pltpu.VMEM((2, D_MODEL, EXP_TF), F32),
            pltpu.VMEM((2, D_MODEL, EXP_TF), F32),
            pltpu.VMEM((2, EXP_TF, D_MODEL), F32),
            pltpu.SemaphoreType.DMA((1,)),
            pltpu.SemaphoreType.DMA((1,)),
            pltpu.SemaphoreType.DMA((2,)),
        ],
    )
    return pl.pallas_call(
        _expert_kernel,
        grid_spec=grid_spec,
        out_shape=jax.ShapeDtypeStruct((n_rows, D_MODEL), F32),
        compiler_params=_params(1),
        name="expert_ffn",
    )(unit_e, unit_blk0, unit_nblk, unit_counts, xs,
      w_in, w_out, b_in[:, None, :], b_out[:, None, :])


COMB_TM = 256


def _combine_kernel(dcur_ref, dnxt_ref, x1_ref, gate_ref, g_ref, ys_ref, o_ref, ybuf, sem):
    i = pl.program_id(0)
    n = pl.num_programs(0)
    slot = lax.rem(i, 2)

    def issue(dref, s):
        def rows8(tt, c):
            for si in range(8):
                for k in range(TOP_K):
                    d = dref[0, (tt * 8 + si) * TOP_K + k]
                    pltpu.make_async_copy(ys_ref.at[pl.ds(d, 1)],
                                          ybuf.at[s, k, tt, pl.ds(si, 1)],
                                          sem.at[s]).start(priority=k % 2)
            return c

        lax.fori_loop(0, COMB_TM // 8, rows8, 0)

    @pl.when(i == 0)
    def _():
        issue(dcur_ref, slot)

    @pl.when(i + 1 < n)
    def _():
        issue(dnxt_ref, 1 - slot)

    for k in range(TOP_K):
        pltpu.make_async_copy(ybuf.at[slot, k], ybuf.at[slot, k], sem.at[slot]).wait()

    gate = gate_ref[...]
    y = x1_ref[...]
    for k in range(TOP_K):
        y = y + gate[:, k:k + 1] * ybuf[slot, k].reshape(COMB_TM, D_MODEL)
    ms = jnp.mean(y * y, axis=-1, keepdims=True)
    o_ref[...] = (y * lax.rsqrt(ms + NORM_EPS)) * g_ref[...]


def _combine(dest, x1, gate, norm_g, ys):
    s = x1.shape[0]
    nt = s // COMB_TM
    dest3 = dest.reshape(nt, 1, COMB_TM * TOP_K)
    smem = lambda imap: pl.BlockSpec((None, 1, COMB_TM * TOP_K), imap, memory_space=pltpu.SMEM)
    row = lambda w: pl.BlockSpec((COMB_TM, w), lambda i: (i, 0))
    return pl.pallas_call(
        _combine_kernel,
        grid=(nt,),
        in_specs=[smem(lambda i: (i, 0, 0)),
                  smem(lambda i: (jnp.minimum(i + 1, nt - 1), 0, 0)),
                  row(D_MODEL), row(TOP_K), pl.BlockSpec((1, D_MODEL), lambda i: (0, 0)),
                  pl.BlockSpec(memory_space=pl.ANY)],
        out_specs=row(D_MODEL),
        out_shape=jax.ShapeDtypeStruct((s, D_MODEL), F32),
        scratch_shapes=[pltpu.VMEM((2, TOP_K, COMB_TM // 8, 8, D_MODEL), F32),
                        pltpu.SemaphoreType.DMA((2,))],
        compiler_params=_params(1),
        name="moe_combine",
    )(dest3, dest3, x1, gate, norm_g, ys)


def _route_tables(idx, pos, counts, n_tok):
    n_assign = n_tok * TOP_K
    counts = counts.reshape(N_EXPERTS).astype(I32)
    nblk_e = (counts + ROW_BLK - 1) // ROW_BLK
    pad_start = (jnp.cumsum(nblk_e) - nblk_e) * ROW_BLK
    onehot = idx[..., None] == jnp.arange(N_EXPERTS, dtype=I32)
    dest = pos + jnp.sum(jnp.where(onehot, pad_start, 0), axis=-1)

    n_blocks = -(-(n_assign + N_EXPERTS * (ROW_BLK - 1)) // ROW_BLK)
    used = jnp.sum(nblk_e)
    blk_ids = jnp.arange(n_blocks, dtype=I32)
    last_blk = pad_start // ROW_BLK + nblk_e - 1
    fill_blocks = jnp.concatenate([jnp.where(nblk_e > 0, last_blk, -1),
                                   jnp.where(blk_ids >= used, blk_ids, -1)]).astype(I32)

    n_units_max = (n_blocks + (UNIT_BLOCKS - 1) * N_EXPERTS) // UNIT_BLOCKS
    chunks_e = (nblk_e + UNIT_BLOCKS - 1) // UNIT_BLOCKS
    chunk_end = jnp.cumsum(chunks_e)
    chunk_start = chunk_end - chunks_e
    n_units = chunk_end[-1]
    uid = jnp.arange(n_units_max, dtype=I32)
    unit_e = jnp.minimum(jnp.searchsorted(chunk_end, uid, side="right"), N_EXPERTS - 1).astype(I32)
    c_in_e = uid - chunk_start[unit_e]
    unit_blk0 = (pad_start[unit_e] // ROW_BLK + c_in_e * UNIT_BLOCKS).astype(I32)
    unit_nblk = jnp.clip(nblk_e[unit_e] - c_in_e * UNIT_BLOCKS, 0, UNIT_BLOCKS).astype(I32)
    active = uid < n_units
    unit_blk0 = jnp.where(active, unit_blk0, 0)
    unit_nblk = jnp.where(active, unit_nblk, 0)
    unit_counts = jnp.stack([n_units, used]).astype(I32)
    return dest.astype(I32), fill_blocks, n_blocks * ROW_BLK, unit_e, unit_blk0, unit_nblk, unit_counts


def kernel(x, norm1_g, w_in, b_gate, conv_w, conv_b, conv_ln_g, conv_ln_b, w_conv_out, b_conv_out,
           w_attn_out, w_out, norm2_g, w_router, b_router, w_exp_in, b_exp_in, w_exp_out, b_exp_out,
           norm_f_g):
    b_sz, s_len, d = x.shape
    n_tok = b_sz * s_len
    x2 = x.reshape(n_tok, d)
    l = 0
    b_gate_ext = jnp.concatenate([jnp.zeros((GATE_COL0,), F32), b_gate[l]])[None, :]
    proj3 = _in_projection(x2, norm1_g[l][None, :], w_in[l], b_gate_ext)
    u_ln = _conv_branch(proj3, conv_w[l], conv_b[l][None, :], conv_ln_g[l][None, :],
                        conv_ln_b[l][None, :])
    outs, lses = [], []
    for g in range(N_GROUPS):
        o_g, lse_g = _dilated_attention(proj3, g)
        outs.append(o_g)
        lses.append(lse_g)
    wr = w_router[l]
    wr_hi = wr.astype(BF16)
    wr_lo = jnp.concatenate([wr_hi, (wr - wr_hi.astype(F32)).astype(BF16)], axis=1)
    x1, hp, logits = _mixer_output(
        x2, outs, lses, u_ln, proj3,
        w_attn_out[l].astype(BF16), w_conv_out[l].astype(BF16), b_conv_out[l][None, :],
        w_out[l].astype(BF16), norm2_g[l][None, :], wr_hi, wr_lo, b_router[l][None, :])

    idx, gate, pos, counts = _route(logits)
    dest, fill_blocks, n_rows, unit_e, unit_blk0, unit_nblk, unit_counts = _route_tables(
        idx, pos, counts, n_tok)
    xs = _dispatch(hp, dest, fill_blocks, n_rows)
    ys = _expert_ffn(xs, unit_e, unit_blk0, unit_nblk, unit_counts,
                     w_exp_in[l], b_exp_in[l], w_exp_out[l], b_exp_out[l])
    out = _combine(dest, x1, gate, norm_f_g[None, :], ys)
    return out.reshape(b_sz, s_len, d)
```

```python
import functools

import jax
import jax.numpy as jnp
from jax import lax
from jax.experimental import pallas as pl
from jax.experimental.pallas import tpu as pltpu

D_MODEL = 2048
D_CONV = D_MODEL // 2
CONV_WIDTH = 31
HEAD_DIM = 128
HEADS_PER_GROUP = 4
ATTN_PATTERNS = ((128, 1), (512, 4), (2048, 16))
N_GROUPS = len(ATTN_PATTERNS)
ATTN_WIDTH = N_GROUPS * HEADS_PER_GROUP * HEAD_DIM
ATTN_BLOCK = 128
D_IN = 2 * D_CONV + 3 * ATTN_WIDTH + 2 * D_MODEL
N_EXPERTS = 32
TOP_K = 4
D_EXPERT = D_MODEL
SWIGLU_LIMIT = 7.0
SWIGLU_ALPHA = 1.702
NORM_EPS = 1e-5

CB = 512
N_CB = D_IN // CB
CB_CONV_A, CB_CONV_G = 0, 2
CB_Q, CB_K, CB_V = 4, 7, 10
CB_GATE_C, CB_GATE_A = 13, 17
GATE_COL0 = CB_GATE_C * CB

V7X_VMEM_BYTES = 64 * 1024 * 1024
VMEM_LIMIT = V7X_VMEM_BYTES - 8 * 1024 * 1024

F32 = jnp.float32
BF16 = jnp.bfloat16
U32 = jnp.uint32
I32 = jnp.int32

PACK_W = D_MODEL // 2
HI_MASK = 0xFFFF0000


def _sigmoid(z):
    return 0.5 * jnp.tanh(0.5 * z) + 0.5


def _pack_bf16_pair(lo, hi):
    lo_bits = lax.bitcast_convert_type(lo.astype(BF16).astype(F32), U32) >> 16
    hi_bits = lax.bitcast_convert_type(hi.astype(BF16).astype(F32), U32) & jnp.uint32(HI_MASK)
    return hi_bits | lo_bits


def _unpack_bf16_pair(w):
    lo = lax.bitcast_convert_type(w << 16, F32).astype(BF16)
    hi = lax.bitcast_convert_type(w & jnp.uint32(HI_MASK), F32).astype(BF16)
    return lo, hi


def _params(n_axes, vmem=VMEM_LIMIT):
    return pltpu.CompilerParams(
        dimension_semantics=("arbitrary",) * n_axes, vmem_limit_bytes=vmem)


NORM_TM = 512
IN_TM = 1024
IN_TN = 1536


def _rmsnorm_kernel(x_ref, g_ref, o_ref):
    x = x_ref[...]
    ms = jnp.mean(x * x, axis=-1, keepdims=True)
    o_ref[...] = ((x * lax.rsqrt(ms + NORM_EPS)) * g_ref[...]).astype(BF16)


def _rmsnorm_bf16(x2, norm_g):
    s = x2.shape[0]
    row = pl.BlockSpec((NORM_TM, D_MODEL), lambda i: (i, 0))
    return pl.pallas_call(
        _rmsnorm_kernel,
        grid=(s // NORM_TM,),
        in_specs=[row, pl.BlockSpec((1, D_MODEL), lambda i: (0, 0))],
        out_specs=row,
        out_shape=jax.ShapeDtypeStruct((s, D_MODEL), BF16),
        compiler_params=_params(1),
        name="rmsnorm_in",
    )(x2, norm_g)


def _inproj_kernel(h_ref, w_ref, bg_ref, o_ref, wbf_ref):
    @pl.when(pl.program_id(1) == 0)
    def _():
        wbf_ref[...] = w_ref[...].astype(BF16)

    acc = jnp.dot(h_ref[...], wbf_ref[...], preferred_element_type=F32) + bg_ref[...]
    for c in range(IN_TN // CB):
        o_ref[c] = acc[:, c * CB:(c + 1) * CB].astype(BF16)


def _in_projection(x2, norm_g, w_in, b_gate_ext):
    s = x2.shape[0]
    h = _rmsnorm_bf16(x2, norm_g)
    grid = (D_IN // IN_TN, s // IN_TM)
    return pl.pallas_call(
        _inproj_kernel,
        grid=grid,
        in_specs=[
            pl.BlockSpec((IN_TM, D_MODEL), lambda j, i: (i, 0)),
            pl.BlockSpec((D_MODEL, IN_TN), lambda j, i: (0, j)),
            pl.BlockSpec((1, IN_TN), lambda j, i: (0, j)),
        ],
        out_specs=pl.BlockSpec((IN_TN // CB, IN_TM, CB), lambda j, i: (j, i, 0)),
        out_shape=jax.ShapeDtypeStruct((N_CB, s, CB), BF16),
        scratch_shapes=[pltpu.VMEM((D_MODEL, IN_TN), BF16)],
        compiler_params=_params(2),
        name="in_projection",
    )(h, w_in, b_gate_ext)


CONV_TS = 512
CONV_HALO = 32
CONV_RC = 32
CONV_N = CONV_TS + CONV_HALO


def _conv_kernel(ac_ref, gc_ref, ah_ref, gh_ref, w_ref, cb_ref, lg_ref, lb_ref, o_ref, r_ref,
                 c_ref):
    i = pl.program_id(0)
    has_prev = i > 0
    half = D_CONV // 2
    for c in range(2):
        cs = slice(c * half, (c + 1) * half)
        uh = ah_ref[c].astype(F32) * _sigmoid(gh_ref[c].astype(F32))
        r_ref[0, 0:CONV_HALO, cs] = jnp.where(has_prev, uh, 0.0)
        r_ref[0, CONV_HALO:CONV_N, cs] = ac_ref[c].astype(F32) * _sigmoid(gc_ref[c].astype(F32))
    u_ext = r_ref[0]
    for b in range(1, 8):
        r_ref[b] = pltpu.roll(u_ext, CONV_N - b, axis=0)

    def chunk(ci, carry):
        r0 = pl.multiple_of(ci * CONV_RC, CONV_RC)
        acc = jnp.broadcast_to(cb_ref[...], (CONV_RC, D_CONV))
        for k in range(CONV_WIDTH):
            kp = k + (CONV_HALO - (CONV_WIDTH - 1))
            a, b = kp // 8, kp % 8
            acc = acc + w_ref[k:k + 1, :] * r_ref[b, pl.ds(r0 + 8 * a, CONV_RC), :]
        c_ref[pl.ds(r0, CONV_RC), :] = acc
        return carry

    lax.fori_loop(0, CONV_TS // CONV_RC, chunk, 0)
    conv = c_ref[...]
    mu = jnp.mean(conv, axis=-1, keepdims=True)
    d = conv - mu
    var = jnp.mean(d * d, axis=-1, keepdims=True)
    y = (d * lax.rsqrt(var + NORM_EPS)) * lg_ref[...] + lb_ref[...]
    o_ref[...] = (y * _sigmoid(y)).astype(BF16)


def _conv_branch(proj3, conv_w, conv_b, ln_g, ln_b):
    s = proj3.shape[1]
    hb = CONV_TS // CONV_HALO
    cur = lambda blk: pl.BlockSpec((2, CONV_TS, CB), lambda i: (blk // 2, i, 0))
    halo = lambda blk: pl.BlockSpec(
        (2, CONV_HALO, CB), lambda i: (blk // 2, jnp.maximum(i * hb - 1, 0), 0))
    vec = pl.BlockSpec((1, D_CONV), lambda i: (0, 0))
    return pl.pallas_call(
        _conv_kernel,
        grid=(s // CONV_TS,),
        in_specs=[cur(CB_CONV_A), cur(CB_CONV_G), halo(CB_CONV_A), halo(CB_CONV_G),
                  pl.BlockSpec((CONV_WIDTH, D_CONV), lambda i: (0, 0)), vec, vec, vec],
        out_specs=pl.BlockSpec((CONV_TS, D_CONV), lambda i: (i, 0)),
        out_shape=jax.ShapeDtypeStruct((s, D_CONV), BF16),
        scratch_shapes=[pltpu.VMEM((8, CONV_N, D_CONV), F32), pltpu.VMEM((CONV_TS, D_CONV), F32)],
        compiler_params=_params(1),
        name="conv_branch",
    )(proj3, proj3, proj3, proj3, conv_w, conv_b, ln_g, ln_b)


LSE_W = 128
LSE_REP = LSE_W // HEADS_PER_GROUP


def _attn_rows(q, k, v, n):
    blk = ATTN_BLOCK
    scale = HEAD_DIM ** -0.5
    qi = lax.broadcasted_iota(I32, (blk, 2 * blk), 0)
    ki = lax.broadcasted_iota(I32, (blk, 2 * blk), 1)
    band = (ki >= qi) & (ki <= qi + blk)
    lane_head = lax.broadcasted_iota(I32, (blk, LSE_W), 1) // LSE_REP
    o_rows, lse_rows = [], []
    for j in range(q.shape[0] // blk):
        mask = band
        if j == 0:
            mask = band & ((ki >= blk) | (n > 0))
        o_heads = []
        lse_tile = jnp.zeros((blk, LSE_W), F32)
        for hh in range(HEADS_PER_GROUP):
            cs = slice(hh * HEAD_DIM, (hh + 1) * HEAD_DIM)
            qh = q[j * blk:(j + 1) * blk, cs]
            kh = k[j * blk:(j + 2) * blk, cs]
            vh = v[j * blk:(j + 2) * blk, cs]
            s = lax.dot_general(qh, kh, (((1,), (1,)), ((), ())),
                                preferred_element_type=F32) * scale
            s = jnp.where(mask, s, -jnp.inf)
            m = jnp.max(s, axis=-1, keepdims=True)
            p = jnp.exp(s - m)
            l = jnp.sum(p, axis=-1, keepdims=True)
            o_heads.append(jnp.dot(p.astype(BF16), vh, preferred_element_type=F32) / l)
            lse_tile = jnp.where(lane_head == hh, m + jnp.log(l), lse_tile)
        o_rows.append(jnp.concatenate(o_heads, axis=1))
        lse_rows.append(lse_tile)
    return jnp.concatenate(o_rows, axis=0), jnp.concatenate(lse_rows, axis=0)


def _attn_kernel(q_ref, kc_ref, kp_ref, vc_ref, vp_ref, o_ref, lse_ref):
    n = pl.program_id(0)
    k = jnp.concatenate([kp_ref[...], kc_ref[...]], axis=0)
    v = jnp.concatenate([vp_ref[...], vc_ref[...]], axis=0)
    o, lse = _attn_rows(q_ref[...], k, v, n)
    o_ref[...] = o.astype(BF16)
    lse_ref[...] = lse


ATTN_TILE = 2048


def _attn_dilated_kernel(q_ref, kc_ref, kp_ref, vc_ref, vp_ref, o_ref, lse_ref,
                         qs, ks, vs, os_, ls, *, dil):
    n = pl.program_id(0)
    tile = q_ref.shape[0]
    mb = tile // dil
    half = dil // 2
    halo_w = ATTN_BLOCK * half
    n_slab = CB // 128
    q32 = pltpu.bitcast(q_ref[...], U32)
    kp32, kc32 = pltpu.bitcast(kp_ref[...], U32), pltpu.bitcast(kc_ref[...], U32)
    vp32, vc32 = pltpu.bitcast(vp_ref[...], U32), pltpu.bitcast(vc_ref[...], U32)
    for c in range(n_slab):
        cs = slice(c * 128, (c + 1) * 128)
        qs[c] = q32[:, cs]
        ks[c, 0:halo_w] = kp32[:, cs]
        ks[c, halo_w:] = kc32[:, cs]
        vs[c, 0:halo_w] = vp32[:, cs]
        vs[c, halo_w:] = vc32[:, cs]

    def gather(slabs, rp, rows):
        return jnp.concatenate(
            [slabs[c, pl.ds(rp, rows, stride=half), :] for c in range(n_slab)], axis=1)

    def residue(w, par):
        bits = (w << 16) if par == 0 else (w & jnp.uint32(HI_MASK))
        return lax.bitcast_convert_type(bits, F32).astype(BF16)

    for rp in range(half):
        wq = gather(qs, rp, mb)
        wk = gather(ks, rp, mb + ATTN_BLOCK)
        wv = gather(vs, rp, mb + ATTN_BLOCK)
        outs = []
        for par in range(2):
            o, lse = _attn_rows(residue(wq, par), residue(wk, par), residue(wv, par), n)
            outs.append(o)
            ls[pl.ds(2 * rp + par, mb, stride=dil), :] = lse
        w_out = _pack_bf16_pair(outs[0], outs[1])
        for c in range(n_slab):
            os_[c, pl.ds(rp, mb, stride=half), :] = w_out[:, c * 128:(c + 1) * 128]
    o32 = jnp.concatenate([os_[c] for c in range(n_slab)], axis=1)
    o_ref[...] = pltpu.bitcast(o32, BF16)
    lse_ref[...] = ls[...]


def _dilated_attention(proj3, group):
    _, dil = ATTN_PATTERNS[group]
    s = proj3.shape[1]
    cbs = (CB_Q + group, CB_K + group, CB_V + group)
    if dil == 1:
        tile, halo = 1024, ATTN_BLOCK
        body, scratch = _attn_kernel, []
    else:
        tile, halo = ATTN_TILE, ATTN_BLOCK * dil
        body = functools.partial(_attn_dilated_kernel, dil=dil)
        scratch = [pltpu.VMEM((CB // 128, tile // 2, 128), U32),
                   pltpu.VMEM((CB // 128, (tile + halo) // 2, 128), U32),
                   pltpu.VMEM((CB // 128, (tile + halo) // 2, 128), U32),
                   pltpu.VMEM((CB // 128, tile // 2, 128), U32),
                   pltpu.VMEM((tile, LSE_W), F32)]
    hb = tile // halo
    cur = lambda cb: pl.BlockSpec((None, tile, CB), lambda n: (cb, n, 0))
    prev = lambda cb: pl.BlockSpec((None, halo, CB), lambda n: (cb, jnp.maximum(n * hb - 1, 0), 0))
    return pl.pallas_call(
        body,
        grid=(s // tile,),
        in_specs=[cur(cbs[0]), cur(cbs[1]), prev(cbs[1]), cur(cbs[2]), prev(cbs[2])],
        out_specs=[pl.BlockSpec((tile, CB), lambda n: (n, 0)),
                   pl.BlockSpec((tile, LSE_W), lambda n: (n, 0))],
        out_shape=[jax.ShapeDtypeStruct((s, CB), BF16),
                   jax.ShapeDtypeStruct((s, LSE_W), F32)],
        scratch_shapes=scratch,
        compiler_params=_params(1),
        name=f"dilated_attention_g{group}",
    )(proj3, proj3, proj3, proj3, proj3)


MIX_TM = 256


def _mixer_out_kernel(x_ref, o0_ref, o1_ref, o2_ref, l0_ref, l1_ref, l2_ref, u_ref,
                      gc0, gc1, gc2, gc3, ga0, ga1, ga2, ga3,
                      wa_ref, wc_ref, bc_ref, wo_ref, g2_ref, wrh_ref, wrl_ref, br_ref,
                      x1_ref, hp_ref, logit_ref):
    yc = jnp.dot(u_ref[...], wc_ref[...], preferred_element_type=F32) + bc_ref[...]
    l_refs = (l0_ref, l1_ref, l2_ref)
    o_refs = (o0_ref, o1_ref, o2_ref)
    parts = []
    for hh in range(HEADS_PER_GROUP):
        cs = slice(hh * HEAD_DIM, (hh + 1) * HEAD_DIM)
        lse = [r[:, hh * LSE_REP:hh * LSE_REP + 1] for r in l_refs]
        m = jnp.maximum(jnp.maximum(lse[0], lse[1]), lse[2])
        e = [jnp.exp(v - m) for v in lse]
        den = e[0] + e[1] + e[2]
        acc = (e[0] / den) * o_refs[0][:, cs].astype(F32)
        for g in range(1, N_GROUPS):
            acc = acc + (e[g] / den) * o_refs[g][:, cs].astype(F32)
        parts.append(acc)
    o = jnp.concatenate(parts, axis=1).astype(BF16)
    ya = jnp.dot(o, wa_ref[...], preferred_element_type=F32)
    gcs = (gc0, gc1, gc2, gc3)
    gas = (ga0, ga1, ga2, ga3)
    merged = []
    for c in range(D_MODEL // CB):
        cs = slice(c * CB, (c + 1) * CB)
        merged.append((_sigmoid(gcs[c][...].astype(F32)) * yc[:, cs]
                       + _sigmoid(gas[c][...].astype(F32)) * ya[:, cs]).astype(BF16))
    merged = jnp.concatenate(merged, axis=1)
    x1 = x_ref[...] + jnp.dot(merged, wo_ref[...], preferred_element_type=F32)
    x1_ref[...] = x1
    ms = jnp.mean(x1 * x1, axis=-1, keepdims=True)
    h2 = (x1 * lax.rsqrt(ms + NORM_EPS)) * g2_ref[...]
    h2_hi = h2.astype(BF16)
    hp_ref[...] = _pack_bf16_pair(h2[:, 0:PACK_W], h2[:, PACK_W:D_MODEL])
    h2_lo = (h2 - h2_hi.astype(F32)).astype(BF16)
    hi_pass = jnp.dot(h2_hi, wrl_ref[...], preferred_element_type=F32)
    logits = (hi_pass[:, 0:N_EXPERTS] + hi_pass[:, N_EXPERTS:2 * N_EXPERTS]
              + jnp.dot(h2_lo, wrh_ref[...], preferred_element_type=F32)) + br_ref[...]
    logit_ref[...] = logits


def _mixer_output(x2, outs, lses, u_ln, proj3, wa, wc, bc, wo, g2, wr_hi, wr_lo, br):
    s = x2.shape[0]
    row = lambda w: pl.BlockSpec((MIX_TM, w), lambda i: (i, 0))
    gate = lambda cb: pl.BlockSpec((None, MIX_TM, CB), lambda i: (cb, i, 0))
    full = lambda a: pl.BlockSpec(a.shape, lambda i: (0,) * a.ndim)
    in_specs = ([row(D_MODEL)] + [row(CB)] * 3 + [row(LSE_W)] * 3 + [row(D_CONV)]
                + [gate(CB_GATE_C + c) for c in range(4)]
                + [gate(CB_GATE_A + c) for c in range(4)]
                + [full(a) for a in (wa, wc, bc, wo, g2, wr_hi, wr_lo, br)])
    return pl.pallas_call(
        _mixer_out_kernel,
        grid=(s // MIX_TM,),
        in_specs=in_specs,
        out_specs=[row(D_MODEL), row(PACK_W), row(N_EXPERTS)],
        out_shape=[jax.ShapeDtypeStruct((s, D_MODEL), F32),
                   jax.ShapeDtypeStruct((s, PACK_W), U32),
                   jax.ShapeDtypeStruct((s, N_EXPERTS), F32)],
        compiler_params=_params(1),
        name="mixer_output",
    )(x2, *outs, *lses, u_ln, *([proj3] * 8), wa, wc, bc, wo, g2, wr_hi, wr_lo, br)


ROUTE_TM = 1024


def _route_kernel(logit_ref, idx_ref, gate_ref, pos_ref, cnt_ref, carry_ref):
    i = pl.program_id(0)
    tm = ROUTE_TM

    @pl.when(i == 0)
    def _():
        carry_ref[...] = jnp.zeros_like(carry_ref)

    logits = logit_ref[...]
    e_iota = lax.broadcasted_iota(I32, (tm, N_EXPERTS), 1).astype(F32)
    k_lane = lax.broadcasted_iota(I32, (tm, TOP_K), 1)
    vals = logits
    sels = []
    idx_t = jnp.zeros((tm, TOP_K), F32)
    val_t = jnp.zeros((tm, TOP_K), F32)
    for k in range(TOP_K):
        mk = jnp.max(vals, axis=-1, keepdims=True)
        ik = jnp.min(jnp.where(vals == mk, e_iota, float(N_EXPERTS)), axis=-1, keepdims=True)
        sel = e_iota == ik
        sels.append(sel)
        vals = jnp.where(sel, -jnp.inf, vals)
        idx_t = jnp.where(k_lane == k, ik, idx_t)
        val_t = jnp.where(k_lane == k, mk, val_t)
    ex = jnp.exp(val_t - val_t[:, 0:1])
    gate_ref[...] = ex / jnp.sum(ex, axis=-1, keepdims=True)
    idx_ref[...] = idx_t.astype(I32)

    cnt = jnp.zeros((tm, N_EXPERTS), F32)
    for sel in sels:
        cnt = cnt + jnp.where(sel, 1.0, 0.0)
    ri = lax.broadcasted_iota(I32, (tm, tm), 0)
    ci = lax.broadcasted_iota(I32, (tm, tm), 1)
    lower = jnp.where(ci < ri, 1.0, 0.0).astype(BF16)
    prefix = jnp.dot(lower, cnt.astype(BF16), preferred_element_type=F32) + carry_ref[...]
    pos_t = jnp.zeros((tm, TOP_K), F32)
    for k, sel in enumerate(sels):
        pk = jnp.sum(jnp.where(sel, prefix, 0.0), axis=-1, keepdims=True)
        pos_t = jnp.where(k_lane == k, pk, pos_t)
    pos_ref[...] = pos_t.astype(I32)
    carry_ref[...] = carry_ref[...] + jnp.sum(cnt, axis=0, keepdims=True)
    cnt_ref[...] = carry_ref[...]


def _route(logits):
    s = logits.shape[0]
    row = lambda w: pl.BlockSpec((ROUTE_TM, w), lambda i: (i, 0))
    return pl.pallas_call(
        _route_kernel,
        grid=(s // ROUTE_TM,),
        in_specs=[row(N_EXPERTS)],
        out_specs=[row(TOP_K), row(TOP_K), row(TOP_K),
                   pl.BlockSpec((1, N_EXPERTS), lambda i: (0, 0))],
        out_shape=[jax.ShapeDtypeStruct((s, TOP_K), I32),
                   jax.ShapeDtypeStruct((s, TOP_K), F32),
                   jax.ShapeDtypeStruct((s, TOP_K), I32),
                   jax.ShapeDtypeStruct((1, N_EXPERTS), F32)],
        scratch_shapes=[pltpu.VMEM((1, N_EXPERTS), F32)],
        compiler_params=_params(1),
        name="moe_route",
    )(logits)


ROW_BLK = 128
DISP_TM = 2048


def _dispatch_kernel(fill_ref, dest_ref, h_ref, xs_ref, zbuf, sem):
    i = pl.program_id(0)
    n_fill = fill_ref.shape[0]

    @pl.when(i == 0)
    def _():
        zbuf[...] = jnp.zeros_like(zbuf)

        def fill_copy(j):
            return pltpu.make_async_copy(
                zbuf, xs_ref.at[pl.ds(fill_ref[j] * ROW_BLK, ROW_BLK)], sem.at[0])

        def start(j, c):
            @pl.when(fill_ref[j] >= 0)
            def _():
                fill_copy(j).start()
            return c

        def wait(j, c):
            @pl.when(fill_ref[j] >= 0)
            def _():
                fill_copy(j).wait()
            return c

        lax.fori_loop(0, n_fill, start, 0)
        lax.fori_loop(0, n_fill, wait, 0)

    def rows8(tt, c):
        for si in range(8):
            for k in range(TOP_K):
                d = dest_ref[0, (tt * 8 + si) * TOP_K + k]
                pltpu.make_async_copy(h_ref.at[tt, pl.ds(si, 1)], xs_ref.at[pl.ds(d, 1)],
                                      sem.at[0]).start(priority=k % 2)
        return c

    lax.fori_loop(0, DISP_TM // 8, rows8, 0)
    for _ in range(TOP_K):
        pltpu.make_async_copy(h_ref, h_ref, sem.at[0]).wait()


def _dispatch(hp, dest, fill_blocks, n_rows):
    s = hp.shape[0]
    dest3 = dest.reshape(s // DISP_TM, 1, DISP_TM * TOP_K)
    grid_spec = pltpu.PrefetchScalarGridSpec(
        num_scalar_prefetch=1,
        grid=(s // DISP_TM,),
        in_specs=[
            pl.BlockSpec((None, 1, DISP_TM * TOP_K), lambda i, fr: (i, 0, 0),
                         memory_space=pltpu.SMEM),
            pl.BlockSpec((DISP_TM // 8, 8, PACK_W), lambda i, fr: (i, 0, 0)),
        ],
        out_specs=pl.BlockSpec(memory_space=pl.ANY),
        scratch_shapes=[pltpu.VMEM((ROW_BLK, PACK_W), U32), pltpu.SemaphoreType.DMA((1,))],
    )
    return pl.pallas_call(
        _dispatch_kernel,
        grid_spec=grid_spec,
        out_shape=jax.ShapeDtypeStruct((n_rows, PACK_W), U32),
        compiler_params=_params(1),
        name="moe_dispatch",
    )(fill_blocks, dest3, hp.reshape(s // 8, 8, PACK_W))


UNIT_BLOCKS = 10
UNIT_ROWS = ROW_BLK * UNIT_BLOCKS
GROUP_SIZES = (8, 4, 2, 1)
EXP_TF = 512
EXP_NF = D_EXPERT // EXP_TF


def _expert_kernel(ue_ref, ub0_ref, unb_ref, nu_ref,
                   xs_ref, win_ref, wout_ref, bin_ref, bo_ref,
                   ys_ref,
                   xbuf, xb16, acc, wg_buf, wu_buf, wo_buf, sem_x, sem_y, sem_w):
    u = pl.program_id(0)
    n_units = nu_ref[0]

    def x_copy(unit, b):
        src = xs_ref.at[pl.ds((ub0_ref[unit] + b) * ROW_BLK, ROW_BLK)]
        dst = xbuf.at[pl.ds(b * ROW_BLK, ROW_BLK)]
        return pltpu.make_async_copy(src, dst, sem_x.at[0])

    def y_copy(unit, b):
        src = acc.at[pl.ds(b * ROW_BLK, ROW_BLK)]
        dst = ys_ref.at[pl.ds((ub0_ref[unit] + b) * ROW_BLK, ROW_BLK)]
        return pltpu.make_async_copy(src, dst, sem_y.at[0])

    def w_copies(unit, f, slot):
        e = ue_ref[unit]
        c0 = pl.multiple_of(f * EXP_TF, EXP_TF)
        return (
            pltpu.make_async_copy(win_ref.at[e, :, pl.ds(c0, EXP_TF)], wg_buf.at[slot],
                                  sem_w.at[slot]),
            pltpu.make_async_copy(win_ref.at[e, :, pl.ds(D_EXPERT + c0, EXP_TF)], wu_buf.at[slot],
                                  sem_w.at[slot]),
            pltpu.make_async_copy(wout_ref.at[e, pl.ds(c0, EXP_TF), :], wo_buf.at[slot],
                                  sem_w.at[slot]),
        )

    def start_w(unit, f, slot):
        for cp in w_copies(unit, f, slot):
            cp.start()

    def wait_w(unit, f, slot):
        for cp in w_copies(unit, f, slot):
            cp.wait()

    def start_x(unit):
        lax.fori_loop(0, unb_ref[unit], lambda b, c: (x_copy(unit, b).start(), c)[1], 0)

    def wait_x(unit):
        lax.fori_loop(0, unb_ref[unit], lambda b, c: (x_copy(unit, b).wait(), c)[1], 0)

    def wait_y(unit):
        lax.fori_loop(0, unb_ref[unit], lambda b, c: (y_copy(unit, b).wait(), c)[1], 0)

    @pl.when(u < n_units)
    def _():
        nblk = unb_ref[u]

        @pl.when(u == 0)
        def _():
            start_x(u)
            start_w(u, 0, 0)
            acc[...] = jnp.zeros_like(acc)

        wait_x(u)

        def unpack(b, c):
            r0 = pl.multiple_of(b * ROW_BLK, ROW_BLK)
            lo, hi = _unpack_bf16_pair(xbuf[pl.ds(r0, ROW_BLK), :])
            xb16[pl.ds(r0, ROW_BLK), 0:PACK_W] = lo
            xb16[pl.ds(r0, ROW_BLK), PACK_W:D_MODEL] = hi
            return c

        lax.fori_loop(0, nblk, unpack, 0)

        @pl.when(u + 1 < n_units)
        def _():
            start_x(u + 1)

        @pl.when(u > 0)
        def _():
            wait_y(u - 1)

        def group(b0, n_blk, f, ws):
            rows = n_blk * ROW_BLK
            r0 = pl.multiple_of(b0 * ROW_BLK, ROW_BLK)
            c0 = pl.multiple_of(f * EXP_TF, EXP_TF)
            x = xb16[pl.ds(r0, rows), :]
            wgu = jnp.concatenate([wg_buf[ws].astype(BF16), wu_buf[ws].astype(BF16)], axis=1)
            gu = jnp.dot(x, wgu, preferred_element_type=F32)
            bg = bin_ref[:, pl.ds(c0, EXP_TF)]
            bu = bin_ref[:, pl.ds(D_EXPERT + c0, EXP_TF)]
            g = jnp.minimum(gu[:, 0:EXP_TF] + bg, SWIGLU_LIMIT)
            up = jnp.clip(gu[:, EXP_TF:2 * EXP_TF] + bu, -SWIGLU_LIMIT, SWIGLU_LIMIT)
            act = (up + 1.0) * (g * _sigmoid(SWIGLU_ALPHA * g))
            start = jnp.where(f == 0, jnp.broadcast_to(bo_ref[...], (rows, D_MODEL)),
                              acc[pl.ds(r0, rows), :])
            acc[pl.ds(r0, rows), :] = start + jnp.dot(
                act.astype(BF16), wo_buf[ws].astype(BF16), preferred_element_type=F32)

            @pl.when(f == EXP_NF - 1)
            def _():
                for j in range(n_blk):
                    y_copy(u, b0 + j).start()

        def f_tile(f, ws):
            big = GROUP_SIZES[0]
            n_big = nblk // big
            lax.fori_loop(0, n_big, lambda q, c: (group(big * q, big, f, ws), c)[1], 0)
            for size in GROUP_SIZES[1:]:
                done = (nblk // (2 * size)) * (2 * size)

                @pl.when(lax.rem(nblk, 2 * size) >= size)
                def _():
                    group(done, size, f, ws)

        def f_step(f, c):
            ws = lax.rem(f, 2)

            @pl.when(f + 1 < EXP_NF)
            def _():
                start_w(u, f + 1, 1 - ws)

            @pl.when((f + 1 == EXP_NF) & (u + 1 < n_units))
            def _():
                start_w(u + 1, 0, 1 - ws)

            wait_w(u, f, ws)
            f_tile(f, ws)
            return c

        lax.fori_loop(0, EXP_NF, f_step, 0)

        @pl.when(u == n_units - 1)
        def _():
            wait_y(u)

    @pl.when(u == pl.num_programs(0) - 1)
    def _():
        n_blocks = ys_ref.shape[0] // ROW_BLK
        acc[0:ROW_BLK, :] = jnp.zeros((ROW_BLK, D_MODEL), F32)

        def tail_copy(b):
            return pltpu.make_async_copy(acc.at[pl.ds(0, ROW_BLK)],
                                         ys_ref.at[pl.ds(b * ROW_BLK, ROW_BLK)], sem_y.at[0])

        lax.fori_loop(nu_ref[1], n_blocks, lambda b, c: (tail_copy(b).start(), c)[1], 0)
        lax.fori_loop(nu_ref[1], n_blocks, lambda b, c: (tail_copy(b).wait(), c)[1], 0)


def _expert_ffn(xs, unit_e, unit_blk0, unit_nblk, unit_counts, w_in, b_in, w_out, b_out):
    n_rows = xs.shape[0]
    n_units_max = unit_e.shape[0]

    def e_eff(u, ue, nu):
        return ue[jnp.minimum(u, jnp.maximum(nu[0] - 1, 0))]

    any_spec = pl.BlockSpec(memory_space=pl.ANY)
    in_specs = [
        any_spec, any_spec, any_spec,
        pl.BlockSpec((None, 1, 2 * D_EXPERT), lambda u, ue, ub, un, nu: (e_eff(u, ue, nu), 0, 0)),
        pl.BlockSpec((None, 1, D_MODEL), lambda u, ue, ub, un, nu: (e_eff(u, ue, nu), 0, 0)),
    ]
    grid_spec = pltpu.PrefetchScalarGridSpec(
        num_scalar_prefetch=4,
        grid=(n_units_max,),
        in_specs=in_specs,
        out_specs=any_spec,
        scratch_shapes=[
            pltpu.VMEM((UNIT_ROWS, PACK_W), U32),
            pltpu.VMEM((UNIT_ROWS, D_MODEL), BF16),
            pltpu.VMEM((UNIT_ROWS, D_MODEL), F32),
            pltpu.VMEM((2, D_MODEL, EXP_TF), F32),
            pltpu.VMEM((2, D_MODEL, EXP_TF), F32),
            pltpu.VMEM((2, EXP_TF, D_MODEL), F32),
            pltpu.SemaphoreType.DMA((1,)),
            pltpu.SemaphoreType.DMA((1,)),
            pltpu.SemaphoreType.DMA((2,)),
        ],
    )
    return pl.pallas_call(
        _expert_kernel,
        grid_spec=grid_spec,
        out_shape=jax.ShapeDtypeStruct((n_rows, D_MODEL), F32),
        compiler_params=_params(1),
        name="expert_ffn",
    )(unit_e, unit_blk0, unit_nblk, unit_counts, xs,
      w_in, w_out, b_in[:, None, :], b_out[:, None, :])


COMB_TM = 256


def _combine_kernel(dcur_ref, dnxt_ref, x1_ref, gate_ref, g_ref, ys_ref, o_ref, ybuf, sem):
    i = pl.program_id(0)
    n = pl.num_programs(0)
    slot = lax.rem(i, 2)

    def issue(dref, s):
        def rows8(tt, c):
            for si in range(8):
                for k in range(TOP_K):
                    d = dref[0, (tt * 8 + si) * TOP_K + k]
                    pltpu.make_async_copy(ys_ref.at[pl.ds(d, 1)],
                                          ybuf.at[s, k, tt, pl.ds(si, 1)],
                                          sem.at[s]).start(priority=k % 2)
            return c

        lax.fori_loop(0, COMB_TM // 8, rows8, 0)

    @pl.when(i == 0)
    def _():
        issue(dcur_ref, slot)

    @pl.when(i + 1 < n)
    def _():
        issue(dnxt_ref, 1 - slot)

    for k in range(TOP_K):
        pltpu.make_async_copy(ybuf.at[slot, k], ybuf.at[slot, k], sem.at[slot]).wait()

    gate = gate_ref[...]
    y = x1_ref[...]
    for k in range(TOP_K):
        y = y + gate[:, k:k + 1] * ybuf[slot, k].reshape(COMB_TM, D_MODEL)
    ms = jnp.mean(y * y, axis=-1, keepdims=True)
    o_ref[...] = (y * lax.rsqrt(ms + NORM_EPS)) * g_ref[...]


def _combine(dest, x1, gate, norm_g, ys):
    s = x1.shape[0]
    nt = s // COMB_TM
    dest3 = dest.reshape(nt, 1, COMB_TM * TOP_K)
    smem = lambda imap: pl.BlockSpec((None, 1, COMB_TM * TOP_K), imap, memory_space=pltpu.SMEM)
    row = lambda w: pl.BlockSpec((COMB_TM, w), lambda i: (i, 0))
    return pl.pallas_call(
        _combine_kernel,
        grid=(nt,),
        in_specs=[smem(lambda i: (i, 0, 0)),
                  smem(lambda i: (jnp.minimum(i + 1, nt - 1), 0, 0)),
                  row(D_MODEL), row(TOP_K), pl.BlockSpec((1, D_MODEL), lambda i: (0, 0)),
                  pl.BlockSpec(memory_space=pl.ANY)],
        out_specs=row(D_MODEL),
        out_shape=jax.ShapeDtypeStruct((s, D_MODEL), F32),
        scratch_shapes=[pltpu.VMEM((2, TOP_K, COMB_TM // 8, 8, D_MODEL), F32),
                        pltpu.SemaphoreType.DMA((2,))],
        compiler_params=_params(1),
        name="moe_combine",
    )(dest3, dest3, x1, gate, norm_g, ys)


def _route_tables(idx, pos, counts, n_tok):
    n_assign = n_tok * TOP_K
    counts = counts.reshape(N_EXPERTS).astype(I32)
    nblk_e = (counts + ROW_BLK - 1) // ROW_BLK
    pad_start = (jnp.cumsum(nblk_e) - nblk_e) * ROW_BLK
    onehot = idx[..., None] == jnp.arange(N_EXPERTS, dtype=I32)
    dest = pos + jnp.sum(jnp.where(onehot, pad_start, 0), axis=-1)

    n_blocks = -(-(n_assign + N_EXPERTS * (ROW_BLK - 1)) // ROW_BLK)
    used = jnp.sum(nblk_e)
    blk_ids = jnp.arange(n_blocks, dtype=I32)
    last_blk = pad_start // ROW_BLK + nblk_e - 1
    fill_blocks = jnp.concatenate([jnp.where(nblk_e > 0, last_blk, -1),
                                   jnp.where(blk_ids >= used, blk_ids, -1)]).astype(I32)

    n_units_max = (n_blocks + (UNIT_BLOCKS - 1) * N_EXPERTS) // UNIT_BLOCKS
    chunks_e = (nblk_e + UNIT_BLOCKS - 1) // UNIT_BLOCKS
    chunk_end = jnp.cumsum(chunks_e)
    chunk_start = chunk_end - chunks_e
    n_units = chunk_end[-1]
    uid = jnp.arange(n_units_max, dtype=I32)
    unit_e = jnp.minimum(jnp.searchsorted(chunk_end, uid, side="right"), N_EXPERTS - 1).astype(I32)
    c_in_e = uid - chunk_start[unit_e]
    unit_blk0 = (pad_start[unit_e] // ROW_BLK + c_in_e * UNIT_BLOCKS).astype(I32)
    unit_nblk = jnp.clip(nblk_e[unit_e] - c_in_e * UNIT_BLOCKS, 0, UNIT_BLOCKS).astype(I32)
    active = uid < n_units
    unit_blk0 = jnp.where(active, unit_blk0, 0)
    unit_nblk = jnp.where(active, unit_nblk, 0)
    unit_counts = jnp.stack([n_units, used]).astype(I32)
    return dest.astype(I32), fill_blocks, n_blocks * ROW_BLK, unit_e, unit_blk0, unit_nblk, unit_counts


def kernel(x, norm1_g, w_in, b_gate, conv_w, conv_b, conv_ln_g, conv_ln_b, w_conv_out, b_conv_out,
           w_attn_out, w_out, norm2_g, w_router, b_router, w_exp_in, b_exp_in, w_exp_out, b_exp_out,
           norm_f_g):
    b_sz, s_len, d = x.shape
    n_tok = b_sz * s_len
    x2 = x.reshape(n_tok, d)
    l = 0
    b_gate_ext = jnp.concatenate([jnp.zeros((GATE_COL0,), F32), b_gate[l]])[None, :]
    proj3 = _in_projection(x2, norm1_g[l][None, :], w_in[l], b_gate_ext)
    u_ln = _conv_branch(proj3, conv_w[l], conv_b[l][None, :], conv_ln_g[l][None, :],
                        conv_ln_b[l][None, :])
    outs, lses = [], []
    for g in range(N_GROUPS):
        o_g, lse_g = _dilated_attention(proj3, g)
        outs.append(o_g)
        lses.append(lse_g)
    wr = w_router[l]
    wr_hi = wr.astype(BF16)
    wr_lo = jnp.concatenate([wr_hi, (wr - wr_hi.astype(F32)).astype(BF16)], axis=1)
    x1, hp, logits = _mixer_output(
        x2, outs, lses, u_ln, proj3,
        w_attn_out[l].astype(BF16), w_conv_out[l].astype(BF16), b_conv_out[l][None, :],
        w_out[l].astype(BF16), norm2_g[l][None, :], wr_hi, wr_lo, b_router[l][None, :])

    idx, gate, pos, counts = _route(logits)
    dest, fill_blocks, n_rows, unit_e, unit_blk0, unit_nblk, unit_counts = _route_tables(
        idx, pos, counts, n_tok)
    xs = _dispatch(hp, dest, fill_blocks, n_rows)
    ys = _expert_ffn(xs, unit_e, unit_blk0, unit_nblk, unit_counts,
                     w_exp_in[l], b_exp_in[l], w_exp_out[l], b_exp_out[l])
    out = _combine(dest, x1, gate, norm_f_g[None, :], ys)
    return out.reshape(b_sz, s_len, d)
```

```python
import functools

import jax
import jax.numpy as jnp
from jax import lax
from jax.experimental import pallas as pl
from jax.experimental.pallas import tpu as pltpu

D_MODEL = 2048
D_CONV = D_MODEL // 2
CONV_WIDTH = 31
HEAD_DIM = 128
HEADS_PER_GROUP = 4
ATTN_PATTERNS = ((128, 1), (512, 4), (2048, 16))
N_GROUPS = len(ATTN_PATTERNS)
ATTN_WIDTH = N_GROUPS * HEADS_PER_GROUP * HEAD_DIM
ATTN_BLOCK = 128
D_IN = 2 * D_CONV + 3 * ATTN_WIDTH + 2 * D_MODEL
N_EXPERTS = 32
TOP_K = 4
D_EXPERT = D_MODEL
SWIGLU_LIMIT = 7.0
SWIGLU_ALPHA = 1.702
NORM_EPS = 1e-5

CB = 512
N_CB = D_IN // CB
CB_CONV_A, CB_CONV_G = 0, 2
CB_Q, CB_K, CB_V = 4, 7, 10
CB_GATE_C, CB_GATE_A = 13, 17
GATE_COL0 = CB_GATE_C * CB

V7X_VMEM_BYTES = 64 * 1024 * 1024
VMEM_LIMIT = V7X_VMEM_BYTES - 8 * 1024 * 1024

F32 = jnp.float32
BF16 = jnp.bfloat16
U32 = jnp.uint32
I32 = jnp.int32

PACK_W = D_MODEL // 2
HI_MASK = 0xFFFF0000


def _sigmoid(z):
    return 0.5 * jnp.tanh(0.5 * z) + 0.5


def _pack_bf16_pair(lo, hi):
    lo_bits = lax.bitcast_convert_type(lo.astype(BF16).astype(F32), U32) >> 16
    hi_bits = lax.bitcast_convert_type(hi.astype(BF16).astype(F32), U32) & jnp.uint32(HI_MASK)
    return hi_bits | lo_bits


def _unpack_bf16_pair(w):
    lo = lax.bitcast_convert_type(w << 16, F32).astype(BF16)
    hi = lax.bitcast_convert_type(w & jnp.uint32(HI_MASK), F32).astype(BF16)
    return lo, hi


def _params(n_axes, vmem=VMEM_LIMIT):
    return pltpu.CompilerParams(
        dimension_semantics=("arbitrary",) * n_axes, vmem_limit_bytes=vmem)


NORM_TM = 512
IN_TM = 1024
IN_TN = 1536


def _rmsnorm_kernel(x_ref, g_ref, o_ref):
    x = x_ref[...]
    ms = jnp.mean(x * x, axis=-1, keepdims=True)
    o_ref[...] = ((x * lax.rsqrt(ms + NORM_EPS)) * g_ref[...]).astype(BF16)


def _rmsnorm_bf16(x2, norm_g):
    s = x2.shape[0]
    row = pl.BlockSpec((NORM_TM, D_MODEL), lambda i: (i, 0))
    return pl.pallas_call(
        _rmsnorm_kernel,
        grid=(s // NORM_TM,),
        in_specs=[row, pl.BlockSpec((1, D_MODEL), lambda i: (0, 0))],
        out_specs=row,
        out_shape=jax.ShapeDtypeStruct((s, D_MODEL), BF16),
        compiler_params=_params(1),
        name="rmsnorm_in",
    )(x2, norm_g)


def _inproj_kernel(h_ref, w_ref, bg_ref, o_ref, wbf_ref):
    @pl.when(pl.program_id(1) == 0)
    def _():
        wbf_ref[...] = w_ref[...].astype(BF16)

    acc = jnp.dot(h_ref[...], wbf_ref[...], preferred_element_type=F32) + bg_ref[...]
    for c in range(IN_TN // CB):
        o_ref[c] = acc[:, c * CB:(c + 1) * CB].astype(BF16)


def _in_projection(x2, norm_g, w_in, b_gate_ext):
    s = x2.shape[0]
    h = _rmsnorm_bf16(x2, norm_g)
    grid = (D_IN // IN_TN, s // IN_TM)
    return pl.pallas_call(
        _inproj_kernel,
        grid=grid,
        in_specs=[
            pl.BlockSpec((IN_TM, D_MODEL), lambda j, i: (i, 0)),
            pl.BlockSpec((D_MODEL, IN_TN), lambda j, i: (0, j)),
            pl.BlockSpec((1, IN_TN), lambda j, i: (0, j)),
        ],
        out_specs=pl.BlockSpec((IN_TN // CB, IN_TM, CB), lambda j, i: (j, i, 0)),
        out_shape=jax.ShapeDtypeStruct((N_CB, s, CB), BF16),
        scratch_shapes=[pltpu.VMEM((D_MODEL, IN_TN), BF16)],
        compiler_params=_params(2),
        name="in_projection",
    )(h, w_in, b_gate_ext)


CONV_TS = 512
CONV_HALO = 32
CONV_RC = 32
CONV_N = CONV_TS + CONV_HALO


def _conv_kernel(ac_ref, gc_ref, ah_ref, gh_ref, w_ref, cb_ref, lg_ref, lb_ref, o_ref, r_ref,
                 c_ref):
    i = pl.program_id(0)
    has_prev = i > 0
    half = D_CONV // 2
    for c in range(2):
        cs = slice(c * half, (c + 1) * half)
        uh = ah_ref[c].astype(F32) * _sigmoid(gh_ref[c].astype(F32))
        r_ref[0, 0:CONV_HALO, cs] = jnp.where(has_prev, uh, 0.0)
        r_ref[0, CONV_HALO:CONV_N, cs] = ac_ref[c].astype(F32) * _sigmoid(gc_ref[c].astype(F32))
    u_ext = r_ref[0]
    for b in range(1, 8):
        r_ref[b] = pltpu.roll(u_ext, CONV_N - b, axis=0)

    def chunk(ci, carry):
        r0 = pl.multiple_of(ci * CONV_RC, CONV_RC)
        acc = jnp.broadcast_to(cb_ref[...], (CONV_RC, D_CONV))
        for k in range(CONV_WIDTH):
            kp = k + (CONV_HALO - (CONV_WIDTH - 1))
            a, b = kp // 8, kp % 8
            acc = acc + w_ref[k:k + 1, :] * r_ref[b, pl.ds(r0 + 8 * a, CONV_RC), :]
        c_ref[pl.ds(r0, CONV_RC), :] = acc
        return carry

    lax.fori_loop(0, CONV_TS // CONV_RC, chunk, 0)
    conv = c_ref[...]
    mu = jnp.mean(conv, axis=-1, keepdims=True)
    d = conv - mu
    var = jnp.mean(d * d, axis=-1, keepdims=True)
    y = (d * lax.rsqrt(var + NORM_EPS)) * lg_ref[...] + lb_ref[...]
    o_ref[...] = (y * _sigmoid(y)).astype(BF16)


def _conv_branch(proj3, conv_w, conv_b, ln_g, ln_b):
    s = proj3.shape[1]
    hb = CONV_TS // CONV_HALO
    cur = lambda blk: pl.BlockSpec((2, CONV_TS, CB), lambda i: (blk // 2, i, 0))
    halo = lambda blk: pl.BlockSpec(
        (2, CONV_HALO, CB), lambda i: (blk // 2, jnp.maximum(i * hb - 1, 0), 0))
    vec = pl.BlockSpec((1, D_CONV), lambda i: (0, 0))
    return pl.pallas_call(
        _conv_kernel,
        grid=(s // CONV_TS,),
        in_specs=[cur(CB_CONV_A), cur(CB_CONV_G), halo(CB_CONV_A), halo(CB_CONV_G),
                  pl.BlockSpec((CONV_WIDTH, D_CONV), lambda i: (0, 0)), vec, vec, vec],
        out_specs=pl.BlockSpec((CONV_TS, D_CONV), lambda i: (i, 0)),
        out_shape=jax.ShapeDtypeStruct((s, D_CONV), BF16),
        scratch_shapes=[pltpu.VMEM((8, CONV_N, D_CONV), F32), pltpu.VMEM((CONV_TS, D_CONV), F32)],
        compiler_params=_params(1),
        name="conv_branch",
    )(proj3, proj3, proj3, proj3, conv_w, conv_b, ln_g, ln_b)


LSE_W = 128
LSE_REP = LSE_W // HEADS_PER_GROUP


def _attn_rows(q, k, v, n):
    blk = ATTN_BLOCK
    scale = HEAD_DIM ** -0.5
    qi = lax.broadcasted_iota(I32, (blk, 2 * blk), 0)
    ki = lax.broadcasted_iota(I32, (blk, 2 * blk), 1)
    band = (ki >= qi) & (ki <= qi + blk)
    lane_head = lax.broadcasted_iota(I32, (blk, LSE_W), 1) // LSE_REP
    o_rows, lse_rows = [], []
    for j in range(q.shape[0] // blk):
        mask = band
        if j == 0:
            mask = band & ((ki >= blk) | (n > 0))
        o_heads = []
        lse_tile = jnp.zeros((blk, LSE_W), F32)
        for hh in range(HEADS_PER_GROUP):
            cs = slice(hh * HEAD_DIM, (hh + 1) * HEAD_DIM)
            qh = q[j * blk:(j + 1) * blk, cs]
            kh = k[j * blk:(j + 2) * blk, cs]
            vh = v[j * blk:(j + 2) * blk, cs]
            s = lax.dot_general(qh, kh, (((1,), (1,)), ((), ())),
                                preferred_element_type=F32) * scale
            s = jnp.where(mask, s, -jnp.inf)
            m = jnp.max(s, axis=-1, keepdims=True)
            p = jnp.exp(s - m)
            l = jnp.sum(p, axis=-1, keepdims=True)
            o_heads.append(jnp.dot(p.astype(BF16), vh, preferred_element_type=F32) / l)
            lse_tile = jnp.where(lane_head == hh, m + jnp.log(l), lse_tile)
        o_rows.append(jnp.concatenate(o_heads, axis=1))
        lse_rows.append(lse_tile)
    return jnp.concatenate(o_rows, axis=0), jnp.concatenate(lse_rows, axis=0)


def _attn_kernel(q_ref, kc_ref, kp_ref, vc_ref, vp_ref, o_ref, lse_ref):
    n = pl.program_id(0)
    k = jnp.concatenate([kp_ref[...], kc_ref[...]], axis=0)
    v = jnp.concatenate([vp_ref[...], vc_ref[...]], axis=0)
    o, lse = _attn_rows(q_ref[...], k, v, n)
    o_ref[...] = o.astype(BF16)
    lse_ref[...] = lse


ATTN_TILE = 2048


def _attn_dilated_kernel(q_ref, kc_ref, kp_ref, vc_ref, vp_ref, o_ref, lse_ref,
                         qs, ks, vs, os_, ls, *, dil):
    n = pl.program_id(0)
    tile = q_ref.shape[0]
    mb = tile // dil
    half = dil // 2
    halo_w = ATTN_BLOCK * half
    n_slab = CB // 128
    q32 = pltpu.bitcast(q_ref[...], U32)
    kp32, kc32 = pltpu.bitcast(kp_ref[...], U32), pltpu.bitcast(kc_ref[...], U32)
    vp32, vc32 = pltpu.bitcast(vp_ref[...], U32), pltpu.bitcast(vc_ref[...], U32)
    for c in range(n_slab):
        cs = slice(c * 128, (c + 1) * 128)
        qs[c] = q32[:, cs]
        ks[c, 0:halo_w] = kp32[:, cs]
        ks[c, halo_w:] = kc32[:, cs]
        vs[c, 0:halo_w] = vp32[:, cs]
        vs[c, halo_w:] = vc32[:, cs]

    def gather(slabs, rp, rows):
        return jnp.concatenate(
            [slabs[c, pl.ds(rp, rows, stride=half), :] for c in range(n_slab)], axis=1)

    def residue(w, par):
        bits = (w << 16) if par == 0 else (w & jnp.uint32(HI_MASK))
        return lax.bitcast_convert_type(bits, F32).astype(BF16)

    for rp in range(half):
        wq = gather(qs, rp, mb)
        wk = gather(ks, rp, mb + ATTN_BLOCK)
        wv = gather(vs, rp, mb + ATTN_BLOCK)
        outs = []
        for par in range(2):
            o, lse = _attn_rows(residue(wq, par), residue(wk, par), residue(wv, par), n)
            outs.append(o)
            ls[pl.ds(2 * rp + par, mb, stride=dil), :] = lse
        w_out = _pack_bf16_pair(outs[0], outs[1])
        for c in range(n_slab):
            os_[c, pl.ds(rp, mb, stride=half), :] = w_out[:, c * 128:(c + 1) * 128]
    o32 = jnp.concatenate([os_[c] for c in range(n_slab)], axis=1)
    o_ref[...] = pltpu.bitcast(o32, BF16)
    lse_ref[...] = ls[...]


def _dilated_attention(proj3, group):
    _, dil = ATTN_PATTERNS[group]
    s = proj3.shape[1]
    cbs = (CB_Q + group, CB_K + group, CB_V + group)
    if dil == 1:
        tile, halo = 1024, ATTN_BLOCK
        body, scratch = _attn_kernel, []
    else:
        tile, halo = ATTN_TILE, ATTN_BLOCK * dil
        body = functools.partial(_attn_dilated_kernel, dil=dil)
        scratch = [pltpu.VMEM((CB // 128, tile // 2, 128), U32),
                   pltpu.VMEM((CB // 128, (tile + halo) // 2, 128), U32),
                   pltpu.VMEM((CB // 128, (tile + halo) // 2, 128), U32),
                   pltpu.VMEM((CB // 128, tile // 2, 128), U32),
                   pltpu.VMEM((tile, LSE_W), F32)]
    hb = tile // halo
    cur = lambda cb: pl.BlockSpec((None, tile, CB), lambda n: (cb, n, 0))
    prev = lambda cb: pl.BlockSpec((None, halo, CB), lambda n: (cb, jnp.maximum(n * hb - 1, 0), 0))
    return pl.pallas_call(
        body,
        grid=(s // tile,),
        in_specs=[cur(cbs[0]), cur(cbs[1]), prev(cbs[1]), cur(cbs[2]), prev(cbs[2])],
        out_specs=[pl.BlockSpec((tile, CB), lambda n: (n, 0)),
                   pl.BlockSpec((tile, LSE_W), lambda n: (n, 0))],
        out_shape=[jax.ShapeDtypeStruct((s, CB), BF16),
                   jax.ShapeDtypeStruct((s, LSE_W), F32)],
        scratch_shapes=scratch,
        compiler_params=_params(1),
        name=f"dilated_attention_g{group}",
    )(proj3, proj3, proj3, proj3, proj3)


MIX_TM = 512


def _mixer_out_kernel(x_ref, o0_ref, o1_ref, o2_ref, l0_ref, l1_ref, l2_ref, u_ref,
                      gc0, gc1, gc2, gc3, ga0, ga1, ga2, ga3,
                      wa_ref, wc_ref, bc_ref, wo_ref, g2_ref, wrh_ref, wrl_ref, br_ref,
                      x1_ref, hp_ref, logit_ref):
    yc = jnp.dot(u_ref[...], wc_ref[...], preferred_element_type=F32) + bc_ref[...]
    l_refs = (l0_ref, l1_ref, l2_ref)
    o_refs = (o0_ref, o1_ref, o2_ref)
    parts = []
    for hh in range(HEADS_PER_GROUP):
        cs = slice(hh * HEAD_DIM, (hh + 1) * HEAD_DIM)
        lse = [r[:, hh * LSE_REP:hh * LSE_REP + 1] for r in l_refs]
        m = jnp.maximum(jnp.maximum(lse[0], lse[1]), lse[2])
        e = [jnp.exp(v - m) for v in lse]
        den = e[0] + e[1] + e[2]
        acc = (e[0] / den) * o_refs[0][:, cs].astype(F32)
        for g in range(1, N_GROUPS):
            acc = acc + (e[g] / den) * o_refs[g][:, cs].astype(F32)
        parts.append(acc)
    o = jnp.concatenate(parts, axis=1).astype(BF16)
    ya = jnp.dot(o, wa_ref[...], preferred_element_type=F32)
    gcs = (gc0, gc1, gc2, gc3)
    gas = (ga0, ga1, ga2, ga3)
    merged = []
    for c in range(D_MODEL // CB):
        cs = slice(c * CB, (c + 1) * CB)
        merged.append((_sigmoid(gcs[c][...].astype(F32)) * yc[:, cs]
                       + _sigmoid(gas[c][...].astype(F32)) * ya[:, cs]).astype(BF16))
    merged = jnp.concatenate(merged, axis=1)
    x1 = x_ref[...] + jnp.dot(merged, wo_ref[...], preferred_element_type=F32)
    x1_ref[...] = x1
    ms = jnp.mean(x1 * x1, axis=-1, keepdims=True)
    h2 = (x1 * lax.rsqrt(ms + NORM_EPS)) * g2_ref[...]
    h2_hi = h2.astype(BF16)
    hp_ref[...] = _pack_bf16_pair(h2[:, 0:PACK_W], h2[:, PACK_W:D_MODEL])
    h2_lo = (h2 - h2_hi.astype(F32)).astype(BF16)
    hi_pass = jnp.dot(h2_hi, wrl_ref[...], preferred_element_type=F32)
    logits = (hi_pass[:, 0:N_EXPERTS] + hi_pass[:, N_EXPERTS:2 * N_EXPERTS]
              + jnp.dot(h2_lo, wrh_ref[...], preferred_element_type=F32)) + br_ref[...]
    logit_ref[...] = logits


def _mixer_output(x2, outs, lses, u_ln, proj3, wa, wc, bc, wo, g2, wr_hi, wr_lo, br):
    s = x2.shape[0]
    row = lambda w: pl.BlockSpec((MIX_TM, w), lambda i: (i, 0))
    gate = lambda cb: pl.BlockSpec((None, MIX_TM, CB), lambda i: (cb, i, 0))
    full = lambda a: pl.BlockSpec(a.shape, lambda i: (0,) * a.ndim)
    in_specs = ([row(D_MODEL)] + [row(CB)] * 3 + [row(LSE_W)] * 3 + [row(D_CONV)]
                + [gate(CB_GATE_C + c) for c in range(4)]
                + [gate(CB_GATE_A + c) for c in range(4)]
                + [full(a) for a in (wa, wc, bc, wo, g2, wr_hi, wr_lo, br)])
    return pl.pallas_call(
        _mixer_out_kernel,
        grid=(s // MIX_TM,),
        in_specs=in_specs,
        out_specs=[row(D_MODEL), row(PACK_W), row(N_EXPERTS)],
        out_shape=[jax.ShapeDtypeStruct((s, D_MODEL), F32),
                   jax.ShapeDtypeStruct((s, PACK_W), U32),
                   jax.ShapeDtypeStruct((s, N_EXPERTS), F32)],
        compiler_params=_params(1),
        name="mixer_output",
    )(x2, *outs, *lses, u_ln, *([proj3] * 8), wa, wc, bc, wo, g2, wr_hi, wr_lo, br)


ROUTE_TM = 1024


def _route_kernel(logit_ref, idx_ref, gate_ref, pos_ref, cnt_ref, carry_ref):
    i = pl.program_id(0)
    tm = ROUTE_TM

    @pl.when(i == 0)
    def _():
        carry_ref[...] = jnp.zeros_like(carry_ref)

    logits = logit_ref[...]
    e_iota = lax.broadcasted_iota(I32, (tm, N_EXPERTS), 1).astype(F32)
    k_lane = lax.broadcasted_iota(I32, (tm, TOP_K), 1)
    vals = logits
    sels = []
    idx_t = jnp.zeros((tm, TOP_K), F32)
    val_t = jnp.zeros((tm, TOP_K), F32)
    for k in range(TOP_K):
        mk = jnp.max(vals, axis=-1, keepdims=True)
        ik = jnp.min(jnp.where(vals == mk, e_iota, float(N_EXPERTS)), axis=-1, keepdims=True)
        sel = e_iota == ik
        sels.append(sel)
        vals = jnp.where(sel, -jnp.inf, vals)
        idx_t = jnp.where(k_lane == k, ik, idx_t)
        val_t = jnp.where(k_lane == k, mk, val_t)
    ex = jnp.exp(val_t - val_t[:, 0:1])
    gate_ref[...] = ex / jnp.sum(ex, axis=-1, keepdims=True)
    idx_ref[...] = idx_t.astype(I32)

    cnt = jnp.zeros((tm, N_EXPERTS), F32)
    for sel in sels:
        cnt = cnt + jnp.where(sel, 1.0, 0.0)
    ri = lax.broadcasted_iota(I32, (tm, tm), 0)
    ci = lax.broadcasted_iota(I32, (tm, tm), 1)
    lower = jnp.where(ci < ri, 1.0, 0.0).astype(BF16)
    prefix = jnp.dot(lower, cnt.astype(BF16), preferred_element_type=F32) + carry_ref[...]
    pos_t = jnp.zeros((tm, TOP_K), F32)
    for k, sel in enumerate(sels):
        pk = jnp.sum(jnp.where(sel, prefix, 0.0), axis=-1, keepdims=True)
        pos_t = jnp.where(k_lane == k, pk, pos_t)
    pos_ref[...] = pos_t.astype(I32)
    carry_ref[...] = carry_ref[...] + jnp.sum(cnt, axis=0, keepdims=True)
    cnt_ref[...] = carry_ref[...]


def _route(logits):
    s = logits.shape[0]
    row = lambda w: pl.BlockSpec((ROUTE_TM, w), lambda i: (i, 0))
    return pl.pallas_call(
        _route_kernel,
        grid=(s // ROUTE_TM,),
        in_specs=[row(N_EXPERTS)],
        out_specs=[row(TOP_K), row(TOP_K), row(TOP_K),
                   pl.BlockSpec((1, N_EXPERTS), lambda i: (0, 0))],
        out_shape=[jax.ShapeDtypeStruct((s, TOP_K), I32),
                   jax.ShapeDtypeStruct((s, TOP_K), F32),
                   jax.ShapeDtypeStruct((s, TOP_K), I32),
                   jax.ShapeDtypeStruct((1, N_EXPERTS), F32)],
        scratch_shapes=[pltpu.VMEM((1, N_EXPERTS), F32)],
        compiler_params=_params(1),
        name="moe_route",
    )(logits)


ROW_BLK = 128
DISP_TM = 2048


def _dispatch_kernel(fill_ref, dest_ref, h_ref, xs_ref, zbuf, sem):
    i = pl.program_id(0)
    n_fill = fill_ref.shape[0]

    @pl.when(i == 0)
    def _():
        zbuf[...] = jnp.zeros_like(zbuf)

        def fill_copy(j):
            return pltpu.make_async_copy(
                zbuf, xs_ref.at[pl.ds(fill_ref[j] * ROW_BLK, ROW_BLK)], sem.at[0])

        def start(j, c):
            @pl.when(fill_ref[j] >= 0)
            def _():
                fill_copy(j).start()
            return c

        def wait(j, c):
            @pl.when(fill_ref[j] >= 0)
            def _():
                fill_copy(j).wait()
            return c

        lax.fori_loop(0, n_fill, start, 0)
        lax.fori_loop(0, n_fill, wait, 0)

    def rows8(tt, c):
        for si in range(8):
            for k in range(TOP_K):
                d = dest_ref[0, (tt * 8 + si) * TOP_K + k]
                pltpu.make_async_copy(h_ref.at[tt, pl.ds(si, 1)], xs_ref.at[pl.ds(d, 1)],
                                      sem.at[0]).start(priority=k % 2)
        return c

    lax.fori_loop(0, DISP_TM // 8, rows8, 0)
    for _ in range(TOP_K):
        pltpu.make_async_copy(h_ref, h_ref, sem.at[0]).wait()


def _dispatch(hp, dest, fill_blocks, n_rows):
    s = hp.shape[0]
    dest3 = dest.reshape(s // DISP_TM, 1, DISP_TM * TOP_K)
    grid_spec = pltpu.PrefetchScalarGridSpec(
        num_scalar_prefetch=1,
        grid=(s // DISP_TM,),
        in_specs=[
            pl.BlockSpec((None, 1, DISP_TM * TOP_K), lambda i, fr: (i, 0, 0),
                         memory_space=pltpu.SMEM),
            pl.BlockSpec((DISP_TM // 8, 8, PACK_W), lambda i, fr: (i, 0, 0)),
        ],
        out_specs=pl.BlockSpec(memory_space=pl.ANY),
        scratch_shapes=[pltpu.VMEM((ROW_BLK, PACK_W), U32), pltpu.SemaphoreType.DMA((1,))],
    )
    return pl.pallas_call(
        _dispatch_kernel,
        grid_spec=grid_spec,
        out_shape=jax.ShapeDtypeStruct((n_rows, PACK_W), U32),
        compiler_params=_params(1),
        name="moe_dispatch",
    )(fill_blocks, dest3, hp.reshape(s // 8, 8, PACK_W))


UNIT_BLOCKS = 10
UNIT_ROWS = ROW_BLK * UNIT_BLOCKS
GROUP_SIZES = (8, 4, 2, 1)
EXP_TF = 512
EXP_NF = D_EXPERT // EXP_TF


def _expert_kernel(ue_ref, ub0_ref, unb_ref, nu_ref,
                   xs_ref, win_ref, wout_ref, bin_ref, bo_ref,
                   ys_ref,
                   xbuf, xb16, acc, wg_buf, wu_buf, wo_buf, sem_x, sem_y, sem_w):
    u = pl.program_id(0)
    n_units = nu_ref[0]

    def x_copy(unit, b):
        src = xs_ref.at[pl.ds((ub0_ref[unit] + b) * ROW_BLK, ROW_BLK)]
        dst = xbuf.at[pl.ds(b * ROW_BLK, ROW_BLK)]
        return pltpu.make_async_copy(src, dst, sem_x.at[0])

    def y_copy(unit, b):
        src = acc.at[pl.ds(b * ROW_BLK, ROW_BLK)]
        dst = ys_ref.at[pl.ds((ub0_ref[unit] + b) * ROW_BLK, ROW_BLK)]
        return pltpu.make_async_copy(src, dst, sem_y.at[0])

    def w_copies(unit, f, slot):
        e = ue_ref[unit]
        c0 = pl.multiple_of(f * EXP_TF, EXP_TF)
        return (
            pltpu.make_async_copy(win_ref.at[e, :, pl.ds(c0, EXP_TF)], wg_buf.at[slot],
                                  sem_w.at[slot]),
            pltpu.make_async_copy(win_ref.at[e, :, pl.ds(D_EXPERT + c0, EXP_TF)], wu_buf.at[slot],
                                  sem_w.at[slot]),
            pltpu.make_async_copy(wout_ref.at[e, pl.ds(c0, EXP_TF), :], wo_buf.at[slot],
                                  sem_w.at[slot]),
        )

    def start_w(unit, f, slot):
        for cp in w_copies(unit, f, slot):
            cp.start()

    def wait_w(unit, f, slot):
        for cp in w_copies(unit, f, slot):
            cp.wait()

    def start_x(unit):
        lax.fori_loop(0, unb_ref[unit], lambda b, c: (x_copy(unit, b).start(), c)[1], 0)

    def wait_x(unit):
        lax.fori_loop(0, unb_ref[unit], lambda b, c: (x_copy(unit, b).wait(), c)[1], 0)

    def wait_y(unit):
        lax.fori_loop(0, unb_ref[unit], lambda b, c: (y_copy(unit, b).wait(), c)[1], 0)

    @pl.when(u < n_units)
    def _():
        nblk = unb_ref[u]

        @pl.when(u == 0)
        def _():
            start_x(u)
            start_w(u, 0, 0)
            acc[...] = jnp.zeros_like(acc)

        wait_x(u)

        def unpack(b, c):
            r0 = pl.multiple_of(b * ROW_BLK, ROW_BLK)
            lo, hi = _unpack_bf16_pair(xbuf[pl.ds(r0, ROW_BLK), :])
            xb16[pl.ds(r0, ROW_BLK), 0:PACK_W] = lo
            xb16[pl.ds(r0, ROW_BLK), PACK_W:D_MODEL] = hi
            return c

        lax.fori_loop(0, nblk, unpack, 0)

        @pl.when(u + 1 < n_units)
        def _():
            start_x(u + 1)

        @pl.when(u > 0)
        def _():
            wait_y(u - 1)

        def group(b0, n_blk, f, ws):
            rows = n_blk * ROW_BLK
            r0 = pl.multiple_of(b0 * ROW_BLK, ROW_BLK)
            c0 = pl.multiple_of(f * EXP_TF, EXP_TF)
            x = xb16[pl.ds(r0, rows), :]
            wgu = jnp.concatenate([wg_buf[ws].astype(BF16), wu_buf[ws].astype(BF16)], axis=1)
            gu = jnp.dot(x, wgu, preferred_element_type=F32)
            bg = bin_ref[:, pl.ds(c0, EXP_TF)]
            bu = bin_ref[:, pl.ds(D_EXPERT + c0, EXP_TF)]
            g = jnp.minimum(gu[:, 0:EXP_TF] + bg, SWIGLU_LIMIT)
            up = jnp.clip(gu[:, EXP_TF:2 * EXP_TF] + bu, -SWIGLU_LIMIT, SWIGLU_LIMIT)
            act = (up + 1.0) * (g * _sigmoid(SWIGLU_ALPHA * g))
            start = jnp.where(f == 0, jnp.broadcast_to(bo_ref[...], (rows, D_MODEL)),
                              acc[pl.ds(r0, rows), :])
            acc[pl.ds(r0, rows), :] = start + jnp.dot(
                act.astype(BF16), wo_buf[ws].astype(BF16), preferred_element_type=F32)

            @pl.when(f == EXP_NF - 1)
            def _():
                for j in range(n_blk):
                    y_copy(u, b0 + j).start()

        def f_tile(f, ws):
            big = GROUP_SIZES[0]
            n_big = nblk // big
            lax.fori_loop(0, n_big, lambda q, c: (group(big * q, big, f, ws), c)[1], 0)
            for size in GROUP_SIZES[1:]:
                done = (nblk // (2 * size)) * (2 * size)

                @pl.when(lax.rem(nblk, 2 * size) >= size)
                def _():
                    group(done, size, f, ws)

        def f_step(f, c):
            ws = lax.rem(f, 2)

            @pl.when(f + 1 < EXP_NF)
            def _():
                start_w(u, f + 1, 1 - ws)

            @pl.when((f + 1 == EXP_NF) & (u + 1 < n_units))
            def _():
                start_w(u + 1, 0, 1 - ws)

            wait_w(u, f, ws)
            f_tile(f, ws)
            return c

        lax.fori_loop(0, EXP_NF, f_step, 0)

        @pl.when(u == n_units - 1)
        def _():
            wait_y(u)

    @pl.when(u == pl.num_programs(0) - 1)
    def _():
        n_blocks = ys_ref.shape[0] // ROW_BLK
        acc[0:ROW_BLK, :] = jnp.zeros((ROW_BLK, D_MODEL), F32)

        def tail_copy(b):
            return pltpu.make_async_copy(acc.at[pl.ds(0, ROW_BLK)],
                                         ys_ref.at[pl.ds(b * ROW_BLK, ROW_BLK)], sem_y.at[0])

        lax.fori_loop(nu_ref[1], n_blocks, lambda b, c: (tail_copy(b).start(), c)[1], 0)
        lax.fori_loop(nu_ref[1], n_blocks, lambda b, c: (tail_copy(b).wait(), c)[1], 0)


def _expert_ffn(xs, unit_e, unit_blk0, unit_nblk, unit_counts, w_in, b_in, w_out, b_out):
    n_rows = xs.shape[0]
    n_units_max = unit_e.shape[0]

    def e_eff(u, ue, nu):
        return ue[jnp.minimum(u, jnp.maximum(nu[0] - 1, 0))]

    any_spec = pl.BlockSpec(memory_space=pl.ANY)
    in_specs = [
        any_spec, any_spec, any_spec,
        pl.BlockSpec((None, 1, 2 * D_EXPERT), lambda u, ue, ub, un, nu: (e_eff(u, ue, nu), 0, 0)),
        pl.BlockSpec((None, 1, D_MODEL), lambda u, ue, ub, un, nu: (e_eff(u, ue, nu), 0, 0)),
    ]
    grid_spec = pltpu.PrefetchScalarGridSpec(
        num_scalar_prefetch=4,
        grid=(n_units_max,),
        in_specs=in_specs,
        out_specs=any_spec,
        scratch_shapes=[
            pltpu.VMEM((UNIT_ROWS, PACK_W), U32),
            pltpu.VMEM((UNIT_ROWS, D_MODEL), BF16),
            pltpu.VMEM((UNIT_ROWS, D_MODEL), F32),
            pltpu.VMEM((2, D_MODEL, EXP_TF), F32),
            pltpu.VMEM((2, D_MODEL, EXP_TF), F32),
            pltpu.VMEM((2, EXP_TF, D_MODEL), F32),
            pltpu.SemaphoreType.DMA((1,)),
            pltpu.SemaphoreType.DMA((1,)),
            pltpu.SemaphoreType.DMA((2,)),
        ],
    )
    return pl.pallas_call(
        _expert_kernel,
        grid_spec=grid_spec,
        out_shape=jax.ShapeDtypeStruct((n_rows, D_MODEL), F32),
        compiler_params=_params(1),
        name="expert_ffn",
    )(unit_e, unit_blk0, unit_nblk, unit_counts, xs,
      w_in, w_out, b_in[:, None, :], b_out[:, None, :])


COMB_TM = 256


def _combine_kernel(dcur_ref, dnxt_ref, x1_ref, gate_ref, g_ref, ys_ref, o_ref, ybuf, sem):
    i = pl.program_id(0)
    n = pl.num_programs(0)
    slot = lax.rem(i, 2)

    def issue(dref, s):
        def rows8(tt, c):
            for si in range(8):
                for k in range(TOP_K):
                    d = dref[0, (tt * 8 + si) * TOP_K + k]
                    pltpu.make_async_copy(ys_ref.at[pl.ds(d, 1)],
                                          ybuf.at[s, k, tt, pl.ds(si, 1)],
                                          sem.at[s]).start(priority=k % 2)
            return c

        lax.fori_loop(0, COMB_TM // 8, rows8, 0)

    @pl.when(i == 0)
    def _():
        issue(dcur_ref, slot)

    @pl.when(i + 1 < n)
    def _():
        issue(dnxt_ref, 1 - slot)

    for k in range(TOP_K):
        pltpu.make_async_copy(ybuf.at[slot, k], ybuf.at[slot, k], sem.at[slot]).wait()

    gate = gate_ref[...]
    y = x1_ref[...]
    for k in range(TOP_K):
        y = y + gate[:, k:k + 1] * ybuf[slot, k].reshape(COMB_TM, D_MODEL)
    ms = jnp.mean(y * y, axis=-1, keepdims=True)
    o_ref[...] = (y * lax.rsqrt(ms + NORM_EPS)) * g_ref[...]


def _combine(dest, x1, gate, norm_g, ys):
    s = x1.shape[0]
    nt = s // COMB_TM
    dest3 = dest.reshape(nt, 1, COMB_TM * TOP_K)
    smem = lambda imap: pl.BlockSpec((None, 1, COMB_TM * TOP_K), imap, memory_space=pltpu.SMEM)
    row = lambda w: pl.BlockSpec((COMB_TM, w), lambda i: (i, 0))
    return pl.pallas_call(
        _combine_kernel,
        grid=(nt,),
        in_specs=[smem(lambda i: (i, 0, 0)),
                  smem(lambda i: (jnp.minimum(i + 1, nt - 1), 0, 0)),
                  row(D_MODEL), row(TOP_K), pl.BlockSpec((1, D_MODEL), lambda i: (0, 0)),
                  pl.BlockSpec(memory_space=pl.ANY)],
        out_specs=row(D_MODEL),
        out_shape=jax.ShapeDtypeStruct((s, D_MODEL), F32),
        scratch_shapes=[pltpu.VMEM((2, TOP_K, COMB_TM // 8, 8, D_MODEL), F32),
                        pltpu.SemaphoreType.DMA((2,))],
        compiler_params=_params(1),
        name="moe_combine",
    )(dest3, dest3, x1, gate, norm_g, ys)


def _route_tables(idx, pos, counts, n_tok):
    n_assign = n_tok * TOP_K
    counts = counts.reshape(N_EXPERTS).astype(I32)
    nblk_e = (counts + ROW_BLK - 1) // ROW_BLK
    pad_start = (jnp.cumsum(nblk_e) - nblk_e) * ROW_BLK
    onehot = idx[..., None] == jnp.arange(N_EXPERTS, dtype=I32)
    dest = pos + jnp.sum(jnp.where(onehot, pad_start, 0), axis=-1)

    n_blocks = -(-(n_assign + N_EXPERTS * (ROW_BLK - 1)) // ROW_BLK)
    used = jnp.sum(nblk_e)
    blk_ids = jnp.arange(n_blocks, dtype=I32)
    last_blk = pad_start // ROW_BLK + nblk_e - 1
    fill_blocks = jnp.concatenate([jnp.where(nblk_e > 0, last_blk, -1),
                                   jnp.where(blk_ids >= used, blk_ids, -1)]).astype(I32)

    n_units_max = (n_blocks + (UNIT_BLOCKS - 1) * N_EXPERTS) // UNIT_BLOCKS
    chunks_e = (nblk_e + UNIT_BLOCKS - 1) // UNIT_BLOCKS
    chunk_end = jnp.cumsum(chunks_e)
    chunk_start = chunk_end - chunks_e
    n_units = chunk_end[-1]
    uid = jnp.arange(n_units_max, dtype=I32)
    unit_e = jnp.minimum(jnp.searchsorted(chunk_end, uid, side="right"), N_EXPERTS - 1).astype(I32)
    c_in_e = uid - chunk_start[unit_e]
    unit_blk0 = (pad_start[unit_e] // ROW_BLK + c_in_e * UNIT_BLOCKS).astype(I32)
    unit_nblk = jnp.clip(nblk_e[unit_e] - c_in_e * UNIT_BLOCKS, 0, UNIT_BLOCKS).astype(I32)
    active = uid < n_units
    unit_blk0 = jnp.where(active, unit_blk0, 0)
    unit_nblk = jnp.where(active, unit_nblk, 0)
    unit_counts = jnp.stack([n_units, used]).astype(I32)
    return dest.astype(I32), fill_blocks, n_blocks * ROW_BLK, unit_e, unit_blk0, unit_nblk, unit_counts


def kernel(x, norm1_g, w_in, b_gate, conv_w, conv_b, conv_ln_g, conv_ln_b, w_conv_out, b_conv_out,
           w_attn_out, w_out, norm2_g, w_router, b_router, w_exp_in, b_exp_in, w_exp_out, b_exp_out,
           norm_f_g):
    b_sz, s_len, d = x.shape
    n_tok = b_sz * s_len
    x2 = x.reshape(n_tok, d)
    l = 0
    b_gate_ext = jnp.concatenate([jnp.zeros((GATE_COL0,), F32), b_gate[l]])[None, :]
    proj3 = _in_projection(x2, norm1_g[l][None, :], w_in[l], b_gate_ext)
    u_ln = _conv_branch(proj3, conv_w[l], conv_b[l][None, :], conv_ln_g[l][None, :],
                        conv_ln_b[l][None, :])
    outs, lses = [], []
    for g in range(N_GROUPS):
        o_g, lse_g = _dilated_attention(proj3, g)
        outs.append(o_g)
        lses.append(lse_g)
    wr = w_router[l]
    wr_hi = wr.astype(BF16)
    wr_lo = jnp.concatenate([wr_hi, (wr - wr_hi.astype(F32)).astype(BF16)], axis=1)
    x1, hp, logits = _mixer_output(
        x2, outs, lses, u_ln, proj3,
        w_attn_out[l].astype(BF16), w_conv_out[l].astype(BF16), b_conv_out[l][None, :],
        w_out[l].astype(BF16), norm2_g[l][None, :], wr_hi, wr_lo, b_router[l][None, :])

    idx, gate, pos, counts = _route(logits)
    dest, fill_blocks, n_rows, unit_e, unit_blk0, unit_nblk, unit_counts = _route_tables(
        idx, pos, counts, n_tok)
    xs = _dispatch(hp, dest, fill_blocks, n_rows)
    ys = _expert_ffn(xs, unit_e, unit_blk0, unit_nblk, unit_counts,
                     w_exp_in[l], b_exp_in[l], w_exp_out[l], b_exp_out[l])
    out = _combine(dest, x1, gate, norm_f_g[None, :], ys)
    return out.reshape(b_sz, s_len, d)
```
